```python
import jax, jax.numpy as jnp
from jax import lax
import numpy as np

D_MODEL = 2048
BATCH = 1
SEQ = 8192
DEPTH = 1
DEC_BATCH = 2
DEC_SEQ = 4096
PAST_LEN = 128

GRID_W = 64
POOL_WIDTH = 1024
POOL_WINDOWS = (2, 4, 8, 16)
N_POOL_GROUPS = 4
POOL_GROUP = POOL_WIDTH // N_POOL_GROUPS
N_HEADS = 16
HEAD_DIM = 64
ATTN_WIDTH = N_HEADS * HEAD_DIM
WIN_R = 8
WIN_C = 16
IN_WIDTH = POOL_WIDTH + 3 * ATTN_WIDTH + 2 * D_MODEL
SPLITS = (POOL_WIDTH,
          POOL_WIDTH + ATTN_WIDTH,
          POOL_WIDTH + 2 * ATTN_WIDTH,
          POOL_WIDTH + 3 * ATTN_WIDTH,
          POOL_WIDTH + 3 * ATTN_WIDTH + D_MODEL)
N_EXPERTS = 64
TOP_K = 8
N_GROUPS = 8
TOPK_GROUPS = 4
EXPERT_DIM = 512
SHARED_DIM = 512
ROUTED_SCALE = 2.5
EXPERT_BLOCK = 128
LN_EPS = 1e-5
DN_ALPHA = (2.0 * DEPTH) ** 0.25
DN_BETA = (8.0 * DEPTH) ** -0.25

kernel_name = "hybrid_pool_natten_moe_encoder"


def layer_norm(x, g, b):
    xf = x.astype(jnp.float32)
    mu = jnp.mean(xf, axis=-1, keepdims=True)
    var = jnp.mean(jnp.square(xf - mu), axis=-1, keepdims=True)
    y = (xf - mu) * lax.rsqrt(var + LN_EPS) * g.astype(jnp.float32) + b.astype(jnp.float32)
    return y.astype(x.dtype)


def multi_scale_pool(u, pool_w, pool_b, pool_scale):
    b, L, _ = u.shape
    ug = u.reshape(b, L, N_POOL_GROUPS, POOL_GROUP).astype(jnp.float32)
    cs = jnp.concatenate([jnp.zeros((b, 1, N_POOL_GROUPS, POOL_GROUP), jnp.float32),
                          jnp.cumsum(ug, axis=1)], axis=1)
    t = jnp.arange(L)
    outs = []
    for g, w in enumerate(POOL_WINDOWS):
        lo = jnp.clip(t - w // 2, 0, L)
        hi = jnp.clip(t - w // 2 + w, 0, L)
        cnt = (hi - lo).astype(jnp.float32)
        csg = cs[:, :, g]
        mean = (csg[:, hi] - csg[:, lo]) / cnt[None, :, None]
        outs.append(mean - ug[:, :, g])
    z = jnp.stack(outs, axis=2).astype(u.dtype)
    y = jnp.einsum('blgc,gcd->blgd', z, pool_w) + pool_b
    return y.reshape(b, L, POOL_WIDTH) * pool_scale


def neighborhood_attention(q, k, v, rpb):
    b, L, h, dh = q.shape
    rows = L // GRID_W
    kr = min(WIN_R, rows)
    kc = WIN_C
    qg = q.reshape(b, rows, GRID_W, h, dh)
    kg = k.reshape(b, rows, GRID_W, h, dh)
    vg = v.reshape(b, rows, GRID_W, h, dh)
    col = jnp.arange(GRID_W)
    col_start = jnp.clip(col - kc // 2, 0, GRID_W - kc)
    col_idx = col_start[:, None] + jnp.arange(kc)[None, :]
    dc_idx = col_idx - col[:, None] + (WIN_C - 1)
    rpb_c = rpb[:, :, dc_idx]
    scale = HEAD_DIM ** -0.5

    def row_block(r):
        rs = jnp.clip(r - kr // 2, 0, rows - kr)
        q_r = lax.dynamic_index_in_dim(qg, r, axis=1, keepdims=False)
        k_s = lax.dynamic_slice_in_dim(kg, rs, kr, axis=1)
        v_s = lax.dynamic_slice_in_dim(vg, rs, kr, axis=1)
        k_w = k_s[:, :, col_idx]
        v_w = v_s[:, :, col_idx]
        dr_idx = rs + jnp.arange(kr) - r + (WIN_R - 1)
        bias = jnp.transpose(jnp.take(rpb_c, dr_idx, axis=1), (0, 2, 1, 3))
        s = jnp.einsum('bqhd,brqjhd->bhqrj', q_r, k_w,
                       preferred_element_type=jnp.float32) * scale
        s = s + bias[None].astype(jnp.float32)
        p = jax.nn.softmax(s.reshape(b, h, GRID_W, kr * kc), axis=-1)
        p = p.reshape(b, h, GRID_W, kr, kc).astype(v.dtype)
        return jnp.einsum('bhqrj,brqjhd->bqhd', p, v_w)

    out = lax.map(row_block, jnp.arange(rows))
    return jnp.transpose(out, (1, 0, 2, 3, 4)).reshape(b, L, h * dh)


def route(x2d, router_w, router_bias):
    t = x2d.shape[0]
    logits = jnp.dot(x2d, router_w, preferred_element_type=jnp.float32)
    scores = jax.nn.sigmoid(logits)
    biased = scores + router_bias.astype(jnp.float32)
    grouped = biased.reshape(t, N_GROUPS, N_EXPERTS // N_GROUPS)
    group_score = jnp.sum(lax.top_k(grouped, 2)[0], axis=-1)
    _, gsel = lax.top_k(group_score, TOPK_GROUPS)
    gmask = jnp.sum(jax.nn.one_hot(gsel, N_GROUPS, dtype=jnp.float32), axis=1) > 0
    emask = jnp.repeat(gmask, N_EXPERTS // N_GROUPS, axis=1)
    masked = jnp.where(emask, biased, -jnp.inf)
    _, idx = lax.top_k(masked, TOP_K)
    w = jnp.take_along_axis(scores, idx, axis=1)
    w = w / jnp.sum(w, axis=-1, keepdims=True) * ROUTED_SCALE
    return idx, w


def routed_experts(x2d, idx, gates, w1, w3, w2):
    t, d = x2d.shape
    m = t * TOP_K
    flat_e = idx.reshape(m)
    flat_tok = jnp.repeat(jnp.arange(t, dtype=jnp.int32), TOP_K)
    flat_w = gates.reshape(m)
    order = jnp.argsort(flat_e, stable=True)
    sorted_e = flat_e[order]
    counts = jnp.bincount(flat_e, length=N_EXPERTS)
    pcounts = (counts + EXPERT_BLOCK - 1) // EXPERT_BLOCK * EXPERT_BLOCK
    pend = jnp.cumsum(pcounts)
    pstart = pend - pcounts
    start = jnp.cumsum(counts) - counts
    dest = pstart[sorted_e] + jnp.arange(m) - start[sorted_e]
    n_blk = (m + N_EXPERTS * (EXPERT_BLOCK - 1) + EXPERT_BLOCK - 1) // EXPERT_BLOCK
    n_pad = n_blk * EXPERT_BLOCK
    buf_tok = jnp.full((n_pad,), t, jnp.int32).at[dest].set(flat_tok[order])
    buf_w = jnp.zeros((n_pad,), jnp.float32).at[dest].set(flat_w[order])
    block_e = jnp.clip(jnp.searchsorted(pend, jnp.arange(n_blk) * EXPERT_BLOCK, side='right'),
                       0, N_EXPERTS - 1)
    x_pad = jnp.concatenate([x2d, jnp.zeros((1, d), x2d.dtype)], axis=0)
    xb = x_pad[buf_tok].reshape(n_blk, EXPERT_BLOCK, d)

    def expert_block(args):
        xi, e = args
        hdn = jax.nn.silu(xi @ w1[e]) * (xi @ w3[e])
        return hdn @ w2[e]

    yb = lax.map(expert_block, (xb, block_e)).reshape(n_pad, d)
    y = jnp.zeros((t + 1, d), x2d.dtype).at[buf_tok].add(yb * buf_w[:, None].astype(yb.dtype))
    return y[:t]


def encoder_layer(x, w_in, pool_w, pool_b, pool_scale, w_pool_br, w_attn_br, attn_rpb, w_out,
                  ln1_g, ln1_b, router_w, router_bias, exp_w1, exp_w3, exp_w2,
                  sh_w1, sh_w3, sh_w2, ln2_g, ln2_b):
    b, L, d = x.shape
    h = x @ w_in
    u_pool, q, k, v, g_pool, g_attn = jnp.split(h, SPLITS, axis=-1)
    a_pool = multi_scale_pool(u_pool, pool_w, pool_b, pool_scale) @ w_pool_br
    attn = neighborhood_attention(q.reshape(b, L, N_HEADS, HEAD_DIM),
                                  k.reshape(b, L, N_HEADS, HEAD_DIM),
                                  v.reshape(b, L, N_HEADS, HEAD_DIM), attn_rpb)
    a_attn = attn @ w_attn_br
    merged = jax.nn.sigmoid(g_pool) * a_pool + jax.nn.sigmoid(g_attn) * a_attn
    x = layer_norm(DN_ALPHA * x + merged @ w_out, ln1_g, ln1_b)
    x2d = x.reshape(b * L, d)
    shared = (jax.nn.silu(x2d @ sh_w1) * (x2d @ sh_w3)) @ sh_w2
    idx, gates = route(x2d, router_w, router_bias)
    routed = routed_experts(x2d, idx, gates, exp_w1, exp_w3, exp_w2)
    ffn = (shared + routed).reshape(b, L, d)
    return layer_norm(DN_ALPHA * x + ffn, ln2_g, ln2_b)


def setup_inputs(seed: int = 0) -> dict:
    key = jax.random.key(seed)
    ks = jax.random.split(key, 24)

    def nrm(k, shape, scale):
        return jax.random.normal(k, shape, jnp.float32) * scale

    x_prompt = nrm(ks[0], (BATCH, SEQ, D_MODEL), 1.0)
    x_sample = nrm(ks[1], (DEC_BATCH, DEC_SEQ, D_MODEL), 1.0)
    col_scale = jnp.concatenate([jnp.ones((POOL_WIDTH + 2 * ATTN_WIDTH,), jnp.float32),
                                 jnp.full((ATTN_WIDTH,), DN_BETA, jnp.float32),
                                 jnp.ones((2 * D_MODEL,), jnp.float32)])
    w_in = nrm(ks[2], (DEPTH, D_MODEL, IN_WIDTH), D_MODEL ** -0.5) * col_scale
    pool_w = nrm(ks[3], (DEPTH, N_POOL_GROUPS, POOL_GROUP, POOL_GROUP), POOL_GROUP ** -0.5)
    pool_b = nrm(ks[4], (DEPTH, N_POOL_GROUPS, POOL_GROUP), 0.01)
    pool_scale = 1.0 + nrm(ks[5], (DEPTH, POOL_WIDTH), 0.1)
    w_pool_br = nrm(ks[6], (DEPTH, POOL_WIDTH, D_MODEL), POOL_WIDTH ** -0.5)
    w_attn_br = nrm(ks[7], (DEPTH, ATTN_WIDTH, D_MODEL), ATTN_WIDTH ** -0.5)
    attn_rpb = nrm(ks[8], (DEPTH, N_HEADS, 2 * WIN_R - 1, 2 * WIN_C - 1), 0.1)
    w_out = nrm(ks[9], (DEPTH, D_MODEL, D_MODEL), D_MODEL ** -0.5 * DN_BETA)
    ln1_g = 1.0 + nrm(ks[10], (DEPTH, D_MODEL), 0.05)
    ln1_b = nrm(ks[11], (DEPTH, D_MODEL), 0.02)
    router_w = nrm(ks[12], (DEPTH, D_MODEL, N_EXPERTS), D_MODEL ** -0.5)
    router_bias = nrm(ks[13], (DEPTH, N_EXPERTS), 0.01)
    exp_w1 = nrm(ks[14], (DEPTH, N_EXPERTS, D_MODEL, EXPERT_DIM), D_MODEL ** -0.5)
    exp_w3 = nrm(ks[15], (DEPTH, N_EXPERTS, D_MODEL, EXPERT_DIM), D_MODEL ** -0.5)
    exp_w2 = nrm(ks[16], (DEPTH, N_EXPERTS, EXPERT_DIM, D_MODEL), EXPERT_DIM ** -0.5 * DN_BETA)
    sh_w1 = nrm(ks[17], (DEPTH, D_MODEL, SHARED_DIM), D_MODEL ** -0.5)
    sh_w3 = nrm(ks[18], (DEPTH, D_MODEL, SHARED_DIM), D_MODEL ** -0.5)
    sh_w2 = nrm(ks[19], (DEPTH, SHARED_DIM, D_MODEL), SHARED_DIM ** -0.5 * DN_BETA)
    ln2_g = 1.0 + nrm(ks[20], (DEPTH, D_MODEL), 0.05)
    ln2_b = nrm(ks[21], (DEPTH, D_MODEL), 0.02)
    return {"x_prompt": x_prompt, "x_sample": x_sample, "w_in": w_in, "pool_w": pool_w,
            "pool_b": pool_b, "pool_scale": pool_scale, "w_pool_br": w_pool_br,
            "w_attn_br": w_attn_br, "attn_rpb": attn_rpb, "w_out": w_out,
            "ln1_g": ln1_g, "ln1_b": ln1_b, "router_w": router_w, "router_bias": router_bias,
            "exp_w1": exp_w1, "exp_w3": exp_w3, "exp_w2": exp_w2,
            "sh_w1": sh_w1, "sh_w3": sh_w3, "sh_w2": sh_w2, "ln2_g": ln2_g, "ln2_b": ln2_b}


def reference(x_prompt, x_sample, w_in, pool_w, pool_b, pool_scale, w_pool_br, w_attn_br,
              attn_rpb, w_out, ln1_g, ln1_b, router_w, router_bias, exp_w1, exp_w3, exp_w2,
              sh_w1, sh_w3, sh_w2, ln2_g, ln2_b):
    y_prompt = x_prompt
    y_sample = x_sample
    for l in range(DEPTH):
        lp = (w_in[l], pool_w[l], pool_b[l], pool_scale[l], w_pool_br[l], w_attn_br[l],
              attn_rpb[l], w_out[l], ln1_g[l], ln1_b[l], router_w[l], router_bias[l],
              exp_w1[l], exp_w3[l], exp_w2[l], sh_w1[l], sh_w3[l], sh_w2[l],
              ln2_g[l], ln2_b[l])
        y_prompt = encoder_layer(y_prompt, *lp)
        y_sample = encoder_layer(y_sample, *lp)
    return (y_prompt, y_sample)
```

```python
import functools

import numpy as np
import jax
import jax.numpy as jnp
from jax import lax
from jax.experimental import pallas as pl
from jax.experimental.pallas import tpu as pltpu

F32 = jnp.float32
BF16 = jnp.bfloat16
I32 = jnp.int32

GRID_W = 64
POOL_WINDOWS = (2, 4, 8, 16)
WIN_R = 8
WIN_C = 16
TOP_K = 8
N_GROUPS = 8
TOPK_GROUPS = 4
ROUTED_SCALE = 2.5
LN_EPS = 1e-5
NEG_BIG = -1e30

LANES = 128
VMEM_LIMIT = 56 * 1024 * 1024

INPROJ_TM = 1024
INPROJ_TN = 1024
ATT_ROWS = 8
MIX_TM = 256
POOL_HALO = 16
EXPERT_BM = 256
COMBINE_TM = 128


def _cparams(sem):
    return pltpu.CompilerParams(dimension_semantics=sem, vmem_limit_bytes=VMEM_LIMIT)


def _inproj_body(x_ref, w_ref, o_ref, xb_ref, *, q_block, gate_block0, q_scale):
    j = pl.program_id(1)

    @pl.when(j == 0)
    def _():
        xb_ref[...] = x_ref[...].astype(BF16)

    acc = jnp.dot(xb_ref[...], w_ref[...], preferred_element_type=F32)
    is_gate = j >= gate_block0

    @pl.when(is_gate)
    def _():
        o_ref[...] = jax.nn.sigmoid(acc).astype(BF16)

    @pl.when(jnp.logical_not(is_gate))
    def _():
        scale = jnp.where(j == q_block, q_scale, 1.0).astype(F32)
        o_ref[...] = (acc * scale).astype(BF16)


def _inproj(x, w_bf16, *, q_block, gate_block0, q_scale):
    t, d = x.shape
    n = w_bf16.shape[1]
    tm, tn = INPROJ_TM, INPROJ_TN
    return pl.pallas_call(
        functools.partial(_inproj_body, q_block=q_block, gate_block0=gate_block0, q_scale=q_scale),
        grid=(t // tm, n // tn),
        in_specs=[pl.BlockSpec((tm, d), lambda i, j: (i, 0)),
                  pl.BlockSpec((d, tn), lambda i, j: (0, j))],
        out_specs=pl.BlockSpec((tm, tn), lambda i, j: (i, j)),
        out_shape=jax.ShapeDtypeStruct((t, n), BF16),
        scratch_shapes=[pltpu.VMEM((tm, d), BF16)],
        compiler_params=_cparams(("arbitrary", "arbitrary")),
        name="inproj",
    )(x, w_bf16)


def _attn_body(prev_ref, next_ref, kst_ref, off_ref,
               q_ref, kp_ref, kc_ref, kn_ref, vp_ref, vc_ref, vn_ref, bias_ref, o_ref,
               kbuf, vbuf, *, heads_per_group, head_dim):
    del prev_ref, next_ref
    b = pl.program_id(0)
    blk = ATT_ROWS * GRID_W
    kbuf[0:blk] = kp_ref[...]
    kbuf[blk:2 * blk] = kc_ref[...]
    kbuf[2 * blk:3 * blk] = kn_ref[...]
    vbuf[0:blk] = vp_ref[...]
    vbuf[blk:2 * blk] = vc_ref[...]
    vbuf[2 * blk:3 * blk] = vn_ref[...]

    gw = heads_per_group * head_dim
    n_groups = q_ref.shape[1] // gw
    win = WIN_R * GRID_W
    nq = heads_per_group * GRID_W
    row_head = lax.broadcasted_iota(I32, (nq, gw), 0) // GRID_W
    lane_head = lax.broadcasted_iota(I32, (nq, gw), 1) // head_dim
    head_mask = row_head == lane_head
    out_lane_head = lax.broadcasted_iota(I32, (GRID_W, gw), 1) // head_dim

    def row_step(i, carry):
        st = pl.multiple_of(kst_ref[b * ATT_ROWS + i] * GRID_W, GRID_W)
        dr0 = (WIN_R - 1) - off_ref[b * ATT_ROWS + i]
        q0 = pl.multiple_of(i * GRID_W, GRID_W)
        for g in range(n_groups):
            qg = q_ref[pl.ds(q0, GRID_W), g * gw:(g + 1) * gw]
            kg = kbuf[pl.ds(st, win), g * gw:(g + 1) * gw]
            vg = vbuf[pl.ds(st, win), g * gw:(g + 1) * gw]
            wt = jnp.concatenate([qg] * heads_per_group, axis=0)
            wt = jnp.where(head_mask, wt, jnp.zeros_like(wt))
            s = lax.dot_general(wt, kg, (((1,), (1,)), ((), ())),
                                preferred_element_type=F32)
            n_chunk = win // LANES
            sj = [s[:, j * LANES:(j + 1) * LANES] + bias_ref[dr0 + 2 * j, g] for j in range(n_chunk)]
            m = sj[0]
            for j in range(1, n_chunk):
                m = jnp.maximum(m, sj[j])
            m = jnp.max(m, axis=1, keepdims=True)
            ej = [jnp.exp(x - m) for x in sj]
            l = ej[0]
            for j in range(1, n_chunk):
                l = l + ej[j]
            inv = 1.0 / jnp.sum(l, axis=1, keepdims=True)
            p = jnp.concatenate([(x * inv).astype(BF16) for x in ej], axis=1)
            of = jnp.dot(p, vg, preferred_element_type=F32)
            out = jnp.zeros((GRID_W, gw), F32)
            for h in range(heads_per_group):
                out = out + jnp.where(out_lane_head == h, of[h * GRID_W:(h + 1) * GRID_W, :], 0.0)
            o_ref[pl.ds(q0, GRID_W), g * gw:(g + 1) * gw] = out.astype(o_ref.dtype)
        return carry

    lax.fori_loop(0, ATT_ROWS, row_step, 0)


def _attn_tables(images):
    prev, nxt, kst, off = [], [], [], []
    for s, n in images:
        rows = n * ATT_ROWS
        kr = min(WIN_R, rows)
        assert kr == WIN_R
        for bl in range(n):
            b = s + bl
            prev.append(max(b - 1, s))
            nxt.append(min(b + 1, s + n - 1))
            for i in range(ATT_ROWS):
                r = bl * ATT_ROWS + i
                rs = min(max(r - kr // 2, 0), rows - kr)
                kst.append(rs - bl * ATT_ROWS + ATT_ROWS)
                off.append(r - rs)
    return (np.asarray(prev, np.int32), np.asarray(nxt, np.int32),
            np.asarray(kst, np.int32), np.asarray(off, np.int32))


def _attn_bias_table(rpb, heads_per_group):
    n_heads = rpb.shape[0]
    c = np.arange(GRID_W)
    cs = np.clip(c - WIN_C // 2, 0, GRID_W - WIN_C)
    kc = np.arange(GRID_W)
    valid = (kc[None, :] >= cs[:, None]) & (kc[None, :] < cs[:, None] + WIN_C)
    dc = np.clip(kc[None, :] - c[:, None] + (WIN_C - 1), 0, 2 * WIN_C - 2)
    a = rpb.astype(F32)[:, :, dc]
    a = jnp.where(valid[None, None], a, NEG_BIG)
    n_groups = n_heads // heads_per_group
    tabs = []
    for dr0 in range(2 * WIN_R - 2):
        pair = a[:, dr0:dr0 + 2]
        pair = pair.reshape(n_groups, heads_per_group, 2, GRID_W, GRID_W)
        pair = jnp.transpose(pair, (0, 1, 3, 2, 4))
        tabs.append(pair.reshape(n_groups, heads_per_group * GRID_W, 2 * GRID_W))
    return jnp.stack(tabs, axis=0)


def _attention(h, bias_tab, images, *, q_block, n_heads, head_dim):
    t = h.shape[0]
    blk = ATT_ROWS * GRID_W
    aw = n_heads * head_dim
    heads_per_group = 2 * LANES // head_dim
    prev, nxt, kst, off = _attn_tables(images)
    n_blocks = t // blk
    assert prev.shape[0] == n_blocks
    kcol, vcol = q_block + 1, q_block + 2
    spec = lambda col, which: pl.BlockSpec(
        (blk, aw), {"cur": lambda b, p, n, k, o: (b, col),
                    "prev": lambda b, p, n, k, o: (p[b], col),
                    "next": lambda b, p, n, k, o: (n[b], col)}[which])
    grid_spec = pltpu.PrefetchScalarGridSpec(
        num_scalar_prefetch=4,
        grid=(n_blocks,),
        in_specs=[spec(q_block, "cur"),
                  spec(kcol, "prev"), spec(kcol, "cur"), spec(kcol, "next"),
                  spec(vcol, "prev"), spec(vcol, "cur"), spec(vcol, "next"),
                  pl.BlockSpec(bias_tab.shape, lambda b, p, n, k, o: (0, 0, 0, 0),
                               pipeline_mode=pl.Buffered(1))],
        out_specs=pl.BlockSpec((blk, aw), lambda b, p, n, k, o: (b, 0)),
        scratch_shapes=[pltpu.VMEM((3 * blk, aw), BF16), pltpu.VMEM((3 * blk, aw), BF16)],
    )
    return pl.pallas_call(
        functools.partial(_attn_body, heads_per_group=heads_per_group, head_dim=head_dim),
        grid_spec=grid_spec,
        out_shape=jax.ShapeDtypeStruct((t, aw), BF16),
        compiler_params=_cparams(("arbitrary",)),
        name="natten",
    )(jnp.asarray(prev), jnp.asarray(nxt), jnp.asarray(kst), jnp.asarray(off),
      h, h, h, h, h, h, h, bias_tab)


def _layer_norm(r, g, b):
    mu = jnp.mean(r, axis=-1, keepdims=True)
    c = r - mu
    var = jnp.mean(c * c, axis=-1, keepdims=True)
    return c * lax.rsqrt(var + LN_EPS) * g + b


def _first_index_of(mask, idx, sentinel, axis):
    return jnp.min(jnp.where(mask, idx, sentinel), axis=axis, keepdims=True)


def _mixer_body(seq_lo_ref, seq_hi_ref,
                x_ref, up_ref, uc_ref, un_ref, gp_ref, ga_ref, at_ref,
                pw_ref, pb_ref, ps_ref, wpb_ref, wab_ref, wo_ref, g1_ref, b1_ref,
                s13_ref, s2_ref, rwh_ref, rwl_ref, rb_ref,
                x1_ref, base_ref, idx_ref, gate_ref, rank_ref, cnt_ref,
                carry_ref, *, alpha, n_experts):
    i = pl.program_id(0)
    tm = x_ref.shape[0]
    t0 = i * tm
    lo_seq = seq_lo_ref[i]
    hi_seq = seq_hi_ref[i]

    @pl.when(i == 0)
    def _():
        carry_ref[...] = jnp.zeros_like(carry_ref)

    u_ext = jnp.concatenate([up_ref[...], uc_ref[...], un_ref[...]], axis=0)
    ext = tm + 2 * POOL_HALO
    tok_r = t0 + lax.broadcasted_iota(I32, (tm, ext), 0)
    tok_c = t0 - POOL_HALO + lax.broadcasted_iota(I32, (tm, ext), 1)
    tok_q = t0 + lax.broadcasted_iota(I32, (tm, 1), 0)
    n_pg = pw_ref.shape[0]
    pg = pw_ref.shape[1]
    parts = []
    for g in range(n_pg):
        w = POOL_WINDOWS[g]
        lo = jnp.maximum(tok_r - w // 2, lo_seq)
        hi = jnp.minimum(tok_r - w // 2 + w, hi_seq)
        band = jnp.where((tok_c >= lo) & (tok_c < hi), 1.0, 0.0).astype(BF16)
        cnt = (jnp.minimum(tok_q - w // 2 + w, hi_seq) - jnp.maximum(tok_q - w // 2, lo_seq)).astype(F32)
        ug = u_ext[:, g * pg:(g + 1) * pg]
        sums = jnp.dot(band, ug, preferred_element_type=F32)
        z = sums / cnt - uc_ref[:, g * pg:(g + 1) * pg].astype(F32)
        y = jnp.dot(z.astype(BF16), pw_ref[g], preferred_element_type=F32)
        y = (y + pb_ref[:, g * pg:(g + 1) * pg]) * ps_ref[:, g * pg:(g + 1) * pg]
        parts.append(y.astype(BF16))
    a = jnp.concatenate(parts, axis=1)
    a_pool = jnp.dot(a, wpb_ref[...], preferred_element_type=F32)
    a_attn = jnp.dot(at_ref[...], wab_ref[...], preferred_element_type=F32)
    merged = gp_ref[...].astype(F32) * a_pool + ga_ref[...].astype(F32) * a_attn
    mix = jnp.dot(merged.astype(BF16), wo_ref[...], preferred_element_type=F32)
    x1 = _layer_norm(alpha * x_ref[...] + mix, g1_ref[...], b1_ref[...])
    x1_ref[...] = x1
    x1b = x1.astype(BF16)

    sd = s2_ref.shape[0]
    hc = jnp.dot(x1b, s13_ref[...], preferred_element_type=F32)
    hdn = jax.nn.silu(hc[:, :sd]) * hc[:, sd:]
    shared = jnp.dot(hdn.astype(BF16), s2_ref[...], preferred_element_type=F32)
    base_ref[...] = alpha * x1 + shared

    x1l = (x1 - x1b.astype(F32)).astype(BF16)
    nt = (((1,), (1,)), ((), ()))
    logits = (lax.dot_general(rwh_ref[...], x1b, nt, preferred_element_type=F32)
              + lax.dot_general(rwh_ref[...], x1l, nt, preferred_element_type=F32)
              + lax.dot_general(rwl_ref[...], x1b, nt, preferred_element_type=F32))
    scores = jax.nn.sigmoid(logits)
    biased = scores + rb_ref[...]
    gsz = n_experts // N_GROUPS
    neg_inf = -jnp.inf

    mem = lax.broadcasted_iota(I32, (gsz, tm), 0)
    gs_rows = []
    for g in range(N_GROUPS):
        bg = biased[g * gsz:(g + 1) * gsz, :]
        m1 = jnp.max(bg, axis=0, keepdims=True)
        first = _first_index_of(bg == m1, mem, gsz, 0)
        m2 = jnp.max(jnp.where(mem == first, neg_inf, bg), axis=0, keepdims=True)
        gs_rows.append(m1 + m2)
    gwork = jnp.concatenate(gs_rows, axis=0)

    gidx = lax.broadcasted_iota(I32, (N_GROUPS, tm), 0)
    egrp = lax.broadcasted_iota(I32, (n_experts, tm), 0) // gsz
    eidx = lax.broadcasted_iota(I32, (n_experts, tm), 0)
    work = jnp.full((n_experts, tm), neg_inf, F32)
    for _ in range(TOPK_GROUPS):
        gm = jnp.max(gwork, axis=0, keepdims=True)
        gf = _first_index_of(gwork == gm, gidx, N_GROUPS, 0)
        gwork = jnp.where(gidx == gf, neg_inf, gwork)
        work = jnp.where(egrp == gf, biased, work)

    hot = jnp.zeros((n_experts, tm), F32)
    sel_idx, sel_w, sel_hit = [], [], []
    for _ in range(TOP_K):
        em = jnp.max(work, axis=0, keepdims=True)
        ef = _first_index_of(work == em, eidx, n_experts, 0)
        hit = eidx == ef
        sel_idx.append(ef)
        sel_w.append(jnp.sum(jnp.where(hit, scores, 0.0), axis=0, keepdims=True))
        sel_hit.append(hit)
        hot = jnp.where(hit, 1.0, hot)
        work = jnp.where(hit, neg_inf, work)
    wsel = jnp.concatenate(sel_w, axis=0)
    gate_ref[...] = wsel / jnp.sum(wsel, axis=0, keepdims=True) * ROUTED_SCALE
    idx_ref[...] = jnp.concatenate(sel_idx, axis=0)

    tri = jnp.where(lax.broadcasted_iota(I32, (tm, tm), 0) < lax.broadcasted_iota(I32, (tm, tm), 1),
                    1.0, 0.0).astype(BF16)
    before = carry_ref[:, 0:1] + jnp.dot(hot.astype(BF16), tri, preferred_element_type=F32)
    rank = [jnp.sum(jnp.where(h, before, 0.0), axis=0, keepdims=True) for h in sel_hit]
    rank_ref[...] = jnp.concatenate(rank, axis=0).astype(I32)
    carry_ref[...] = carry_ref[...] + jnp.sum(hot, axis=1, keepdims=True)
    cnt_ref[...] = carry_ref[...]


def _mixer(x, h, attn, seq_lo, seq_hi, p, *, alpha, u_block, gate_block0):
    t, d = x.shape
    tm = MIX_TM
    pw = p["wpb"].shape[0]
    aw = attn.shape[1]
    n_experts = p["rwh"].shape[0]
    halo_per_tile = tm // POOL_HALO
    n_halo_blocks = t // POOL_HALO
    gp_col = gate_block0 * INPROJ_TN // d
    in_specs = [
        pl.BlockSpec((tm, d), lambda i, lo, hi: (i, 0)),
        pl.BlockSpec((POOL_HALO, pw), lambda i, lo, hi: (jnp.maximum(i * halo_per_tile - 1, 0), u_block)),
        pl.BlockSpec((tm, pw), lambda i, lo, hi: (i, u_block)),
        pl.BlockSpec((POOL_HALO, pw),
                     lambda i, lo, hi: (jnp.minimum((i + 1) * halo_per_tile, n_halo_blocks - 1), u_block)),
        pl.BlockSpec((tm, d), lambda i, lo, hi: (i, gp_col)),
        pl.BlockSpec((tm, d), lambda i, lo, hi: (i, gp_col + 1)),
        pl.BlockSpec((tm, aw), lambda i, lo, hi: (i, 0)),
    ]
    weights = [p["pool_w"], p["pool_b"], p["pool_scale"], p["wpb"], p["wab"], p["wo"], p["ln1_g"], p["ln1_b"],
               p["s13"], p["s2"], p["rwh"], p["rwl"], p["rb"]]
    in_specs += [pl.BlockSpec(w.shape, (lambda nd: (lambda i, lo, hi: (0,) * nd))(w.ndim),
                              pipeline_mode=pl.Buffered(1)) for w in weights]
    out_shape = (jax.ShapeDtypeStruct((t, d), F32),
                 jax.ShapeDtypeStruct((t, d), F32),
                 jax.ShapeDtypeStruct((TOP_K, t), I32),
                 jax.ShapeDtypeStruct((TOP_K, t), F32),
                 jax.ShapeDtypeStruct((TOP_K, t), I32),
                 jax.ShapeDtypeStruct((n_experts, LANES), F32))
    out_specs = (pl.BlockSpec((tm, d), lambda i, lo, hi: (i, 0)),
                 pl.BlockSpec((tm, d), lambda i, lo, hi: (i, 0)),
                 pl.BlockSpec((TOP_K, tm), lambda i, lo, hi: (0, i)),
                 pl.BlockSpec((TOP_K, tm), lambda i, lo, hi: (0, i)),
                 pl.BlockSpec((TOP_K, tm), lambda i, lo, hi: (0, i)),
                 pl.BlockSpec((n_experts, LANES), lambda i, lo, hi: (0, 0)))
    grid_spec = pltpu.PrefetchScalarGridSpec(
        num_scalar_prefetch=2, grid=(t // tm,), in_specs=in_specs, out_specs=out_specs,
        scratch_shapes=[pltpu.VMEM((n_experts, LANES), F32)])
    return pl.pallas_call(
        functools.partial(_mixer_body, alpha=alpha, n_experts=n_experts),
        grid_spec=grid_spec, out_shape=out_shape,
        compiler_params=_cparams(("arbitrary",)),
        name="mixer",
    )(seq_lo, seq_hi, x, h, h, h, h, h, attn, *weights)


def _expert_gather(tok_ref, x_hbm, xbuf, sem, slot):
    def copy(r):
        return pltpu.make_async_copy(x_hbm.at[pl.ds(tok_ref[0, 0, r], 1)],
                                     xbuf.at[slot, pl.ds(r, 1)], sem.at[slot])
    return copy


def _experts_body(be_ref, nact_ref, tok_cur_ref, tok_nxt_ref, x_hbm, w1_ref, w3_ref, w2_ref, o_ref,
                  xbuf, w13b, w2b, sem):
    b = pl.program_id(0)
    nact = nact_ref[0]
    bm = xbuf.shape[1]
    ed = w2_ref.shape[1]
    slot = b % 2

    def start_rows(tok_ref, s):
        copy = _expert_gather(tok_ref, x_hbm, xbuf, sem, s)

        def body(r, c):
            copy(r).start()
            return c
        lax.fori_loop(0, bm, body, 0)

    def wait_rows(tok_ref, s):
        copy = _expert_gather(tok_ref, x_hbm, xbuf, sem, s)

        def body(r, c):
            copy(r).wait()
            return c
        lax.fori_loop(0, bm, body, 0)

    @pl.when((b == 0) & (nact > 0))
    def _():
        start_rows(tok_cur_ref, 0)

    @pl.when(b + 1 < nact)
    def _():
        start_rows(tok_nxt_ref, 1 - slot)

    e = be_ref[b]
    e_prev = be_ref[jnp.maximum(b - 1, 0)]

    @pl.when((b == 0) | (e != e_prev))
    def _():
        w13b[:, 0:ed] = w1_ref[0].astype(BF16)
        w13b[:, ed:2 * ed] = w3_ref[0].astype(BF16)
        w2b[...] = w2_ref[0].astype(BF16)

    @pl.when(b < nact)
    def _():
        wait_rows(tok_cur_ref, slot)
        xb = xbuf[slot].astype(BF16)
        hc = jnp.dot(xb, w13b[...], preferred_element_type=F32)
        hdn = jax.nn.silu(hc[:, :ed]) * hc[:, ed:]
        o_ref[...] = jnp.dot(hdn.astype(BF16), w2b[...], preferred_element_type=F32)

    @pl.when(b >= nact)
    def _():
        o_ref[...] = jnp.zeros_like(o_ref)


def _experts(x1, buf_tok, block_e, nact, w1, w3, w2):
    t, d = x1.shape
    n_blk, bm = buf_tok.shape
    ed = w1.shape[2]
    tok3 = buf_tok.reshape(n_blk, 1, bm)
    grid_spec = pltpu.PrefetchScalarGridSpec(
        num_scalar_prefetch=2,
        grid=(n_blk,),
        in_specs=[pl.BlockSpec((1, 1, bm), lambda b, be, na: (b, 0, 0), memory_space=pltpu.SMEM),
                  pl.BlockSpec((1, 1, bm), lambda b, be, na: (jnp.minimum(b + 1, n_blk - 1), 0, 0),
                               memory_space=pltpu.SMEM),
                  pl.BlockSpec(memory_space=pl.ANY),
                  pl.BlockSpec((1, d, ed), lambda b, be, na: (be[b], 0, 0)),
                  pl.BlockSpec((1, d, ed), lambda b, be, na: (be[b], 0, 0)),
                  pl.BlockSpec((1, ed, d), lambda b, be, na: (be[b], 0, 0))],
        out_specs=pl.BlockSpec((bm, d), lambda b, be, na: (b, 0)),
        scratch_shapes=[pltpu.VMEM((2, bm, d), F32),
                        pltpu.VMEM((d, 2 * ed), BF16),
                        pltpu.VMEM((ed, d), BF16),
                        pltpu.SemaphoreType.DMA((2,))],
    )
    return pl.pallas_call(
        _experts_body,
        grid_spec=grid_spec,
        out_shape=jax.ShapeDtypeStruct((n_blk * bm, d), F32),
        compiler_params=_cparams(("arbitrary",)),
        name="experts",
    )(block_e, nact, tok3, tok3, x1, w1, w3, w2)


def _combine_body(dest_ref, base_ref, gate_ref, g2_ref, b2_ref, yb_hbm, o_ref, ybuf, sem):
    tm = base_ref.shape[0]

    def copy(k, r):
        return pltpu.make_async_copy(yb_hbm.at[pl.ds(dest_ref[k, r], 1)],
                                     ybuf.at[k, pl.ds(r, 1)], sem.at[0])

    def start_body(r, c):
        for k in range(TOP_K):
            copy(k, r).start()
        return c

    def wait_body(r, c):
        for k in range(TOP_K):
            copy(k, r).wait()
        return c

    lax.fori_loop(0, tm, start_body, 0)
    lax.fori_loop(0, tm, wait_body, 0)
    acc = base_ref[...]
    for k in range(TOP_K):
        acc = acc + gate_ref[:, k:k + 1] * ybuf[k]
    o_ref[...] = _layer_norm(acc, g2_ref[...], b2_ref[...])


def _combine(base, dest, gate_t, yb, ln_g, ln_b):
    t, d = base.shape
    tm = COMBINE_TM
    return pl.pallas_call(
        _combine_body,
        grid=(t // tm,),
        in_specs=[pl.BlockSpec((TOP_K, tm), lambda i: (0, i), memory_space=pltpu.SMEM),
                  pl.BlockSpec((tm, d), lambda i: (i, 0)),
                  pl.BlockSpec((tm, TOP_K), lambda i: (i, 0)),
                  pl.BlockSpec((1, d), lambda i: (0, 0)),
                  pl.BlockSpec((1, d), lambda i: (0, 0)),
                  pl.BlockSpec(memory_space=pl.ANY)],
        out_specs=pl.BlockSpec((tm, d), lambda i: (i, 0)),
        out_shape=jax.ShapeDtypeStruct((t, d), F32),
        scratch_shapes=[pltpu.VMEM((TOP_K, tm, d), F32), pltpu.SemaphoreType.DMA((1,))],
        compiler_params=_cparams(("arbitrary",)),
        name="combine",
    )(dest, base, gate_t, ln_g, ln_b, yb)


def _split_hi_lo(w):
    hi = w.astype(BF16)
    lo = (w - hi.astype(F32)).astype(BF16)
    return hi, lo


def _encoder_layer(x, seqs, lp, alpha):
    (w_in, pool_w, pool_b, pool_scale, w_pool_br, w_attn_br, attn_rpb, w_out, ln1_g, ln1_b,
     router_w, router_bias, exp_w1, exp_w3, exp_w2, sh_w1, sh_w3, sh_w2, ln2_g, ln2_b) = lp
    t, d = x.shape
    pw = w_pool_br.shape[0]
    aw = w_attn_br.shape[0]
    n_heads = attn_rpb.shape[0]
    head_dim = aw // n_heads
    n_experts = router_w.shape[1]
    assert pw == INPROJ_TN and aw == INPROJ_TN and d % INPROJ_TN == 0
    u_block, q_block = 0, pw // INPROJ_TN
    gate_block0 = (pw + 3 * aw) // INPROJ_TN

    h = _inproj(x, w_in.astype(BF16), q_block=q_block, gate_block0=gate_block0, q_scale=head_dim ** -0.5)

    blk = ATT_ROWS * GRID_W
    images, seq_lo, seq_hi = [], [], []
    for s0, s1 in seqs:
        images.append((s0 // blk, (s1 - s0) // blk))
        for _ in range((s1 - s0) // MIX_TM):
            seq_lo.append(s0)
            seq_hi.append(s1)
    bias_tab = _attn_bias_table(attn_rpb, 2 * LANES // head_dim)
    attn = _attention(h, bias_tab, images, q_block=q_block, n_heads=n_heads, head_dim=head_dim)

    rwh, rwl = _split_hi_lo(router_w.T)
    p = dict(pool_w=pool_w.astype(BF16), pool_b=pool_b.reshape(1, pw), pool_scale=pool_scale.reshape(1, pw),
             wpb=w_pool_br.astype(BF16), wab=w_attn_br.astype(BF16), wo=w_out.astype(BF16),
             ln1_g=ln1_g.reshape(1, d), ln1_b=ln1_b.reshape(1, d),
             s13=jnp.concatenate([sh_w1, sh_w3], axis=1).astype(BF16), s2=sh_w2.astype(BF16),
             rwh=rwh, rwl=rwl, rb=router_bias.reshape(n_experts, 1).astype(F32))
    x1, base, idx_t, gate_t, rank_t, counts = _mixer(
        x, h, attn, jnp.asarray(seq_lo, I32), jnp.asarray(seq_hi, I32), p,
        alpha=alpha, u_block=u_block, gate_block0=gate_block0)

    bm = EXPERT_BM
    cnt = counts[:, 0].astype(I32)
    pcnt = (cnt + bm - 1) // bm * bm
    pend = jnp.cumsum(pcnt)
    pstart = pend - pcnt
    dest = pstart[idx_t] + rank_t
    n_blk = (t * TOP_K + n_experts * (bm - 1)) // bm
    tok = jnp.broadcast_to(jnp.arange(t, dtype=I32)[None, :], (TOP_K, t))
    buf_tok = jnp.zeros((n_blk * bm,), I32).at[dest.reshape(-1)].set(tok.reshape(-1))
    block_e = jnp.clip(jnp.searchsorted(pend, jnp.arange(n_blk, dtype=I32) * bm, side="right"),
                       0, n_experts - 1).astype(I32)
    nact = (pend[-1] // bm).astype(I32).reshape(1)
    block_e = jnp.where(jnp.arange(n_blk) < nact[0], block_e, block_e[jnp.maximum(nact[0] - 1, 0)])
    yb = _experts(x1, buf_tok.reshape(n_blk, bm), block_e, nact, exp_w1, exp_w3, exp_w2)

    return _combine(base, dest, gate_t.T, yb, ln2_g.reshape(1, d), ln2_b.reshape(1, d))


def kernel(x_prompt, x_sample, w_in, pool_w, pool_b, pool_scale, w_pool_br, w_attn_br, attn_rpb, w_out,
           ln1_g, ln1_b, router_w, router_bias, exp_w1, exp_w3, exp_w2, sh_w1, sh_w3, sh_w2, ln2_g, ln2_b):
    depth = w_in.shape[0]
    alpha = (2.0 * depth) ** 0.25
    bp, sp, d = x_prompt.shape
    bs, ss, _ = x_sample.shape
    seqs = [(i * sp, (i + 1) * sp) for i in range(bp)]
    seqs += [(bp * sp + i * ss, bp * sp + (i + 1) * ss) for i in range(bs)]
    x = jnp.concatenate([x_prompt.reshape(bp * sp, d), x_sample.reshape(bs * ss, d)], axis=0)
    for l in range(depth):
        lp = (w_in[l], pool_w[l], pool_b[l], pool_scale[l], w_pool_br[l], w_attn_br[l], attn_rpb[l], w_out[l],
              ln1_g[l], ln1_b[l], router_w[l], router_bias[l], exp_w1[l], exp_w3[l], exp_w2[l],
              sh_w1[l], sh_w3[l], sh_w2[l], ln2_g[l], ln2_b[l])
        x = _encoder_layer(x, seqs, lp, alpha)
    y_prompt = x[:bp * sp].reshape(bp, sp, d)
    y_sample = x[bp * sp:].reshape(bs, ss, d)
    return (y_prompt, y_sample)
```

```python
import functools

import numpy as np
import jax
import jax.numpy as jnp
from jax import lax
from jax.experimental import pallas as pl
from jax.experimental.pallas import tpu as pltpu

F32 = jnp.float32
BF16 = jnp.bfloat16
I32 = jnp.int32

GRID_W = 64
POOL_WINDOWS = (2, 4, 8, 16)
WIN_R = 8
WIN_C = 16
TOP_K = 8
N_GROUPS = 8
TOPK_GROUPS = 4
ROUTED_SCALE = 2.5
LN_EPS = 1e-5
NEG_BIG = -1e30

LANES = 128
VMEM_LIMIT = 56 * 1024 * 1024

INPROJ_TM = 1024
INPROJ_TN = 1024
ATT_ROWS = 8
MIX_TM = 256
POOL_HALO = 16
EXPERT_BM = 256
COMBINE_TM = 128


def _cparams(sem):
    return pltpu.CompilerParams(dimension_semantics=sem, vmem_limit_bytes=VMEM_LIMIT)


def _inproj_body(xa_ref, xs_ref, w_ref, o_ref, xb_ref, *, n_a, q_block, gate_block0, q_scale):
    i = pl.program_id(0)
    j = pl.program_id(1)

    @pl.when((j == 0) & (i < n_a))
    def _():
        xb_ref[...] = xa_ref[...].astype(BF16)

    @pl.when((j == 0) & (i >= n_a))
    def _():
        xb_ref[...] = xs_ref[...].astype(BF16)

    acc = jnp.dot(xb_ref[...], w_ref[...], preferred_element_type=F32)
    is_gate = j >= gate_block0

    @pl.when(is_gate)
    def _():
        o_ref[...] = jax.nn.sigmoid(acc).astype(BF16)

    @pl.when(jnp.logical_not(is_gate))
    def _():
        scale = jnp.where(j == q_block, q_scale, 1.0).astype(F32)
        o_ref[...] = (acc * scale).astype(BF16)


def _two_array_maps(n_a):
    return (lambda i, *_: (jnp.minimum(i, n_a - 1), 0)), (lambda i, *_: (jnp.maximum(i - n_a, 0), 0))


def _inproj(xa, xs, w_bf16, *, q_block, gate_block0, q_scale):
    d = xa.shape[1]
    t = xa.shape[0] + xs.shape[0]
    n = w_bf16.shape[1]
    tm, tn = INPROJ_TM, INPROJ_TN
    assert xa.shape[0] % tm == 0 and xs.shape[0] % tm == 0 and xa.shape[0] > 0 and xs.shape[0] > 0
    n_a = xa.shape[0] // tm
    map_a, map_s = _two_array_maps(n_a)
    return pl.pallas_call(
        functools.partial(_inproj_body, n_a=n_a, q_block=q_block, gate_block0=gate_block0, q_scale=q_scale),
        grid=(t // tm, n // tn),
        in_specs=[pl.BlockSpec((tm, d), map_a),
                  pl.BlockSpec((tm, d), map_s),
                  pl.BlockSpec((d, tn), lambda i, j: (0, j))],
        out_specs=pl.BlockSpec((tm, tn), lambda i, j: (i, j)),
        out_shape=jax.ShapeDtypeStruct((t, n), BF16),
        scratch_shapes=[pltpu.VMEM((tm, d), BF16)],
        compiler_params=_cparams(("arbitrary", "arbitrary")),
        name="inproj",
    )(xa, xs, w_bf16)


def _attn_body(prev_ref, next_ref, kst_ref, off_ref,
               q_ref, kp_ref, kc_ref, kn_ref, vp_ref, vc_ref, vn_ref, bias_ref, o_ref,
               kbuf, vbuf, *, heads_per_group, head_dim):
    del prev_ref, next_ref
    b = pl.program_id(0)
    blk = ATT_ROWS * GRID_W
    kbuf[0:blk] = kp_ref[...]
    kbuf[blk:2 * blk] = kc_ref[...]
    kbuf[2 * blk:3 * blk] = kn_ref[...]
    vbuf[0:blk] = vp_ref[...]
    vbuf[blk:2 * blk] = vc_ref[...]
    vbuf[2 * blk:3 * blk] = vn_ref[...]

    gw = heads_per_group * head_dim
    n_groups = q_ref.shape[1] // gw
    win = WIN_R * GRID_W
    nq = heads_per_group * GRID_W
    row_head = lax.broadcasted_iota(I32, (nq, gw), 0) // GRID_W
    lane_head = lax.broadcasted_iota(I32, (nq, gw), 1) // head_dim
    head_mask = row_head == lane_head
    out_lane_head = lax.broadcasted_iota(I32, (GRID_W, gw), 1) // head_dim

    def row_step(i, carry):
        st = pl.multiple_of(kst_ref[b * ATT_ROWS + i] * GRID_W, GRID_W)
        dr0 = (WIN_R - 1) - off_ref[b * ATT_ROWS + i]
        q0 = pl.multiple_of(i * GRID_W, GRID_W)
        for g in range(n_groups):
            qg = q_ref[pl.ds(q0, GRID_W), g * gw:(g + 1) * gw]
            kg = kbuf[pl.ds(st, win), g * gw:(g + 1) * gw]
            vg = vbuf[pl.ds(st, win), g * gw:(g + 1) * gw]
            wt = jnp.concatenate([qg] * heads_per_group, axis=0)
            wt = jnp.where(head_mask, wt, jnp.zeros_like(wt))
            s = lax.dot_general(wt, kg, (((1,), (1,)), ((), ())),
                                preferred_element_type=F32)
            n_chunk = win // LANES
            sj = [s[:, j * LANES:(j + 1) * LANES] + bias_ref[dr0 + 2 * j, g] for j in range(n_chunk)]
            m = sj[0]
            for j in range(1, n_chunk):
                m = jnp.maximum(m, sj[j])
            m = jnp.max(m, axis=1, keepdims=True)
            ej = [jnp.exp(x - m) for x in sj]
            l = ej[0]
            for j in range(1, n_chunk):
                l = l + ej[j]
            inv = 1.0 / jnp.sum(l, axis=1, keepdims=True)
            p = jnp.concatenate([(x * inv).astype(BF16) for x in ej], axis=1)
            of = jnp.dot(p, vg, preferred_element_type=F32)
            out = jnp.zeros((GRID_W, gw), F32)
            for h in range(heads_per_group):
                out = out + jnp.where(out_lane_head == h, of[h * GRID_W:(h + 1) * GRID_W, :], 0.0)
            o_ref[pl.ds(q0, GRID_W), g * gw:(g + 1) * gw] = out.astype(o_ref.dtype)
        return carry

    lax.fori_loop(0, ATT_ROWS, row_step, 0)


def _attn_tables(images):
    prev, nxt, kst, off = [], [], [], []
    for s, n in images:
        rows = n * ATT_ROWS
        kr = min(WIN_R, rows)
        assert kr == WIN_R
        for bl in range(n):
            b = s + bl
            prev.append(max(b - 1, s))
            nxt.append(min(b + 1, s + n - 1))
            for i in range(ATT_ROWS):
                r = bl * ATT_ROWS + i
                rs = min(max(r - kr // 2, 0), rows - kr)
                kst.append(rs - bl * ATT_ROWS + ATT_ROWS)
                off.append(r - rs)
    return (np.asarray(prev, np.int32), np.asarray(nxt, np.int32),
            np.asarray(kst, np.int32), np.asarray(off, np.int32))


def _attn_bias_table(rpb, heads_per_group):
    n_heads, n_dr, n_dc = rpb.shape
    n_pair = n_dr - 1
    c = np.arange(GRID_W)
    cs = np.clip(c - WIN_C // 2, 0, GRID_W - WIN_C)
    kc = np.arange(GRID_W)
    valid = (kc[None, :] >= cs[:, None]) & (kc[None, :] < cs[:, None] + WIN_C)
    dc = np.clip(kc[None, :] - c[:, None] + (WIN_C - 1), 0, n_dc - 1)
    sel = np.zeros((2, n_dc, GRID_W, 2, GRID_W), np.float32)
    ci, ki = np.nonzero(valid)
    for p in range(2):
        sel[p, dc[ci, ki], ci, p, ki] = 1.0
    sel = jnp.asarray(sel.reshape(2 * n_dc, GRID_W * 2 * GRID_W))
    r = rpb.astype(F32)
    pair = jnp.stack([r[:, :n_pair], r[:, 1:]], axis=2)
    pair = jnp.transpose(pair, (1, 0, 2, 3)).reshape(n_pair * n_heads, 2 * n_dc)
    tab = jnp.dot(pair, sel, precision=lax.Precision.HIGHEST)
    tab = tab.reshape(n_pair, n_heads, GRID_W, 2 * GRID_W)
    valid2 = jnp.asarray(np.tile(valid[:, None, :], (1, 2, 1)).reshape(GRID_W, 2 * GRID_W))
    tab = jnp.where(valid2[None, None], tab, NEG_BIG)
    return tab.reshape(n_pair, n_heads // heads_per_group, heads_per_group * GRID_W, 2 * GRID_W)


def _attention(h, bias_tab, images, *, q_block, n_heads, head_dim):
    t = h.shape[0]
    blk = ATT_ROWS * GRID_W
    aw = n_heads * head_dim
    heads_per_group = 2 * LANES // head_dim
    prev, nxt, kst, off = _attn_tables(images)
    n_blocks = t // blk
    assert prev.shape[0] == n_blocks
    kcol, vcol = q_block + 1, q_block + 2
    spec = lambda col, which: pl.BlockSpec(
        (blk, aw), {"cur": lambda b, p, n, k, o: (b, col),
                    "prev": lambda b, p, n, k, o: (p[b], col),
                    "next": lambda b, p, n, k, o: (n[b], col)}[which])
    grid_spec = pltpu.PrefetchScalarGridSpec(
        num_scalar_prefetch=4,
        grid=(n_blocks,),
        in_specs=[spec(q_block, "cur"),
                  spec(kcol, "prev"), spec(kcol, "cur"), spec(kcol, "next"),
                  spec(vcol, "prev"), spec(vcol, "cur"), spec(vcol, "next"),
                  pl.BlockSpec(bias_tab.shape, lambda b, p, n, k, o: (0, 0, 0, 0),
                               pipeline_mode=pl.Buffered(1))],
        out_specs=pl.BlockSpec((blk, aw), lambda b, p, n, k, o: (b, 0)),
        scratch_shapes=[pltpu.VMEM((3 * blk, aw), BF16), pltpu.VMEM((3 * blk, aw), BF16)],
    )
    return pl.pallas_call(
        functools.partial(_attn_body, heads_per_group=heads_per_group, head_dim=head_dim),
        grid_spec=grid_spec,
        out_shape=jax.ShapeDtypeStruct((t, aw), BF16),
        compiler_params=_cparams(("arbitrary",)),
        name="natten",
    )(jnp.asarray(prev), jnp.asarray(nxt), jnp.asarray(kst), jnp.asarray(off),
      h, h, h, h, h, h, h, bias_tab)


def _layer_norm(r, g, b):
    mu = jnp.mean(r, axis=-1, keepdims=True)
    c = r - mu
    var = jnp.mean(c * c, axis=-1, keepdims=True)
    return c * lax.rsqrt(var + LN_EPS) * g + b


def _first_index_of(mask, idx, sentinel, axis):
    return jnp.min(jnp.where(mask, idx, sentinel), axis=axis, keepdims=True)


def _mixer_body(seq_lo_ref, seq_hi_ref,
                xa_ref, xs_ref, up_ref, uc_ref, un_ref, gp_ref, ga_ref, at_ref,
                pw_ref, pb_ref, ps_ref, wpb_ref, wab_ref, wo_ref, g1_ref, b1_ref,
                s13_ref, s2_ref, rwh_ref, rwl_ref, rb_ref,
                x1_ref, base_ref, idx_ref, gate_ref, rank_ref, cnt_ref,
                carry_ref, *, n_a, alpha, n_experts):
    i = pl.program_id(0)
    tm = xa_ref.shape[0]
    t0 = i * tm
    lo_seq = seq_lo_ref[i]
    hi_seq = seq_hi_ref[i]

    @pl.when(i == 0)
    def _():
        carry_ref[...] = jnp.zeros_like(carry_ref)

    u_ext = jnp.concatenate([up_ref[...], uc_ref[...], un_ref[...]], axis=0)
    ext = tm + 2 * POOL_HALO
    tok_r = t0 + lax.broadcasted_iota(I32, (tm, ext), 0)
    tok_c = t0 - POOL_HALO + lax.broadcasted_iota(I32, (tm, ext), 1)
    tok_q = t0 + lax.broadcasted_iota(I32, (tm, 1), 0)
    n_pg = pw_ref.shape[0]
    pg = pw_ref.shape[1]
    parts = []
    for g in range(n_pg):
        w = POOL_WINDOWS[g]
        lo = jnp.maximum(tok_r - w // 2, lo_seq)
        hi = jnp.minimum(tok_r - w // 2 + w, hi_seq)
        band = jnp.where((tok_c >= lo) & (tok_c < hi), 1.0, 0.0).astype(BF16)
        cnt = (jnp.minimum(tok_q - w // 2 + w, hi_seq) - jnp.maximum(tok_q - w // 2, lo_seq)).astype(F32)
        ug = u_ext[:, g * pg:(g + 1) * pg]
        sums = jnp.dot(band, ug, preferred_element_type=F32)
        z = sums / cnt - uc_ref[:, g * pg:(g + 1) * pg].astype(F32)
        y = jnp.dot(z.astype(BF16), pw_ref[g], preferred_element_type=F32)
        y = (y + pb_ref[:, g * pg:(g + 1) * pg]) * ps_ref[:, g * pg:(g + 1) * pg]
        parts.append(y.astype(BF16))
    a = jnp.concatenate(parts, axis=1)
    a_pool = jnp.dot(a, wpb_ref[...], preferred_element_type=F32)
    a_attn = jnp.dot(at_ref[...], wab_ref[...], preferred_element_type=F32)
    merged = gp_ref[...].astype(F32) * a_pool + ga_ref[...].astype(F32) * a_attn
    mix = jnp.dot(merged.astype(BF16), wo_ref[...], preferred_element_type=F32)
    x_in = jnp.where(i < n_a, xa_ref[...], xs_ref[...])
    x1 = _layer_norm(alpha * x_in + mix, g1_ref[...], b1_ref[...])
    x1_ref[...] = x1
    x1b = x1.astype(BF16)

    sd = s2_ref.shape[0]
    hc = jnp.dot(x1b, s13_ref[...], preferred_element_type=F32)
    hdn = jax.nn.silu(hc[:, :sd]) * hc[:, sd:]
    shared = jnp.dot(hdn.astype(BF16), s2_ref[...], preferred_element_type=F32)
    base_ref[...] = alpha * x1 + shared

    x1l = (x1 - x1b.astype(F32)).astype(BF16)
    nt = (((1,), (1,)), ((), ()))
    logits = (lax.dot_general(rwh_ref[...], x1b, nt, preferred_element_type=F32)
              + lax.dot_general(rwh_ref[...], x1l, nt, preferred_element_type=F32)
              + lax.dot_general(rwl_ref[...], x1b, nt, preferred_element_type=F32))
    scores = jax.nn.sigmoid(logits)
    biased = scores + rb_ref[...]
    gsz = n_experts // N_GROUPS
    neg_inf = -jnp.inf

    mem = lax.broadcasted_iota(I32, (gsz, tm), 0)
    gs_rows = []
    for g in range(N_GROUPS):
        bg = biased[g * gsz:(g + 1) * gsz, :]
        m1 = jnp.max(bg, axis=0, keepdims=True)
        first = _first_index_of(bg == m1, mem, gsz, 0)
        m2 = jnp.max(jnp.where(mem == first, neg_inf, bg), axis=0, keepdims=True)
        gs_rows.append(m1 + m2)
    gwork = jnp.concatenate(gs_rows, axis=0)

    gidx = lax.broadcasted_iota(I32, (N_GROUPS, tm), 0)
    egrp = lax.broadcasted_iota(I32, (n_experts, tm), 0) // gsz
    eidx = lax.broadcasted_iota(I32, (n_experts, tm), 0)
    work = jnp.full((n_experts, tm), neg_inf, F32)
    for _ in range(TOPK_GROUPS):
        gm = jnp.max(gwork, axis=0, keepdims=True)
        gf = _first_index_of(gwork == gm, gidx, N_GROUPS, 0)
        gwork = jnp.where(gidx == gf, neg_inf, gwork)
        work = jnp.where(egrp == gf, biased, work)

    hot = jnp.zeros((n_experts, tm), F32)
    sel_idx, sel_w, sel_hit = [], [], []
    for _ in range(TOP_K):
        em = jnp.max(work, axis=0, keepdims=True)
        ef = _first_index_of(work == em, eidx, n_experts, 0)
        hit = eidx == ef
        sel_idx.append(ef)
        sel_w.append(jnp.sum(jnp.where(hit, scores, 0.0), axis=0, keepdims=True))
        sel_hit.append(hit)
        hot = jnp.where(hit, 1.0, hot)
        work = jnp.where(hit, neg_inf, work)
    wsel = jnp.concatenate(sel_w, axis=0)
    gate_ref[...] = wsel / jnp.sum(wsel, axis=0, keepdims=True) * ROUTED_SCALE
    idx_ref[...] = jnp.concatenate(sel_idx, axis=0)

    tri = jnp.where(lax.broadcasted_iota(I32, (tm, tm), 0) < lax.broadcasted_iota(I32, (tm, tm), 1),
                    1.0, 0.0).astype(BF16)
    before = carry_ref[:, 0:1] + jnp.dot(hot.astype(BF16), tri, preferred_element_type=F32)
    rank = [jnp.sum(jnp.where(h, before, 0.0), axis=0, keepdims=True) for h in sel_hit]
    rank_ref[...] = jnp.concatenate(rank, axis=0).astype(I32)
    carry_ref[...] = carry_ref[...] + jnp.sum(hot, axis=1, keepdims=True)
    cnt_ref[...] = carry_ref[...]


def _mixer(xa, xs, h, attn, seq_lo, seq_hi, p, *, alpha, u_block, gate_block0):
    d = xa.shape[1]
    t = xa.shape[0] + xs.shape[0]
    tm = MIX_TM
    assert xa.shape[0] % tm == 0 and xs.shape[0] % tm == 0
    n_a = xa.shape[0] // tm
    map_a, map_s = _two_array_maps(n_a)
    pw = p["wpb"].shape[0]
    aw = attn.shape[1]
    n_experts = p["rwh"].shape[0]
    halo_per_tile = tm // POOL_HALO
    n_halo_blocks = t // POOL_HALO
    gp_col = gate_block0 * INPROJ_TN // d
    in_specs = [
        pl.BlockSpec((tm, d), map_a),
        pl.BlockSpec((tm, d), map_s),
        pl.BlockSpec((POOL_HALO, pw), lambda i, lo, hi: (jnp.maximum(i * halo_per_tile - 1, 0), u_block)),
        pl.BlockSpec((tm, pw), lambda i, lo, hi: (i, u_block)),
        pl.BlockSpec((POOL_HALO, pw),
                     lambda i, lo, hi: (jnp.minimum((i + 1) * halo_per_tile, n_halo_blocks - 1), u_block)),
        pl.BlockSpec((tm, d), lambda i, lo, hi: (i, gp_col)),
        pl.BlockSpec((tm, d), lambda i, lo, hi: (i, gp_col + 1)),
        pl.BlockSpec((tm, aw), lambda i, lo, hi: (i, 0)),
    ]
    weights = [p["pool_w"], p["pool_b"], p["pool_scale"], p["wpb"], p["wab"], p["wo"], p["ln1_g"], p["ln1_b"],
               p["s13"], p["s2"], p["rwh"], p["rwl"], p["rb"]]
    in_specs += [pl.BlockSpec(w.shape, (lambda nd: (lambda i, lo, hi: (0,) * nd))(w.ndim),
                              pipeline_mode=pl.Buffered(1)) for w in weights]
    out_shape = (jax.ShapeDtypeStruct((t, d), F32),
                 jax.ShapeDtypeStruct((t, d), F32),
                 jax.ShapeDtypeStruct((TOP_K, t), I32),
                 jax.ShapeDtypeStruct((TOP_K, t), F32),
                 jax.ShapeDtypeStruct((TOP_K, t), I32),
                 jax.ShapeDtypeStruct((n_experts, LANES), F32))
    out_specs = (pl.BlockSpec((tm, d), lambda i, lo, hi: (i, 0)),
                 pl.BlockSpec((tm, d), lambda i, lo, hi: (i, 0)),
                 pl.BlockSpec((TOP_K, tm), lambda i, lo, hi: (0, i)),
                 pl.BlockSpec((TOP_K, tm), lambda i, lo, hi: (0, i)),
                 pl.BlockSpec((TOP_K, tm), lambda i, lo, hi: (0, i)),
                 pl.BlockSpec((n_experts, LANES), lambda i, lo, hi: (0, 0)))
    grid_spec = pltpu.PrefetchScalarGridSpec(
        num_scalar_prefetch=2, grid=(t // tm,), in_specs=in_specs, out_specs=out_specs,
        scratch_shapes=[pltpu.VMEM((n_experts, LANES), F32)])
    return pl.pallas_call(
        functools.partial(_mixer_body, n_a=n_a, alpha=alpha, n_experts=n_experts),
        grid_spec=grid_spec, out_shape=out_shape,
        compiler_params=_cparams(("arbitrary",)),
        name="mixer",
    )(seq_lo, seq_hi, xa, xs, h, h, h, h, h, attn, *weights)


GATHER_UNROLL = 8


def _experts_body(be_ref, nact_ref, tok_cur_ref, tok_nxt_ref, x_hbm, w1_ref, w3_ref, w2_ref, o_ref,
                  xbuf, w13b, w2b, sem):
    b = pl.program_id(0)
    nact = nact_ref[0]
    bm = xbuf.shape[1]
    ed = w2_ref.shape[1]
    slot = b % 2

    def start_rows(tok_ref, s):
        def body(r0, c):
            for u in range(GATHER_UNROLL):
                r = r0 * GATHER_UNROLL + u
                pltpu.make_async_copy(x_hbm.at[pl.ds(tok_ref[0, 0, r], 1)],
                                      xbuf.at[s, pl.ds(r, 1)], sem.at[s]).start()
            return c
        lax.fori_loop(0, bm // GATHER_UNROLL, body, 0)

    def wait_rows(s):
        pltpu.make_async_copy(x_hbm.at[pl.ds(0, bm)], xbuf.at[s], sem.at[s]).wait()

    @pl.when((b == 0) & (nact > 0))
    def _():
        start_rows(tok_cur_ref, 0)

    @pl.when(b + 1 < nact)
    def _():
        start_rows(tok_nxt_ref, 1 - slot)

    e = be_ref[b]
    e_prev = be_ref[jnp.maximum(b - 1, 0)]

    @pl.when((b == 0) | (e != e_prev))
    def _():
        w13b[:, 0:ed] = w1_ref[0].astype(BF16)
        w13b[:, ed:2 * ed] = w3_ref[0].astype(BF16)
        w2b[...] = w2_ref[0].astype(BF16)

    @pl.when(b < nact)
    def _():
        wait_rows(slot)
        xb = xbuf[slot].astype(BF16)
        hc = jnp.dot(xb, w13b[...], preferred_element_type=F32)
        hdn = jax.nn.silu(hc[:, :ed]) * hc[:, ed:]
        o_ref[...] = jnp.dot(hdn.astype(BF16), w2b[...], preferred_element_type=F32)

    @pl.when(b >= nact)
    def _():
        o_ref[...] = jnp.zeros_like(o_ref)


def _experts(x1, buf_tok, block_e, nact, w1, w3, w2):
    t, d = x1.shape
    n_blk, bm = buf_tok.shape
    ed = w1.shape[2]
    tok3 = buf_tok.reshape(n_blk, 1, bm)
    grid_spec = pltpu.PrefetchScalarGridSpec(
        num_scalar_prefetch=2,
        grid=(n_blk,),
        in_specs=[pl.BlockSpec((1, 1, bm), lambda b, be, na: (b, 0, 0), memory_space=pltpu.SMEM),
                  pl.BlockSpec((1, 1, bm), lambda b, be, na: (jnp.minimum(b + 1, n_blk - 1), 0, 0),
                               memory_space=pltpu.SMEM),
                  pl.BlockSpec(memory_space=pl.ANY),
                  pl.BlockSpec((1, d, ed), lambda b, be, na: (be[b], 0, 0)),
                  pl.BlockSpec((1, d, ed), lambda b, be, na: (be[b], 0, 0)),
                  pl.BlockSpec((1, ed, d), lambda b, be, na: (be[b], 0, 0))],
        out_specs=pl.BlockSpec((bm, d), lambda b, be, na: (b, 0)),
        scratch_shapes=[pltpu.VMEM((2, bm, d), F32),
                        pltpu.VMEM((d, 2 * ed), BF16),
                        pltpu.VMEM((ed, d), BF16),
                        pltpu.SemaphoreType.DMA((2,))],
    )
    return pl.pallas_call(
        _experts_body,
        grid_spec=grid_spec,
        out_shape=jax.ShapeDtypeStruct((n_blk * bm, d), F32),
        compiler_params=_cparams(("arbitrary",)),
        name="experts",
    )(block_e, nact, tok3, tok3, x1, w1, w3, w2)


def _combine_body(dest_ref, base_ref, gate_ref, g2_ref, b2_ref, yb_hbm, oa_ref, os_ref, ybuf, sem, *, n_a):
    i = pl.program_id(0)
    tm = base_ref.shape[0]

    def start_body(r0, c):
        for u in range(2):
            r = r0 * 2 + u
            for k in range(TOP_K):
                pltpu.make_async_copy(yb_hbm.at[pl.ds(dest_ref[k, r], 1)],
                                      ybuf.at[k, pl.ds(r, 1)], sem.at[0]).start()
        return c

    lax.fori_loop(0, tm // 2, start_body, 0)
    for k in range(TOP_K):
        pltpu.make_async_copy(yb_hbm.at[pl.ds(0, tm)], ybuf.at[k], sem.at[0]).wait()
    acc = base_ref[...]
    for k in range(TOP_K):
        acc = acc + gate_ref[:, k:k + 1] * ybuf[k]
    y = _layer_norm(acc, g2_ref[...], b2_ref[...])

    @pl.when(i < n_a)
    def _():
        oa_ref[...] = y

    @pl.when(i >= n_a)
    def _():
        os_ref[...] = y


def _combine(base, dest, gate_t, yb, ln_g, ln_b, t_a):
    t, d = base.shape
    tm = COMBINE_TM
    assert t_a % tm == 0 and 0 < t_a < t
    n_a = t_a // tm
    map_a, map_s = _two_array_maps(n_a)
    return pl.pallas_call(
        functools.partial(_combine_body, n_a=n_a),
        grid=(t // tm,),
        in_specs=[pl.BlockSpec((TOP_K, tm), lambda i: (0, i), memory_space=pltpu.SMEM),
                  pl.BlockSpec((tm, d), lambda i: (i, 0)),
                  pl.BlockSpec((tm, TOP_K), lambda i: (i, 0)),
                  pl.BlockSpec((1, d), lambda i: (0, 0)),
                  pl.BlockSpec((1, d), lambda i: (0, 0)),
                  pl.BlockSpec(memory_space=pl.ANY)],
        out_specs=(pl.BlockSpec((tm, d), map_a), pl.BlockSpec((tm, d), map_s)),
        out_shape=(jax.ShapeDtypeStruct((t_a, d), F32), jax.ShapeDtypeStruct((t - t_a, d), F32)),
        scratch_shapes=[pltpu.VMEM((TOP_K, tm, d), F32), pltpu.SemaphoreType.DMA((1,))],
        compiler_params=_cparams(("arbitrary",)),
        name="combine",
    )(dest, base, gate_t, ln_g, ln_b, yb)


def _split_hi_lo(w):
    hi = w.astype(BF16)
    lo = (w - hi.astype(F32)).astype(BF16)
    return hi, lo


def _encoder_layer(xa, xs, seqs, lp, alpha):
    (w_in, pool_w, pool_b, pool_scale, w_pool_br, w_attn_br, attn_rpb, w_out, ln1_g, ln1_b,
     router_w, router_bias, exp_w1, exp_w3, exp_w2, sh_w1, sh_w3, sh_w2, ln2_g, ln2_b) = lp
    d = xa.shape[1]
    t = xa.shape[0] + xs.shape[0]
    pw = w_pool_br.shape[0]
    aw = w_attn_br.shape[0]
    n_heads = attn_rpb.shape[0]
    head_dim = aw // n_heads
    n_experts = router_w.shape[1]
    assert pw == INPROJ_TN and aw == INPROJ_TN and d % INPROJ_TN == 0
    u_block, q_block = 0, pw // INPROJ_TN
    gate_block0 = (pw + 3 * aw) // INPROJ_TN

    h = _inproj(xa, xs, w_in.astype(BF16), q_block=q_block, gate_block0=gate_block0, q_scale=head_dim ** -0.5)

    blk = ATT_ROWS * GRID_W
    images, seq_lo, seq_hi = [], [], []
    for s0, s1 in seqs:
        images.append((s0 // blk, (s1 - s0) // blk))
        for _ in range((s1 - s0) // MIX_TM):
            seq_lo.append(s0)
            seq_hi.append(s1)
    bias_tab = _attn_bias_table(attn_rpb, 2 * LANES // head_dim)
    attn = _attention(h, bias_tab, images, q_block=q_block, n_heads=n_heads, head_dim=head_dim)

    rwh, rwl = _split_hi_lo(router_w.T)
    p = dict(pool_w=pool_w.astype(BF16), pool_b=pool_b.reshape(1, pw), pool_scale=pool_scale.reshape(1, pw),
             wpb=w_pool_br.astype(BF16), wab=w_attn_br.astype(BF16), wo=w_out.astype(BF16),
             ln1_g=ln1_g.reshape(1, d), ln1_b=ln1_b.reshape(1, d),
             s13=jnp.concatenate([sh_w1, sh_w3], axis=1).astype(BF16), s2=sh_w2.astype(BF16),
             rwh=rwh, rwl=rwl, rb=router_bias.reshape(n_experts, 1).astype(F32))
    x1, base, idx_t, gate_t, rank_t, counts = _mixer(
        xa, xs, h, attn, jnp.asarray(seq_lo, I32), jnp.asarray(seq_hi, I32), p,
        alpha=alpha, u_block=u_block, gate_block0=gate_block0)

    bm = EXPERT_BM
    cnt = counts[:, 0].astype(I32)
    pcnt = (cnt + bm - 1) // bm * bm
    pend = jnp.cumsum(pcnt)
    pstart = pend - pcnt
    eids = jnp.arange(n_experts, dtype=I32)[:, None, None]
    dest = jnp.sum(jnp.where(idx_t[None] == eids, pstart[:, None, None], 0), axis=0) + rank_t
    n_blk = (t * TOP_K + n_experts * (bm - 1)) // bm
    tok = jnp.broadcast_to(jnp.arange(t, dtype=I32)[None, :], (TOP_K, t))
    buf_tok = jnp.zeros((n_blk * bm,), I32).at[dest.reshape(-1)].set(tok.reshape(-1))
    blk_row = jnp.arange(n_blk, dtype=I32) * bm
    block_e = jnp.minimum(jnp.sum((pend[None, :] <= blk_row[:, None]).astype(I32), axis=1), n_experts - 1)
    nact = (pend[-1] // bm).astype(I32).reshape(1)
    block_e = jnp.where(jnp.arange(n_blk) < nact[0], block_e, block_e[jnp.maximum(nact[0] - 1, 0)])
    yb = _experts(x1, buf_tok.reshape(n_blk, bm), block_e, nact, exp_w1, exp_w3, exp_w2)

    return _combine(base, dest, gate_t.T, yb, ln2_g.reshape(1, d), ln2_b.reshape(1, d), xa.shape[0])


def kernel(x_prompt, x_sample, w_in, pool_w, pool_b, pool_scale, w_pool_br, w_attn_br, attn_rpb, w_out,
           ln1_g, ln1_b, router_w, router_bias, exp_w1, exp_w3, exp_w2, sh_w1, sh_w3, sh_w2, ln2_g, ln2_b):
    depth = w_in.shape[0]
    alpha = (2.0 * depth) ** 0.25
    bp, sp, d = x_prompt.shape
    bs, ss, _ = x_sample.shape
    seqs = [(i * sp, (i + 1) * sp) for i in range(bp)]
    seqs += [(bp * sp + i * ss, bp * sp + (i + 1) * ss) for i in range(bs)]
    xa, xs = x_prompt.reshape(bp * sp, d), x_sample.reshape(bs * ss, d)
    for l in range(depth):
        lp = (w_in[l], pool_w[l], pool_b[l], pool_scale[l], w_pool_br[l], w_attn_br[l], attn_rpb[l], w_out[l],
              ln1_g[l], ln1_b[l], router_w[l], router_bias[l], exp_w1[l], exp_w3[l], exp_w2[l],
              sh_w1[l], sh_w3[l], sh_w2[l], ln2_g[l], ln2_b[l])
        xa, xs = _encoder_layer(xa, xs, seqs, lp, alpha)
    return (xa.reshape(bp, sp, d), xs.reshape(bs, ss, d))
```

```python
import functools

import numpy as np
import jax
import jax.numpy as jnp
from jax import lax
from jax.experimental import pallas as pl
from jax.experimental.pallas import tpu as pltpu

F32 = jnp.float32
BF16 = jnp.bfloat16
I32 = jnp.int32

GRID_W = 64
POOL_WINDOWS = (2, 4, 8, 16)
WIN_R = 8
WIN_C = 16
TOP_K = 8
N_GROUPS = 8
TOPK_GROUPS = 4
ROUTED_SCALE = 2.5
LN_EPS = 1e-5
NEG_BIG = -1e30

LANES = 128
VMEM_LIMIT = 56 * 1024 * 1024

INPROJ_TM = 1024
INPROJ_TN = 1024
ATT_ROWS = 8
MIX_TM = 256
POOL_HALO = 16
EXPERT_BM = 256
COMBINE_TM = 128


def _cparams(sem):
    return pltpu.CompilerParams(dimension_semantics=sem, vmem_limit_bytes=VMEM_LIMIT)


def _inproj_body(xa_ref, xs_ref, w_ref, o_ref, xb_ref, *, n_a, q_block, gate_block0, q_scale):
    i = pl.program_id(0)
    j = pl.program_id(1)

    @pl.when((j == 0) & (i < n_a))
    def _():
        xb_ref[...] = xa_ref[...].astype(BF16)

    @pl.when((j == 0) & (i >= n_a))
    def _():
        xb_ref[...] = xs_ref[...].astype(BF16)

    acc = jnp.dot(xb_ref[...], w_ref[...], preferred_element_type=F32)
    is_gate = j >= gate_block0

    @pl.when(is_gate)
    def _():
        o_ref[...] = jax.nn.sigmoid(acc).astype(BF16)

    @pl.when(jnp.logical_not(is_gate))
    def _():
        scale = jnp.where(j == q_block, q_scale, 1.0).astype(F32)
        o_ref[...] = (acc * scale).astype(BF16)


def _two_array_maps(n_a):
    return (lambda i, *_: (jnp.minimum(i, n_a - 1), 0)), (lambda i, *_: (jnp.maximum(i - n_a, 0), 0))


def _inproj(xa, xs, w_bf16, *, q_block, gate_block0, q_scale):
    d = xa.shape[1]
    t = xa.shape[0] + xs.shape[0]
    n = w_bf16.shape[1]
    tm, tn = INPROJ_TM, INPROJ_TN
    assert xa.shape[0] % tm == 0 and xs.shape[0] % tm == 0 and xa.shape[0] > 0 and xs.shape[0] > 0
    n_a = xa.shape[0] // tm
    map_a, map_s = _two_array_maps(n_a)
    return pl.pallas_call(
        functools.partial(_inproj_body, n_a=n_a, q_block=q_block, gate_block0=gate_block0, q_scale=q_scale),
        grid=(t // tm, n // tn),
        in_specs=[pl.BlockSpec((tm, d), map_a),
                  pl.BlockSpec((tm, d), map_s),
                  pl.BlockSpec((d, tn), lambda i, j: (0, j))],
        out_specs=pl.BlockSpec((tm, tn), lambda i, j: (i, j)),
        out_shape=jax.ShapeDtypeStruct((t, n), BF16),
        scratch_shapes=[pltpu.VMEM((tm, d), BF16)],
        compiler_params=_cparams(("arbitrary", "arbitrary")),
        name="inproj",
    )(xa, xs, w_bf16)


def _attn_body(prev_ref, next_ref, kst_ref, off_ref,
               q_ref, kp_ref, kc_ref, kn_ref, vp_ref, vc_ref, vn_ref, bias_ref, o_ref,
               kbuf, vbuf, *, heads_per_group, head_dim):
    del prev_ref, next_ref
    b = pl.program_id(0)
    blk = ATT_ROWS * GRID_W
    kbuf[0:blk] = kp_ref[...]
    kbuf[blk:2 * blk] = kc_ref[...]
    kbuf[2 * blk:3 * blk] = kn_ref[...]
    vbuf[0:blk] = vp_ref[...]
    vbuf[blk:2 * blk] = vc_ref[...]
    vbuf[2 * blk:3 * blk] = vn_ref[...]

    gw = heads_per_group * head_dim
    n_groups = q_ref.shape[1] // gw
    win = WIN_R * GRID_W
    nq = heads_per_group * GRID_W
    row_head = lax.broadcasted_iota(I32, (nq, gw), 0) // GRID_W
    lane_head = lax.broadcasted_iota(I32, (nq, gw), 1) // head_dim
    head_mask = row_head == lane_head
    out_lane_head = lax.broadcasted_iota(I32, (GRID_W, gw), 1) // head_dim

    def row_step(i, carry):
        st = pl.multiple_of(kst_ref[b * ATT_ROWS + i] * GRID_W, GRID_W)
        dr0 = (WIN_R - 1) - off_ref[b * ATT_ROWS + i]
        q0 = pl.multiple_of(i * GRID_W, GRID_W)
        for g in range(n_groups):
            qg = q_ref[pl.ds(q0, GRID_W), g * gw:(g + 1) * gw]
            kg = kbuf[pl.ds(st, win), g * gw:(g + 1) * gw]
            vg = vbuf[pl.ds(st, win), g * gw:(g + 1) * gw]
            wt = jnp.concatenate([qg] * heads_per_group, axis=0)
            wt = jnp.where(head_mask, wt, jnp.zeros_like(wt))
            s = lax.dot_general(wt, kg, (((1,), (1,)), ((), ())),
                                preferred_element_type=F32)
            n_chunk = win // LANES
            sj = [s[:, j * LANES:(j + 1) * LANES] + bias_ref[dr0 + 2 * j, g] for j in range(n_chunk)]
            m = sj[0]
            for j in range(1, n_chunk):
                m = jnp.maximum(m, sj[j])
            m = jnp.max(m, axis=1, keepdims=True)
            ej = [jnp.exp(x - m) for x in sj]
            l = ej[0]
            for j in range(1, n_chunk):
                l = l + ej[j]
            inv = 1.0 / jnp.sum(l, axis=1, keepdims=True)
            p = jnp.concatenate([(x * inv).astype(BF16) for x in ej], axis=1)
            of = jnp.dot(p, vg, preferred_element_type=F32)
            out = jnp.zeros((GRID_W, gw), F32)
            for h in range(heads_per_group):
                out = out + jnp.where(out_lane_head == h, of[h * GRID_W:(h + 1) * GRID_W, :], 0.0)
            o_ref[pl.ds(q0, GRID_W), g * gw:(g + 1) * gw] = out.astype(o_ref.dtype)
        return carry

    lax.fori_loop(0, ATT_ROWS, row_step, 0)


def _attn_tables(images):
    prev, nxt, kst, off = [], [], [], []
    for s, n in images:
        rows = n * ATT_ROWS
        kr = min(WIN_R, rows)
        assert kr == WIN_R
        for bl in range(n):
            b = s + bl
            prev.append(max(b - 1, s))
            nxt.append(min(b + 1, s + n - 1))
            for i in range(ATT_ROWS):
                r = bl * ATT_ROWS + i
                rs = min(max(r - kr // 2, 0), rows - kr)
                kst.append(rs - bl * ATT_ROWS + ATT_ROWS)
                off.append(r - rs)
    return (np.asarray(prev, np.int32), np.asarray(nxt, np.int32),
            np.asarray(kst, np.int32), np.asarray(off, np.int32))


def _attn_bias_table(rpb, heads_per_group):
    n_heads, n_dr, n_dc = rpb.shape
    n_pair = n_dr - 1
    c = np.arange(GRID_W)
    cs = np.clip(c - WIN_C // 2, 0, GRID_W - WIN_C)
    kc = np.arange(GRID_W)
    valid = (kc[None, :] >= cs[:, None]) & (kc[None, :] < cs[:, None] + WIN_C)
    dc = np.clip(kc[None, :] - c[:, None] + (WIN_C - 1), 0, n_dc - 1)
    sel = np.zeros((2, n_dc, GRID_W, 2, GRID_W), np.float32)
    ci, ki = np.nonzero(valid)
    for p in range(2):
        sel[p, dc[ci, ki], ci, p, ki] = 1.0
    sel = jnp.asarray(sel.reshape(2 * n_dc, GRID_W * 2 * GRID_W))
    r = rpb.astype(F32)
    pair = jnp.stack([r[:, :n_pair], r[:, 1:]], axis=2)
    pair = jnp.transpose(pair, (1, 0, 2, 3)).reshape(n_pair * n_heads, 2 * n_dc)
    tab = jnp.dot(pair, sel, precision=lax.Precision.HIGHEST)
    tab = tab.reshape(n_pair, n_heads, GRID_W, 2 * GRID_W)
    valid2 = jnp.asarray(np.tile(valid[:, None, :], (1, 2, 1)).reshape(GRID_W, 2 * GRID_W))
    tab = jnp.where(valid2[None, None], tab, NEG_BIG)
    return tab.reshape(n_pair, n_heads // heads_per_group, heads_per_group * GRID_W, 2 * GRID_W)


def _attention(h, bias_tab, images, *, q_block, n_heads, head_dim):
    t = h.shape[0]
    blk = ATT_ROWS * GRID_W
    aw = n_heads * head_dim
    heads_per_group = 2 * LANES // head_dim
    prev, nxt, kst, off = _attn_tables(images)
    n_blocks = t // blk
    assert prev.shape[0] == n_blocks
    kcol, vcol = q_block + 1, q_block + 2
    spec = lambda col, which: pl.BlockSpec(
        (blk, aw), {"cur": lambda b, p, n, k, o: (b, col),
                    "prev": lambda b, p, n, k, o: (p[b], col),
                    "next": lambda b, p, n, k, o: (n[b], col)}[which])
    grid_spec = pltpu.PrefetchScalarGridSpec(
        num_scalar_prefetch=4,
        grid=(n_blocks,),
        in_specs=[spec(q_block, "cur"),
                  spec(kcol, "prev"), spec(kcol, "cur"), spec(kcol, "next"),
                  spec(vcol, "prev"), spec(vcol, "cur"), spec(vcol, "next"),
                  pl.BlockSpec(bias_tab.shape, lambda b, p, n, k, o: (0, 0, 0, 0),
                               pipeline_mode=pl.Buffered(1))],
        out_specs=pl.BlockSpec((blk, aw), lambda b, p, n, k, o: (b, 0)),
        scratch_shapes=[pltpu.VMEM((3 * blk, aw), BF16), pltpu.VMEM((3 * blk, aw), BF16)],
    )
    return pl.pallas_call(
        functools.partial(_attn_body, heads_per_group=heads_per_group, head_dim=head_dim),
        grid_spec=grid_spec,
        out_shape=jax.ShapeDtypeStruct((t, aw), BF16),
        compiler_params=_cparams(("arbitrary",)),
        name="natten",
    )(jnp.asarray(prev), jnp.asarray(nxt), jnp.asarray(kst), jnp.asarray(off),
      h, h, h, h, h, h, h, bias_tab)


def _layer_norm(r, g, b):
    mu = jnp.mean(r, axis=-1, keepdims=True)
    c = r - mu
    var = jnp.mean(c * c, axis=-1, keepdims=True)
    return c * lax.rsqrt(var + LN_EPS) * g + b


def _first_index_of(mask, idx, sentinel, axis):
    return jnp.min(jnp.where(mask, idx, sentinel), axis=axis, keepdims=True)


def _mixer_body(seq_lo_ref, seq_hi_ref,
                xa_ref, xs_ref, up_ref, uc_ref, un_ref, gp_ref, ga_ref, at_ref,
                pw_ref, pb_ref, ps_ref, wpb_ref, wab_ref, wo_ref, g1_ref, b1_ref,
                s13_ref, s2_ref, rwh_ref, rwl_ref, rb_ref,
                x1_ref, base_ref, idx_ref, gate_ref, rank_ref, cnt_ref,
                carry_ref, *, n_a, alpha, n_experts):
    i = pl.program_id(0)
    tm = xa_ref.shape[0]
    t0 = i * tm
    lo_seq = seq_lo_ref[i]
    hi_seq = seq_hi_ref[i]

    @pl.when(i == 0)
    def _():
        carry_ref[...] = jnp.zeros_like(carry_ref)

    u_ext = jnp.concatenate([up_ref[...], uc_ref[...], un_ref[...]], axis=0)
    ext = tm + 2 * POOL_HALO
    tok_r = t0 + lax.broadcasted_iota(I32, (tm, ext), 0)
    tok_c = t0 - POOL_HALO + lax.broadcasted_iota(I32, (tm, ext), 1)
    tok_q = t0 + lax.broadcasted_iota(I32, (tm, 1), 0)
    n_pg = pw_ref.shape[0]
    pg = pw_ref.shape[1]
    parts = []
    for g in range(n_pg):
        w = POOL_WINDOWS[g]
        lo = jnp.maximum(tok_r - w // 2, lo_seq)
        hi = jnp.minimum(tok_r - w // 2 + w, hi_seq)
        band = jnp.where((tok_c >= lo) & (tok_c < hi), 1.0, 0.0).astype(BF16)
        cnt = (jnp.minimum(tok_q - w // 2 + w, hi_seq) - jnp.maximum(tok_q - w // 2, lo_seq)).astype(F32)
        ug = u_ext[:, g * pg:(g + 1) * pg]
        sums = jnp.dot(band, ug, preferred_element_type=F32)
        z = sums / cnt - uc_ref[:, g * pg:(g + 1) * pg].astype(F32)
        y = jnp.dot(z.astype(BF16), pw_ref[g], preferred_element_type=F32)
        y = (y + pb_ref[:, g * pg:(g + 1) * pg]) * ps_ref[:, g * pg:(g + 1) * pg]
        parts.append(y.astype(BF16))
    a = jnp.concatenate(parts, axis=1)
    a_pool = jnp.dot(a, wpb_ref[...], preferred_element_type=F32)
    a_attn = jnp.dot(at_ref[...], wab_ref[...], preferred_element_type=F32)
    merged = gp_ref[...].astype(F32) * a_pool + ga_ref[...].astype(F32) * a_attn
    mix = jnp.dot(merged.astype(BF16), wo_ref[...], preferred_element_type=F32)
    x_in = jnp.where(i < n_a, xa_ref[...], xs_ref[...])
    x1 = _layer_norm(alpha * x_in + mix, g1_ref[...], b1_ref[...])
    x1_ref[...] = x1
    x1b = x1.astype(BF16)

    sd = s2_ref.shape[0]
    hc = jnp.dot(x1b, s13_ref[...], preferred_element_type=F32)
    hdn = jax.nn.silu(hc[:, :sd]) * hc[:, sd:]
    shared = jnp.dot(hdn.astype(BF16), s2_ref[...], preferred_element_type=F32)
    base_ref[...] = alpha * x1 + shared

    x1l = (x1 - x1b.astype(F32)).astype(BF16)
    nt = (((1,), (1,)), ((), ()))
    logits = (lax.dot_general(rwh_ref[...], x1b, nt, preferred_element_type=F32)
              + lax.dot_general(rwh_ref[...], x1l, nt, preferred_element_type=F32)
              + lax.dot_general(rwl_ref[...], x1b, nt, preferred_element_type=F32))
    scores = jax.nn.sigmoid(logits)
    biased = scores + rb_ref[...]
    gsz = n_experts // N_GROUPS
    neg_inf = -jnp.inf

    mem = lax.broadcasted_iota(I32, (gsz, tm), 0)
    gs_rows = []
    for g in range(N_GROUPS):
        bg = biased[g * gsz:(g + 1) * gsz, :]
        m1 = jnp.max(bg, axis=0, keepdims=True)
        first = _first_index_of(bg == m1, mem, gsz, 0)
        m2 = jnp.max(jnp.where(mem == first, neg_inf, bg), axis=0, keepdims=True)
        gs_rows.append(m1 + m2)
    gwork = jnp.concatenate(gs_rows, axis=0)

    gidx = lax.broadcasted_iota(I32, (N_GROUPS, tm), 0)
    egrp = lax.broadcasted_iota(I32, (n_experts, tm), 0) // gsz
    eidx = lax.broadcasted_iota(I32, (n_experts, tm), 0)
    work = jnp.full((n_experts, tm), neg_inf, F32)
    for _ in range(TOPK_GROUPS):
        gm = jnp.max(gwork, axis=0, keepdims=True)
        gf = _first_index_of(gwork == gm, gidx, N_GROUPS, 0)
        gwork = jnp.where(gidx == gf, neg_inf, gwork)
        work = jnp.where(egrp == gf, biased, work)

    hot = jnp.zeros((n_experts, tm), F32)
    sel_idx, sel_w, sel_hit = [], [], []
    for _ in range(TOP_K):
        em = jnp.max(work, axis=0, keepdims=True)
        ef = _first_index_of(work == em, eidx, n_experts, 0)
        hit = eidx == ef
        sel_idx.append(ef)
        sel_w.append(jnp.sum(jnp.where(hit, scores, 0.0), axis=0, keepdims=True))
        sel_hit.append(hit)
        hot = jnp.where(hit, 1.0, hot)
        work = jnp.where(hit, neg_inf, work)
    wsel = jnp.concatenate(sel_w, axis=0)
    gate_ref[...] = wsel / jnp.sum(wsel, axis=0, keepdims=True) * ROUTED_SCALE
    idx_ref[...] = jnp.concatenate(sel_idx, axis=0)

    tri = jnp.where(lax.broadcasted_iota(I32, (tm, tm), 0) < lax.broadcasted_iota(I32, (tm, tm), 1),
                    1.0, 0.0).astype(BF16)
    before = carry_ref[:, 0:1] + jnp.dot(hot.astype(BF16), tri, preferred_element_type=F32)
    rank = [jnp.sum(jnp.where(h, before, 0.0), axis=0, keepdims=True) for h in sel_hit]
    rank_ref[...] = jnp.concatenate(rank, axis=0).astype(I32)
    carry_ref[...] = carry_ref[...] + jnp.sum(hot, axis=1, keepdims=True)
    cnt_ref[...] = carry_ref[...]


def _mixer(xa, xs, h, attn, seq_lo, seq_hi, p, *, alpha, u_block, gate_block0):
    d = xa.shape[1]
    t = xa.shape[0] + xs.shape[0]
    tm = MIX_TM
    assert xa.shape[0] % tm == 0 and xs.shape[0] % tm == 0
    n_a = xa.shape[0] // tm
    map_a, map_s = _two_array_maps(n_a)
    pw = p["wpb"].shape[0]
    aw = attn.shape[1]
    n_experts = p["rwh"].shape[0]
    halo_per_tile = tm // POOL_HALO
    n_halo_blocks = t // POOL_HALO
    gp_col = gate_block0 * INPROJ_TN // d
    in_specs = [
        pl.BlockSpec((tm, d), map_a),
        pl.BlockSpec((tm, d), map_s),
        pl.BlockSpec((POOL_HALO, pw), lambda i, lo, hi: (jnp.maximum(i * halo_per_tile - 1, 0), u_block)),
        pl.BlockSpec((tm, pw), lambda i, lo, hi: (i, u_block)),
        pl.BlockSpec((POOL_HALO, pw),
                     lambda i, lo, hi: (jnp.minimum((i + 1) * halo_per_tile, n_halo_blocks - 1), u_block)),
        pl.BlockSpec((tm, d), lambda i, lo, hi: (i, gp_col)),
        pl.BlockSpec((tm, d), lambda i, lo, hi: (i, gp_col + 1)),
        pl.BlockSpec((tm, aw), lambda i, lo, hi: (i, 0)),
    ]
    weights = [p["pool_w"], p["pool_b"], p["pool_scale"], p["wpb"], p["wab"], p["wo"], p["ln1_g"], p["ln1_b"],
               p["s13"], p["s2"], p["rwh"], p["rwl"], p["rb"]]
    in_specs += [pl.BlockSpec(w.shape, (lambda nd: (lambda i, lo, hi: (0,) * nd))(w.ndim),
                              pipeline_mode=pl.Buffered(1)) for w in weights]
    out_shape = (jax.ShapeDtypeStruct((t, d), F32),
                 jax.ShapeDtypeStruct((t, d), F32),
                 jax.ShapeDtypeStruct((TOP_K, t), I32),
                 jax.ShapeDtypeStruct((TOP_K, t), F32),
                 jax.ShapeDtypeStruct((TOP_K, t), I32),
                 jax.ShapeDtypeStruct((n_experts, LANES), F32))
    out_specs = (pl.BlockSpec((tm, d), lambda i, lo, hi: (i, 0)),
                 pl.BlockSpec((tm, d), lambda i, lo, hi: (i, 0)),
                 pl.BlockSpec((TOP_K, tm), lambda i, lo, hi: (0, i)),
                 pl.BlockSpec((TOP_K, tm), lambda i, lo, hi: (0, i)),
                 pl.BlockSpec((TOP_K, tm), lambda i, lo, hi: (0, i)),
                 pl.BlockSpec((n_experts, LANES), lambda i, lo, hi: (0, 0)))
    grid_spec = pltpu.PrefetchScalarGridSpec(
        num_scalar_prefetch=2, grid=(t // tm,), in_specs=in_specs, out_specs=out_specs,
        scratch_shapes=[pltpu.VMEM((n_experts, LANES), F32)])
    return pl.pallas_call(
        functools.partial(_mixer_body, n_a=n_a, alpha=alpha, n_experts=n_experts),
        grid_spec=grid_spec, out_shape=out_shape,
        compiler_params=_cparams(("arbitrary",)),
        name="mixer",
    )(seq_lo, seq_hi, xa, xs, h, h, h, h, h, attn, *weights)


SUBLANES = 8
MXU_N = 256


def _tile_row_view(x):
    r, d = x.shape
    return x.reshape(r // SUBLANES, SUBLANES, d)


def _row_src(x3, row):
    return x3.at[lax.shift_right_logical(row, 3), pl.ds(row & (SUBLANES - 1), 1)]


def _experts_body(be_ref, nact_ref, tok_cur_ref, tok_nxt_ref, x3_hbm, w1_ref, w3_ref, w2_ref, o_ref,
                  xbuf, w13b, w2b, sem, *, n_blk):
    b = pl.program_id(0)
    nact = nact_ref[0]
    bm = xbuf.shape[1] * SUBLANES
    d = xbuf.shape[3]
    ed = w2_ref.shape[1]
    slot = b % 2
    active = b < nact

    def row_copy(tok_ref, s, r_hi, r_lo, r):
        return pltpu.make_async_copy(_row_src(x3_hbm, tok_ref[0, 0, r]),
                                     xbuf.at[s, r_hi, pl.ds(r_lo, 1)], sem.at[s])

    def wait_rows(s):
        pltpu.make_async_copy(x3_hbm.at[pl.ds(0, bm // SUBLANES)], xbuf.at[s], sem.at[s]).wait()

    @pl.when((b == 0) & (nact > 0))
    def _():
        def body(r_hi, c):
            for u in range(SUBLANES):
                row_copy(tok_cur_ref, 0, r_hi, u, r_hi * SUBLANES + u).start()
            return c
        lax.fori_loop(0, bm // SUBLANES, body, 0)

    e = be_ref[b]
    e_prev = be_ref[jnp.maximum(b - 1, 0)]

    @pl.when((b == 0) | (e != e_prev))
    def _():
        for c in range(ed // MXU_N):
            w13b[:, 2 * c * MXU_N:(2 * c + 1) * MXU_N] = w1_ref[0, :, c * MXU_N:(c + 1) * MXU_N].astype(BF16)
            w13b[:, (2 * c + 1) * MXU_N:(2 * c + 2) * MXU_N] = w3_ref[0, :, c * MXU_N:(c + 1) * MXU_N].astype(BF16)
        w2b[...] = w2_ref[0].astype(BF16)

    @pl.when(b <= nact)
    def _():
        wait_rows(slot)

    @pl.when(active)
    def _():
        n_up = ed // MXU_N
        n_down = d // MXU_N
        n_stage = n_up + n_down
        per_stage = -(-bm // n_stage)
        issued = [0]

        def issue_some():
            lo = issued[0]
            hi = min(bm, lo + per_stage)
            for r in range(lo, hi):
                row_copy(tok_nxt_ref, 1 - slot, r // SUBLANES, r % SUBLANES, r).start()
            issued[0] = hi

        xb = xbuf[slot].reshape(bm, d).astype(BF16)
        parts = []
        for c in range(n_up):
            hc = jnp.dot(xb, w13b[:, 2 * c * MXU_N:(2 * c + 2) * MXU_N], preferred_element_type=F32)
            parts.append((jax.nn.silu(hc[:, :MXU_N]) * hc[:, MXU_N:]).astype(BF16))
            issue_some()
        hdn = jnp.concatenate(parts, axis=1)
        for c in range(n_down):
            o_ref[:, c * MXU_N:(c + 1) * MXU_N] = jnp.dot(hdn, w2b[:, c * MXU_N:(c + 1) * MXU_N],
                                                          preferred_element_type=F32)
            issue_some()
        assert issued[0] == bm

    @pl.when(jnp.logical_not(active))
    def _():
        o_ref[...] = jnp.zeros_like(o_ref)

    @pl.when(active & (b == n_blk - 1))
    def _():
        wait_rows(1 - slot)


def _experts(x1, buf_tok, block_e, nact, w1, w3, w2):
    t, d = x1.shape
    n_blk, bm = buf_tok.shape
    ed = w1.shape[2]
    assert ed % MXU_N == 0 and d % MXU_N == 0 and bm % SUBLANES == 0 and t % SUBLANES == 0
    tok3 = buf_tok.reshape(n_blk, 1, bm)
    grid_spec = pltpu.PrefetchScalarGridSpec(
        num_scalar_prefetch=2,
        grid=(n_blk,),
        in_specs=[pl.BlockSpec((1, 1, bm), lambda b, be, na: (b, 0, 0), memory_space=pltpu.SMEM),
                  pl.BlockSpec((1, 1, bm), lambda b, be, na: (jnp.minimum(b + 1, n_blk - 1), 0, 0),
                               memory_space=pltpu.SMEM),
                  pl.BlockSpec(memory_space=pl.ANY),
                  pl.BlockSpec((1, d, ed), lambda b, be, na: (be[b], 0, 0)),
                  pl.BlockSpec((1, d, ed), lambda b, be, na: (be[b], 0, 0)),
                  pl.BlockSpec((1, ed, d), lambda b, be, na: (be[b], 0, 0))],
        out_specs=pl.BlockSpec((bm, d), lambda b, be, na: (b, 0)),
        scratch_shapes=[pltpu.VMEM((2, bm // SUBLANES, SUBLANES, d), F32),
                        pltpu.VMEM((d, 2 * ed), BF16),
                        pltpu.VMEM((ed, d), BF16),
                        pltpu.SemaphoreType.DMA((2,))],
    )
    return pl.pallas_call(
        functools.partial(_experts_body, n_blk=n_blk),
        grid_spec=grid_spec,
        out_shape=jax.ShapeDtypeStruct((n_blk * bm, d), F32),
        compiler_params=_cparams(("arbitrary",)),
        name="experts",
    )(block_e, nact, tok3, tok3, _tile_row_view(x1), w1, w3, w2)


def _combine_body(dest_cur_ref, dest_nxt_ref, base_ref, gate_ref, g2_ref, b2_ref, yb3_hbm, oa_ref, os_ref,
                  ybuf, sem, *, n_a):
    i = pl.program_id(0)
    n_steps = pl.num_programs(0)
    tm = base_ref.shape[0]
    d = base_ref.shape[1]
    slot = i % 2

    def start_rows(dest_ref, s):
        def body(r_hi, c):
            for u in range(SUBLANES):
                for k in range(TOP_K):
                    row = dest_ref[k, r_hi * SUBLANES + u]
                    pltpu.make_async_copy(_row_src(yb3_hbm, row), ybuf.at[s, k, r_hi, pl.ds(u, 1)],
                                          sem.at[s]).start()
            return c
        lax.fori_loop(0, tm // SUBLANES, body, 0)

    @pl.when(i == 0)
    def _():
        start_rows(dest_cur_ref, 0)

    @pl.when(i + 1 < n_steps)
    def _():
        start_rows(dest_nxt_ref, 1 - slot)

    for k in range(TOP_K):
        pltpu.make_async_copy(yb3_hbm.at[pl.ds(0, tm // SUBLANES)], ybuf.at[slot, k], sem.at[slot]).wait()
    acc = base_ref[...]
    for k in range(TOP_K):
        acc = acc + gate_ref[:, k:k + 1] * ybuf[slot, k].reshape(tm, d)
    y = _layer_norm(acc, g2_ref[...], b2_ref[...])

    @pl.when(i < n_a)
    def _():
        oa_ref[...] = y

    @pl.when(i >= n_a)
    def _():
        os_ref[...] = y


def _combine(base, dest, gate_t, yb, ln_g, ln_b, t_a):
    t, d = base.shape
    tm = COMBINE_TM
    assert t_a % tm == 0 and 0 < t_a < t
    n_a = t_a // tm
    n_steps = t // tm
    map_a, map_s = _two_array_maps(n_a)
    return pl.pallas_call(
        functools.partial(_combine_body, n_a=n_a),
        grid=(n_steps,),
        in_specs=[pl.BlockSpec((TOP_K, tm), lambda i: (0, i), memory_space=pltpu.SMEM),
                  pl.BlockSpec((TOP_K, tm), lambda i: (0, jnp.minimum(i + 1, n_steps - 1)),
                               memory_space=pltpu.SMEM),
                  pl.BlockSpec((tm, d), lambda i: (i, 0)),
                  pl.BlockSpec((tm, TOP_K), lambda i: (i, 0)),
                  pl.BlockSpec((1, d), lambda i: (0, 0)),
                  pl.BlockSpec((1, d), lambda i: (0, 0)),
                  pl.BlockSpec(memory_space=pl.ANY)],
        out_specs=(pl.BlockSpec((tm, d), map_a), pl.BlockSpec((tm, d), map_s)),
        out_shape=(jax.ShapeDtypeStruct((t_a, d), F32), jax.ShapeDtypeStruct((t - t_a, d), F32)),
        scratch_shapes=[pltpu.VMEM((2, TOP_K, tm // SUBLANES, SUBLANES, d), F32),
                        pltpu.SemaphoreType.DMA((2,))],
        compiler_params=_cparams(("arbitrary",)),
        name="combine",
    )(dest, dest, base, gate_t, ln_g, ln_b, _tile_row_view(yb))


def _split_hi_lo(w):
    hi = w.astype(BF16)
    lo = (w - hi.astype(F32)).astype(BF16)
    return hi, lo


def _encoder_layer(xa, xs, seqs, lp, alpha):
    (w_in, pool_w, pool_b, pool_scale, w_pool_br, w_attn_br, attn_rpb, w_out, ln1_g, ln1_b,
     router_w, router_bias, exp_w1, exp_w3, exp_w2, sh_w1, sh_w3, sh_w2, ln2_g, ln2_b) = lp
    d = xa.shape[1]
    t = xa.shape[0] + xs.shape[0]
    pw = w_pool_br.shape[0]
    aw = w_attn_br.shape[0]
    n_heads = attn_rpb.shape[0]
    head_dim = aw // n_heads
    n_experts = router_w.shape[1]
    assert pw == INPROJ_TN and aw == INPROJ_TN and d % INPROJ_TN == 0
    u_block, q_block = 0, pw // INPROJ_TN
    gate_block0 = (pw + 3 * aw) // INPROJ_TN

    h = _inproj(xa, xs, w_in.astype(BF16), q_block=q_block, gate_block0=gate_block0, q_scale=head_dim ** -0.5)

    blk = ATT_ROWS * GRID_W
    images, seq_lo, seq_hi = [], [], []
    for s0, s1 in seqs:
        images.append((s0 // blk, (s1 - s0) // blk))
        for _ in range((s1 - s0) // MIX_TM):
            seq_lo.append(s0)
            seq_hi.append(s1)
    bias_tab = _attn_bias_table(attn_rpb, 2 * LANES // head_dim)
    attn = _attention(h, bias_tab, images, q_block=q_block, n_heads=n_heads, head_dim=head_dim)

    rwh, rwl = _split_hi_lo(router_w.T)
    p = dict(pool_w=pool_w.astype(BF16), pool_b=pool_b.reshape(1, pw), pool_scale=pool_scale.reshape(1, pw),
             wpb=w_pool_br.astype(BF16), wab=w_attn_br.astype(BF16), wo=w_out.astype(BF16),
             ln1_g=ln1_g.reshape(1, d), ln1_b=ln1_b.reshape(1, d),
             s13=jnp.concatenate([sh_w1, sh_w3], axis=1).astype(BF16), s2=sh_w2.astype(BF16),
             rwh=rwh, rwl=rwl, rb=router_bias.reshape(n_experts, 1).astype(F32))
    x1, base, idx_t, gate_t, rank_t, counts = _mixer(
        xa, xs, h, attn, jnp.asarray(seq_lo, I32), jnp.asarray(seq_hi, I32), p,
        alpha=alpha, u_block=u_block, gate_block0=gate_block0)

    bm = EXPERT_BM
    cnt = counts[:, 0].astype(I32)
    pcnt = (cnt + bm - 1) // bm * bm
    pend = jnp.cumsum(pcnt)
    pstart = pend - pcnt
    eids = jnp.arange(n_experts, dtype=I32)[:, None, None]
    dest = jnp.sum(jnp.where(idx_t[None] == eids, pstart[:, None, None], 0), axis=0) + rank_t
    n_blk = (t * TOP_K + n_experts * (bm - 1)) // bm
    tok = jnp.broadcast_to(jnp.arange(t, dtype=I32)[None, :], (TOP_K, t))
    buf_tok = jnp.zeros((n_blk * bm,), I32).at[dest.reshape(-1)].set(tok.reshape(-1))
    blk_row = jnp.arange(n_blk, dtype=I32) * bm
    block_e = jnp.minimum(jnp.sum((pend[None, :] <= blk_row[:, None]).astype(I32), axis=1), n_experts - 1)
    nact = (pend[-1] // bm).astype(I32).reshape(1)
    block_e = jnp.where(jnp.arange(n_blk) < nact[0], block_e, block_e[jnp.maximum(nact[0] - 1, 0)])
    yb = _experts(x1, buf_tok.reshape(n_blk, bm), block_e, nact, exp_w1, exp_w3, exp_w2)

    return _combine(base, dest, gate_t.T, yb, ln2_g.reshape(1, d), ln2_b.reshape(1, d), xa.shape[0])


def kernel(x_prompt, x_sample, w_in, pool_w, pool_b, pool_scale, w_pool_br, w_attn_br, attn_rpb, w_out,
           ln1_g, ln1_b, router_w, router_bias, exp_w1, exp_w3, exp_w2, sh_w1, sh_w3, sh_w2, ln2_g, ln2_b):
    depth = w_in.shape[0]
    alpha = (2.0 * depth) ** 0.25
    bp, sp, d = x_prompt.shape
    bs, ss, _ = x_sample.shape
    seqs = [(i * sp, (i + 1) * sp) for i in range(bp)]
    seqs += [(bp * sp + i * ss, bp * sp + (i + 1) * ss) for i in range(bs)]
    xa, xs = x_prompt.reshape(bp * sp, d), x_sample.reshape(bs * ss, d)
    for l in range(depth):
        lp = (w_in[l], pool_w[l], pool_b[l], pool_scale[l], w_pool_br[l], w_attn_br[l], attn_rpb[l], w_out[l],
              ln1_g[l], ln1_b[l], router_w[l], router_bias[l], exp_w1[l], exp_w3[l], exp_w2[l],
              sh_w1[l], sh_w3[l], sh_w2[l], ln2_g[l], ln2_b[l])
        xa, xs = _encoder_layer(xa, xs, seqs, lp, alpha)
    return (xa.reshape(bp, sp, d), xs.reshape(bs, ss, d))
```

```python
import functools

import numpy as np
import jax
import jax.numpy as jnp
from jax import lax
from jax.experimental import pallas as pl
from jax.experimental.pallas import tpu as pltpu

F32 = jnp.float32
BF16 = jnp.bfloat16
I32 = jnp.int32

GRID_W = 64
POOL_WINDOWS = (2, 4, 8, 16)
WIN_R = 8
WIN_C = 16
TOP_K = 8
N_GROUPS = 8
TOPK_GROUPS = 4
ROUTED_SCALE = 2.5
LN_EPS = 1e-5
NEG_BIG = -1e30

LANES = 128
VMEM_LIMIT = 56 * 1024 * 1024

INPROJ_TM = 1024
INPROJ_TN = 1024
ATT_ROWS = 8
MIX_TM = 256
POOL_HALO = 16
EXPERT_BM = 256
COMBINE_TM = 128


def _cparams(sem):
    return pltpu.CompilerParams(dimension_semantics=sem, vmem_limit_bytes=VMEM_LIMIT)


def _inproj_body(xa_ref, xs_ref, w_ref, o_ref, xb_ref, *, n_a, q_block, gate_block0, q_scale):
    i = pl.program_id(0)
    j = pl.program_id(1)

    @pl.when((j == 0) & (i < n_a))
    def _():
        xb_ref[...] = xa_ref[...].astype(BF16)

    @pl.when((j == 0) & (i >= n_a))
    def _():
        xb_ref[...] = xs_ref[...].astype(BF16)

    acc = jnp.dot(xb_ref[...], w_ref[...], preferred_element_type=F32)
    is_gate = j >= gate_block0

    @pl.when(is_gate)
    def _():
        o_ref[...] = jax.nn.sigmoid(acc).astype(BF16)

    @pl.when(jnp.logical_not(is_gate))
    def _():
        scale = jnp.where(j == q_block, q_scale, 1.0).astype(F32)
        o_ref[...] = (acc * scale).astype(BF16)


def _two_array_maps(n_a):
    return (lambda i, *_: (jnp.minimum(i, n_a - 1), 0)), (lambda i, *_: (jnp.maximum(i - n_a, 0), 0))


def _inproj(xa, xs, w_bf16, *, q_block, gate_block0, q_scale):
    d = xa.shape[1]
    t = xa.shape[0] + xs.shape[0]
    n = w_bf16.shape[1]
    tm, tn = INPROJ_TM, INPROJ_TN
    assert xa.shape[0] % tm == 0 and xs.shape[0] % tm == 0 and xa.shape[0] > 0 and xs.shape[0] > 0
    n_a = xa.shape[0] // tm
    map_a, map_s = _two_array_maps(n_a)
    return pl.pallas_call(
        functools.partial(_inproj_body, n_a=n_a, q_block=q_block, gate_block0=gate_block0, q_scale=q_scale),
        grid=(t // tm, n // tn),
        in_specs=[pl.BlockSpec((tm, d), map_a),
                  pl.BlockSpec((tm, d), map_s),
                  pl.BlockSpec((d, tn), lambda i, j: (0, j))],
        out_specs=pl.BlockSpec((tm, tn), lambda i, j: (i, j)),
        out_shape=jax.ShapeDtypeStruct((t, n), BF16),
        scratch_shapes=[pltpu.VMEM((tm, d), BF16)],
        compiler_params=_cparams(("arbitrary", "arbitrary")),
        name="inproj",
    )(xa, xs, w_bf16)


def _attn_body(prev_ref, next_ref, kst_ref, off_ref,
               q_ref, kp_ref, kc_ref, kn_ref, vp_ref, vc_ref, vn_ref, bias_ref, o_ref,
               kbuf, vbuf, *, heads_per_group, head_dim):
    del prev_ref, next_ref
    b = pl.program_id(0)
    blk = ATT_ROWS * GRID_W
    kbuf[0:blk] = kp_ref[...]
    kbuf[blk:2 * blk] = kc_ref[...]
    kbuf[2 * blk:3 * blk] = kn_ref[...]
    vbuf[0:blk] = vp_ref[...]
    vbuf[blk:2 * blk] = vc_ref[...]
    vbuf[2 * blk:3 * blk] = vn_ref[...]

    gw = heads_per_group * head_dim
    n_groups = q_ref.shape[1] // gw
    win = WIN_R * GRID_W
    nq = heads_per_group * GRID_W
    row_head = lax.broadcasted_iota(I32, (nq, gw), 0) // GRID_W
    lane_head = lax.broadcasted_iota(I32, (nq, gw), 1) // head_dim
    head_mask = row_head == lane_head
    out_lane_head = lax.broadcasted_iota(I32, (GRID_W, gw), 1) // head_dim

    def row_step(i, carry):
        st = pl.multiple_of(kst_ref[b * ATT_ROWS + i] * GRID_W, GRID_W)
        dr0 = (WIN_R - 1) - off_ref[b * ATT_ROWS + i]
        q0 = pl.multiple_of(i * GRID_W, GRID_W)
        for g in range(n_groups):
            qg = q_ref[pl.ds(q0, GRID_W), g * gw:(g + 1) * gw]
            kg = kbuf[pl.ds(st, win), g * gw:(g + 1) * gw]
            vg = vbuf[pl.ds(st, win), g * gw:(g + 1) * gw]
            wt = jnp.concatenate([qg] * heads_per_group, axis=0)
            wt = jnp.where(head_mask, wt, jnp.zeros_like(wt))
            s = lax.dot_general(wt, kg, (((1,), (1,)), ((), ())),
                                preferred_element_type=F32)
            n_chunk = win // LANES
            sj = [s[:, j * LANES:(j + 1) * LANES] + bias_ref[dr0 + 2 * j, g] for j in range(n_chunk)]
            m = sj[0]
            for j in range(1, n_chunk):
                m = jnp.maximum(m, sj[j])
            m = jnp.max(m, axis=1, keepdims=True)
            ej = [jnp.exp(x - m) for x in sj]
            l = ej[0]
            for j in range(1, n_chunk):
                l = l + ej[j]
            inv = 1.0 / jnp.sum(l, axis=1, keepdims=True)
            p = jnp.concatenate([(x * inv).astype(BF16) for x in ej], axis=1)
            of = jnp.dot(p, vg, preferred_element_type=F32)
            out = jnp.zeros((GRID_W, gw), F32)
            for h in range(heads_per_group):
                out = out + jnp.where(out_lane_head == h, of[h * GRID_W:(h + 1) * GRID_W, :], 0.0)
            o_ref[pl.ds(q0, GRID_W), g * gw:(g + 1) * gw] = out.astype(o_ref.dtype)
        return carry

    lax.fori_loop(0, ATT_ROWS, row_step, 0)


def _attn_tables(images):
    prev, nxt, kst, off = [], [], [], []
    for s, n in images:
        rows = n * ATT_ROWS
        kr = min(WIN_R, rows)
        assert kr == WIN_R
        for bl in range(n):
            b = s + bl
            prev.append(max(b - 1, s))
            nxt.append(min(b + 1, s + n - 1))
            for i in range(ATT_ROWS):
                r = bl * ATT_ROWS + i
                rs = min(max(r - kr // 2, 0), rows - kr)
                kst.append(rs - bl * ATT_ROWS + ATT_ROWS)
                off.append(r - rs)
    return (np.asarray(prev, np.int32), np.asarray(nxt, np.int32),
            np.asarray(kst, np.int32), np.asarray(off, np.int32))


def _attn_bias_table(rpb, heads_per_group):
    n_heads, n_dr, n_dc = rpb.shape
    n_pair = n_dr - 1
    c = np.arange(GRID_W)
    cs = np.clip(c - WIN_C // 2, 0, GRID_W - WIN_C)
    kc = np.arange(GRID_W)
    valid = (kc[None, :] >= cs[:, None]) & (kc[None, :] < cs[:, None] + WIN_C)
    dc = np.clip(kc[None, :] - c[:, None] + (WIN_C - 1), 0, n_dc - 1)
    sel = np.zeros((2, n_dc, GRID_W, 2, GRID_W), np.float32)
    ci, ki = np.nonzero(valid)
    for p in range(2):
        sel[p, dc[ci, ki], ci, p, ki] = 1.0
    sel = jnp.asarray(sel.reshape(2 * n_dc, GRID_W * 2 * GRID_W))
    r = rpb.astype(F32)
    pair = jnp.stack([r[:, :n_pair], r[:, 1:]], axis=2)
    pair = jnp.transpose(pair, (1, 0, 2, 3)).reshape(n_pair * n_heads, 2 * n_dc)
    tab = jnp.dot(pair, sel, precision=lax.Precision.HIGHEST)
    tab = tab.reshape(n_pair, n_heads, GRID_W, 2 * GRID_W)
    valid2 = jnp.asarray(np.tile(valid[:, None, :], (1, 2, 1)).reshape(GRID_W, 2 * GRID_W))
    tab = jnp.where(valid2[None, None], tab, NEG_BIG)
    return tab.reshape(n_pair, n_heads // heads_per_group, heads_per_group * GRID_W, 2 * GRID_W)


def _attention(h, bias_tab, images, *, q_block, n_heads, head_dim):
    t = h.shape[0]
    blk = ATT_ROWS * GRID_W
    aw = n_heads * head_dim
    heads_per_group = 2 * LANES // head_dim
    prev, nxt, kst, off = _attn_tables(images)
    n_blocks = t // blk
    assert prev.shape[0] == n_blocks
    kcol, vcol = q_block + 1, q_block + 2
    spec = lambda col, which: pl.BlockSpec(
        (blk, aw), {"cur": lambda b, p, n, k, o: (b, col),
                    "prev": lambda b, p, n, k, o: (p[b], col),
                    "next": lambda b, p, n, k, o: (n[b], col)}[which])
    grid_spec = pltpu.PrefetchScalarGridSpec(
        num_scalar_prefetch=4,
        grid=(n_blocks,),
        in_specs=[spec(q_block, "cur"),
                  spec(kcol, "prev"), spec(kcol, "cur"), spec(kcol, "next"),
                  spec(vcol, "prev"), spec(vcol, "cur"), spec(vcol, "next"),
                  pl.BlockSpec(bias_tab.shape, lambda b, p, n, k, o: (0, 0, 0, 0),
                               pipeline_mode=pl.Buffered(1))],
        out_specs=pl.BlockSpec((blk, aw), lambda b, p, n, k, o: (b, 0)),
        scratch_shapes=[pltpu.VMEM((3 * blk, aw), BF16), pltpu.VMEM((3 * blk, aw), BF16)],
    )
    return pl.pallas_call(
        functools.partial(_attn_body, heads_per_group=heads_per_group, head_dim=head_dim),
        grid_spec=grid_spec,
        out_shape=jax.ShapeDtypeStruct((t, aw), BF16),
        compiler_params=_cparams(("arbitrary",)),
        name="natten",
    )(jnp.asarray(prev), jnp.asarray(nxt), jnp.asarray(kst), jnp.asarray(off),
      h, h, h, h, h, h, h, bias_tab)


def _layer_norm(r, g, b):
    mu = jnp.mean(r, axis=-1, keepdims=True)
    c = r - mu
    var = jnp.mean(c * c, axis=-1, keepdims=True)
    return c * lax.rsqrt(var + LN_EPS) * g + b


def _first_index_of(mask, idx, sentinel, axis):
    return jnp.min(jnp.where(mask, idx, sentinel), axis=axis, keepdims=True)


def _mixer_body(seq_lo_ref, seq_hi_ref,
                xa_ref, xs_ref, up_ref, uc_ref, un_ref, gp_ref, ga_ref, at_ref,
                pw_ref, pb_ref, ps_ref, wpb_ref, wab_ref, wo_ref, g1_ref, b1_ref,
                s13_ref, s2_ref, rwh_ref, rwl_ref, rb_ref,
                x1_ref, base_ref, idx_ref, gate_ref, rank_ref, cnt_ref,
                carry_ref, *, n_a, alpha, n_experts):
    i = pl.program_id(0)
    tm = xa_ref.shape[0]
    t0 = i * tm
    lo_seq = seq_lo_ref[i]
    hi_seq = seq_hi_ref[i]

    @pl.when(i == 0)
    def _():
        carry_ref[...] = jnp.zeros_like(carry_ref)

    u_ext = jnp.concatenate([up_ref[...], uc_ref[...], un_ref[...]], axis=0)
    ext = tm + 2 * POOL_HALO
    tok_r = t0 + lax.broadcasted_iota(I32, (tm, ext), 0)
    tok_c = t0 - POOL_HALO + lax.broadcasted_iota(I32, (tm, ext), 1)
    tok_q = t0 + lax.broadcasted_iota(I32, (tm, 1), 0)
    n_pg = pw_ref.shape[0]
    pg = pw_ref.shape[1]
    parts = []
    for g in range(n_pg):
        w = POOL_WINDOWS[g]
        lo = jnp.maximum(tok_r - w // 2, lo_seq)
        hi = jnp.minimum(tok_r - w // 2 + w, hi_seq)
        band = jnp.where((tok_c >= lo) & (tok_c < hi), 1.0, 0.0).astype(BF16)
        cnt = (jnp.minimum(tok_q - w // 2 + w, hi_seq) - jnp.maximum(tok_q - w // 2, lo_seq)).astype(F32)
        ug = u_ext[:, g * pg:(g + 1) * pg]
        sums = jnp.dot(band, ug, preferred_element_type=F32)
        z = sums / cnt - uc_ref[:, g * pg:(g + 1) * pg].astype(F32)
        y = jnp.dot(z.astype(BF16), pw_ref[g], preferred_element_type=F32)
        y = (y + pb_ref[:, g * pg:(g + 1) * pg]) * ps_ref[:, g * pg:(g + 1) * pg]
        parts.append(y.astype(BF16))
    a = jnp.concatenate(parts, axis=1)
    a_pool = jnp.dot(a, wpb_ref[...], preferred_element_type=F32)
    a_attn = jnp.dot(at_ref[...], wab_ref[...], preferred_element_type=F32)
    merged = gp_ref[...].astype(F32) * a_pool + ga_ref[...].astype(F32) * a_attn
    mix = jnp.dot(merged.astype(BF16), wo_ref[...], preferred_element_type=F32)
    x_in = jnp.where(i < n_a, xa_ref[...], xs_ref[...])
    x1 = _layer_norm(alpha * x_in + mix, g1_ref[...], b1_ref[...])
    x1_ref[...] = x1
    x1b = x1.astype(BF16)

    sd = s2_ref.shape[0]
    hc = jnp.dot(x1b, s13_ref[...], preferred_element_type=F32)
    hdn = jax.nn.silu(hc[:, :sd]) * hc[:, sd:]
    shared = jnp.dot(hdn.astype(BF16), s2_ref[...], preferred_element_type=F32)
    base_ref[...] = alpha * x1 + shared

    x1l = (x1 - x1b.astype(F32)).astype(BF16)
    nt = (((1,), (1,)), ((), ()))
    logits = (lax.dot_general(rwh_ref[...], x1b, nt, preferred_element_type=F32)
              + lax.dot_general(rwh_ref[...], x1l, nt, preferred_element_type=F32)
              + lax.dot_general(rwl_ref[...], x1b, nt, preferred_element_type=F32))
    scores = jax.nn.sigmoid(logits)
    biased = scores + rb_ref[...]
    gsz = n_experts // N_GROUPS
    neg_inf = -jnp.inf

    mem = lax.broadcasted_iota(I32, (gsz, tm), 0)
    gs_rows = []
    for g in range(N_GROUPS):
        bg = biased[g * gsz:(g + 1) * gsz, :]
        m1 = jnp.max(bg, axis=0, keepdims=True)
        first = _first_index_of(bg == m1, mem, gsz, 0)
        m2 = jnp.max(jnp.where(mem == first, neg_inf, bg), axis=0, keepdims=True)
        gs_rows.append(m1 + m2)
    gwork = jnp.concatenate(gs_rows, axis=0)

    gidx = lax.broadcasted_iota(I32, (N_GROUPS, tm), 0)
    egrp = lax.broadcasted_iota(I32, (n_experts, tm), 0) // gsz
    eidx = lax.broadcasted_iota(I32, (n_experts, tm), 0)
    work = jnp.full((n_experts, tm), neg_inf, F32)
    for _ in range(TOPK_GROUPS):
        gm = jnp.max(gwork, axis=0, keepdims=True)
        gf = _first_index_of(gwork == gm, gidx, N_GROUPS, 0)
        gwork = jnp.where(gidx == gf, neg_inf, gwork)
        work = jnp.where(egrp == gf, biased, work)

    hot = jnp.zeros((n_experts, tm), F32)
    sel_idx, sel_w, sel_hit = [], [], []
    for _ in range(TOP_K):
        em = jnp.max(work, axis=0, keepdims=True)
        ef = _first_index_of(work == em, eidx, n_experts, 0)
        hit = eidx == ef
        sel_idx.append(ef)
        sel_w.append(jnp.sum(jnp.where(hit, scores, 0.0), axis=0, keepdims=True))
        sel_hit.append(hit)
        hot = jnp.where(hit, 1.0, hot)
        work = jnp.where(hit, neg_inf, work)
    wsel = jnp.concatenate(sel_w, axis=0)
    gate_ref[...] = wsel / jnp.sum(wsel, axis=0, keepdims=True) * ROUTED_SCALE
    idx_ref[...] = jnp.concatenate(sel_idx, axis=0)

    tri = jnp.where(lax.broadcasted_iota(I32, (tm, tm), 0) < lax.broadcasted_iota(I32, (tm, tm), 1),
                    1.0, 0.0).astype(BF16)
    before = carry_ref[:, 0:1] + jnp.dot(hot.astype(BF16), tri, preferred_element_type=F32)
    rank = [jnp.sum(jnp.where(h, before, 0.0), axis=0, keepdims=True) for h in sel_hit]
    rank_ref[...] = jnp.concatenate(rank, axis=0).astype(I32)
    carry_ref[...] = carry_ref[...] + jnp.sum(hot, axis=1, keepdims=True)
    cnt_ref[...] = carry_ref[...]


def _mixer(xa, xs, h, attn, seq_lo, seq_hi, p, *, alpha, u_block, gate_block0):
    d = xa.shape[1]
    t = xa.shape[0] + xs.shape[0]
    tm = MIX_TM
    assert xa.shape[0] % tm == 0 and xs.shape[0] % tm == 0
    n_a = xa.shape[0] // tm
    map_a, map_s = _two_array_maps(n_a)
    pw = p["wpb"].shape[0]
    aw = attn.shape[1]
    n_experts = p["rwh"].shape[0]
    halo_per_tile = tm // POOL_HALO
    n_halo_blocks = t // POOL_HALO
    gp_col = gate_block0 * INPROJ_TN // d
    in_specs = [
        pl.BlockSpec((tm, d), map_a),
        pl.BlockSpec((tm, d), map_s),
        pl.BlockSpec((POOL_HALO, pw), lambda i, lo, hi: (jnp.maximum(i * halo_per_tile - 1, 0), u_block)),
        pl.BlockSpec((tm, pw), lambda i, lo, hi: (i, u_block)),
        pl.BlockSpec((POOL_HALO, pw),
                     lambda i, lo, hi: (jnp.minimum((i + 1) * halo_per_tile, n_halo_blocks - 1), u_block)),
        pl.BlockSpec((tm, d), lambda i, lo, hi: (i, gp_col)),
        pl.BlockSpec((tm, d), lambda i, lo, hi: (i, gp_col + 1)),
        pl.BlockSpec((tm, aw), lambda i, lo, hi: (i, 0)),
    ]
    weights = [p["pool_w"], p["pool_b"], p["pool_scale"], p["wpb"], p["wab"], p["wo"], p["ln1_g"], p["ln1_b"],
               p["s13"], p["s2"], p["rwh"], p["rwl"], p["rb"]]
    in_specs += [pl.BlockSpec(w.shape, (lambda nd: (lambda i, lo, hi: (0,) * nd))(w.ndim),
                              pipeline_mode=pl.Buffered(1)) for w in weights]
    out_shape = (jax.ShapeDtypeStruct((t, d), F32),
                 jax.ShapeDtypeStruct((t, d), F32),
                 jax.ShapeDtypeStruct((TOP_K, t), I32),
                 jax.ShapeDtypeStruct((TOP_K, t), F32),
                 jax.ShapeDtypeStruct((TOP_K, t), I32),
                 jax.ShapeDtypeStruct((n_experts, LANES), F32))
    out_specs = (pl.BlockSpec((tm, d), lambda i, lo, hi: (i, 0)),
                 pl.BlockSpec((tm, d), lambda i, lo, hi: (i, 0)),
                 pl.BlockSpec((TOP_K, tm), lambda i, lo, hi: (0, i)),
                 pl.BlockSpec((TOP_K, tm), lambda i, lo, hi: (0, i)),
                 pl.BlockSpec((TOP_K, tm), lambda i, lo, hi: (0, i)),
                 pl.BlockSpec((n_experts, LANES), lambda i, lo, hi: (0, 0)))
    grid_spec = pltpu.PrefetchScalarGridSpec(
        num_scalar_prefetch=2, grid=(t // tm,), in_specs=in_specs, out_specs=out_specs,
        scratch_shapes=[pltpu.VMEM((n_experts, LANES), F32)])
    return pl.pallas_call(
        functools.partial(_mixer_body, n_a=n_a, alpha=alpha, n_experts=n_experts),
        grid_spec=grid_spec, out_shape=out_shape,
        compiler_params=_cparams(("arbitrary",)),
        name="mixer",
    )(seq_lo, seq_hi, xa, xs, h, h, h, h, h, attn, *weights)


SUBLANES = 8
MXU_N = 256


def _tile_row_view(x):
    r, d = x.shape
    return x.reshape(r // SUBLANES, SUBLANES, d)


def _row_src(x3, row):
    return x3.at[lax.shift_right_logical(row, 3), pl.ds(row & (SUBLANES - 1), 1)]


GATHER_AHEAD = 2
N_XBUF = GATHER_AHEAD + 1


def _experts_body(be_ref, nact_ref, first_ref, nxte_ref, wslot_ref,
                  tok0_ref, tok1_ref, tok2_ref, x3_hbm, w1_hbm, w3_hbm, w2_hbm, o_ref,
                  xbuf, wst1, wst3, wst2, w13b, w2b, sem_x, sem_w, *, n_blk):
    b = pl.program_id(0)
    nact = nact_ref[0]
    bm = xbuf.shape[1] * SUBLANES
    d = xbuf.shape[3]
    ed = wst2.shape[1]
    slot = b % N_XBUF
    slot_ahead = (b + GATHER_AHEAD) % N_XBUF
    active = b < nact

    def row_copy(tok_ref, s, r_hi, r_lo, r):
        return pltpu.make_async_copy(_row_src(x3_hbm, tok_ref[0, 0, r]),
                                     xbuf.at[s, r_hi, pl.ds(r_lo, 1)], sem_x.at[s])

    def start_rows_loop(tok_ref, s):
        def body(r_hi, c):
            for u in range(SUBLANES):
                row_copy(tok_ref, s, r_hi, u, r_hi * SUBLANES + u).start()
            return c
        lax.fori_loop(0, bm // SUBLANES, body, 0)

    def wait_rows(s):
        pltpu.make_async_copy(x3_hbm.at[pl.ds(0, bm // SUBLANES)], xbuf.at[s], sem_x.at[s]).wait()

    def weight_copies(e, s):
        return (pltpu.make_async_copy(w1_hbm.at[e], wst1.at[s], sem_w.at[s]),
                pltpu.make_async_copy(w3_hbm.at[e], wst3.at[s], sem_w.at[s]),
                pltpu.make_async_copy(w2_hbm.at[e], wst2.at[s], sem_w.at[s]))

    @pl.when(b == 0)
    def _():
        start_rows_loop(tok0_ref, 0)
        start_rows_loop(tok1_ref, 1)
        for cp in weight_copies(be_ref[0], wslot_ref[0]):
            cp.start()

    wait_rows(slot)

    @pl.when(first_ref[b] == 1)
    def _():
        ws = wslot_ref[b]
        for cp in weight_copies(be_ref[b], ws):
            cp.wait()

        @pl.when(nxte_ref[b] >= 0)
        def _():
            for cp in weight_copies(nxte_ref[b], 1 - ws):
                cp.start()

        for c in range(ed // MXU_N):
            w13b[:, 2 * c * MXU_N:(2 * c + 1) * MXU_N] = wst1[ws, :, c * MXU_N:(c + 1) * MXU_N].astype(BF16)
            w13b[:, (2 * c + 1) * MXU_N:(2 * c + 2) * MXU_N] = wst3[ws, :, c * MXU_N:(c + 1) * MXU_N].astype(BF16)
        w2b[...] = wst2[ws].astype(BF16)

    @pl.when(active)
    def _():
        n_up = ed // MXU_N
        n_down = d // MXU_N
        per_stage = -(-bm // (n_up + n_down))
        issued = [0]

        def issue_some():
            lo = issued[0]
            hi = min(bm, lo + per_stage)
            for r in range(lo, hi):
                row_copy(tok2_ref, slot_ahead, r // SUBLANES, r % SUBLANES, r).start()
            issued[0] = hi

        xb = xbuf[slot].reshape(bm, d).astype(BF16)
        parts = []
        for c in range(n_up):
            hc = jnp.dot(xb, w13b[:, 2 * c * MXU_N:(2 * c + 2) * MXU_N], preferred_element_type=F32)
            parts.append((jax.nn.silu(hc[:, :MXU_N]) * hc[:, MXU_N:]).astype(BF16))
            issue_some()
        hdn = jnp.concatenate(parts, axis=1)
        for c in range(n_down):
            o_ref[:, c * MXU_N:(c + 1) * MXU_N] = jnp.dot(hdn, w2b[:, c * MXU_N:(c + 1) * MXU_N],
                                                          preferred_element_type=F32)
            issue_some()
        assert issued[0] == bm

    @pl.when(jnp.logical_not(active))
    def _():
        o_ref[...] = jnp.zeros_like(o_ref)
        start_rows_loop(tok2_ref, slot_ahead)

    @pl.when(b + GATHER_AHEAD >= n_blk)
    def _():
        wait_rows(slot_ahead)


def _experts(x1, buf_tok, block_e, nact, first, nxt_e, wslot, w1, w3, w2):
    t, d = x1.shape
    n_blk, bm = buf_tok.shape
    ed = w1.shape[2]
    assert ed % MXU_N == 0 and d % MXU_N == 0 and bm % SUBLANES == 0 and t % SUBLANES == 0
    assert n_blk > GATHER_AHEAD
    tok3 = buf_tok.reshape(n_blk, 1, bm)
    tok_spec = lambda ahead: pl.BlockSpec((1, 1, bm), lambda b, *_: (jnp.minimum(b + ahead, n_blk - 1), 0, 0),
                                          memory_space=pltpu.SMEM)
    grid_spec = pltpu.PrefetchScalarGridSpec(
        num_scalar_prefetch=5,
        grid=(n_blk,),
        in_specs=[tok_spec(0), tok_spec(1), tok_spec(GATHER_AHEAD),
                  pl.BlockSpec(memory_space=pl.ANY),
                  pl.BlockSpec(memory_space=pl.ANY),
                  pl.BlockSpec(memory_space=pl.ANY),
                  pl.BlockSpec(memory_space=pl.ANY)],
        out_specs=pl.BlockSpec((bm, d), lambda b, *_: (b, 0)),
        scratch_shapes=[pltpu.VMEM((N_XBUF, bm // SUBLANES, SUBLANES, d), F32),
                        pltpu.VMEM((2, d, ed), F32),
                        pltpu.VMEM((2, d, ed), F32),
                        pltpu.VMEM((2, ed, d), F32),
                        pltpu.VMEM((d, 2 * ed), BF16),
                        pltpu.VMEM((ed, d), BF16),
                        pltpu.SemaphoreType.DMA((N_XBUF,)),
                        pltpu.SemaphoreType.DMA((2,))],
    )
    return pl.pallas_call(
        functools.partial(_experts_body, n_blk=n_blk),
        grid_spec=grid_spec,
        out_shape=jax.ShapeDtypeStruct((n_blk * bm, d), F32),
        compiler_params=_cparams(("arbitrary",)),
        name="experts",
    )(block_e, nact, first, nxt_e, wslot, tok3, tok3, tok3, _tile_row_view(x1), w1, w3, w2)


def _combine_body(dest_cur_ref, dest_nxt_ref, base_ref, gate_ref, g2_ref, b2_ref, yb3_hbm, oa_ref, os_ref,
                  ybuf, sem, *, n_a):
    i = pl.program_id(0)
    n_steps = pl.num_programs(0)
    tm = base_ref.shape[0]
    d = base_ref.shape[1]
    slot = i % 2

    def start_rows(dest_ref, s):
        def body(r_hi, c):
            for u in range(SUBLANES):
                for k in range(TOP_K):
                    row = dest_ref[k, r_hi * SUBLANES + u]
                    pltpu.make_async_copy(_row_src(yb3_hbm, row), ybuf.at[s, k, r_hi, pl.ds(u, 1)],
                                          sem.at[s]).start()
            return c
        lax.fori_loop(0, tm // SUBLANES, body, 0)

    @pl.when(i == 0)
    def _():
        start_rows(dest_cur_ref, 0)

    @pl.when(i + 1 < n_steps)
    def _():
        start_rows(dest_nxt_ref, 1 - slot)

    for k in range(TOP_K):
        pltpu.make_async_copy(yb3_hbm.at[pl.ds(0, tm // SUBLANES)], ybuf.at[slot, k], sem.at[slot]).wait()
    acc = base_ref[...]
    for k in range(TOP_K):
        acc = acc + gate_ref[:, k:k + 1] * ybuf[slot, k].reshape(tm, d)
    y = _layer_norm(acc, g2_ref[...], b2_ref[...])

    @pl.when(i < n_a)
    def _():
        oa_ref[...] = y

    @pl.when(i >= n_a)
    def _():
        os_ref[...] = y


def _combine(base, dest, gate_t, yb, ln_g, ln_b, t_a):
    t, d = base.shape
    tm = COMBINE_TM
    assert t_a % tm == 0 and 0 < t_a < t
    n_a = t_a // tm
    n_steps = t // tm
    map_a, map_s = _two_array_maps(n_a)
    return pl.pallas_call(
        functools.partial(_combine_body, n_a=n_a),
        grid=(n_steps,),
        in_specs=[pl.BlockSpec((TOP_K, tm), lambda i: (0, i), memory_space=pltpu.SMEM),
                  pl.BlockSpec((TOP_K, tm), lambda i: (0, jnp.minimum(i + 1, n_steps - 1)),
                               memory_space=pltpu.SMEM),
                  pl.BlockSpec((tm, d), lambda i: (i, 0)),
                  pl.BlockSpec((tm, TOP_K), lambda i: (i, 0)),
                  pl.BlockSpec((1, d), lambda i: (0, 0)),
                  pl.BlockSpec((1, d), lambda i: (0, 0)),
                  pl.BlockSpec(memory_space=pl.ANY)],
        out_specs=(pl.BlockSpec((tm, d), map_a), pl.BlockSpec((tm, d), map_s)),
        out_shape=(jax.ShapeDtypeStruct((t_a, d), F32), jax.ShapeDtypeStruct((t - t_a, d), F32)),
        scratch_shapes=[pltpu.VMEM((2, TOP_K, tm // SUBLANES, SUBLANES, d), F32),
                        pltpu.SemaphoreType.DMA((2,))],
        compiler_params=_cparams(("arbitrary",)),
        name="combine",
    )(dest, dest, base, gate_t, ln_g, ln_b, _tile_row_view(yb))


def _split_hi_lo(w):
    hi = w.astype(BF16)
    lo = (w - hi.astype(F32)).astype(BF16)
    return hi, lo


def _encoder_layer(xa, xs, seqs, lp, alpha):
    (w_in, pool_w, pool_b, pool_scale, w_pool_br, w_attn_br, attn_rpb, w_out, ln1_g, ln1_b,
     router_w, router_bias, exp_w1, exp_w3, exp_w2, sh_w1, sh_w3, sh_w2, ln2_g, ln2_b) = lp
    d = xa.shape[1]
    t = xa.shape[0] + xs.shape[0]
    pw = w_pool_br.shape[0]
    aw = w_attn_br.shape[0]
    n_heads = attn_rpb.shape[0]
    head_dim = aw // n_heads
    n_experts = router_w.shape[1]
    assert pw == INPROJ_TN and aw == INPROJ_TN and d % INPROJ_TN == 0
    u_block, q_block = 0, pw // INPROJ_TN
    gate_block0 = (pw + 3 * aw) // INPROJ_TN

    h = _inproj(xa, xs, w_in.astype(BF16), q_block=q_block, gate_block0=gate_block0, q_scale=head_dim ** -0.5)

    blk = ATT_ROWS * GRID_W
    images, seq_lo, seq_hi = [], [], []
    for s0, s1 in seqs:
        images.append((s0 // blk, (s1 - s0) // blk))
        for _ in range((s1 - s0) // MIX_TM):
            seq_lo.append(s0)
            seq_hi.append(s1)
    bias_tab = _attn_bias_table(attn_rpb, 2 * LANES // head_dim)
    attn = _attention(h, bias_tab, images, q_block=q_block, n_heads=n_heads, head_dim=head_dim)

    rwh, rwl = _split_hi_lo(router_w.T)
    p = dict(pool_w=pool_w.astype(BF16), pool_b=pool_b.reshape(1, pw), pool_scale=pool_scale.reshape(1, pw),
             wpb=w_pool_br.astype(BF16), wab=w_attn_br.astype(BF16), wo=w_out.astype(BF16),
             ln1_g=ln1_g.reshape(1, d), ln1_b=ln1_b.reshape(1, d),
             s13=jnp.concatenate([sh_w1, sh_w3], axis=1).astype(BF16), s2=sh_w2.astype(BF16),
             rwh=rwh, rwl=rwl, rb=router_bias.reshape(n_experts, 1).astype(F32))
    x1, base, idx_t, gate_t, rank_t, counts = _mixer(
        xa, xs, h, attn, jnp.asarray(seq_lo, I32), jnp.asarray(seq_hi, I32), p,
        alpha=alpha, u_block=u_block, gate_block0=gate_block0)

    bm = EXPERT_BM
    cnt = counts[:, 0].astype(I32)
    pcnt = (cnt + bm - 1) // bm * bm
    pend = jnp.cumsum(pcnt)
    pstart = pend - pcnt
    eids = jnp.arange(n_experts, dtype=I32)[:, None, None]
    dest = jnp.sum(jnp.where(idx_t[None] == eids, pstart[:, None, None], 0), axis=0) + rank_t
    n_blk = (t * TOP_K + n_experts * (bm - 1)) // bm
    tok = jnp.broadcast_to(jnp.arange(t, dtype=I32)[None, :], (TOP_K, t))
    buf_tok = jnp.zeros((n_blk * bm,), I32).at[dest.reshape(-1)].set(tok.reshape(-1))
    blk_row = jnp.arange(n_blk, dtype=I32) * bm
    block_e = jnp.minimum(jnp.sum((pend[None, :] <= blk_row[:, None]).astype(I32), axis=1), n_experts - 1)
    nact = (pend[-1] // bm).astype(I32).reshape(1)
    block_e = jnp.where(jnp.arange(n_blk) < nact[0], block_e, block_e[jnp.maximum(nact[0] - 1, 0)])
    eids1 = jnp.arange(n_experts, dtype=I32)
    present = cnt > 0
    later = present[None, :] & (eids1[None, :] > eids1[:, None])
    nxt_of_e = jnp.min(jnp.where(later, eids1[None, :], n_experts), axis=1)
    nxt_of_e = jnp.where(nxt_of_e >= n_experts, -1, nxt_of_e).astype(I32)
    ord_of_e = (jnp.cumsum(present.astype(I32)) - 1).astype(I32)
    onehot_be = (block_e[:, None] == eids1[None, :]).astype(I32)
    blk_ids = jnp.arange(n_blk, dtype=I32)
    is_act = blk_ids < nact[0]
    prev_e = jnp.concatenate([jnp.full((1,), -1, I32), block_e[:-1]])
    first = (is_act & (block_e != prev_e)).astype(I32)
    nxt_e = jnp.sum(onehot_be * nxt_of_e[None, :], axis=1).astype(I32)
    wslot = (jnp.sum(onehot_be * ord_of_e[None, :], axis=1) % 2).astype(I32)
    yb = _experts(x1, buf_tok.reshape(n_blk, bm), block_e, nact, first, nxt_e, wslot, exp_w1, exp_w3, exp_w2)

    return _combine(base, dest, gate_t.T, yb, ln2_g.reshape(1, d), ln2_b.reshape(1, d), xa.shape[0])


def kernel(x_prompt, x_sample, w_in, pool_w, pool_b, pool_scale, w_pool_br, w_attn_br, attn_rpb, w_out,
           ln1_g, ln1_b, router_w, router_bias, exp_w1, exp_w3, exp_w2, sh_w1, sh_w3, sh_w2, ln2_g, ln2_b):
    depth = w_in.shape[0]
    alpha = (2.0 * depth) ** 0.25
    bp, sp, d = x_prompt.shape
    bs, ss, _ = x_sample.shape
    seqs = [(i * sp, (i + 1) * sp) for i in range(bp)]
    seqs += [(bp * sp + i * ss, bp * sp + (i + 1) * ss) for i in range(bs)]
    xa, xs = x_prompt.reshape(bp * sp, d), x_sample.reshape(bs * ss, d)
    for l in range(depth):
        lp = (w_in[l], pool_w[l], pool_b[l], pool_scale[l], w_pool_br[l], w_attn_br[l], attn_rpb[l], w_out[l],
              ln1_g[l], ln1_b[l], router_w[l], router_bias[l], exp_w1[l], exp_w3[l], exp_w2[l],
              sh_w1[l], sh_w3[l], sh_w2[l], ln2_g[l], ln2_b[l])
        xa, xs = _encoder_layer(xa, xs, seqs, lp, alpha)
    return (xa.reshape(bp, sp, d), xs.reshape(bs, ss, d))
```

```python
import functools

import numpy as np
import jax
import jax.numpy as jnp
from jax import lax
from jax.experimental import pallas as pl
from jax.experimental.pallas import tpu as pltpu

F32 = jnp.float32
BF16 = jnp.bfloat16
I32 = jnp.int32

GRID_W = 64
POOL_WINDOWS = (2, 4, 8, 16)
WIN_R = 8
WIN_C = 16
TOP_K = 8
N_GROUPS = 8
TOPK_GROUPS = 4
ROUTED_SCALE = 2.5
LN_EPS = 1e-5
NEG_BIG = -1e30

LANES = 128
VMEM_LIMIT = 56 * 1024 * 1024

INPROJ_TM = 1024
INPROJ_TN = 1024
ATT_ROWS = 8
MIX_TM = 256
POOL_HALO = 16
EXPERT_BM = 256
COMBINE_TM = 128


def _cparams(sem):
    return pltpu.CompilerParams(dimension_semantics=sem, vmem_limit_bytes=VMEM_LIMIT)


def _inproj_body(xa_ref, xs_ref, w_ref, o_ref, xb_ref, *, n_a, q_block, gate_block0, q_scale):
    i = pl.program_id(0)
    j = pl.program_id(1)

    @pl.when((j == 0) & (i < n_a))
    def _():
        xb_ref[...] = xa_ref[...].astype(BF16)

    @pl.when((j == 0) & (i >= n_a))
    def _():
        xb_ref[...] = xs_ref[...].astype(BF16)

    acc = jnp.dot(xb_ref[...], w_ref[...], preferred_element_type=F32)
    is_gate = j >= gate_block0

    @pl.when(is_gate)
    def _():
        o_ref[...] = jax.nn.sigmoid(acc).astype(BF16)

    @pl.when(jnp.logical_not(is_gate))
    def _():
        scale = jnp.where(j == q_block, q_scale, 1.0).astype(F32)
        o_ref[...] = (acc * scale).astype(BF16)


def _two_array_maps(n_a):
    return (lambda i, *_: (jnp.minimum(i, n_a - 1), 0)), (lambda i, *_: (jnp.maximum(i - n_a, 0), 0))


def _inproj(xa, xs, w_bf16, *, q_block, gate_block0, q_scale):
    d = xa.shape[1]
    t = xa.shape[0] + xs.shape[0]
    n = w_bf16.shape[1]
    tm, tn = INPROJ_TM, INPROJ_TN
    assert xa.shape[0] % tm == 0 and xs.shape[0] % tm == 0 and xa.shape[0] > 0 and xs.shape[0] > 0
    n_a = xa.shape[0] // tm
    map_a, map_s = _two_array_maps(n_a)
    return pl.pallas_call(
        functools.partial(_inproj_body, n_a=n_a, q_block=q_block, gate_block0=gate_block0, q_scale=q_scale),
        grid=(t // tm, n // tn),
        in_specs=[pl.BlockSpec((tm, d), map_a),
                  pl.BlockSpec((tm, d), map_s),
                  pl.BlockSpec((d, tn), lambda i, j: (0, j))],
        out_specs=pl.BlockSpec((tm, tn), lambda i, j: (i, j)),
        out_shape=jax.ShapeDtypeStruct((t, n), BF16),
        scratch_shapes=[pltpu.VMEM((tm, d), BF16)],
        compiler_params=_cparams(("arbitrary", "arbitrary")),
        name="inproj",
    )(xa, xs, w_bf16)


def _attn_body(prev_ref, next_ref, kst_ref, off_ref,
               q_ref, kp_ref, kc_ref, kn_ref, vp_ref, vc_ref, vn_ref, bias_ref, o_ref,
               kbuf, vbuf, *, heads_per_group, head_dim):
    del prev_ref, next_ref
    b = pl.program_id(0)
    blk = ATT_ROWS * GRID_W
    kbuf[0:blk] = kp_ref[...]
    kbuf[blk:2 * blk] = kc_ref[...]
    kbuf[2 * blk:3 * blk] = kn_ref[...]
    vbuf[0:blk] = vp_ref[...]
    vbuf[blk:2 * blk] = vc_ref[...]
    vbuf[2 * blk:3 * blk] = vn_ref[...]

    gw = heads_per_group * head_dim
    n_groups = q_ref.shape[1] // gw
    win = WIN_R * GRID_W
    nq = heads_per_group * GRID_W
    row_head = lax.broadcasted_iota(I32, (nq, gw), 0) // GRID_W
    lane_head = lax.broadcasted_iota(I32, (nq, gw), 1) // head_dim
    head_mask = row_head == lane_head
    out_lane_head = lax.broadcasted_iota(I32, (GRID_W, gw), 1) // head_dim

    def row_step(i, carry):
        st = pl.multiple_of(kst_ref[b * ATT_ROWS + i] * GRID_W, GRID_W)
        dr0 = (WIN_R - 1) - off_ref[b * ATT_ROWS + i]
        q0 = pl.multiple_of(i * GRID_W, GRID_W)
        for g in range(n_groups):
            qg = q_ref[pl.ds(q0, GRID_W), g * gw:(g + 1) * gw]
            kg = kbuf[pl.ds(st, win), g * gw:(g + 1) * gw]
            vg = vbuf[pl.ds(st, win), g * gw:(g + 1) * gw]
            wt = jnp.concatenate([qg] * heads_per_group, axis=0)
            wt = jnp.where(head_mask, wt, jnp.zeros_like(wt))
            s = lax.dot_general(wt, kg, (((1,), (1,)), ((), ())),
                                preferred_element_type=F32)
            n_chunk = win // LANES
            sj = [s[:, j * LANES:(j + 1) * LANES] + bias_ref[dr0 + 2 * j, g] for j in range(n_chunk)]
            m = sj[0]
            for j in range(1, n_chunk):
                m = jnp.maximum(m, sj[j])
            m = jnp.max(m, axis=1, keepdims=True)
            ej = [jnp.exp(x - m) for x in sj]
            l = ej[0]
            for j in range(1, n_chunk):
                l = l + ej[j]
            inv = 1.0 / jnp.sum(l, axis=1, keepdims=True)
            p = jnp.concatenate([(x * inv).astype(BF16) for x in ej], axis=1)
            of = jnp.dot(p, vg, preferred_element_type=F32)
            out = jnp.zeros((GRID_W, gw), F32)
            for h in range(heads_per_group):
                out = out + jnp.where(out_lane_head == h, of[h * GRID_W:(h + 1) * GRID_W, :], 0.0)
            o_ref[pl.ds(q0, GRID_W), g * gw:(g + 1) * gw] = out.astype(o_ref.dtype)
        return carry

    lax.fori_loop(0, ATT_ROWS, row_step, 0)


def _attn_tables(images):
    prev, nxt, kst, off = [], [], [], []
    for s, n in images:
        rows = n * ATT_ROWS
        kr = min(WIN_R, rows)
        assert kr == WIN_R
        for bl in range(n):
            b = s + bl
            prev.append(max(b - 1, s))
            nxt.append(min(b + 1, s + n - 1))
            for i in range(ATT_ROWS):
                r = bl * ATT_ROWS + i
                rs = min(max(r - kr // 2, 0), rows - kr)
                kst.append(rs - bl * ATT_ROWS + ATT_ROWS)
                off.append(r - rs)
    return (np.asarray(prev, np.int32), np.asarray(nxt, np.int32),
            np.asarray(kst, np.int32), np.asarray(off, np.int32))


def _attn_bias_table(rpb, heads_per_group):
    n_heads, n_dr, n_dc = rpb.shape
    n_pair = n_dr - 1
    c = np.arange(GRID_W)
    cs = np.clip(c - WIN_C // 2, 0, GRID_W - WIN_C)
    kc = np.arange(GRID_W)
    valid = (kc[None, :] >= cs[:, None]) & (kc[None, :] < cs[:, None] + WIN_C)
    dc = np.clip(kc[None, :] - c[:, None] + (WIN_C - 1), 0, n_dc - 1)
    sel = np.zeros((2, n_dc, GRID_W, 2, GRID_W), np.float32)
    ci, ki = np.nonzero(valid)
    for p in range(2):
        sel[p, dc[ci, ki], ci, p, ki] = 1.0
    sel = jnp.asarray(sel.reshape(2 * n_dc, GRID_W * 2 * GRID_W))
    r = rpb.astype(F32)
    pair = jnp.stack([r[:, :n_pair], r[:, 1:]], axis=2)
    pair = jnp.transpose(pair, (1, 0, 2, 3)).reshape(n_pair * n_heads, 2 * n_dc)
    tab = jnp.dot(pair, sel, precision=lax.Precision.HIGHEST)
    tab = tab.reshape(n_pair, n_heads, GRID_W, 2 * GRID_W)
    valid2 = jnp.asarray(np.tile(valid[:, None, :], (1, 2, 1)).reshape(GRID_W, 2 * GRID_W))
    tab = jnp.where(valid2[None, None], tab, NEG_BIG)
    return tab.reshape(n_pair, n_heads // heads_per_group, heads_per_group * GRID_W, 2 * GRID_W)


def _attention(h, bias_tab, images, *, q_block, n_heads, head_dim):
    t = h.shape[0]
    blk = ATT_ROWS * GRID_W
    aw = n_heads * head_dim
    heads_per_group = 2 * LANES // head_dim
    prev, nxt, kst, off = _attn_tables(images)
    n_blocks = t // blk
    assert prev.shape[0] == n_blocks
    kcol, vcol = q_block + 1, q_block + 2
    spec = lambda col, which: pl.BlockSpec(
        (blk, aw), {"cur": lambda b, p, n, k, o: (b, col),
                    "prev": lambda b, p, n, k, o: (p[b], col),
                    "next": lambda b, p, n, k, o: (n[b], col)}[which])
    grid_spec = pltpu.PrefetchScalarGridSpec(
        num_scalar_prefetch=4,
        grid=(n_blocks,),
        in_specs=[spec(q_block, "cur"),
                  spec(kcol, "prev"), spec(kcol, "cur"), spec(kcol, "next"),
                  spec(vcol, "prev"), spec(vcol, "cur"), spec(vcol, "next"),
                  pl.BlockSpec(bias_tab.shape, lambda b, p, n, k, o: (0, 0, 0, 0),
                               pipeline_mode=pl.Buffered(1))],
        out_specs=pl.BlockSpec((blk, aw), lambda b, p, n, k, o: (b, 0)),
        scratch_shapes=[pltpu.VMEM((3 * blk, aw), BF16), pltpu.VMEM((3 * blk, aw), BF16)],
    )
    return pl.pallas_call(
        functools.partial(_attn_body, heads_per_group=heads_per_group, head_dim=head_dim),
        grid_spec=grid_spec,
        out_shape=jax.ShapeDtypeStruct((t, aw), BF16),
        compiler_params=_cparams(("arbitrary",)),
        name="natten",
    )(jnp.asarray(prev), jnp.asarray(nxt), jnp.asarray(kst), jnp.asarray(off),
      h, h, h, h, h, h, h, bias_tab)


def _layer_norm(r, g, b):
    mu = jnp.mean(r, axis=-1, keepdims=True)
    c = r - mu
    var = jnp.mean(c * c, axis=-1, keepdims=True)
    return c * lax.rsqrt(var + LN_EPS) * g + b


def _first_index_of(mask, idx, sentinel, axis):
    return jnp.min(jnp.where(mask, idx, sentinel), axis=axis, keepdims=True)


def _mixer_body(seq_lo_ref, seq_hi_ref,
                xa_ref, xs_ref, up_ref, uc_ref, un_ref, gp_ref, ga_ref, at_ref,
                pw_ref, pb_ref, ps_ref, wpb_ref, wab_ref, wo_ref, g1_ref, b1_ref,
                s13_ref, s2_ref, rwh_ref, rwl_ref, rb_ref,
                x1_ref, base_ref, idx_ref, gate_ref, rank_ref, cnt_ref,
                carry_ref, *, n_a, alpha, n_experts):
    i = pl.program_id(0)
    tm = xa_ref.shape[0]
    t0 = i * tm
    lo_seq = seq_lo_ref[i]
    hi_seq = seq_hi_ref[i]

    @pl.when(i == 0)
    def _():
        carry_ref[...] = jnp.zeros_like(carry_ref)

    u_ext = jnp.concatenate([up_ref[...], uc_ref[...], un_ref[...]], axis=0)
    ext = tm + 2 * POOL_HALO
    tok_r = t0 + lax.broadcasted_iota(I32, (tm, ext), 0)
    tok_c = t0 - POOL_HALO + lax.broadcasted_iota(I32, (tm, ext), 1)
    tok_q = t0 + lax.broadcasted_iota(I32, (tm, 1), 0)
    n_pg = pw_ref.shape[0]
    pg = pw_ref.shape[1]
    parts = []
    for g in range(n_pg):
        w = POOL_WINDOWS[g]
        lo = jnp.maximum(tok_r - w // 2, lo_seq)
        hi = jnp.minimum(tok_r - w // 2 + w, hi_seq)
        band = jnp.where((tok_c >= lo) & (tok_c < hi), 1.0, 0.0).astype(BF16)
        cnt = (jnp.minimum(tok_q - w // 2 + w, hi_seq) - jnp.maximum(tok_q - w // 2, lo_seq)).astype(F32)
        ug = u_ext[:, g * pg:(g + 1) * pg]
        sums = jnp.dot(band, ug, preferred_element_type=F32)
        z = sums / cnt - uc_ref[:, g * pg:(g + 1) * pg].astype(F32)
        y = jnp.dot(z.astype(BF16), pw_ref[g], preferred_element_type=F32)
        y = (y + pb_ref[:, g * pg:(g + 1) * pg]) * ps_ref[:, g * pg:(g + 1) * pg]
        parts.append(y.astype(BF16))
    a = jnp.concatenate(parts, axis=1)
    a_pool = jnp.dot(a, wpb_ref[...], preferred_element_type=F32)
    a_attn = jnp.dot(at_ref[...], wab_ref[...], preferred_element_type=F32)
    merged = gp_ref[...].astype(F32) * a_pool + ga_ref[...].astype(F32) * a_attn
    mix = jnp.dot(merged.astype(BF16), wo_ref[...], preferred_element_type=F32)
    x_in = jnp.where(i < n_a, xa_ref[...], xs_ref[...])
    x1 = _layer_norm(alpha * x_in + mix, g1_ref[...], b1_ref[...])
    x1_ref[...] = x1
    x1b = x1.astype(BF16)

    sd = s2_ref.shape[0]
    hc = jnp.dot(x1b, s13_ref[...], preferred_element_type=F32)
    hdn = jax.nn.silu(hc[:, :sd]) * hc[:, sd:]
    shared = jnp.dot(hdn.astype(BF16), s2_ref[...], preferred_element_type=F32)
    base_ref[...] = alpha * x1 + shared

    x1l = (x1 - x1b.astype(F32)).astype(BF16)
    nt = (((1,), (1,)), ((), ()))
    logits = (lax.dot_general(rwh_ref[...], x1b, nt, preferred_element_type=F32)
              + lax.dot_general(rwh_ref[...], x1l, nt, preferred_element_type=F32)
              + lax.dot_general(rwl_ref[...], x1b, nt, preferred_element_type=F32))
    scores = jax.nn.sigmoid(logits)
    biased = scores + rb_ref[...]
    gsz = n_experts // N_GROUPS
    neg_inf = -jnp.inf

    mem = lax.broadcasted_iota(I32, (gsz, tm), 0)
    gs_rows = []
    for g in range(N_GROUPS):
        bg = biased[g * gsz:(g + 1) * gsz, :]
        m1 = jnp.max(bg, axis=0, keepdims=True)
        first = _first_index_of(bg == m1, mem, gsz, 0)
        m2 = jnp.max(jnp.where(mem == first, neg_inf, bg), axis=0, keepdims=True)
        gs_rows.append(m1 + m2)
    gwork = jnp.concatenate(gs_rows, axis=0)

    gidx = lax.broadcasted_iota(I32, (N_GROUPS, tm), 0)
    egrp = lax.broadcasted_iota(I32, (n_experts, tm), 0) // gsz
    eidx = lax.broadcasted_iota(I32, (n_experts, tm), 0)
    work = jnp.full((n_experts, tm), neg_inf, F32)
    for _ in range(TOPK_GROUPS):
        gm = jnp.max(gwork, axis=0, keepdims=True)
        gf = _first_index_of(gwork == gm, gidx, N_GROUPS, 0)
        gwork = jnp.where(gidx == gf, neg_inf, gwork)
        work = jnp.where(egrp == gf, biased, work)

    hot = jnp.zeros((n_experts, tm), F32)
    sel_idx, sel_w, sel_hit = [], [], []
    for _ in range(TOP_K):
        em = jnp.max(work, axis=0, keepdims=True)
        ef = _first_index_of(work == em, eidx, n_experts, 0)
        hit = eidx == ef
        sel_idx.append(ef)
        sel_w.append(jnp.sum(jnp.where(hit, scores, 0.0), axis=0, keepdims=True))
        sel_hit.append(hit)
        hot = jnp.where(hit, 1.0, hot)
        work = jnp.where(hit, neg_inf, work)
    wsel = jnp.concatenate(sel_w, axis=0)
    gate_ref[...] = wsel / jnp.sum(wsel, axis=0, keepdims=True) * ROUTED_SCALE
    idx_ref[...] = jnp.concatenate(sel_idx, axis=0)

    tri = jnp.where(lax.broadcasted_iota(I32, (tm, tm), 0) < lax.broadcasted_iota(I32, (tm, tm), 1),
                    1.0, 0.0).astype(BF16)
    before = carry_ref[:, 0:1] + jnp.dot(hot.astype(BF16), tri, preferred_element_type=F32)
    rank = [jnp.sum(jnp.where(h, before, 0.0), axis=0, keepdims=True) for h in sel_hit]
    rank_ref[...] = jnp.concatenate(rank, axis=0).astype(I32)
    carry_ref[...] = carry_ref[...] + jnp.sum(hot, axis=1, keepdims=True)
    cnt_ref[...] = carry_ref[...]


def _mixer(xa, xs, h, attn, seq_lo, seq_hi, p, *, alpha, u_block, gate_block0):
    d = xa.shape[1]
    t = xa.shape[0] + xs.shape[0]
    tm = MIX_TM
    assert xa.shape[0] % tm == 0 and xs.shape[0] % tm == 0
    n_a = xa.shape[0] // tm
    map_a, map_s = _two_array_maps(n_a)
    pw = p["wpb"].shape[0]
    aw = attn.shape[1]
    n_experts = p["rwh"].shape[0]
    halo_per_tile = tm // POOL_HALO
    n_halo_blocks = t // POOL_HALO
    gp_col = gate_block0 * INPROJ_TN // d
    in_specs = [
        pl.BlockSpec((tm, d), map_a),
        pl.BlockSpec((tm, d), map_s),
        pl.BlockSpec((POOL_HALO, pw), lambda i, lo, hi: (jnp.maximum(i * halo_per_tile - 1, 0), u_block)),
        pl.BlockSpec((tm, pw), lambda i, lo, hi: (i, u_block)),
        pl.BlockSpec((POOL_HALO, pw),
                     lambda i, lo, hi: (jnp.minimum((i + 1) * halo_per_tile, n_halo_blocks - 1), u_block)),
        pl.BlockSpec((tm, d), lambda i, lo, hi: (i, gp_col)),
        pl.BlockSpec((tm, d), lambda i, lo, hi: (i, gp_col + 1)),
        pl.BlockSpec((tm, aw), lambda i, lo, hi: (i, 0)),
    ]
    weights = [p["pool_w"], p["pool_b"], p["pool_scale"], p["wpb"], p["wab"], p["wo"], p["ln1_g"], p["ln1_b"],
               p["s13"], p["s2"], p["rwh"], p["rwl"], p["rb"]]
    in_specs += [pl.BlockSpec(w.shape, (lambda nd: (lambda i, lo, hi: (0,) * nd))(w.ndim),
                              pipeline_mode=pl.Buffered(1)) for w in weights]
    out_shape = (jax.ShapeDtypeStruct((t, d), F32),
                 jax.ShapeDtypeStruct((t, d), F32),
                 jax.ShapeDtypeStruct((TOP_K, t), I32),
                 jax.ShapeDtypeStruct((TOP_K, t), F32),
                 jax.ShapeDtypeStruct((TOP_K, t), I32),
                 jax.ShapeDtypeStruct((n_experts, LANES), F32))
    out_specs = (pl.BlockSpec((tm, d), lambda i, lo, hi: (i, 0)),
                 pl.BlockSpec((tm, d), lambda i, lo, hi: (i, 0)),
                 pl.BlockSpec((TOP_K, tm), lambda i, lo, hi: (0, i)),
                 pl.BlockSpec((TOP_K, tm), lambda i, lo, hi: (0, i)),
                 pl.BlockSpec((TOP_K, tm), lambda i, lo, hi: (0, i)),
                 pl.BlockSpec((n_experts, LANES), lambda i, lo, hi: (0, 0)))
    grid_spec = pltpu.PrefetchScalarGridSpec(
        num_scalar_prefetch=2, grid=(t // tm,), in_specs=in_specs, out_specs=out_specs,
        scratch_shapes=[pltpu.VMEM((n_experts, LANES), F32)])
    return pl.pallas_call(
        functools.partial(_mixer_body, n_a=n_a, alpha=alpha, n_experts=n_experts),
        grid_spec=grid_spec, out_shape=out_shape,
        compiler_params=_cparams(("arbitrary",)),
        name="mixer",
    )(seq_lo, seq_hi, xa, xs, h, h, h, h, h, attn, *weights)


SUBLANES = 8
MXU_N = 256


def _tile_row_view(x):
    r, d = x.shape
    return x.reshape(r // SUBLANES, SUBLANES, d)


def _row_src(x3, row):
    return x3.at[lax.shift_right_logical(row, 3), pl.ds(row & (SUBLANES - 1), 1)]


GATHER_AHEAD = 2
N_XBUF = GATHER_AHEAD + 1


def _experts_body(be_ref, nact_ref, first_ref, nxte_ref, wslot_ref,
                  tok0_ref, tok1_ref, tok2_ref, x3_hbm, w1_hbm, w3_hbm, w2_hbm, o_ref,
                  xbuf, wst1, wst3, wst2, w13b, w2b, sem_x, sem_w, *, n_blk):
    b = pl.program_id(0)
    nact = nact_ref[0]
    bm = xbuf.shape[1] * SUBLANES
    d = xbuf.shape[3]
    ed = wst2.shape[1]
    slot = b % N_XBUF
    slot_ahead = (b + GATHER_AHEAD) % N_XBUF
    active = b < nact

    def row_copy(tok_ref, s, r_hi, r_lo, r):
        return pltpu.make_async_copy(_row_src(x3_hbm, tok_ref[0, 0, r]),
                                     xbuf.at[s, r_hi, pl.ds(r_lo, 1)], sem_x.at[s])

    def start_rows_loop(tok_ref, s):
        def body(r_hi, c):
            for u in range(SUBLANES):
                row_copy(tok_ref, s, r_hi, u, r_hi * SUBLANES + u).start(priority=u % 2)
            return c
        lax.fori_loop(0, bm // SUBLANES, body, 0)

    def wait_rows(s):
        pltpu.make_async_copy(x3_hbm.at[pl.ds(0, bm // SUBLANES)], xbuf.at[s], sem_x.at[s]).wait()

    def weight_copies(e, s):
        return (pltpu.make_async_copy(w1_hbm.at[e], wst1.at[s], sem_w.at[s]),
                pltpu.make_async_copy(w3_hbm.at[e], wst3.at[s], sem_w.at[s]),
                pltpu.make_async_copy(w2_hbm.at[e], wst2.at[s], sem_w.at[s]))

    @pl.when(b == 0)
    def _():
        start_rows_loop(tok0_ref, 0)
        start_rows_loop(tok1_ref, 1)
        for cp in weight_copies(be_ref[0], wslot_ref[0]):
            cp.start(priority=1)

    wait_rows(slot)

    @pl.when(first_ref[b] == 1)
    def _():
        ws = wslot_ref[b]
        for cp in weight_copies(be_ref[b], ws):
            cp.wait()

        @pl.when(nxte_ref[b] >= 0)
        def _():
            for cp in weight_copies(nxte_ref[b], 1 - ws):
                cp.start(priority=1)

        for c in range(ed // MXU_N):
            w13b[:, 2 * c * MXU_N:(2 * c + 1) * MXU_N] = wst1[ws, :, c * MXU_N:(c + 1) * MXU_N].astype(BF16)
            w13b[:, (2 * c + 1) * MXU_N:(2 * c + 2) * MXU_N] = wst3[ws, :, c * MXU_N:(c + 1) * MXU_N].astype(BF16)
        w2b[...] = wst2[ws].astype(BF16)

    @pl.when(active)
    def _():
        n_up = ed // MXU_N
        n_down = d // MXU_N
        per_stage = -(-bm // (n_up + n_down))
        issued = [0]

        def issue_some():
            lo = issued[0]
            hi = min(bm, lo + per_stage)
            for r in range(lo, hi):
                row_copy(tok2_ref, slot_ahead, r // SUBLANES, r % SUBLANES, r).start(priority=r % 2)
            issued[0] = hi

        xb = xbuf[slot].reshape(bm, d).astype(BF16)
        parts = []
        for c in range(n_up):
            hc = jnp.dot(xb, w13b[:, 2 * c * MXU_N:(2 * c + 2) * MXU_N], preferred_element_type=F32)
            parts.append((jax.nn.silu(hc[:, :MXU_N]) * hc[:, MXU_N:]).astype(BF16))
            issue_some()
        hdn = jnp.concatenate(parts, axis=1)
        for c in range(n_down):
            o_ref[:, c * MXU_N:(c + 1) * MXU_N] = jnp.dot(hdn, w2b[:, c * MXU_N:(c + 1) * MXU_N],
                                                          preferred_element_type=F32)
            issue_some()
        assert issued[0] == bm

    @pl.when(jnp.logical_not(active))
    def _():
        o_ref[...] = jnp.zeros_like(o_ref)
        start_rows_loop(tok2_ref, slot_ahead)

    @pl.when(b + GATHER_AHEAD >= n_blk)
    def _():
        wait_rows(slot_ahead)


def _experts(x1, buf_tok, block_e, nact, first, nxt_e, wslot, w1, w3, w2):
    t, d = x1.shape
    n_blk, bm = buf_tok.shape
    ed = w1.shape[2]
    assert ed % MXU_N == 0 and d % MXU_N == 0 and bm % SUBLANES == 0 and t % SUBLANES == 0
    assert n_blk > GATHER_AHEAD
    tok3 = buf_tok.reshape(n_blk, 1, bm)
    tok_spec = lambda ahead: pl.BlockSpec((1, 1, bm), lambda b, *_: (jnp.minimum(b + ahead, n_blk - 1), 0, 0),
                                          memory_space=pltpu.SMEM)
    grid_spec = pltpu.PrefetchScalarGridSpec(
        num_scalar_prefetch=5,
        grid=(n_blk,),
        in_specs=[tok_spec(0), tok_spec(1), tok_spec(GATHER_AHEAD),
                  pl.BlockSpec(memory_space=pl.ANY),
                  pl.BlockSpec(memory_space=pl.ANY),
                  pl.BlockSpec(memory_space=pl.ANY),
                  pl.BlockSpec(memory_space=pl.ANY)],
        out_specs=pl.BlockSpec((bm, d), lambda b, *_: (b, 0)),
        scratch_shapes=[pltpu.VMEM((N_XBUF, bm // SUBLANES, SUBLANES, d), F32),
                        pltpu.VMEM((2, d, ed), F32),
                        pltpu.VMEM((2, d, ed), F32),
                        pltpu.VMEM((2, ed, d), F32),
                        pltpu.VMEM((d, 2 * ed), BF16),
                        pltpu.VMEM((ed, d), BF16),
                        pltpu.SemaphoreType.DMA((N_XBUF,)),
                        pltpu.SemaphoreType.DMA((2,))],
    )
    return pl.pallas_call(
        functools.partial(_experts_body, n_blk=n_blk),
        grid_spec=grid_spec,
        out_shape=jax.ShapeDtypeStruct((n_blk * bm, d), F32),
        compiler_params=_cparams(("arbitrary",)),
        name="experts",
    )(block_e, nact, first, nxt_e, wslot, tok3, tok3, tok3, _tile_row_view(x1), w1, w3, w2)


def _combine_body(dest_cur_ref, dest_nxt_ref, base_ref, gate_ref, g2_ref, b2_ref, yb3_hbm, oa_ref, os_ref,
                  ybuf, sem, *, n_a):
    i = pl.program_id(0)
    n_steps = pl.num_programs(0)
    tm = base_ref.shape[0]
    d = base_ref.shape[1]
    slot = i % 2

    def start_rows(dest_ref, s):
        def body(r_hi, c):
            for u in range(SUBLANES):
                for k in range(TOP_K):
                    row = dest_ref[k, r_hi * SUBLANES + u]
                    pltpu.make_async_copy(_row_src(yb3_hbm, row), ybuf.at[s, k, r_hi, pl.ds(u, 1)],
                                          sem.at[s]).start(priority=k % 2)
            return c
        lax.fori_loop(0, tm // SUBLANES, body, 0)

    @pl.when(i == 0)
    def _():
        start_rows(dest_cur_ref, 0)

    @pl.when(i + 1 < n_steps)
    def _():
        start_rows(dest_nxt_ref, 1 - slot)

    for k in range(TOP_K):
        pltpu.make_async_copy(yb3_hbm.at[pl.ds(0, tm // SUBLANES)], ybuf.at[slot, k], sem.at[slot]).wait()
    acc = base_ref[...]
    for k in range(TOP_K):
        acc = acc + gate_ref[:, k:k + 1] * ybuf[slot, k].reshape(tm, d)
    y = _layer_norm(acc, g2_ref[...], b2_ref[...])

    @pl.when(i < n_a)
    def _():
        oa_ref[...] = y

    @pl.when(i >= n_a)
    def _():
        os_ref[...] = y


def _combine(base, dest, gate_t, yb, ln_g, ln_b, t_a):
    t, d = base.shape
    tm = COMBINE_TM
    assert t_a % tm == 0 and 0 < t_a < t
    n_a = t_a // tm
    n_steps = t // tm
    map_a, map_s = _two_array_maps(n_a)
    return pl.pallas_call(
        functools.partial(_combine_body, n_a=n_a),
        grid=(n_steps,),
        in_specs=[pl.BlockSpec((TOP_K, tm), lambda i: (0, i), memory_space=pltpu.SMEM),
                  pl.BlockSpec((TOP_K, tm), lambda i: (0, jnp.minimum(i + 1, n_steps - 1)),
                               memory_space=pltpu.SMEM),
                  pl.BlockSpec((tm, d), lambda i: (i, 0)),
                  pl.BlockSpec((tm, TOP_K), lambda i: (i, 0)),
                  pl.BlockSpec((1, d), lambda i: (0, 0)),
                  pl.BlockSpec((1, d), lambda i: (0, 0)),
                  pl.BlockSpec(memory_space=pl.ANY)],
        out_specs=(pl.BlockSpec((tm, d), map_a), pl.BlockSpec((tm, d), map_s)),
        out_shape=(jax.ShapeDtypeStruct((t_a, d), F32), jax.ShapeDtypeStruct((t - t_a, d), F32)),
        scratch_shapes=[pltpu.VMEM((2, TOP_K, tm // SUBLANES, SUBLANES, d), F32),
                        pltpu.SemaphoreType.DMA((2,))],
        compiler_params=_cparams(("arbitrary",)),
        name="combine",
    )(dest, dest, base, gate_t, ln_g, ln_b, _tile_row_view(yb))


def _split_hi_lo(w):
    hi = w.astype(BF16)
    lo = (w - hi.astype(F32)).astype(BF16)
    return hi, lo


def _encoder_layer(xa, xs, seqs, lp, alpha):
    (w_in, pool_w, pool_b, pool_scale, w_pool_br, w_attn_br, attn_rpb, w_out, ln1_g, ln1_b,
     router_w, router_bias, exp_w1, exp_w3, exp_w2, sh_w1, sh_w3, sh_w2, ln2_g, ln2_b) = lp
    d = xa.shape[1]
    t = xa.shape[0] + xs.shape[0]
    pw = w_pool_br.shape[0]
    aw = w_attn_br.shape[0]
    n_heads = attn_rpb.shape[0]
    head_dim = aw // n_heads
    n_experts = router_w.shape[1]
    assert pw == INPROJ_TN and aw == INPROJ_TN and d % INPROJ_TN == 0
    u_block, q_block = 0, pw // INPROJ_TN
    gate_block0 = (pw + 3 * aw) // INPROJ_TN

    h = _inproj(xa, xs, w_in.astype(BF16), q_block=q_block, gate_block0=gate_block0, q_scale=head_dim ** -0.5)

    blk = ATT_ROWS * GRID_W
    images, seq_lo, seq_hi = [], [], []
    for s0, s1 in seqs:
        images.append((s0 // blk, (s1 - s0) // blk))
        for _ in range((s1 - s0) // MIX_TM):
            seq_lo.append(s0)
            seq_hi.append(s1)
    bias_tab = _attn_bias_table(attn_rpb, 2 * LANES // head_dim)
    attn = _attention(h, bias_tab, images, q_block=q_block, n_heads=n_heads, head_dim=head_dim)

    rwh, rwl = _split_hi_lo(router_w.T)
    p = dict(pool_w=pool_w.astype(BF16), pool_b=pool_b.reshape(1, pw), pool_scale=pool_scale.reshape(1, pw),
             wpb=w_pool_br.astype(BF16), wab=w_attn_br.astype(BF16), wo=w_out.astype(BF16),
             ln1_g=ln1_g.reshape(1, d), ln1_b=ln1_b.reshape(1, d),
             s13=jnp.concatenate([sh_w1, sh_w3], axis=1).astype(BF16), s2=sh_w2.astype(BF16),
             rwh=rwh, rwl=rwl, rb=router_bias.reshape(n_experts, 1).astype(F32))
    x1, base, idx_t, gate_t, rank_t, counts = _mixer(
        xa, xs, h, attn, jnp.asarray(seq_lo, I32), jnp.asarray(seq_hi, I32), p,
        alpha=alpha, u_block=u_block, gate_block0=gate_block0)

    bm = EXPERT_BM
    cnt = counts[:, 0].astype(I32)
    pcnt = (cnt + bm - 1) // bm * bm
    pend = jnp.cumsum(pcnt)
    pstart = pend - pcnt
    eids = jnp.arange(n_experts, dtype=I32)[:, None, None]
    dest = jnp.sum(jnp.where(idx_t[None] == eids, pstart[:, None, None], 0), axis=0) + rank_t
    n_blk = (t * TOP_K + n_experts * (bm - 1)) // bm
    tok = jnp.broadcast_to(jnp.arange(t, dtype=I32)[None, :], (TOP_K, t))
    buf_tok = jnp.zeros((n_blk * bm,), I32).at[dest.reshape(-1)].set(tok.reshape(-1))
    blk_row = jnp.arange(n_blk, dtype=I32) * bm
    block_e = jnp.minimum(jnp.sum((pend[None, :] <= blk_row[:, None]).astype(I32), axis=1), n_experts - 1)
    nact = (pend[-1] // bm).astype(I32).reshape(1)
    block_e = jnp.where(jnp.arange(n_blk) < nact[0], block_e, block_e[jnp.maximum(nact[0] - 1, 0)])
    eids1 = jnp.arange(n_experts, dtype=I32)
    present = cnt > 0
    later = present[None, :] & (eids1[None, :] > eids1[:, None])
    nxt_of_e = jnp.min(jnp.where(later, eids1[None, :], n_experts), axis=1)
    nxt_of_e = jnp.where(nxt_of_e >= n_experts, -1, nxt_of_e).astype(I32)
    ord_of_e = (jnp.cumsum(present.astype(I32)) - 1).astype(I32)
    onehot_be = (block_e[:, None] == eids1[None, :]).astype(I32)
    blk_ids = jnp.arange(n_blk, dtype=I32)
    is_act = blk_ids < nact[0]
    prev_e = jnp.concatenate([jnp.full((1,), -1, I32), block_e[:-1]])
    first = (is_act & (block_e != prev_e)).astype(I32)
    nxt_e = jnp.sum(onehot_be * nxt_of_e[None, :], axis=1).astype(I32)
    wslot = (jnp.sum(onehot_be * ord_of_e[None, :], axis=1) % 2).astype(I32)
    yb = _experts(x1, buf_tok.reshape(n_blk, bm), block_e, nact, first, nxt_e, wslot, exp_w1, exp_w3, exp_w2)

    return _combine(base, dest, gate_t.T, yb, ln2_g.reshape(1, d), ln2_b.reshape(1, d), xa.shape[0])


def kernel(x_prompt, x_sample, w_in, pool_w, pool_b, pool_scale, w_pool_br, w_attn_br, attn_rpb, w_out,
           ln1_g, ln1_b, router_w, router_bias, exp_w1, exp_w3, exp_w2, sh_w1, sh_w3, sh_w2, ln2_g, ln2_b):
    depth = w_in.shape[0]
    alpha = (2.0 * depth) ** 0.25
    bp, sp, d = x_prompt.shape
    bs, ss, _ = x_sample.shape
    seqs = [(i * sp, (i + 1) * sp) for i in range(bp)]
    seqs += [(bp * sp + i * ss, bp * sp + (i + 1) * ss) for i in range(bs)]
    xa, xs = x_prompt.reshape(bp * sp, d), x_sample.reshape(bs * ss, d)
    for l in range(depth):
        lp = (w_in[l], pool_w[l], pool_b[l], pool_scale[l], w_pool_br[l], w_attn_br[l], attn_rpb[l], w_out[l],
              ln1_g[l], ln1_b[l], router_w[l], router_bias[l], exp_w1[l], exp_w3[l], exp_w2[l],
              sh_w1[l], sh_w3[l], sh_w2[l], ln2_g[l], ln2_b[l])
        xa, xs = _encoder_layer(xa, xs, seqs, lp, alpha)
    return (xa.reshape(bp, sp, d), xs.reshape(bs, ss, d))
```

```python
import functools

import numpy as np
import jax
import jax.numpy as jnp
from jax import lax
from jax.experimental import pallas as pl
from jax.experimental.pallas import tpu as pltpu

F32 = jnp.float32
BF16 = jnp.bfloat16
I32 = jnp.int32

GRID_W = 64
POOL_WINDOWS = (2, 4, 8, 16)
WIN_R = 8
WIN_C = 16
TOP_K = 8
N_GROUPS = 8
TOPK_GROUPS = 4
ROUTED_SCALE = 2.5
LN_EPS = 1e-5
NEG_BIG = -1e30

LANES = 128
VMEM_LIMIT = 56 * 1024 * 1024

INPROJ_TM = 1024
INPROJ_TN = 1024
ATT_ROWS = 8
MIX_TM = 256
POOL_HALO = 16
EXPERT_BM = 256
COMBINE_TM = 128


def _cparams(sem):
    return pltpu.CompilerParams(dimension_semantics=sem, vmem_limit_bytes=VMEM_LIMIT)


def _inproj_body(xa_ref, xs_ref, w_ref, o_ref, xb_ref, *, n_a, q_block, gate_block0, q_scale):
    i = pl.program_id(0)
    j = pl.program_id(1)

    @pl.when((j == 0) & (i < n_a))
    def _():
        xb_ref[...] = xa_ref[...].astype(BF16)

    @pl.when((j == 0) & (i >= n_a))
    def _():
        xb_ref[...] = xs_ref[...].astype(BF16)

    acc = jnp.dot(xb_ref[...], w_ref[...], preferred_element_type=F32)
    is_gate = j >= gate_block0

    @pl.when(is_gate)
    def _():
        o_ref[...] = jax.nn.sigmoid(acc).astype(BF16)

    @pl.when(jnp.logical_not(is_gate))
    def _():
        scale = jnp.where(j == q_block, q_scale, 1.0).astype(F32)
        o_ref[...] = (acc * scale).astype(BF16)


def _two_array_maps(n_a):
    return (lambda i, *_: (jnp.minimum(i, n_a - 1), 0)), (lambda i, *_: (jnp.maximum(i - n_a, 0), 0))


def _inproj(xa, xs, w_bf16, *, q_block, gate_block0, q_scale):
    d = xa.shape[1]
    t = xa.shape[0] + xs.shape[0]
    n = w_bf16.shape[1]
    tm, tn = INPROJ_TM, INPROJ_TN
    assert xa.shape[0] % tm == 0 and xs.shape[0] % tm == 0 and xa.shape[0] > 0 and xs.shape[0] > 0
    n_a = xa.shape[0] // tm
    map_a, map_s = _two_array_maps(n_a)
    return pl.pallas_call(
        functools.partial(_inproj_body, n_a=n_a, q_block=q_block, gate_block0=gate_block0, q_scale=q_scale),
        grid=(t // tm, n // tn),
        in_specs=[pl.BlockSpec((tm, d), map_a),
                  pl.BlockSpec((tm, d), map_s),
                  pl.BlockSpec((d, tn), lambda i, j: (0, j))],
        out_specs=pl.BlockSpec((tm, tn), lambda i, j: (i, j)),
        out_shape=jax.ShapeDtypeStruct((t, n), BF16),
        scratch_shapes=[pltpu.VMEM((tm, d), BF16)],
        compiler_params=_cparams(("arbitrary", "arbitrary")),
        name="inproj",
    )(xa, xs, w_bf16)


def _attn_body(prev_ref, next_ref, kst_ref, off_ref,
               q_ref, kp_ref, kc_ref, kn_ref, vp_ref, vc_ref, vn_ref, bias_ref, o_ref,
               kbuf, vbuf, *, heads_per_group, head_dim):
    del prev_ref, next_ref
    b = pl.program_id(0)
    blk = ATT_ROWS * GRID_W
    kbuf[0:blk] = kp_ref[...]
    kbuf[blk:2 * blk] = kc_ref[...]
    kbuf[2 * blk:3 * blk] = kn_ref[...]
    vbuf[0:blk] = vp_ref[...]
    vbuf[blk:2 * blk] = vc_ref[...]
    vbuf[2 * blk:3 * blk] = vn_ref[...]

    gw = heads_per_group * head_dim
    n_groups = q_ref.shape[1] // gw
    win = WIN_R * GRID_W
    nq = heads_per_group * GRID_W
    row_head = lax.broadcasted_iota(I32, (nq, gw), 0) // GRID_W
    lane_head = lax.broadcasted_iota(I32, (nq, gw), 1) // head_dim
    head_mask = row_head == lane_head
    out_lane_head = lax.broadcasted_iota(I32, (GRID_W, gw), 1) // head_dim

    def row_step(i, carry):
        st = pl.multiple_of(kst_ref[b * ATT_ROWS + i] * GRID_W, GRID_W)
        dr0 = (WIN_R - 1) - off_ref[b * ATT_ROWS + i]
        q0 = pl.multiple_of(i * GRID_W, GRID_W)
        for g in range(n_groups):
            qg = q_ref[pl.ds(q0, GRID_W), g * gw:(g + 1) * gw]
            kg = kbuf[pl.ds(st, win), g * gw:(g + 1) * gw]
            vg = vbuf[pl.ds(st, win), g * gw:(g + 1) * gw]
            wt = jnp.concatenate([qg] * heads_per_group, axis=0)
            wt = jnp.where(head_mask, wt, jnp.zeros_like(wt))
            s = lax.dot_general(wt, kg, (((1,), (1,)), ((), ())),
                                preferred_element_type=F32)
            n_chunk = win // LANES
            sj = [s[:, j * LANES:(j + 1) * LANES] + bias_ref[dr0 + 2 * j, g] for j in range(n_chunk)]
            m = sj[0]
            for j in range(1, n_chunk):
                m = jnp.maximum(m, sj[j])
            m = jnp.max(m, axis=1, keepdims=True)
            ej = [jnp.exp(x - m) for x in sj]
            l = ej[0]
            for j in range(1, n_chunk):
                l = l + ej[j]
            inv = 1.0 / jnp.sum(l, axis=1, keepdims=True)
            p = jnp.concatenate([(x * inv).astype(BF16) for x in ej], axis=1)
            of = jnp.dot(p, vg, preferred_element_type=F32)
            out = jnp.zeros((GRID_W, gw), F32)
            for h in range(heads_per_group):
                out = out + jnp.where(out_lane_head == h, of[h * GRID_W:(h + 1) * GRID_W, :], 0.0)
            o_ref[pl.ds(q0, GRID_W), g * gw:(g + 1) * gw] = out.astype(o_ref.dtype)
        return carry

    lax.fori_loop(0, ATT_ROWS, row_step, 0)


def _attn_tables(images):
    prev, nxt, kst, off = [], [], [], []
    for s, n in images:
        rows = n * ATT_ROWS
        kr = min(WIN_R, rows)
        assert kr == WIN_R
        for bl in range(n):
            b = s + bl
            prev.append(max(b - 1, s))
            nxt.append(min(b + 1, s + n - 1))
            for i in range(ATT_ROWS):
                r = bl * ATT_ROWS + i
                rs = min(max(r - kr // 2, 0), rows - kr)
                kst.append(rs - bl * ATT_ROWS + ATT_ROWS)
                off.append(r - rs)
    return (np.asarray(prev, np.int32), np.asarray(nxt, np.int32),
            np.asarray(kst, np.int32), np.asarray(off, np.int32))


def _attn_bias_table(rpb, heads_per_group):
    n_heads, n_dr, n_dc = rpb.shape
    n_pair = n_dr - 1
    c = np.arange(GRID_W)
    cs = np.clip(c - WIN_C // 2, 0, GRID_W - WIN_C)
    kc = np.arange(GRID_W)
    valid = (kc[None, :] >= cs[:, None]) & (kc[None, :] < cs[:, None] + WIN_C)
    dc = np.clip(kc[None, :] - c[:, None] + (WIN_C - 1), 0, n_dc - 1)
    sel = np.zeros((2, n_dc, GRID_W, 2, GRID_W), np.float32)
    ci, ki = np.nonzero(valid)
    for p in range(2):
        sel[p, dc[ci, ki], ci, p, ki] = 1.0
    sel = jnp.asarray(sel.reshape(2 * n_dc, GRID_W * 2 * GRID_W))
    r = rpb.astype(F32)
    pair = jnp.stack([r[:, :n_pair], r[:, 1:]], axis=2)
    pair = jnp.transpose(pair, (1, 0, 2, 3)).reshape(n_pair * n_heads, 2 * n_dc)
    tab = jnp.dot(pair, sel, precision=lax.Precision.HIGHEST)
    tab = tab.reshape(n_pair, n_heads, GRID_W, 2 * GRID_W)
    valid2 = jnp.asarray(np.tile(valid[:, None, :], (1, 2, 1)).reshape(GRID_W, 2 * GRID_W))
    tab = jnp.where(valid2[None, None], tab, NEG_BIG)
    return tab.reshape(n_pair, n_heads // heads_per_group, heads_per_group * GRID_W, 2 * GRID_W)


def _attention(h, bias_tab, images, *, q_block, n_heads, head_dim):
    t = h.shape[0]
    blk = ATT_ROWS * GRID_W
    aw = n_heads * head_dim
    heads_per_group = 2 * LANES // head_dim
    prev, nxt, kst, off = _attn_tables(images)
    n_blocks = t // blk
    assert prev.shape[0] == n_blocks
    kcol, vcol = q_block + 1, q_block + 2
    spec = lambda col, which: pl.BlockSpec(
        (blk, aw), {"cur": lambda b, p, n, k, o: (b, col),
                    "prev": lambda b, p, n, k, o: (p[b], col),
                    "next": lambda b, p, n, k, o: (n[b], col)}[which])
    grid_spec = pltpu.PrefetchScalarGridSpec(
        num_scalar_prefetch=4,
        grid=(n_blocks,),
        in_specs=[spec(q_block, "cur"),
                  spec(kcol, "prev"), spec(kcol, "cur"), spec(kcol, "next"),
                  spec(vcol, "prev"), spec(vcol, "cur"), spec(vcol, "next"),
                  pl.BlockSpec(bias_tab.shape, lambda b, p, n, k, o: (0, 0, 0, 0),
                               pipeline_mode=pl.Buffered(1))],
        out_specs=pl.BlockSpec((blk, aw), lambda b, p, n, k, o: (b, 0)),
        scratch_shapes=[pltpu.VMEM((3 * blk, aw), BF16), pltpu.VMEM((3 * blk, aw), BF16)],
    )
    return pl.pallas_call(
        functools.partial(_attn_body, heads_per_group=heads_per_group, head_dim=head_dim),
        grid_spec=grid_spec,
        out_shape=jax.ShapeDtypeStruct((t, aw), BF16),
        compiler_params=_cparams(("arbitrary",)),
        name="natten",
    )(jnp.asarray(prev), jnp.asarray(nxt), jnp.asarray(kst), jnp.asarray(off),
      h, h, h, h, h, h, h, bias_tab)


def _layer_norm(r, g, b):
    mu = jnp.mean(r, axis=-1, keepdims=True)
    c = r - mu
    var = jnp.mean(c * c, axis=-1, keepdims=True)
    return c * lax.rsqrt(var + LN_EPS) * g + b


def _first_index_of(mask, idx, sentinel, axis):
    return jnp.min(jnp.where(mask, idx, sentinel), axis=axis, keepdims=True)


def _mixer_body(seq_lo_ref, seq_hi_ref,
                xa_ref, xs_ref, up_ref, uc_ref, un_ref, gp_ref, ga_ref, at_ref,
                pw_ref, pb_ref, ps_ref, wpb_ref, wab_ref, wo_ref, g1_ref, b1_ref,
                s13_ref, s2_ref, rwh_ref, rwl_ref, rb_ref,
                x1_ref, base_ref, idx_ref, gate_ref, rank_ref, cnt_ref,
                carry_ref, *, n_a, alpha, n_experts):
    i = pl.program_id(0)
    tm = xa_ref.shape[0]
    t0 = i * tm
    lo_seq = seq_lo_ref[i]
    hi_seq = seq_hi_ref[i]

    @pl.when(i == 0)
    def _():
        carry_ref[...] = jnp.zeros_like(carry_ref)

    u_ext = jnp.concatenate([up_ref[...], uc_ref[...], un_ref[...]], axis=0)
    ext = tm + 2 * POOL_HALO
    tok_r = t0 + lax.broadcasted_iota(I32, (tm, ext), 0)
    tok_c = t0 - POOL_HALO + lax.broadcasted_iota(I32, (tm, ext), 1)
    tok_q = t0 + lax.broadcasted_iota(I32, (tm, 1), 0)
    n_pg = pw_ref.shape[0]
    pg = pw_ref.shape[1]
    parts = []
    for g in range(n_pg):
        w = POOL_WINDOWS[g]
        lo = jnp.maximum(tok_r - w // 2, lo_seq)
        hi = jnp.minimum(tok_r - w // 2 + w, hi_seq)
        band = jnp.where((tok_c >= lo) & (tok_c < hi), 1.0, 0.0).astype(BF16)
        cnt = (jnp.minimum(tok_q - w // 2 + w, hi_seq) - jnp.maximum(tok_q - w // 2, lo_seq)).astype(F32)
        ug = u_ext[:, g * pg:(g + 1) * pg]
        sums = jnp.dot(band, ug, preferred_element_type=F32)
        z = sums / cnt - uc_ref[:, g * pg:(g + 1) * pg].astype(F32)
        y = jnp.dot(z.astype(BF16), pw_ref[g], preferred_element_type=F32)
        y = (y + pb_ref[:, g * pg:(g + 1) * pg]) * ps_ref[:, g * pg:(g + 1) * pg]
        parts.append(y.astype(BF16))
    a = jnp.concatenate(parts, axis=1)
    a_pool = jnp.dot(a, wpb_ref[...], preferred_element_type=F32)
    a_attn = jnp.dot(at_ref[...], wab_ref[...], preferred_element_type=F32)
    merged = gp_ref[...].astype(F32) * a_pool + ga_ref[...].astype(F32) * a_attn
    mix = jnp.dot(merged.astype(BF16), wo_ref[...], preferred_element_type=F32)
    x_in = jnp.where(i < n_a, xa_ref[...], xs_ref[...])
    x1 = _layer_norm(alpha * x_in + mix, g1_ref[...], b1_ref[...])
    n_lt = x1.shape[1] // LANES
    for c in range(n_lt):
        x1_ref[pl.ds(c, tm, stride=n_lt), :] = x1[:, c * LANES:(c + 1) * LANES]
    x1b = x1.astype(BF16)

    sd = s2_ref.shape[0]
    hc = jnp.dot(x1b, s13_ref[...], preferred_element_type=F32)
    hdn = jax.nn.silu(hc[:, :sd]) * hc[:, sd:]
    shared = jnp.dot(hdn.astype(BF16), s2_ref[...], preferred_element_type=F32)
    base_ref[...] = alpha * x1 + shared

    x1l = (x1 - x1b.astype(F32)).astype(BF16)
    nt = (((1,), (1,)), ((), ()))
    logits = (lax.dot_general(rwh_ref[...], x1b, nt, preferred_element_type=F32)
              + lax.dot_general(rwh_ref[...], x1l, nt, preferred_element_type=F32)
              + lax.dot_general(rwl_ref[...], x1b, nt, preferred_element_type=F32))
    scores = jax.nn.sigmoid(logits)
    biased = scores + rb_ref[...]
    gsz = n_experts // N_GROUPS
    neg_inf = -jnp.inf

    mem = lax.broadcasted_iota(I32, (gsz, tm), 0)
    gs_rows = []
    for g in range(N_GROUPS):
        bg = biased[g * gsz:(g + 1) * gsz, :]
        m1 = jnp.max(bg, axis=0, keepdims=True)
        first = _first_index_of(bg == m1, mem, gsz, 0)
        m2 = jnp.max(jnp.where(mem == first, neg_inf, bg), axis=0, keepdims=True)
        gs_rows.append(m1 + m2)
    gwork = jnp.concatenate(gs_rows, axis=0)

    gidx = lax.broadcasted_iota(I32, (N_GROUPS, tm), 0)
    egrp = lax.broadcasted_iota(I32, (n_experts, tm), 0) // gsz
    eidx = lax.broadcasted_iota(I32, (n_experts, tm), 0)
    work = jnp.full((n_experts, tm), neg_inf, F32)
    for _ in range(TOPK_GROUPS):
        gm = jnp.max(gwork, axis=0, keepdims=True)
        gf = _first_index_of(gwork == gm, gidx, N_GROUPS, 0)
        gwork = jnp.where(gidx == gf, neg_inf, gwork)
        work = jnp.where(egrp == gf, biased, work)

    hot = jnp.zeros((n_experts, tm), F32)
    sel_idx, sel_w, sel_hit = [], [], []
    for _ in range(TOP_K):
        em = jnp.max(work, axis=0, keepdims=True)
        ef = _first_index_of(work == em, eidx, n_experts, 0)
        hit = eidx == ef
        sel_idx.append(ef)
        sel_w.append(jnp.sum(jnp.where(hit, scores, 0.0), axis=0, keepdims=True))
        sel_hit.append(hit)
        hot = jnp.where(hit, 1.0, hot)
        work = jnp.where(hit, neg_inf, work)
    wsel = jnp.concatenate(sel_w, axis=0)
    gate_ref[...] = wsel / jnp.sum(wsel, axis=0, keepdims=True) * ROUTED_SCALE
    idx_ref[...] = jnp.concatenate(sel_idx, axis=0)

    tri = jnp.where(lax.broadcasted_iota(I32, (tm, tm), 0) < lax.broadcasted_iota(I32, (tm, tm), 1),
                    1.0, 0.0).astype(BF16)
    before = carry_ref[:, 0:1] + jnp.dot(hot.astype(BF16), tri, preferred_element_type=F32)
    rank = [jnp.sum(jnp.where(h, before, 0.0), axis=0, keepdims=True) for h in sel_hit]
    rank_ref[...] = jnp.concatenate(rank, axis=0).astype(I32)
    carry_ref[...] = carry_ref[...] + jnp.sum(hot, axis=1, keepdims=True)
    cnt_ref[...] = carry_ref[...]


def _mixer(xa, xs, h, attn, seq_lo, seq_hi, p, *, alpha, u_block, gate_block0):
    d = xa.shape[1]
    t = xa.shape[0] + xs.shape[0]
    tm = MIX_TM
    assert xa.shape[0] % tm == 0 and xs.shape[0] % tm == 0
    n_a = xa.shape[0] // tm
    map_a, map_s = _two_array_maps(n_a)
    pw = p["wpb"].shape[0]
    aw = attn.shape[1]
    n_experts = p["rwh"].shape[0]
    halo_per_tile = tm // POOL_HALO
    n_halo_blocks = t // POOL_HALO
    gp_col = gate_block0 * INPROJ_TN // d
    in_specs = [
        pl.BlockSpec((tm, d), map_a),
        pl.BlockSpec((tm, d), map_s),
        pl.BlockSpec((POOL_HALO, pw), lambda i, lo, hi: (jnp.maximum(i * halo_per_tile - 1, 0), u_block)),
        pl.BlockSpec((tm, pw), lambda i, lo, hi: (i, u_block)),
        pl.BlockSpec((POOL_HALO, pw),
                     lambda i, lo, hi: (jnp.minimum((i + 1) * halo_per_tile, n_halo_blocks - 1), u_block)),
        pl.BlockSpec((tm, d), lambda i, lo, hi: (i, gp_col)),
        pl.BlockSpec((tm, d), lambda i, lo, hi: (i, gp_col + 1)),
        pl.BlockSpec((tm, aw), lambda i, lo, hi: (i, 0)),
    ]
    weights = [p["pool_w"], p["pool_b"], p["pool_scale"], p["wpb"], p["wab"], p["wo"], p["ln1_g"], p["ln1_b"],
               p["s13"], p["s2"], p["rwh"], p["rwl"], p["rb"]]
    in_specs += [pl.BlockSpec(w.shape, (lambda nd: (lambda i, lo, hi: (0,) * nd))(w.ndim),
                              pipeline_mode=pl.Buffered(1)) for w in weights]
    n_lt = d // LANES
    out_shape = (jax.ShapeDtypeStruct((t * n_lt, LANES), F32),
                 jax.ShapeDtypeStruct((t, d), F32),
                 jax.ShapeDtypeStruct((TOP_K, t), I32),
                 jax.ShapeDtypeStruct((TOP_K, t), F32),
                 jax.ShapeDtypeStruct((TOP_K, t), I32),
                 jax.ShapeDtypeStruct((n_experts, LANES), F32))
    out_specs = (pl.BlockSpec((tm * n_lt, LANES), lambda i, lo, hi: (i, 0)),
                 pl.BlockSpec((tm, d), lambda i, lo, hi: (i, 0)),
                 pl.BlockSpec((TOP_K, tm), lambda i, lo, hi: (0, i)),
                 pl.BlockSpec((TOP_K, tm), lambda i, lo, hi: (0, i)),
                 pl.BlockSpec((TOP_K, tm), lambda i, lo, hi: (0, i)),
                 pl.BlockSpec((n_experts, LANES), lambda i, lo, hi: (0, 0)))
    grid_spec = pltpu.PrefetchScalarGridSpec(
        num_scalar_prefetch=2, grid=(t // tm,), in_specs=in_specs, out_specs=out_specs,
        scratch_shapes=[pltpu.VMEM((n_experts, LANES), F32)])
    return pl.pallas_call(
        functools.partial(_mixer_body, n_a=n_a, alpha=alpha, n_experts=n_experts),
        grid_spec=grid_spec, out_shape=out_shape,
        compiler_params=_cparams(("arbitrary",)),
        name="mixer",
    )(seq_lo, seq_hi, xa, xs, h, h, h, h, h, attn, *weights)


SUBLANES = 8
MXU_N = 256


def _tile_row_view(x):
    r, d = x.shape
    return x.reshape(r // SUBLANES, SUBLANES, d)


def _row_src(x3, row):
    return x3.at[lax.shift_right_logical(row, 3), pl.ds(row & (SUBLANES - 1), 1)]


GATHER_AHEAD = 2
N_XBUF = GATHER_AHEAD + 1


def _experts_body(be_ref, nact_ref, first_ref, nxte_ref, wslot_ref,
                  tok0_ref, tok1_ref, tok2_ref, xt_hbm, w1_hbm, w3_hbm, w2_hbm, o_ref,
                  xbuf, wst1, wst3, wst2, w13b, w2b, sem_x, sem_w, *, n_blk):
    b = pl.program_id(0)
    nact = nact_ref[0]
    ed = wst2.shape[1]
    d = wst2.shape[2]
    n_lt = d // LANES
    bm = xbuf.shape[1] // n_lt
    slot = b % N_XBUF
    slot_ahead = (b + GATHER_AHEAD) % N_XBUF
    active = b < nact

    def row_copy(tok_ref, s, r):
        src0 = pl.multiple_of(tok_ref[0, 0, r] * n_lt, n_lt)
        dst0 = r * n_lt if isinstance(r, int) else pl.multiple_of(r * n_lt, n_lt)
        return pltpu.make_async_copy(xt_hbm.at[pl.ds(src0, n_lt)], xbuf.at[s, pl.ds(dst0, n_lt)], sem_x.at[s])

    def start_rows_loop(tok_ref, s):
        def body(r_hi, c):
            for u in range(SUBLANES):
                row_copy(tok_ref, s, r_hi * SUBLANES + u).start()
            return c
        lax.fori_loop(0, bm // SUBLANES, body, 0)

    def wait_rows(s):
        pltpu.make_async_copy(xt_hbm.at[pl.ds(0, bm * n_lt)], xbuf.at[s], sem_x.at[s]).wait()

    def weight_copies(e, s):
        return (pltpu.make_async_copy(w1_hbm.at[e], wst1.at[s], sem_w.at[s]),
                pltpu.make_async_copy(w3_hbm.at[e], wst3.at[s], sem_w.at[s]),
                pltpu.make_async_copy(w2_hbm.at[e], wst2.at[s], sem_w.at[s]))

    @pl.when(b == 0)
    def _():
        start_rows_loop(tok0_ref, 0)
        start_rows_loop(tok1_ref, 1)
        for cp in weight_copies(be_ref[0], wslot_ref[0]):
            cp.start()

    wait_rows(slot)

    @pl.when(first_ref[b] == 1)
    def _():
        ws = wslot_ref[b]
        for cp in weight_copies(be_ref[b], ws):
            cp.wait()

        @pl.when(nxte_ref[b] >= 0)
        def _():
            for cp in weight_copies(nxte_ref[b], 1 - ws):
                cp.start()

        for c in range(ed // MXU_N):
            w13b[:, 2 * c * MXU_N:(2 * c + 1) * MXU_N] = wst1[ws, :, c * MXU_N:(c + 1) * MXU_N].astype(BF16)
            w13b[:, (2 * c + 1) * MXU_N:(2 * c + 2) * MXU_N] = wst3[ws, :, c * MXU_N:(c + 1) * MXU_N].astype(BF16)
        w2b[...] = wst2[ws].astype(BF16)

    @pl.when(active)
    def _():
        n_up = ed // MXU_N
        n_down = d // MXU_N
        per_stage = -(-bm // (n_up + n_down))
        issued = [0]

        def issue_some():
            lo = issued[0]
            hi = min(bm, lo + per_stage)
            for r in range(lo, hi):
                row_copy(tok2_ref, slot_ahead, r).start()
            issued[0] = hi

        xb = jnp.concatenate([xbuf[slot, pl.ds(c, bm, stride=n_lt), :].astype(BF16) for c in range(n_lt)],
                             axis=1)
        parts = []
        for c in range(n_up):
            hc = jnp.dot(xb, w13b[:, 2 * c * MXU_N:(2 * c + 2) * MXU_N], preferred_element_type=F32)
            parts.append((jax.nn.silu(hc[:, :MXU_N]) * hc[:, MXU_N:]).astype(BF16))
            issue_some()
        hdn = jnp.concatenate(parts, axis=1)
        for c in range(n_down):
            o_ref[:, c * MXU_N:(c + 1) * MXU_N] = jnp.dot(hdn, w2b[:, c * MXU_N:(c + 1) * MXU_N],
                                                          preferred_element_type=F32)
            issue_some()
        assert issued[0] == bm

    @pl.when(jnp.logical_not(active))
    def _():
        o_ref[...] = jnp.zeros_like(o_ref)
        start_rows_loop(tok2_ref, slot_ahead)

    @pl.when(b + GATHER_AHEAD >= n_blk)
    def _():
        wait_rows(slot_ahead)


def _experts(x1t, buf_tok, block_e, nact, first, nxt_e, wslot, w1, w3, w2):
    n_blk, bm = buf_tok.shape
    d, ed = w1.shape[1], w1.shape[2]
    n_lt = d // LANES
    assert x1t.shape[1] == LANES and x1t.shape[0] % n_lt == 0
    assert ed % MXU_N == 0 and d % MXU_N == 0 and bm % SUBLANES == 0
    assert n_blk > GATHER_AHEAD
    tok3 = buf_tok.reshape(n_blk, 1, bm)
    tok_spec = lambda ahead: pl.BlockSpec((1, 1, bm), lambda b, *_: (jnp.minimum(b + ahead, n_blk - 1), 0, 0),
                                          memory_space=pltpu.SMEM)
    grid_spec = pltpu.PrefetchScalarGridSpec(
        num_scalar_prefetch=5,
        grid=(n_blk,),
        in_specs=[tok_spec(0), tok_spec(1), tok_spec(GATHER_AHEAD),
                  pl.BlockSpec(memory_space=pl.ANY),
                  pl.BlockSpec(memory_space=pl.ANY),
                  pl.BlockSpec(memory_space=pl.ANY),
                  pl.BlockSpec(memory_space=pl.ANY)],
        out_specs=pl.BlockSpec((bm, d), lambda b, *_: (b, 0)),
        scratch_shapes=[pltpu.VMEM((N_XBUF, bm * n_lt, LANES), F32),
                        pltpu.VMEM((2, d, ed), F32),
                        pltpu.VMEM((2, d, ed), F32),
                        pltpu.VMEM((2, ed, d), F32),
                        pltpu.VMEM((d, 2 * ed), BF16),
                        pltpu.VMEM((ed, d), BF16),
                        pltpu.SemaphoreType.DMA((N_XBUF,)),
                        pltpu.SemaphoreType.DMA((2,))],
    )
    return pl.pallas_call(
        functools.partial(_experts_body, n_blk=n_blk),
        grid_spec=grid_spec,
        out_shape=jax.ShapeDtypeStruct((n_blk * bm, d), F32),
        compiler_params=_cparams(("arbitrary",)),
        name="experts",
    )(block_e, nact, first, nxt_e, wslot, tok3, tok3, tok3, x1t, w1, w3, w2)


def _combine_body(dest_cur_ref, dest_nxt_ref, base_ref, gate_ref, g2_ref, b2_ref, yb3_hbm, oa_ref, os_ref,
                  ybuf, sem, *, n_a):
    i = pl.program_id(0)
    n_steps = pl.num_programs(0)
    tm = base_ref.shape[0]
    d = base_ref.shape[1]
    slot = i % 2

    def start_rows(dest_ref, s):
        def body(r_hi, c):
            for u in range(SUBLANES):
                for k in range(TOP_K):
                    row = dest_ref[k, r_hi * SUBLANES + u]
                    pltpu.make_async_copy(_row_src(yb3_hbm, row), ybuf.at[s, k, r_hi, pl.ds(u, 1)],
                                          sem.at[s]).start()
            return c
        lax.fori_loop(0, tm // SUBLANES, body, 0)

    @pl.when(i == 0)
    def _():
        start_rows(dest_cur_ref, 0)

    @pl.when(i + 1 < n_steps)
    def _():
        start_rows(dest_nxt_ref, 1 - slot)

    for k in range(TOP_K):
        pltpu.make_async_copy(yb3_hbm.at[pl.ds(0, tm // SUBLANES)], ybuf.at[slot, k], sem.at[slot]).wait()
    acc = base_ref[...]
    for k in range(TOP_K):
        acc = acc + gate_ref[:, k:k + 1] * ybuf[slot, k].reshape(tm, d)
    y = _layer_norm(acc, g2_ref[...], b2_ref[...])

    @pl.when(i < n_a)
    def _():
        oa_ref[...] = y

    @pl.when(i >= n_a)
    def _():
        os_ref[...] = y


def _combine(base, dest, gate_t, yb, ln_g, ln_b, t_a):
    t, d = base.shape
    tm = COMBINE_TM
    assert t_a % tm == 0 and 0 < t_a < t
    n_a = t_a // tm
    n_steps = t // tm
    map_a, map_s = _two_array_maps(n_a)
    return pl.pallas_call(
        functools.partial(_combine_body, n_a=n_a),
        grid=(n_steps,),
        in_specs=[pl.BlockSpec((TOP_K, tm), lambda i: (0, i), memory_space=pltpu.SMEM),
                  pl.BlockSpec((TOP_K, tm), lambda i: (0, jnp.minimum(i + 1, n_steps - 1)),
                               memory_space=pltpu.SMEM),
                  pl.BlockSpec((tm, d), lambda i: (i, 0)),
                  pl.BlockSpec((tm, TOP_K), lambda i: (i, 0)),
                  pl.BlockSpec((1, d), lambda i: (0, 0)),
                  pl.BlockSpec((1, d), lambda i: (0, 0)),
                  pl.BlockSpec(memory_space=pl.ANY)],
        out_specs=(pl.BlockSpec((tm, d), map_a), pl.BlockSpec((tm, d), map_s)),
        out_shape=(jax.ShapeDtypeStruct((t_a, d), F32), jax.ShapeDtypeStruct((t - t_a, d), F32)),
        scratch_shapes=[pltpu.VMEM((2, TOP_K, tm // SUBLANES, SUBLANES, d), F32),
                        pltpu.SemaphoreType.DMA((2,))],
        compiler_params=_cparams(("arbitrary",)),
        name="combine",
    )(dest, dest, base, gate_t, ln_g, ln_b, _tile_row_view(yb))


def _split_hi_lo(w):
    hi = w.astype(BF16)
    lo = (w - hi.astype(F32)).astype(BF16)
    return hi, lo


def _encoder_layer(xa, xs, seqs, lp, alpha):
    (w_in, pool_w, pool_b, pool_scale, w_pool_br, w_attn_br, attn_rpb, w_out, ln1_g, ln1_b,
     router_w, router_bias, exp_w1, exp_w3, exp_w2, sh_w1, sh_w3, sh_w2, ln2_g, ln2_b) = lp
    d = xa.shape[1]
    t = xa.shape[0] + xs.shape[0]
    pw = w_pool_br.shape[0]
    aw = w_attn_br.shape[0]
    n_heads = attn_rpb.shape[0]
    head_dim = aw // n_heads
    n_experts = router_w.shape[1]
    assert pw == INPROJ_TN and aw == INPROJ_TN and d % INPROJ_TN == 0
    u_block, q_block = 0, pw // INPROJ_TN
    gate_block0 = (pw + 3 * aw) // INPROJ_TN

    h = _inproj(xa, xs, w_in.astype(BF16), q_block=q_block, gate_block0=gate_block0, q_scale=head_dim ** -0.5)

    blk = ATT_ROWS * GRID_W
    images, seq_lo, seq_hi = [], [], []
    for s0, s1 in seqs:
        images.append((s0 // blk, (s1 - s0) // blk))
        for _ in range((s1 - s0) // MIX_TM):
            seq_lo.append(s0)
            seq_hi.append(s1)
    bias_tab = _attn_bias_table(attn_rpb, 2 * LANES // head_dim)
    attn = _attention(h, bias_tab, images, q_block=q_block, n_heads=n_heads, head_dim=head_dim)

    rwh, rwl = _split_hi_lo(router_w.T)
    p = dict(pool_w=pool_w.astype(BF16), pool_b=pool_b.reshape(1, pw), pool_scale=pool_scale.reshape(1, pw),
             wpb=w_pool_br.astype(BF16), wab=w_attn_br.astype(BF16), wo=w_out.astype(BF16),
             ln1_g=ln1_g.reshape(1, d), ln1_b=ln1_b.reshape(1, d),
             s13=jnp.concatenate([sh_w1, sh_w3], axis=1).astype(BF16), s2=sh_w2.astype(BF16),
             rwh=rwh, rwl=rwl, rb=router_bias.reshape(n_experts, 1).astype(F32))
    x1, base, idx_t, gate_t, rank_t, counts = _mixer(
        xa, xs, h, attn, jnp.asarray(seq_lo, I32), jnp.asarray(seq_hi, I32), p,
        alpha=alpha, u_block=u_block, gate_block0=gate_block0)

    bm = EXPERT_BM
    cnt = counts[:, 0].astype(I32)
    pcnt = (cnt + bm - 1) // bm * bm
    pend = jnp.cumsum(pcnt)
    pstart = pend - pcnt
    eids = jnp.arange(n_experts, dtype=I32)[:, None, None]
    dest = jnp.sum(jnp.where(idx_t[None] == eids, pstart[:, None, None], 0), axis=0) + rank_t
    n_blk = (t * TOP_K + n_experts * (bm - 1)) // bm
    tok = jnp.broadcast_to(jnp.arange(t, dtype=I32)[None, :], (TOP_K, t))
    buf_tok = jnp.zeros((n_blk * bm,), I32).at[dest.reshape(-1)].set(tok.reshape(-1))
    blk_row = jnp.arange(n_blk, dtype=I32) * bm
    block_e = jnp.minimum(jnp.sum((pend[None, :] <= blk_row[:, None]).astype(I32), axis=1), n_experts - 1)
    nact = (pend[-1] // bm).astype(I32).reshape(1)
    block_e = jnp.where(jnp.arange(n_blk) < nact[0], block_e, block_e[jnp.maximum(nact[0] - 1, 0)])
    eids1 = jnp.arange(n_experts, dtype=I32)
    present = cnt > 0
    later = present[None, :] & (eids1[None, :] > eids1[:, None])
    nxt_of_e = jnp.min(jnp.where(later, eids1[None, :], n_experts), axis=1)
    nxt_of_e = jnp.where(nxt_of_e >= n_experts, -1, nxt_of_e).astype(I32)
    ord_of_e = (jnp.cumsum(present.astype(I32)) - 1).astype(I32)
    onehot_be = (block_e[:, None] == eids1[None, :]).astype(I32)
    blk_ids = jnp.arange(n_blk, dtype=I32)
    is_act = blk_ids < nact[0]
    prev_e = jnp.concatenate([jnp.full((1,), -1, I32), block_e[:-1]])
    first = (is_act & (block_e != prev_e)).astype(I32)
    nxt_e = jnp.sum(onehot_be * nxt_of_e[None, :], axis=1).astype(I32)
    wslot = (jnp.sum(onehot_be * ord_of_e[None, :], axis=1) % 2).astype(I32)
    yb = _experts(x1, buf_tok.reshape(n_blk, bm), block_e, nact, first, nxt_e, wslot, exp_w1, exp_w3, exp_w2)

    return _combine(base, dest, gate_t.T, yb, ln2_g.reshape(1, d), ln2_b.reshape(1, d), xa.shape[0])


def kernel(x_prompt, x_sample, w_in, pool_w, pool_b, pool_scale, w_pool_br, w_attn_br, attn_rpb, w_out,
           ln1_g, ln1_b, router_w, router_bias, exp_w1, exp_w3, exp_w2, sh_w1, sh_w3, sh_w2, ln2_g, ln2_b):
    depth = w_in.shape[0]
    alpha = (2.0 * depth) ** 0.25
    bp, sp, d = x_prompt.shape
    bs, ss, _ = x_sample.shape
    seqs = [(i * sp, (i + 1) * sp) for i in range(bp)]
    seqs += [(bp * sp + i * ss, bp * sp + (i + 1) * ss) for i in range(bs)]
    xa, xs = x_prompt.reshape(bp * sp, d), x_sample.reshape(bs * ss, d)
    for l in range(depth):
        lp = (w_in[l], pool_w[l], pool_b[l], pool_scale[l], w_pool_br[l], w_attn_br[l], attn_rpb[l], w_out[l],
              ln1_g[l], ln1_b[l], router_w[l], router_bias[l], exp_w1[l], exp_w3[l], exp_w2[l],
              sh_w1[l], sh_w3[l], sh_w2[l], ln2_g[l], ln2_b[l])
        xa, xs = _encoder_layer(xa, xs, seqs, lp, alpha)
    return (xa.reshape(bp, sp, d), xs.reshape(bs, ss, d))
```

```python
import functools

import numpy as np
import jax
import jax.numpy as jnp
from jax import lax
from jax.experimental import pallas as pl
from jax.experimental.pallas import tpu as pltpu
from jax.experimental.pallas import tpu_sc as plsc

F32 = jnp.float32
BF16 = jnp.bfloat16
I32 = jnp.int32

GRID_W = 64
POOL_WINDOWS = (2, 4, 8, 16)
WIN_R = 8
WIN_C = 16
TOP_K = 8
N_GROUPS = 8
TOPK_GROUPS = 4
ROUTED_SCALE = 2.5
LN_EPS = 1e-5
NEG_BIG = -1e30

LANES = 128
VMEM_LIMIT = 56 * 1024 * 1024

INPROJ_TM = 1024
INPROJ_TN = 1024
ATT_ROWS = 8
MIX_TM = 256
POOL_HALO = 16
EXPERT_BM = 256
COMBINE_TM = 128


def _cparams(sem):
    return pltpu.CompilerParams(dimension_semantics=sem, vmem_limit_bytes=VMEM_LIMIT)


def _inproj_body(xa_ref, xs_ref, w_ref, o_ref, xb_ref, *, n_a, q_block, gate_block0, q_scale):
    i = pl.program_id(0)
    j = pl.program_id(1)

    @pl.when((j == 0) & (i < n_a))
    def _():
        xb_ref[...] = xa_ref[...].astype(BF16)

    @pl.when((j == 0) & (i >= n_a))
    def _():
        xb_ref[...] = xs_ref[...].astype(BF16)

    acc = jnp.dot(xb_ref[...], w_ref[...], preferred_element_type=F32)
    is_gate = j >= gate_block0

    @pl.when(is_gate)
    def _():
        o_ref[...] = jax.nn.sigmoid(acc).astype(BF16)

    @pl.when(jnp.logical_not(is_gate))
    def _():
        scale = jnp.where(j == q_block, q_scale, 1.0).astype(F32)
        o_ref[...] = (acc * scale).astype(BF16)


def _two_array_maps(n_a):
    return (lambda i, *_: (jnp.minimum(i, n_a - 1), 0)), (lambda i, *_: (jnp.maximum(i - n_a, 0), 0))


def _inproj(xa, xs, w_bf16, *, q_block, gate_block0, q_scale):
    d = xa.shape[1]
    t = xa.shape[0] + xs.shape[0]
    n = w_bf16.shape[1]
    tm, tn = INPROJ_TM, INPROJ_TN
    assert xa.shape[0] % tm == 0 and xs.shape[0] % tm == 0 and xa.shape[0] > 0 and xs.shape[0] > 0
    n_a = xa.shape[0] // tm
    map_a, map_s = _two_array_maps(n_a)
    return pl.pallas_call(
        functools.partial(_inproj_body, n_a=n_a, q_block=q_block, gate_block0=gate_block0, q_scale=q_scale),
        grid=(t // tm, n // tn),
        in_specs=[pl.BlockSpec((tm, d), map_a),
                  pl.BlockSpec((tm, d), map_s),
                  pl.BlockSpec((d, tn), lambda i, j: (0, j))],
        out_specs=pl.BlockSpec((tm, tn), lambda i, j: (i, j)),
        out_shape=jax.ShapeDtypeStruct((t, n), BF16),
        scratch_shapes=[pltpu.VMEM((tm, d), BF16)],
        compiler_params=_cparams(("arbitrary", "arbitrary")),
        name="inproj",
    )(xa, xs, w_bf16)


def _attn_body(prev_ref, next_ref, kst_ref, off_ref,
               q_ref, kp_ref, kc_ref, kn_ref, vp_ref, vc_ref, vn_ref, bias_ref, o_ref,
               kbuf, vbuf, *, heads_per_group, head_dim):
    del prev_ref, next_ref
    b = pl.program_id(0)
    blk = ATT_ROWS * GRID_W
    kbuf[0:blk] = kp_ref[...]
    kbuf[blk:2 * blk] = kc_ref[...]
    kbuf[2 * blk:3 * blk] = kn_ref[...]
    vbuf[0:blk] = vp_ref[...]
    vbuf[blk:2 * blk] = vc_ref[...]
    vbuf[2 * blk:3 * blk] = vn_ref[...]

    gw = heads_per_group * head_dim
    n_groups = q_ref.shape[1] // gw
    win = WIN_R * GRID_W
    nq = heads_per_group * GRID_W
    row_head = lax.broadcasted_iota(I32, (nq, gw), 0) // GRID_W
    lane_head = lax.broadcasted_iota(I32, (nq, gw), 1) // head_dim
    head_mask = row_head == lane_head
    out_lane_head = lax.broadcasted_iota(I32, (GRID_W, gw), 1) // head_dim

    def row_step(i, carry):
        st = pl.multiple_of(kst_ref[b * ATT_ROWS + i] * GRID_W, GRID_W)
        dr0 = (WIN_R - 1) - off_ref[b * ATT_ROWS + i]
        q0 = pl.multiple_of(i * GRID_W, GRID_W)
        for g in range(n_groups):
            qg = q_ref[pl.ds(q0, GRID_W), g * gw:(g + 1) * gw]
            kg = kbuf[pl.ds(st, win), g * gw:(g + 1) * gw]
            vg = vbuf[pl.ds(st, win), g * gw:(g + 1) * gw]
            wt = jnp.concatenate([qg] * heads_per_group, axis=0)
            wt = jnp.where(head_mask, wt, jnp.zeros_like(wt))
            s = lax.dot_general(wt, kg, (((1,), (1,)), ((), ())),
                                preferred_element_type=F32)
            n_chunk = win // LANES
            sj = [s[:, j * LANES:(j + 1) * LANES] + bias_ref[dr0 + 2 * j, g] for j in range(n_chunk)]
            m = sj[0]
            for j in range(1, n_chunk):
                m = jnp.maximum(m, sj[j])
            m = jnp.max(m, axis=1, keepdims=True)
            ej = [jnp.exp(x - m) for x in sj]
            l = ej[0]
            for j in range(1, n_chunk):
                l = l + ej[j]
            inv = 1.0 / jnp.sum(l, axis=1, keepdims=True)
            p = jnp.concatenate([(x * inv).astype(BF16) for x in ej], axis=1)
            of = jnp.dot(p, vg, preferred_element_type=F32)
            out = jnp.zeros((GRID_W, gw), F32)
            for h in range(heads_per_group):
                out = out + jnp.where(out_lane_head == h, of[h * GRID_W:(h + 1) * GRID_W, :], 0.0)
            o_ref[pl.ds(q0, GRID_W), g * gw:(g + 1) * gw] = out.astype(o_ref.dtype)
        return carry

    lax.fori_loop(0, ATT_ROWS, row_step, 0)


def _attn_tables(images):
    prev, nxt, kst, off = [], [], [], []
    for s, n in images:
        rows = n * ATT_ROWS
        kr = min(WIN_R, rows)
        assert kr == WIN_R
        for bl in range(n):
            b = s + bl
            prev.append(max(b - 1, s))
            nxt.append(min(b + 1, s + n - 1))
            for i in range(ATT_ROWS):
                r = bl * ATT_ROWS + i
                rs = min(max(r - kr // 2, 0), rows - kr)
                kst.append(rs - bl * ATT_ROWS + ATT_ROWS)
                off.append(r - rs)
    return (np.asarray(prev, np.int32), np.asarray(nxt, np.int32),
            np.asarray(kst, np.int32), np.asarray(off, np.int32))


def _attn_bias_table(rpb, heads_per_group):
    n_heads, n_dr, n_dc = rpb.shape
    n_pair = n_dr - 1
    c = np.arange(GRID_W)
    cs = np.clip(c - WIN_C // 2, 0, GRID_W - WIN_C)
    kc = np.arange(GRID_W)
    valid = (kc[None, :] >= cs[:, None]) & (kc[None, :] < cs[:, None] + WIN_C)
    dc = np.clip(kc[None, :] - c[:, None] + (WIN_C - 1), 0, n_dc - 1)
    sel = np.zeros((2, n_dc, GRID_W, 2, GRID_W), np.float32)
    ci, ki = np.nonzero(valid)
    for p in range(2):
        sel[p, dc[ci, ki], ci, p, ki] = 1.0
    sel = jnp.asarray(sel.reshape(2 * n_dc, GRID_W * 2 * GRID_W))
    r = rpb.astype(F32)
    pair = jnp.stack([r[:, :n_pair], r[:, 1:]], axis=2)
    pair = jnp.transpose(pair, (1, 0, 2, 3)).reshape(n_pair * n_heads, 2 * n_dc)
    tab = jnp.dot(pair, sel, precision=lax.Precision.HIGHEST)
    tab = tab.reshape(n_pair, n_heads, GRID_W, 2 * GRID_W)
    valid2 = jnp.asarray(np.tile(valid[:, None, :], (1, 2, 1)).reshape(GRID_W, 2 * GRID_W))
    tab = jnp.where(valid2[None, None], tab, NEG_BIG)
    return tab.reshape(n_pair, n_heads // heads_per_group, heads_per_group * GRID_W, 2 * GRID_W)


def _attention(h, bias_tab, images, *, q_block, n_heads, head_dim):
    t = h.shape[0]
    blk = ATT_ROWS * GRID_W
    aw = n_heads * head_dim
    heads_per_group = 2 * LANES // head_dim
    prev, nxt, kst, off = _attn_tables(images)
    n_blocks = t // blk
    assert prev.shape[0] == n_blocks
    kcol, vcol = q_block + 1, q_block + 2
    spec = lambda col, which: pl.BlockSpec(
        (blk, aw), {"cur": lambda b, p, n, k, o: (b, col),
                    "prev": lambda b, p, n, k, o: (p[b], col),
                    "next": lambda b, p, n, k, o: (n[b], col)}[which])
    grid_spec = pltpu.PrefetchScalarGridSpec(
        num_scalar_prefetch=4,
        grid=(n_blocks,),
        in_specs=[spec(q_block, "cur"),
                  spec(kcol, "prev"), spec(kcol, "cur"), spec(kcol, "next"),
                  spec(vcol, "prev"), spec(vcol, "cur"), spec(vcol, "next"),
                  pl.BlockSpec(bias_tab.shape, lambda b, p, n, k, o: (0, 0, 0, 0),
                               pipeline_mode=pl.Buffered(1))],
        out_specs=pl.BlockSpec((blk, aw), lambda b, p, n, k, o: (b, 0)),
        scratch_shapes=[pltpu.VMEM((3 * blk, aw), BF16), pltpu.VMEM((3 * blk, aw), BF16)],
    )
    return pl.pallas_call(
        functools.partial(_attn_body, heads_per_group=heads_per_group, head_dim=head_dim),
        grid_spec=grid_spec,
        out_shape=jax.ShapeDtypeStruct((t, aw), BF16),
        compiler_params=_cparams(("arbitrary",)),
        name="natten",
    )(jnp.asarray(prev), jnp.asarray(nxt), jnp.asarray(kst), jnp.asarray(off),
      h, h, h, h, h, h, h, bias_tab)


def _layer_norm(r, g, b):
    mu = jnp.mean(r, axis=-1, keepdims=True)
    c = r - mu
    var = jnp.mean(c * c, axis=-1, keepdims=True)
    return c * lax.rsqrt(var + LN_EPS) * g + b


def _first_index_of(mask, idx, sentinel, axis):
    return jnp.min(jnp.where(mask, idx, sentinel), axis=axis, keepdims=True)


def _mixer_body(seq_lo_ref, seq_hi_ref,
                xa_ref, xs_ref, up_ref, uc_ref, un_ref, gp_ref, ga_ref, at_ref,
                pw_ref, pb_ref, ps_ref, wpb_ref, wab_ref, wo_ref, g1_ref, b1_ref,
                s13_ref, s2_ref, rwh_ref, rwl_ref, rb_ref,
                x1_ref, base_ref, idx_ref, gate_ref, rank_ref, cnt_ref,
                carry_ref, *, n_a, alpha, n_experts):
    i = pl.program_id(0)
    tm = xa_ref.shape[0]
    t0 = i * tm
    lo_seq = seq_lo_ref[i]
    hi_seq = seq_hi_ref[i]

    @pl.when(i == 0)
    def _():
        carry_ref[...] = jnp.zeros_like(carry_ref)

    u_ext = jnp.concatenate([up_ref[...], uc_ref[...], un_ref[...]], axis=0)
    ext = tm + 2 * POOL_HALO
    tok_r = t0 + lax.broadcasted_iota(I32, (tm, ext), 0)
    tok_c = t0 - POOL_HALO + lax.broadcasted_iota(I32, (tm, ext), 1)
    tok_q = t0 + lax.broadcasted_iota(I32, (tm, 1), 0)
    n_pg = pw_ref.shape[0]
    pg = pw_ref.shape[1]
    parts = []
    for g in range(n_pg):
        w = POOL_WINDOWS[g]
        lo = jnp.maximum(tok_r - w // 2, lo_seq)
        hi = jnp.minimum(tok_r - w // 2 + w, hi_seq)
        band = jnp.where((tok_c >= lo) & (tok_c < hi), 1.0, 0.0).astype(BF16)
        cnt = (jnp.minimum(tok_q - w // 2 + w, hi_seq) - jnp.maximum(tok_q - w // 2, lo_seq)).astype(F32)
        ug = u_ext[:, g * pg:(g + 1) * pg]
        sums = jnp.dot(band, ug, preferred_element_type=F32)
        z = sums / cnt - uc_ref[:, g * pg:(g + 1) * pg].astype(F32)
        y = jnp.dot(z.astype(BF16), pw_ref[g], preferred_element_type=F32)
        y = (y + pb_ref[:, g * pg:(g + 1) * pg]) * ps_ref[:, g * pg:(g + 1) * pg]
        parts.append(y.astype(BF16))
    a = jnp.concatenate(parts, axis=1)
    a_pool = jnp.dot(a, wpb_ref[...], preferred_element_type=F32)
    a_attn = jnp.dot(at_ref[...], wab_ref[...], preferred_element_type=F32)
    merged = gp_ref[...].astype(F32) * a_pool + ga_ref[...].astype(F32) * a_attn
    mix = jnp.dot(merged.astype(BF16), wo_ref[...], preferred_element_type=F32)
    x_in = jnp.where(i < n_a, xa_ref[...], xs_ref[...])
    x1 = _layer_norm(alpha * x_in + mix, g1_ref[...], b1_ref[...])
    n_lt = x1.shape[1] // LANES
    for c in range(n_lt):
        x1_ref[pl.ds(c, tm, stride=n_lt), :] = x1[:, c * LANES:(c + 1) * LANES]
    x1b = x1.astype(BF16)

    sd = s2_ref.shape[0]
    hc = jnp.dot(x1b, s13_ref[...], preferred_element_type=F32)
    hdn = jax.nn.silu(hc[:, :sd]) * hc[:, sd:]
    shared = jnp.dot(hdn.astype(BF16), s2_ref[...], preferred_element_type=F32)
    base_ref[...] = alpha * x1 + shared

    x1l = (x1 - x1b.astype(F32)).astype(BF16)
    nt = (((1,), (1,)), ((), ()))
    logits = (lax.dot_general(rwh_ref[...], x1b, nt, preferred_element_type=F32)
              + lax.dot_general(rwh_ref[...], x1l, nt, preferred_element_type=F32)
              + lax.dot_general(rwl_ref[...], x1b, nt, preferred_element_type=F32))
    scores = jax.nn.sigmoid(logits)
    biased = scores + rb_ref[...]
    gsz = n_experts // N_GROUPS
    neg_inf = -jnp.inf

    mem = lax.broadcasted_iota(I32, (gsz, tm), 0)
    gs_rows = []
    for g in range(N_GROUPS):
        bg = biased[g * gsz:(g + 1) * gsz, :]
        m1 = jnp.max(bg, axis=0, keepdims=True)
        first = _first_index_of(bg == m1, mem, gsz, 0)
        m2 = jnp.max(jnp.where(mem == first, neg_inf, bg), axis=0, keepdims=True)
        gs_rows.append(m1 + m2)
    gwork = jnp.concatenate(gs_rows, axis=0)

    gidx = lax.broadcasted_iota(I32, (N_GROUPS, tm), 0)
    egrp = lax.broadcasted_iota(I32, (n_experts, tm), 0) // gsz
    eidx = lax.broadcasted_iota(I32, (n_experts, tm), 0)
    work = jnp.full((n_experts, tm), neg_inf, F32)
    for _ in range(TOPK_GROUPS):
        gm = jnp.max(gwork, axis=0, keepdims=True)
        gf = _first_index_of(gwork == gm, gidx, N_GROUPS, 0)
        gwork = jnp.where(gidx == gf, neg_inf, gwork)
        work = jnp.where(egrp == gf, biased, work)

    hot = jnp.zeros((n_experts, tm), F32)
    sel_idx, sel_w, sel_hit = [], [], []
    for _ in range(TOP_K):
        em = jnp.max(work, axis=0, keepdims=True)
        ef = _first_index_of(work == em, eidx, n_experts, 0)
        hit = eidx == ef
        sel_idx.append(ef)
        sel_w.append(jnp.sum(jnp.where(hit, scores, 0.0), axis=0, keepdims=True))
        sel_hit.append(hit)
        hot = jnp.where(hit, 1.0, hot)
        work = jnp.where(hit, neg_inf, work)
    wsel = jnp.concatenate(sel_w, axis=0)
    gate_ref[...] = wsel / jnp.sum(wsel, axis=0, keepdims=True) * ROUTED_SCALE
    idx_ref[...] = jnp.concatenate(sel_idx, axis=0)

    tri = jnp.where(lax.broadcasted_iota(I32, (tm, tm), 0) < lax.broadcasted_iota(I32, (tm, tm), 1),
                    1.0, 0.0).astype(BF16)
    before = carry_ref[:, 0:1] + jnp.dot(hot.astype(BF16), tri, preferred_element_type=F32)
    rank = [jnp.sum(jnp.where(h, before, 0.0), axis=0, keepdims=True) for h in sel_hit]
    rank_ref[...] = jnp.concatenate(rank, axis=0).astype(I32)
    carry_ref[...] = carry_ref[...] + jnp.sum(hot, axis=1, keepdims=True)
    cnt_ref[...] = carry_ref[...]


def _mixer(xa, xs, h, attn, seq_lo, seq_hi, p, *, alpha, u_block, gate_block0):
    d = xa.shape[1]
    t = xa.shape[0] + xs.shape[0]
    tm = MIX_TM
    assert xa.shape[0] % tm == 0 and xs.shape[0] % tm == 0
    n_a = xa.shape[0] // tm
    map_a, map_s = _two_array_maps(n_a)
    pw = p["wpb"].shape[0]
    aw = attn.shape[1]
    n_experts = p["rwh"].shape[0]
    halo_per_tile = tm // POOL_HALO
    n_halo_blocks = t // POOL_HALO
    gp_col = gate_block0 * INPROJ_TN // d
    in_specs = [
        pl.BlockSpec((tm, d), map_a),
        pl.BlockSpec((tm, d), map_s),
        pl.BlockSpec((POOL_HALO, pw), lambda i, lo, hi: (jnp.maximum(i * halo_per_tile - 1, 0), u_block)),
        pl.BlockSpec((tm, pw), lambda i, lo, hi: (i, u_block)),
        pl.BlockSpec((POOL_HALO, pw),
                     lambda i, lo, hi: (jnp.minimum((i + 1) * halo_per_tile, n_halo_blocks - 1), u_block)),
        pl.BlockSpec((tm, d), lambda i, lo, hi: (i, gp_col)),
        pl.BlockSpec((tm, d), lambda i, lo, hi: (i, gp_col + 1)),
        pl.BlockSpec((tm, aw), lambda i, lo, hi: (i, 0)),
    ]
    weights = [p["pool_w"], p["pool_b"], p["pool_scale"], p["wpb"], p["wab"], p["wo"], p["ln1_g"], p["ln1_b"],
               p["s13"], p["s2"], p["rwh"], p["rwl"], p["rb"]]
    in_specs += [pl.BlockSpec(w.shape, (lambda nd: (lambda i, lo, hi: (0,) * nd))(w.ndim),
                              pipeline_mode=pl.Buffered(1)) for w in weights]
    n_lt = d // LANES
    out_shape = (jax.ShapeDtypeStruct((t * n_lt, LANES), F32),
                 jax.ShapeDtypeStruct((t, d), F32),
                 jax.ShapeDtypeStruct((TOP_K, t), I32),
                 jax.ShapeDtypeStruct((TOP_K, t), F32),
                 jax.ShapeDtypeStruct((TOP_K, t), I32),
                 jax.ShapeDtypeStruct((n_experts, LANES), F32))
    out_specs = (pl.BlockSpec((tm * n_lt, LANES), lambda i, lo, hi: (i, 0)),
                 pl.BlockSpec((tm, d), lambda i, lo, hi: (i, 0)),
                 pl.BlockSpec((TOP_K, tm), lambda i, lo, hi: (0, i)),
                 pl.BlockSpec((TOP_K, tm), lambda i, lo, hi: (0, i)),
                 pl.BlockSpec((TOP_K, tm), lambda i, lo, hi: (0, i)),
                 pl.BlockSpec((n_experts, LANES), lambda i, lo, hi: (0, 0)))
    grid_spec = pltpu.PrefetchScalarGridSpec(
        num_scalar_prefetch=2, grid=(t // tm,), in_specs=in_specs, out_specs=out_specs,
        scratch_shapes=[pltpu.VMEM((n_experts, LANES), F32)])
    return pl.pallas_call(
        functools.partial(_mixer_body, n_a=n_a, alpha=alpha, n_experts=n_experts),
        grid_spec=grid_spec, out_shape=out_shape,
        compiler_params=_cparams(("arbitrary",)),
        name="mixer",
    )(seq_lo, seq_hi, xa, xs, h, h, h, h, h, attn, *weights)


SUBLANES = 8
MXU_N = 256


def _tile_row_view(x):
    r, d = x.shape
    return x.reshape(r // SUBLANES, SUBLANES, d)


def _row_src(x3, row):
    return x3.at[lax.shift_right_logical(row, 3), pl.ds(row & (SUBLANES - 1), 1)]


SC_CORES = 2
SC_SUBCORES = 16
SC_CHUNK = 32


def _sc_dispatch(x1t3, dest3, n_rows):
    t, n_lt, _ = x1t3.shape
    n_workers = SC_CORES * SC_SUBCORES
    assert t % (n_workers * SC_CHUNK) == 0
    chunks_per_worker = t // (n_workers * SC_CHUNK)
    mesh = plsc.VectorSubcoreMesh(core_axis_name="c", subcore_axis_name="s",
                                  num_cores=SC_CORES, num_subcores=SC_SUBCORES)

    @functools.partial(
        pl.kernel, mesh=mesh,
        out_type=jax.ShapeDtypeStruct((n_rows, n_lt, LANES), F32),
        scratch_types=[pltpu.VMEM((TOP_K, SC_CHUNK), I32),
                       pltpu.VMEM((SC_CHUNK, n_lt, LANES), F32),
                       pltpu.SemaphoreType.DMA],
        name="sc_dispatch")
    def dispatch(x_hbm, dest_hbm, out_hbm, idx_v, rows_v, sem):
        wid = lax.axis_index("s") * SC_CORES + lax.axis_index("c")

        @pl.loop(0, chunks_per_worker)
        def _(j):
            chunk = wid * chunks_per_worker + j
            base = pl.multiple_of(chunk * SC_CHUNK, SC_CHUNK)
            pltpu.sync_copy(x_hbm.at[pl.ds(base, SC_CHUNK)], rows_v)
            pltpu.sync_copy(dest_hbm.at[chunk], idx_v)
            copies = [pltpu.make_async_copy(rows_v, out_hbm.at[idx_v.at[k]], sem) for k in range(TOP_K)]
            for cp in copies:
                cp.start()
            for cp in copies:
                cp.wait()

    return dispatch(x1t3, dest3)


def _experts_body(be_ref, nact_ref, first_ref, nxte_ref, wslot_ref, valid_ref,
                  xs_ref, w1_hbm, w3_hbm, w2_hbm, o_ref,
                  wst1, wst3, wst2, w13b, w2b, sem_w):
    b = pl.program_id(0)
    nact = nact_ref[0]
    ed = wst2.shape[1]
    d = wst2.shape[2]
    n_lt = d // LANES
    bm = xs_ref.shape[0] // n_lt
    active = b < nact

    def weight_copies(e, s):
        return (pltpu.make_async_copy(w1_hbm.at[e], wst1.at[s], sem_w.at[s]),
                pltpu.make_async_copy(w3_hbm.at[e], wst3.at[s], sem_w.at[s]),
                pltpu.make_async_copy(w2_hbm.at[e], wst2.at[s], sem_w.at[s]))

    @pl.when(b == 0)
    def _():
        for cp in weight_copies(be_ref[0], wslot_ref[0]):
            cp.start()

    @pl.when(first_ref[b] == 1)
    def _():
        ws = wslot_ref[b]
        for cp in weight_copies(be_ref[b], ws):
            cp.wait()

        @pl.when(nxte_ref[b] >= 0)
        def _():
            for cp in weight_copies(nxte_ref[b], 1 - ws):
                cp.start()

        for c in range(ed // MXU_N):
            w13b[:, 2 * c * MXU_N:(2 * c + 1) * MXU_N] = wst1[ws, :, c * MXU_N:(c + 1) * MXU_N].astype(BF16)
            w13b[:, (2 * c + 1) * MXU_N:(2 * c + 2) * MXU_N] = wst3[ws, :, c * MXU_N:(c + 1) * MXU_N].astype(BF16)
        w2b[...] = wst2[ws].astype(BF16)

    @pl.when(active)
    def _():
        keep = lax.broadcasted_iota(I32, (bm, LANES), 0) < valid_ref[b]
        xb = jnp.concatenate(
            [jnp.where(keep, xs_ref[pl.ds(c, bm, stride=n_lt), :], 0.0).astype(BF16) for c in range(n_lt)], axis=1)
        parts = []
        for c in range(ed // MXU_N):
            hc = jnp.dot(xb, w13b[:, 2 * c * MXU_N:(2 * c + 2) * MXU_N], preferred_element_type=F32)
            parts.append((jax.nn.silu(hc[:, :MXU_N]) * hc[:, MXU_N:]).astype(BF16))
        hdn = jnp.concatenate(parts, axis=1)
        for c in range(d // MXU_N):
            o_ref[:, c * MXU_N:(c + 1) * MXU_N] = jnp.dot(hdn, w2b[:, c * MXU_N:(c + 1) * MXU_N],
                                                          preferred_element_type=F32)

    @pl.when(jnp.logical_not(active))
    def _():
        o_ref[...] = jnp.zeros_like(o_ref)


def _experts(xs2d, block_e, nact, first, nxt_e, wslot, valid, w1, w3, w2, bm):
    d, ed = w1.shape[1], w1.shape[2]
    n_lt = d // LANES
    n_blk = block_e.shape[0]
    assert xs2d.shape == (n_blk * bm * n_lt, LANES)
    assert ed % MXU_N == 0 and d % MXU_N == 0 and bm % SUBLANES == 0
    grid_spec = pltpu.PrefetchScalarGridSpec(
        num_scalar_prefetch=6,
        grid=(n_blk,),
        in_specs=[pl.BlockSpec((bm * n_lt, LANES),
                               lambda b, be, na, *_: (jnp.maximum(jnp.minimum(b, na[0] - 1), 0), 0)),
                  pl.BlockSpec(memory_space=pl.ANY),
                  pl.BlockSpec(memory_space=pl.ANY),
                  pl.BlockSpec(memory_space=pl.ANY)],
        out_specs=pl.BlockSpec((bm, d), lambda b, *_: (b, 0)),
        scratch_shapes=[pltpu.VMEM((2, d, ed), F32),
                        pltpu.VMEM((2, d, ed), F32),
                        pltpu.VMEM((2, ed, d), F32),
                        pltpu.VMEM((d, 2 * ed), BF16),
                        pltpu.VMEM((ed, d), BF16),
                        pltpu.SemaphoreType.DMA((2,))],
    )
    return pl.pallas_call(
        _experts_body,
        grid_spec=grid_spec,
        out_shape=jax.ShapeDtypeStruct((n_blk * bm, d), F32),
        compiler_params=_cparams(("arbitrary",)),
        name="experts",
    )(block_e, nact, first, nxt_e, wslot, valid, xs2d, w1, w3, w2)


def _combine_body(dest_cur_ref, dest_nxt_ref, base_ref, gate_ref, g2_ref, b2_ref, yb3_hbm, oa_ref, os_ref,
                  ybuf, sem, *, n_a):
    i = pl.program_id(0)
    n_steps = pl.num_programs(0)
    tm = base_ref.shape[0]
    d = base_ref.shape[1]
    slot = i % 2

    def start_rows(dest_ref, s):
        def body(r_hi, c):
            for u in range(SUBLANES):
                for k in range(TOP_K):
                    row = dest_ref[k, r_hi * SUBLANES + u]
                    pltpu.make_async_copy(_row_src(yb3_hbm, row), ybuf.at[s, k, r_hi, pl.ds(u, 1)],
                                          sem.at[s]).start()
            return c
        lax.fori_loop(0, tm // SUBLANES, body, 0)

    @pl.when(i == 0)
    def _():
        start_rows(dest_cur_ref, 0)

    @pl.when(i + 1 < n_steps)
    def _():
        start_rows(dest_nxt_ref, 1 - slot)

    for k in range(TOP_K):
        pltpu.make_async_copy(yb3_hbm.at[pl.ds(0, tm // SUBLANES)], ybuf.at[slot, k], sem.at[slot]).wait()
    acc = base_ref[...]
    for k in range(TOP_K):
        acc = acc + gate_ref[:, k:k + 1] * ybuf[slot, k].reshape(tm, d)
    y = _layer_norm(acc, g2_ref[...], b2_ref[...])

    @pl.when(i < n_a)
    def _():
        oa_ref[...] = y

    @pl.when(i >= n_a)
    def _():
        os_ref[...] = y


def _combine(base, dest, gate_t, yb, ln_g, ln_b, t_a):
    t, d = base.shape
    tm = COMBINE_TM
    assert t_a % tm == 0 and 0 < t_a < t
    n_a = t_a // tm
    n_steps = t // tm
    map_a, map_s = _two_array_maps(n_a)
    return pl.pallas_call(
        functools.partial(_combine_body, n_a=n_a),
        grid=(n_steps,),
        in_specs=[pl.BlockSpec((TOP_K, tm), lambda i: (0, i), memory_space=pltpu.SMEM),
                  pl.BlockSpec((TOP_K, tm), lambda i: (0, jnp.minimum(i + 1, n_steps - 1)),
                               memory_space=pltpu.SMEM),
                  pl.BlockSpec((tm, d), lambda i: (i, 0)),
                  pl.BlockSpec((tm, TOP_K), lambda i: (i, 0)),
                  pl.BlockSpec((1, d), lambda i: (0, 0)),
                  pl.BlockSpec((1, d), lambda i: (0, 0)),
                  pl.BlockSpec(memory_space=pl.ANY)],
        out_specs=(pl.BlockSpec((tm, d), map_a), pl.BlockSpec((tm, d), map_s)),
        out_shape=(jax.ShapeDtypeStruct((t_a, d), F32), jax.ShapeDtypeStruct((t - t_a, d), F32)),
        scratch_shapes=[pltpu.VMEM((2, TOP_K, tm // SUBLANES, SUBLANES, d), F32),
                        pltpu.SemaphoreType.DMA((2,))],
        compiler_params=_cparams(("arbitrary",)),
        name="combine",
    )(dest, dest, base, gate_t, ln_g, ln_b, _tile_row_view(yb))


def _split_hi_lo(w):
    hi = w.astype(BF16)
    lo = (w - hi.astype(F32)).astype(BF16)
    return hi, lo


def _encoder_layer(xa, xs, seqs, lp, alpha):
    (w_in, pool_w, pool_b, pool_scale, w_pool_br, w_attn_br, attn_rpb, w_out, ln1_g, ln1_b,
     router_w, router_bias, exp_w1, exp_w3, exp_w2, sh_w1, sh_w3, sh_w2, ln2_g, ln2_b) = lp
    d = xa.shape[1]
    t = xa.shape[0] + xs.shape[0]
    pw = w_pool_br.shape[0]
    aw = w_attn_br.shape[0]
    n_heads = attn_rpb.shape[0]
    head_dim = aw // n_heads
    n_experts = router_w.shape[1]
    assert pw == INPROJ_TN and aw == INPROJ_TN and d % INPROJ_TN == 0
    u_block, q_block = 0, pw // INPROJ_TN
    gate_block0 = (pw + 3 * aw) // INPROJ_TN

    h = _inproj(xa, xs, w_in.astype(BF16), q_block=q_block, gate_block0=gate_block0, q_scale=head_dim ** -0.5)

    blk = ATT_ROWS * GRID_W
    images, seq_lo, seq_hi = [], [], []
    for s0, s1 in seqs:
        images.append((s0 // blk, (s1 - s0) // blk))
        for _ in range((s1 - s0) // MIX_TM):
            seq_lo.append(s0)
            seq_hi.append(s1)
    bias_tab = _attn_bias_table(attn_rpb, 2 * LANES // head_dim)
    attn = _attention(h, bias_tab, images, q_block=q_block, n_heads=n_heads, head_dim=head_dim)

    rwh, rwl = _split_hi_lo(router_w.T)
    p = dict(pool_w=pool_w.astype(BF16), pool_b=pool_b.reshape(1, pw), pool_scale=pool_scale.reshape(1, pw),
             wpb=w_pool_br.astype(BF16), wab=w_attn_br.astype(BF16), wo=w_out.astype(BF16),
             ln1_g=ln1_g.reshape(1, d), ln1_b=ln1_b.reshape(1, d),
             s13=jnp.concatenate([sh_w1, sh_w3], axis=1).astype(BF16), s2=sh_w2.astype(BF16),
             rwh=rwh, rwl=rwl, rb=router_bias.reshape(n_experts, 1).astype(F32))
    x1, base, idx_t, gate_t, rank_t, counts = _mixer(
        xa, xs, h, attn, jnp.asarray(seq_lo, I32), jnp.asarray(seq_hi, I32), p,
        alpha=alpha, u_block=u_block, gate_block0=gate_block0)

    bm = EXPERT_BM
    cnt = counts[:, 0].astype(I32)
    pcnt = (cnt + bm - 1) // bm * bm
    pend = jnp.cumsum(pcnt)
    pstart = pend - pcnt
    eids = jnp.arange(n_experts, dtype=I32)[:, None, None]
    dest = jnp.sum(jnp.where(idx_t[None] == eids, pstart[:, None, None], 0), axis=0) + rank_t
    n_blk = (t * TOP_K + n_experts * (bm - 1)) // bm
    blk_row = jnp.arange(n_blk, dtype=I32) * bm
    block_e = jnp.minimum(jnp.sum((pend[None, :] <= blk_row[:, None]).astype(I32), axis=1), n_experts - 1)
    nact = (pend[-1] // bm).astype(I32).reshape(1)
    block_e = jnp.where(jnp.arange(n_blk) < nact[0], block_e, block_e[jnp.maximum(nact[0] - 1, 0)])
    eids1 = jnp.arange(n_experts, dtype=I32)
    present = cnt > 0
    later = present[None, :] & (eids1[None, :] > eids1[:, None])
    nxt_of_e = jnp.min(jnp.where(later, eids1[None, :], n_experts), axis=1)
    nxt_of_e = jnp.where(nxt_of_e >= n_experts, -1, nxt_of_e).astype(I32)
    ord_of_e = (jnp.cumsum(present.astype(I32)) - 1).astype(I32)
    onehot_be = (block_e[:, None] == eids1[None, :]).astype(I32)
    blk_ids = jnp.arange(n_blk, dtype=I32)
    is_act = blk_ids < nact[0]
    prev_e = jnp.concatenate([jnp.full((1,), -1, I32), block_e[:-1]])
    first = (is_act & (block_e != prev_e)).astype(I32)
    nxt_e = jnp.sum(onehot_be * nxt_of_e[None, :], axis=1).astype(I32)
    wslot = (jnp.sum(onehot_be * ord_of_e[None, :], axis=1) % 2).astype(I32)
    start_of_e = jnp.sum(onehot_be * pstart[None, :], axis=1)
    cnt_of_e = jnp.sum(onehot_be * cnt[None, :], axis=1)
    valid = jnp.where(is_act, jnp.clip(cnt_of_e - (blk_row - start_of_e), 0, bm), 0).astype(I32)
    n_lt = d // LANES
    dest3 = jnp.transpose(dest.reshape(TOP_K, t // SC_CHUNK, SC_CHUNK), (1, 0, 2))
    xs = _sc_dispatch(x1.reshape(t, n_lt, LANES), dest3, n_blk * bm)
    yb = _experts(xs.reshape(n_blk * bm * n_lt, LANES), block_e, nact, first, nxt_e, wslot, valid,
                  exp_w1, exp_w3, exp_w2, bm)

    return _combine(base, dest, gate_t.T, yb, ln2_g.reshape(1, d), ln2_b.reshape(1, d), xa.shape[0])


def kernel(x_prompt, x_sample, w_in, pool_w, pool_b, pool_scale, w_pool_br, w_attn_br, attn_rpb, w_out,
           ln1_g, ln1_b, router_w, router_bias, exp_w1, exp_w3, exp_w2, sh_w1, sh_w3, sh_w2, ln2_g, ln2_b):
    depth = w_in.shape[0]
    alpha = (2.0 * depth) ** 0.25
    bp, sp, d = x_prompt.shape
    bs, ss, _ = x_sample.shape
    seqs = [(i * sp, (i + 1) * sp) for i in range(bp)]
    seqs += [(bp * sp + i * ss, bp * sp + (i + 1) * ss) for i in range(bs)]
    xa, xs = x_prompt.reshape(bp * sp, d), x_sample.reshape(bs * ss, d)
    for l in range(depth):
        lp = (w_in[l], pool_w[l], pool_b[l], pool_scale[l], w_pool_br[l], w_attn_br[l], attn_rpb[l], w_out[l],
              ln1_g[l], ln1_b[l], router_w[l], router_bias[l], exp_w1[l], exp_w3[l], exp_w2[l],
              sh_w1[l], sh_w3[l], sh_w2[l], ln2_g[l], ln2_b[l])
        xa, xs = _encoder_layer(xa, xs, seqs, lp, alpha)
    return (xa.reshape(bp, sp, d), xs.reshape(bs, ss, d))
```

```python
import functools

import numpy as np
import jax
import jax.numpy as jnp
from jax import lax
from jax.experimental import pallas as pl
from jax.experimental.pallas import tpu as pltpu
from jax.experimental.pallas import tpu_sc as plsc

F32 = jnp.float32
BF16 = jnp.bfloat16
I32 = jnp.int32

GRID_W = 64
POOL_WINDOWS = (2, 4, 8, 16)
WIN_R = 8
WIN_C = 16
TOP_K = 8
N_GROUPS = 8
TOPK_GROUPS = 4
ROUTED_SCALE = 2.5
LN_EPS = 1e-5
NEG_BIG = -1e30

LANES = 128
VMEM_LIMIT = 56 * 1024 * 1024

INPROJ_TM = 1024
INPROJ_TN = 1024
ATT_ROWS = 8
MIX_TM = 256
POOL_HALO = 16
EXPERT_BM = 256
COMBINE_TM = 128


def _cparams(sem):
    return pltpu.CompilerParams(dimension_semantics=sem, vmem_limit_bytes=VMEM_LIMIT)


def _inproj_body(xa_ref, xs_ref, w_ref, o_ref, xb_ref, *, n_a, q_block, gate_block0, q_scale):
    i = pl.program_id(0)
    j = pl.program_id(1)

    @pl.when((j == 0) & (i < n_a))
    def _():
        xb_ref[...] = xa_ref[...].astype(BF16)

    @pl.when((j == 0) & (i >= n_a))
    def _():
        xb_ref[...] = xs_ref[...].astype(BF16)

    acc = jnp.dot(xb_ref[...], w_ref[...], preferred_element_type=F32)
    is_gate = j >= gate_block0

    @pl.when(is_gate)
    def _():
        o_ref[...] = jax.nn.sigmoid(acc).astype(BF16)

    @pl.when(jnp.logical_not(is_gate))
    def _():
        scale = jnp.where(j == q_block, q_scale, 1.0).astype(F32)
        o_ref[...] = (acc * scale).astype(BF16)


def _two_array_maps(n_a):
    return (lambda i, *_: (jnp.minimum(i, n_a - 1), 0)), (lambda i, *_: (jnp.maximum(i - n_a, 0), 0))


def _inproj(xa, xs, w_bf16, *, q_block, gate_block0, q_scale):
    d = xa.shape[1]
    t = xa.shape[0] + xs.shape[0]
    n = w_bf16.shape[1]
    tm, tn = INPROJ_TM, INPROJ_TN
    assert xa.shape[0] % tm == 0 and xs.shape[0] % tm == 0 and xa.shape[0] > 0 and xs.shape[0] > 0
    n_a = xa.shape[0] // tm
    map_a, map_s = _two_array_maps(n_a)
    return pl.pallas_call(
        functools.partial(_inproj_body, n_a=n_a, q_block=q_block, gate_block0=gate_block0, q_scale=q_scale),
        grid=(t // tm, n // tn),
        in_specs=[pl.BlockSpec((tm, d), map_a),
                  pl.BlockSpec((tm, d), map_s),
                  pl.BlockSpec((d, tn), lambda i, j: (0, j))],
        out_specs=pl.BlockSpec((tm, tn), lambda i, j: (i, j)),
        out_shape=jax.ShapeDtypeStruct((t, n), BF16),
        scratch_shapes=[pltpu.VMEM((tm, d), BF16)],
        compiler_params=_cparams(("arbitrary", "arbitrary")),
        name="inproj",
    )(xa, xs, w_bf16)


def _attn_body(prev_ref, next_ref, kst_ref, off_ref,
               q_ref, kp_ref, kc_ref, kn_ref, vp_ref, vc_ref, vn_ref, bias_ref, o_ref,
               kbuf, vbuf, *, heads_per_group, head_dim):
    del prev_ref, next_ref
    b = pl.program_id(0)
    blk = ATT_ROWS * GRID_W
    kbuf[0:blk] = kp_ref[...]
    kbuf[blk:2 * blk] = kc_ref[...]
    kbuf[2 * blk:3 * blk] = kn_ref[...]
    vbuf[0:blk] = vp_ref[...]
    vbuf[blk:2 * blk] = vc_ref[...]
    vbuf[2 * blk:3 * blk] = vn_ref[...]

    gw = heads_per_group * head_dim
    n_groups = q_ref.shape[1] // gw
    win = WIN_R * GRID_W
    nq = heads_per_group * GRID_W
    row_head = lax.broadcasted_iota(I32, (nq, gw), 0) // GRID_W
    lane_head = lax.broadcasted_iota(I32, (nq, gw), 1) // head_dim
    head_mask = row_head == lane_head
    out_lane_head = lax.broadcasted_iota(I32, (GRID_W, gw), 1) // head_dim

    def row_step(i, carry):
        st = pl.multiple_of(kst_ref[b * ATT_ROWS + i] * GRID_W, GRID_W)
        dr0 = (WIN_R - 1) - off_ref[b * ATT_ROWS + i]
        q0 = pl.multiple_of(i * GRID_W, GRID_W)
        for g in range(n_groups):
            qg = q_ref[pl.ds(q0, GRID_W), g * gw:(g + 1) * gw]
            kg = kbuf[pl.ds(st, win), g * gw:(g + 1) * gw]
            vg = vbuf[pl.ds(st, win), g * gw:(g + 1) * gw]
            wt = jnp.concatenate([qg] * heads_per_group, axis=0)
            wt = jnp.where(head_mask, wt, jnp.zeros_like(wt))
            s = lax.dot_general(wt, kg, (((1,), (1,)), ((), ())),
                                preferred_element_type=F32)
            n_chunk = win // LANES
            sj = [s[:, j * LANES:(j + 1) * LANES] + bias_ref[dr0 + 2 * j, g] for j in range(n_chunk)]
            m = sj[0]
            for j in range(1, n_chunk):
                m = jnp.maximum(m, sj[j])
            m = jnp.max(m, axis=1, keepdims=True)
            ej = [jnp.exp(x - m) for x in sj]
            l = ej[0]
            for j in range(1, n_chunk):
                l = l + ej[j]
            inv = 1.0 / jnp.sum(l, axis=1, keepdims=True)
            p = jnp.concatenate([(x * inv).astype(BF16) for x in ej], axis=1)
            of = jnp.dot(p, vg, preferred_element_type=F32)
            out = jnp.zeros((GRID_W, gw), F32)
            for h in range(heads_per_group):
                out = out + jnp.where(out_lane_head == h, of[h * GRID_W:(h + 1) * GRID_W, :], 0.0)
            o_ref[pl.ds(q0, GRID_W), g * gw:(g + 1) * gw] = out.astype(o_ref.dtype)
        return carry

    lax.fori_loop(0, ATT_ROWS, row_step, 0)


def _attn_tables(images):
    prev, nxt, kst, off = [], [], [], []
    for s, n in images:
        rows = n * ATT_ROWS
        kr = min(WIN_R, rows)
        assert kr == WIN_R
        for bl in range(n):
            b = s + bl
            prev.append(max(b - 1, s))
            nxt.append(min(b + 1, s + n - 1))
            for i in range(ATT_ROWS):
                r = bl * ATT_ROWS + i
                rs = min(max(r - kr // 2, 0), rows - kr)
                kst.append(rs - bl * ATT_ROWS + ATT_ROWS)
                off.append(r - rs)
    return (np.asarray(prev, np.int32), np.asarray(nxt, np.int32),
            np.asarray(kst, np.int32), np.asarray(off, np.int32))


def _attn_bias_table(rpb, heads_per_group):
    n_heads, n_dr, n_dc = rpb.shape
    n_pair = n_dr - 1
    c = np.arange(GRID_W)
    cs = np.clip(c - WIN_C // 2, 0, GRID_W - WIN_C)
    kc = np.arange(GRID_W)
    valid = (kc[None, :] >= cs[:, None]) & (kc[None, :] < cs[:, None] + WIN_C)
    dc = np.clip(kc[None, :] - c[:, None] + (WIN_C - 1), 0, n_dc - 1)
    sel = np.zeros((2, n_dc, GRID_W, 2, GRID_W), np.float32)
    ci, ki = np.nonzero(valid)
    for p in range(2):
        sel[p, dc[ci, ki], ci, p, ki] = 1.0
    sel = jnp.asarray(sel.reshape(2 * n_dc, GRID_W * 2 * GRID_W))
    r = rpb.astype(F32)
    pair = jnp.stack([r[:, :n_pair], r[:, 1:]], axis=2)
    pair = jnp.transpose(pair, (1, 0, 2, 3)).reshape(n_pair * n_heads, 2 * n_dc)
    tab = jnp.dot(pair, sel, precision=lax.Precision.HIGHEST)
    tab = tab.reshape(n_pair, n_heads, GRID_W, 2 * GRID_W)
    valid2 = jnp.asarray(np.tile(valid[:, None, :], (1, 2, 1)).reshape(GRID_W, 2 * GRID_W))
    tab = jnp.where(valid2[None, None], tab, NEG_BIG)
    return tab.reshape(n_pair, n_heads // heads_per_group, heads_per_group * GRID_W, 2 * GRID_W)


def _attention(h, bias_tab, images, *, q_block, n_heads, head_dim):
    t = h.shape[0]
    blk = ATT_ROWS * GRID_W
    aw = n_heads * head_dim
    heads_per_group = 2 * LANES // head_dim
    prev, nxt, kst, off = _attn_tables(images)
    n_blocks = t // blk
    assert prev.shape[0] == n_blocks
    kcol, vcol = q_block + 1, q_block + 2
    spec = lambda col, which: pl.BlockSpec(
        (blk, aw), {"cur": lambda b, p, n, k, o: (b, col),
                    "prev": lambda b, p, n, k, o: (p[b], col),
                    "next": lambda b, p, n, k, o: (n[b], col)}[which])
    grid_spec = pltpu.PrefetchScalarGridSpec(
        num_scalar_prefetch=4,
        grid=(n_blocks,),
        in_specs=[spec(q_block, "cur"),
                  spec(kcol, "prev"), spec(kcol, "cur"), spec(kcol, "next"),
                  spec(vcol, "prev"), spec(vcol, "cur"), spec(vcol, "next"),
                  pl.BlockSpec(bias_tab.shape, lambda b, p, n, k, o: (0, 0, 0, 0),
                               pipeline_mode=pl.Buffered(1))],
        out_specs=pl.BlockSpec((blk, aw), lambda b, p, n, k, o: (b, 0)),
        scratch_shapes=[pltpu.VMEM((3 * blk, aw), BF16), pltpu.VMEM((3 * blk, aw), BF16)],
    )
    return pl.pallas_call(
        functools.partial(_attn_body, heads_per_group=heads_per_group, head_dim=head_dim),
        grid_spec=grid_spec,
        out_shape=jax.ShapeDtypeStruct((t, aw), BF16),
        compiler_params=_cparams(("arbitrary",)),
        name="natten",
    )(jnp.asarray(prev), jnp.asarray(nxt), jnp.asarray(kst), jnp.asarray(off),
      h, h, h, h, h, h, h, bias_tab)


def _layer_norm(r, g, b):
    mu = jnp.mean(r, axis=-1, keepdims=True)
    c = r - mu
    var = jnp.mean(c * c, axis=-1, keepdims=True)
    return c * lax.rsqrt(var + LN_EPS) * g + b


HI_MASK = -65536


def _pack_bf16_pair(lo, hi):
    lo_bits = lax.bitcast_convert_type(lo.astype(BF16).astype(F32), I32)
    hi_bits = lax.bitcast_convert_type(hi.astype(BF16).astype(F32), I32)
    return (hi_bits & HI_MASK) | lax.shift_right_logical(lo_bits, 16)


def _unpack_bf16_pair(word):
    lo = lax.bitcast_convert_type(lax.shift_left(word, 16), F32)
    hi = lax.bitcast_convert_type(word & HI_MASK, F32)
    return lo, hi


def _first_index_of(mask, idx, sentinel, axis):
    return jnp.min(jnp.where(mask, idx, sentinel), axis=axis, keepdims=True)


def _mixer_body(seq_lo_ref, seq_hi_ref,
                xa_ref, xs_ref, up_ref, uc_ref, un_ref, gp_ref, ga_ref, at_ref,
                pw_ref, pb_ref, ps_ref, wpb_ref, wab_ref, wo_ref, g1_ref, b1_ref,
                s13_ref, s2_ref, rwh_ref, rwl_ref, rb_ref,
                x1_ref, base_ref, idx_ref, gate_ref, rank_ref, cnt_ref,
                carry_ref, *, n_a, alpha, n_experts):
    i = pl.program_id(0)
    tm = xa_ref.shape[0]
    t0 = i * tm
    lo_seq = seq_lo_ref[i]
    hi_seq = seq_hi_ref[i]

    @pl.when(i == 0)
    def _():
        carry_ref[...] = jnp.zeros_like(carry_ref)

    u_ext = jnp.concatenate([up_ref[...], uc_ref[...], un_ref[...]], axis=0)
    ext = tm + 2 * POOL_HALO
    tok_r = t0 + lax.broadcasted_iota(I32, (tm, ext), 0)
    tok_c = t0 - POOL_HALO + lax.broadcasted_iota(I32, (tm, ext), 1)
    tok_q = t0 + lax.broadcasted_iota(I32, (tm, 1), 0)
    n_pg = pw_ref.shape[0]
    pg = pw_ref.shape[1]
    parts = []
    for g in range(n_pg):
        w = POOL_WINDOWS[g]
        lo = jnp.maximum(tok_r - w // 2, lo_seq)
        hi = jnp.minimum(tok_r - w // 2 + w, hi_seq)
        band = jnp.where((tok_c >= lo) & (tok_c < hi), 1.0, 0.0).astype(BF16)
        cnt = (jnp.minimum(tok_q - w // 2 + w, hi_seq) - jnp.maximum(tok_q - w // 2, lo_seq)).astype(F32)
        ug = u_ext[:, g * pg:(g + 1) * pg]
        sums = jnp.dot(band, ug, preferred_element_type=F32)
        z = sums / cnt - uc_ref[:, g * pg:(g + 1) * pg].astype(F32)
        y = jnp.dot(z.astype(BF16), pw_ref[g], preferred_element_type=F32)
        y = (y + pb_ref[:, g * pg:(g + 1) * pg]) * ps_ref[:, g * pg:(g + 1) * pg]
        parts.append(y.astype(BF16))
    a = jnp.concatenate(parts, axis=1)
    a_pool = jnp.dot(a, wpb_ref[...], preferred_element_type=F32)
    a_attn = jnp.dot(at_ref[...], wab_ref[...], preferred_element_type=F32)
    merged = gp_ref[...].astype(F32) * a_pool + ga_ref[...].astype(F32) * a_attn
    mix = jnp.dot(merged.astype(BF16), wo_ref[...], preferred_element_type=F32)
    x_in = jnp.where(i < n_a, xa_ref[...], xs_ref[...])
    x1 = _layer_norm(alpha * x_in + mix, g1_ref[...], b1_ref[...])
    half = x1.shape[1] // 2
    n_pk = half // LANES
    word = _pack_bf16_pair(x1[:, :half], x1[:, half:])
    for c in range(n_pk):
        x1_ref[pl.ds(c, tm, stride=n_pk), :] = word[:, c * LANES:(c + 1) * LANES]
    x1b = x1.astype(BF16)

    sd = s2_ref.shape[0]
    hc = jnp.dot(x1b, s13_ref[...], preferred_element_type=F32)
    hdn = jax.nn.silu(hc[:, :sd]) * hc[:, sd:]
    shared = jnp.dot(hdn.astype(BF16), s2_ref[...], preferred_element_type=F32)
    base_ref[...] = alpha * x1 + shared

    x1l = (x1 - x1b.astype(F32)).astype(BF16)
    nt = (((1,), (1,)), ((), ()))
    logits = (lax.dot_general(rwh_ref[...], x1b, nt, preferred_element_type=F32)
              + lax.dot_general(rwh_ref[...], x1l, nt, preferred_element_type=F32)
              + lax.dot_general(rwl_ref[...], x1b, nt, preferred_element_type=F32))
    scores = jax.nn.sigmoid(logits)
    biased = scores + rb_ref[...]
    gsz = n_experts // N_GROUPS
    neg_inf = -jnp.inf

    mem = lax.broadcasted_iota(I32, (gsz, tm), 0)
    gs_rows = []
    for g in range(N_GROUPS):
        bg = biased[g * gsz:(g + 1) * gsz, :]
        m1 = jnp.max(bg, axis=0, keepdims=True)
        first = _first_index_of(bg == m1, mem, gsz, 0)
        m2 = jnp.max(jnp.where(mem == first, neg_inf, bg), axis=0, keepdims=True)
        gs_rows.append(m1 + m2)
    gwork = jnp.concatenate(gs_rows, axis=0)

    gidx = lax.broadcasted_iota(I32, (N_GROUPS, tm), 0)
    egrp = lax.broadcasted_iota(I32, (n_experts, tm), 0) // gsz
    eidx = lax.broadcasted_iota(I32, (n_experts, tm), 0)
    work = jnp.full((n_experts, tm), neg_inf, F32)
    for _ in range(TOPK_GROUPS):
        gm = jnp.max(gwork, axis=0, keepdims=True)
        gf = _first_index_of(gwork == gm, gidx, N_GROUPS, 0)
        gwork = jnp.where(gidx == gf, neg_inf, gwork)
        work = jnp.where(egrp == gf, biased, work)

    hot = jnp.zeros((n_experts, tm), F32)
    sel_idx, sel_w, sel_hit = [], [], []
    for _ in range(TOP_K):
        em = jnp.max(work, axis=0, keepdims=True)
        ef = _first_index_of(work == em, eidx, n_experts, 0)
        hit = eidx == ef
        sel_idx.append(ef)
        sel_w.append(jnp.sum(jnp.where(hit, scores, 0.0), axis=0, keepdims=True))
        sel_hit.append(hit)
        hot = jnp.where(hit, 1.0, hot)
        work = jnp.where(hit, neg_inf, work)
    wsel = jnp.concatenate(sel_w, axis=0)
    gate_ref[...] = wsel / jnp.sum(wsel, axis=0, keepdims=True) * ROUTED_SCALE
    idx_ref[...] = jnp.concatenate(sel_idx, axis=0)

    tri = jnp.where(lax.broadcasted_iota(I32, (tm, tm), 0) < lax.broadcasted_iota(I32, (tm, tm), 1),
                    1.0, 0.0).astype(BF16)
    before = carry_ref[:, 0:1] + jnp.dot(hot.astype(BF16), tri, preferred_element_type=F32)
    rank = [jnp.sum(jnp.where(h, before, 0.0), axis=0, keepdims=True) for h in sel_hit]
    rank_ref[...] = jnp.concatenate(rank, axis=0).astype(I32)
    carry_ref[...] = carry_ref[...] + jnp.sum(hot, axis=1, keepdims=True)
    cnt_ref[...] = carry_ref[...]


def _mixer(xa, xs, h, attn, seq_lo, seq_hi, p, *, alpha, u_block, gate_block0):
    d = xa.shape[1]
    t = xa.shape[0] + xs.shape[0]
    tm = MIX_TM
    assert xa.shape[0] % tm == 0 and xs.shape[0] % tm == 0
    n_a = xa.shape[0] // tm
    map_a, map_s = _two_array_maps(n_a)
    pw = p["wpb"].shape[0]
    aw = attn.shape[1]
    n_experts = p["rwh"].shape[0]
    halo_per_tile = tm // POOL_HALO
    n_halo_blocks = t // POOL_HALO
    gp_col = gate_block0 * INPROJ_TN // d
    in_specs = [
        pl.BlockSpec((tm, d), map_a),
        pl.BlockSpec((tm, d), map_s),
        pl.BlockSpec((POOL_HALO, pw), lambda i, lo, hi: (jnp.maximum(i * halo_per_tile - 1, 0), u_block)),
        pl.BlockSpec((tm, pw), lambda i, lo, hi: (i, u_block)),
        pl.BlockSpec((POOL_HALO, pw),
                     lambda i, lo, hi: (jnp.minimum((i + 1) * halo_per_tile, n_halo_blocks - 1), u_block)),
        pl.BlockSpec((tm, d), lambda i, lo, hi: (i, gp_col)),
        pl.BlockSpec((tm, d), lambda i, lo, hi: (i, gp_col + 1)),
        pl.BlockSpec((tm, aw), lambda i, lo, hi: (i, 0)),
    ]
    weights = [p["pool_w"], p["pool_b"], p["pool_scale"], p["wpb"], p["wab"], p["wo"], p["ln1_g"], p["ln1_b"],
               p["s13"], p["s2"], p["rwh"], p["rwl"], p["rb"]]
    in_specs += [pl.BlockSpec(w.shape, (lambda nd: (lambda i, lo, hi: (0,) * nd))(w.ndim),
                              pipeline_mode=pl.Buffered(1)) for w in weights]
    n_pk = d // (2 * LANES)
    out_shape = (jax.ShapeDtypeStruct((t * n_pk, LANES), I32),
                 jax.ShapeDtypeStruct((t, d), F32),
                 jax.ShapeDtypeStruct((TOP_K, t), I32),
                 jax.ShapeDtypeStruct((TOP_K, t), F32),
                 jax.ShapeDtypeStruct((TOP_K, t), I32),
                 jax.ShapeDtypeStruct((n_experts, LANES), F32))
    out_specs = (pl.BlockSpec((tm * n_pk, LANES), lambda i, lo, hi: (i, 0)),
                 pl.BlockSpec((tm, d), lambda i, lo, hi: (i, 0)),
                 pl.BlockSpec((TOP_K, tm), lambda i, lo, hi: (0, i)),
                 pl.BlockSpec((TOP_K, tm), lambda i, lo, hi: (0, i)),
                 pl.BlockSpec((TOP_K, tm), lambda i, lo, hi: (0, i)),
                 pl.BlockSpec((n_experts, LANES), lambda i, lo, hi: (0, 0)))
    grid_spec = pltpu.PrefetchScalarGridSpec(
        num_scalar_prefetch=2, grid=(t // tm,), in_specs=in_specs, out_specs=out_specs,
        scratch_shapes=[pltpu.VMEM((n_experts, LANES), F32)])
    return pl.pallas_call(
        functools.partial(_mixer_body, n_a=n_a, alpha=alpha, n_experts=n_experts),
        grid_spec=grid_spec, out_shape=out_shape,
        compiler_params=_cparams(("arbitrary",)),
        name="mixer",
    )(seq_lo, seq_hi, xa, xs, h, h, h, h, h, attn, *weights)


SUBLANES = 8
MXU_N = 256


def _tile_row_view(x):
    r, d = x.shape
    return x.reshape(r // SUBLANES, SUBLANES, d)


def _row_src(x3, row):
    return x3.at[lax.shift_right_logical(row, 3), pl.ds(row & (SUBLANES - 1), 1)]


SC_CORES = 2
SC_SUBCORES = 16
SC_CHUNK = 32


def _sc_dispatch(x1t3, dest3, n_rows):
    t, n_lt, _ = x1t3.shape
    n_workers = SC_CORES * SC_SUBCORES
    assert t % (n_workers * SC_CHUNK) == 0
    chunks_per_worker = t // (n_workers * SC_CHUNK)
    mesh = plsc.VectorSubcoreMesh(core_axis_name="c", subcore_axis_name="s",
                                  num_cores=SC_CORES, num_subcores=SC_SUBCORES)

    @functools.partial(
        pl.kernel, mesh=mesh,
        out_type=jax.ShapeDtypeStruct((n_rows, n_lt, LANES), x1t3.dtype),
        scratch_types=[pltpu.VMEM((TOP_K, SC_CHUNK), I32),
                       pltpu.VMEM((SC_CHUNK, n_lt, LANES), x1t3.dtype),
                       pltpu.SemaphoreType.DMA],
        name="sc_dispatch")
    def dispatch(x_hbm, dest_hbm, out_hbm, idx_v, rows_v, sem):
        wid = lax.axis_index("s") * SC_CORES + lax.axis_index("c")

        @pl.loop(0, chunks_per_worker)
        def _(j):
            chunk = wid * chunks_per_worker + j
            base = pl.multiple_of(chunk * SC_CHUNK, SC_CHUNK)
            pltpu.sync_copy(x_hbm.at[pl.ds(base, SC_CHUNK)], rows_v)
            pltpu.sync_copy(dest_hbm.at[chunk], idx_v)
            copies = [pltpu.make_async_copy(rows_v, out_hbm.at[idx_v.at[k]], sem) for k in range(TOP_K)]
            for cp in copies:
                cp.start()
            for cp in copies:
                cp.wait()

    return dispatch(x1t3, dest3)


def _experts_body(be_ref, nact_ref, first_ref, nxte_ref, wslot_ref, valid_ref,
                  xs_ref, w1_hbm, w3_hbm, w2_hbm, o_ref,
                  wst1, wst3, wst2, w13b, w2b, sem_w):
    b = pl.program_id(0)
    nact = nact_ref[0]
    ed = wst2.shape[1]
    d = wst2.shape[2]
    n_pk = d // (2 * LANES)
    bm = xs_ref.shape[0] // n_pk
    active = b < nact

    def weight_copies(e, s):
        return (pltpu.make_async_copy(w1_hbm.at[e], wst1.at[s], sem_w.at[s]),
                pltpu.make_async_copy(w3_hbm.at[e], wst3.at[s], sem_w.at[s]),
                pltpu.make_async_copy(w2_hbm.at[e], wst2.at[s], sem_w.at[s]))

    @pl.when(b == 0)
    def _():
        for cp in weight_copies(be_ref[0], wslot_ref[0]):
            cp.start()

    @pl.when(first_ref[b] == 1)
    def _():
        ws = wslot_ref[b]
        for cp in weight_copies(be_ref[b], ws):
            cp.wait()

        @pl.when(nxte_ref[b] >= 0)
        def _():
            for cp in weight_copies(nxte_ref[b], 1 - ws):
                cp.start()

        for c in range(ed // MXU_N):
            w13b[:, 2 * c * MXU_N:(2 * c + 1) * MXU_N] = wst1[ws, :, c * MXU_N:(c + 1) * MXU_N].astype(BF16)
            w13b[:, (2 * c + 1) * MXU_N:(2 * c + 2) * MXU_N] = wst3[ws, :, c * MXU_N:(c + 1) * MXU_N].astype(BF16)
        w2b[...] = wst2[ws].astype(BF16)

    @pl.when(active)
    def _():
        keep = lax.broadcasted_iota(I32, (bm, LANES), 0) < valid_ref[b]
        los, his = [], []
        for c in range(n_pk):
            lo, hi = _unpack_bf16_pair(jnp.where(keep, xs_ref[pl.ds(c, bm, stride=n_pk), :], 0))
            los.append(lo.astype(BF16))
            his.append(hi.astype(BF16))
        xb = jnp.concatenate(los + his, axis=1)
        parts = []
        for c in range(ed // MXU_N):
            hc = jnp.dot(xb, w13b[:, 2 * c * MXU_N:(2 * c + 2) * MXU_N], preferred_element_type=F32)
            parts.append((jax.nn.silu(hc[:, :MXU_N]) * hc[:, MXU_N:]).astype(BF16))
        hdn = jnp.concatenate(parts, axis=1)
        n_down = d // MXU_N
        ys = [jnp.dot(hdn, w2b[:, c * MXU_N:(c + 1) * MXU_N], preferred_element_type=F32) for c in range(n_down)]
        for c in range(n_down // 2):
            o_ref[:, c * MXU_N:(c + 1) * MXU_N] = _pack_bf16_pair(ys[c], ys[c + n_down // 2])

    @pl.when(jnp.logical_not(active))
    def _():
        o_ref[...] = jnp.zeros_like(o_ref)


def _experts(xs2d, block_e, nact, first, nxt_e, wslot, valid, w1, w3, w2, bm):
    d, ed = w1.shape[1], w1.shape[2]
    n_pk = d // (2 * LANES)
    n_blk = block_e.shape[0]
    assert xs2d.shape == (n_blk * bm * n_pk, LANES)
    assert ed % MXU_N == 0 and d % MXU_N == 0 and bm % SUBLANES == 0
    grid_spec = pltpu.PrefetchScalarGridSpec(
        num_scalar_prefetch=6,
        grid=(n_blk,),
        in_specs=[pl.BlockSpec((bm * n_pk, LANES),
                               lambda b, be, na, *_: (jnp.maximum(jnp.minimum(b, na[0] - 1), 0), 0)),
                  pl.BlockSpec(memory_space=pl.ANY),
                  pl.BlockSpec(memory_space=pl.ANY),
                  pl.BlockSpec(memory_space=pl.ANY)],
        out_specs=pl.BlockSpec((bm, d // 2), lambda b, *_: (b, 0)),
        scratch_shapes=[pltpu.VMEM((2, d, ed), F32),
                        pltpu.VMEM((2, d, ed), F32),
                        pltpu.VMEM((2, ed, d), F32),
                        pltpu.VMEM((d, 2 * ed), BF16),
                        pltpu.VMEM((ed, d), BF16),
                        pltpu.SemaphoreType.DMA((2,))],
    )
    return pl.pallas_call(
        _experts_body,
        grid_spec=grid_spec,
        out_shape=jax.ShapeDtypeStruct((n_blk * bm, d // 2), I32),
        compiler_params=_cparams(("arbitrary",)),
        name="experts",
    )(block_e, nact, first, nxt_e, wslot, valid, xs2d, w1, w3, w2)


def _combine_body(dest_cur_ref, dest_nxt_ref, base_ref, gate_ref, g2_ref, b2_ref, yb3_hbm, oa_ref, os_ref,
                  ybuf, sem, *, n_a):
    i = pl.program_id(0)
    n_steps = pl.num_programs(0)
    tm = base_ref.shape[0]
    d = base_ref.shape[1]
    slot = i % 2

    def start_rows(dest_ref, s):
        def body(r_hi, c):
            for u in range(SUBLANES):
                for k in range(TOP_K):
                    row = dest_ref[k, r_hi * SUBLANES + u]
                    pltpu.make_async_copy(_row_src(yb3_hbm, row), ybuf.at[s, k, r_hi, pl.ds(u, 1)],
                                          sem.at[s]).start()
            return c
        lax.fori_loop(0, tm // SUBLANES, body, 0)

    @pl.when(i == 0)
    def _():
        start_rows(dest_cur_ref, 0)

    @pl.when(i + 1 < n_steps)
    def _():
        start_rows(dest_nxt_ref, 1 - slot)

    for k in range(TOP_K):
        pltpu.make_async_copy(yb3_hbm.at[pl.ds(0, tm // SUBLANES)], ybuf.at[slot, k], sem.at[slot]).wait()
    half = d // 2
    acc_lo = base_ref[:, :half]
    acc_hi = base_ref[:, half:]
    for k in range(TOP_K):
        lo, hi = _unpack_bf16_pair(ybuf[slot, k].reshape(tm, half))
        g = gate_ref[:, k:k + 1]
        acc_lo = acc_lo + g * lo
        acc_hi = acc_hi + g * hi
    y = _layer_norm(jnp.concatenate([acc_lo, acc_hi], axis=1), g2_ref[...], b2_ref[...])

    @pl.when(i < n_a)
    def _():
        oa_ref[...] = y

    @pl.when(i >= n_a)
    def _():
        os_ref[...] = y


def _combine(base, dest, gate_t, yb, ln_g, ln_b, t_a):
    t, d = base.shape
    tm = COMBINE_TM
    assert t_a % tm == 0 and 0 < t_a < t
    n_a = t_a // tm
    n_steps = t // tm
    map_a, map_s = _two_array_maps(n_a)
    return pl.pallas_call(
        functools.partial(_combine_body, n_a=n_a),
        grid=(n_steps,),
        in_specs=[pl.BlockSpec((TOP_K, tm), lambda i: (0, i), memory_space=pltpu.SMEM),
                  pl.BlockSpec((TOP_K, tm), lambda i: (0, jnp.minimum(i + 1, n_steps - 1)),
                               memory_space=pltpu.SMEM),
                  pl.BlockSpec((tm, d), lambda i: (i, 0)),
                  pl.BlockSpec((tm, TOP_K), lambda i: (i, 0)),
                  pl.BlockSpec((1, d), lambda i: (0, 0)),
                  pl.BlockSpec((1, d), lambda i: (0, 0)),
                  pl.BlockSpec(memory_space=pl.ANY)],
        out_specs=(pl.BlockSpec((tm, d), map_a), pl.BlockSpec((tm, d), map_s)),
        out_shape=(jax.ShapeDtypeStruct((t_a, d), F32), jax.ShapeDtypeStruct((t - t_a, d), F32)),
        scratch_shapes=[pltpu.VMEM((2, TOP_K, tm // SUBLANES, SUBLANES, d // 2), I32),
                        pltpu.SemaphoreType.DMA((2,))],
        compiler_params=_cparams(("arbitrary",)),
        name="combine",
    )(dest, dest, base, gate_t, ln_g, ln_b, _tile_row_view(yb))


def _split_hi_lo(w):
    hi = w.astype(BF16)
    lo = (w - hi.astype(F32)).astype(BF16)
    return hi, lo


def _encoder_layer(xa, xs, seqs, lp, alpha):
    (w_in, pool_w, pool_b, pool_scale, w_pool_br, w_attn_br, attn_rpb, w_out, ln1_g, ln1_b,
     router_w, router_bias, exp_w1, exp_w3, exp_w2, sh_w1, sh_w3, sh_w2, ln2_g, ln2_b) = lp
    d = xa.shape[1]
    t = xa.shape[0] + xs.shape[0]
    pw = w_pool_br.shape[0]
    aw = w_attn_br.shape[0]
    n_heads = attn_rpb.shape[0]
    head_dim = aw // n_heads
    n_experts = router_w.shape[1]
    assert pw == INPROJ_TN and aw == INPROJ_TN and d % INPROJ_TN == 0
    u_block, q_block = 0, pw // INPROJ_TN
    gate_block0 = (pw + 3 * aw) // INPROJ_TN

    h = _inproj(xa, xs, w_in.astype(BF16), q_block=q_block, gate_block0=gate_block0, q_scale=head_dim ** -0.5)

    blk = ATT_ROWS * GRID_W
    images, seq_lo, seq_hi = [], [], []
    for s0, s1 in seqs:
        images.append((s0 // blk, (s1 - s0) // blk))
        for _ in range((s1 - s0) // MIX_TM):
            seq_lo.append(s0)
            seq_hi.append(s1)
    bias_tab = _attn_bias_table(attn_rpb, 2 * LANES // head_dim)
    attn = _attention(h, bias_tab, images, q_block=q_block, n_heads=n_heads, head_dim=head_dim)

    rwh, rwl = _split_hi_lo(router_w.T)
    p = dict(pool_w=pool_w.astype(BF16), pool_b=pool_b.reshape(1, pw), pool_scale=pool_scale.reshape(1, pw),
             wpb=w_pool_br.astype(BF16), wab=w_attn_br.astype(BF16), wo=w_out.astype(BF16),
             ln1_g=ln1_g.reshape(1, d), ln1_b=ln1_b.reshape(1, d),
             s13=jnp.concatenate([sh_w1, sh_w3], axis=1).astype(BF16), s2=sh_w2.astype(BF16),
             rwh=rwh, rwl=rwl, rb=router_bias.reshape(n_experts, 1).astype(F32))
    x1, base, idx_t, gate_t, rank_t, counts = _mixer(
        xa, xs, h, attn, jnp.asarray(seq_lo, I32), jnp.asarray(seq_hi, I32), p,
        alpha=alpha, u_block=u_block, gate_block0=gate_block0)

    bm = EXPERT_BM
    cnt = counts[:, 0].astype(I32)
    pcnt = (cnt + bm - 1) // bm * bm
    pend = jnp.cumsum(pcnt)
    pstart = pend - pcnt
    eids = jnp.arange(n_experts, dtype=I32)[:, None, None]
    dest = jnp.sum(jnp.where(idx_t[None] == eids, pstart[:, None, None], 0), axis=0) + rank_t
    n_blk = (t * TOP_K + n_experts * (bm - 1)) // bm
    blk_row = jnp.arange(n_blk, dtype=I32) * bm
    block_e = jnp.minimum(jnp.sum((pend[None, :] <= blk_row[:, None]).astype(I32), axis=1), n_experts - 1)
    nact = (pend[-1] // bm).astype(I32).reshape(1)
    block_e = jnp.where(jnp.arange(n_blk) < nact[0], block_e, block_e[jnp.maximum(nact[0] - 1, 0)])
    eids1 = jnp.arange(n_experts, dtype=I32)
    present = cnt > 0
    later = present[None, :] & (eids1[None, :] > eids1[:, None])
    nxt_of_e = jnp.min(jnp.where(later, eids1[None, :], n_experts), axis=1)
    nxt_of_e = jnp.where(nxt_of_e >= n_experts, -1, nxt_of_e).astype(I32)
    ord_of_e = (jnp.cumsum(present.astype(I32)) - 1).astype(I32)
    onehot_be = (block_e[:, None] == eids1[None, :]).astype(I32)
    blk_ids = jnp.arange(n_blk, dtype=I32)
    is_act = blk_ids < nact[0]
    prev_e = jnp.concatenate([jnp.full((1,), -1, I32), block_e[:-1]])
    first = (is_act & (block_e != prev_e)).astype(I32)
    nxt_e = jnp.sum(onehot_be * nxt_of_e[None, :], axis=1).astype(I32)
    wslot = (jnp.sum(onehot_be * ord_of_e[None, :], axis=1) % 2).astype(I32)
    start_of_e = jnp.sum(onehot_be * pstart[None, :], axis=1)
    cnt_of_e = jnp.sum(onehot_be * cnt[None, :], axis=1)
    valid = jnp.where(is_act, jnp.clip(cnt_of_e - (blk_row - start_of_e), 0, bm), 0).astype(I32)
    n_pk = d // (2 * LANES)
    dest3 = jnp.transpose(dest.reshape(TOP_K, t // SC_CHUNK, SC_CHUNK), (1, 0, 2))
    xs = _sc_dispatch(x1.reshape(t, n_pk, LANES), dest3, n_blk * bm)
    yb = _experts(xs.reshape(n_blk * bm * n_pk, LANES), block_e, nact, first, nxt_e, wslot, valid,
                  exp_w1, exp_w3, exp_w2, bm)

    return _combine(base, dest, gate_t.T, yb, ln2_g.reshape(1, d), ln2_b.reshape(1, d), xa.shape[0])


def kernel(x_prompt, x_sample, w_in, pool_w, pool_b, pool_scale, w_pool_br, w_attn_br, attn_rpb, w_out,
           ln1_g, ln1_b, router_w, router_bias, exp_w1, exp_w3, exp_w2, sh_w1, sh_w3, sh_w2, ln2_g, ln2_b):
    depth = w_in.shape[0]
    alpha = (2.0 * depth) ** 0.25
    bp, sp, d = x_prompt.shape
    bs, ss, _ = x_sample.shape
    seqs = [(i * sp, (i + 1) * sp) for i in range(bp)]
    seqs += [(bp * sp + i * ss, bp * sp + (i + 1) * ss) for i in range(bs)]
    xa, xs = x_prompt.reshape(bp * sp, d), x_sample.reshape(bs * ss, d)
    for l in range(depth):
        lp = (w_in[l], pool_w[l], pool_b[l], pool_scale[l], w_pool_br[l], w_attn_br[l], attn_rpb[l], w_out[l],
              ln1_g[l], ln1_b[l], router_w[l], router_bias[l], exp_w1[l], exp_w3[l], exp_w2[l],
              sh_w1[l], sh_w3[l], sh_w2[l], ln2_g[l], ln2_b[l])
        xa, xs = _encoder_layer(xa, xs, seqs, lp, alpha)
    return (xa.reshape(bp, sp, d), xs.reshape(bs, ss, d))
```

```python
import functools

import numpy as np
import jax
import jax.numpy as jnp
from jax import lax
from jax.experimental import pallas as pl
from jax.experimental.pallas import tpu as pltpu
from jax.experimental.pallas import tpu_sc as plsc

F32 = jnp.float32
BF16 = jnp.bfloat16
I32 = jnp.int32

GRID_W = 64
POOL_WINDOWS = (2, 4, 8, 16)
WIN_R = 8
WIN_C = 16
TOP_K = 8
N_GROUPS = 8
TOPK_GROUPS = 4
ROUTED_SCALE = 2.5
LN_EPS = 1e-5
NEG_BIG = -1e30

LANES = 128
VMEM_LIMIT = 56 * 1024 * 1024

INPROJ_TM = 1024
INPROJ_TN = 1024
ATT_ROWS = 8
MIX_TM = 256
POOL_HALO = 16
EXPERT_BM = 256
FINAL_TM = 256


def _cparams(sem):
    return pltpu.CompilerParams(dimension_semantics=sem, vmem_limit_bytes=VMEM_LIMIT)


def _inproj_body(xa_ref, xs_ref, w_ref, o_ref, xb_ref, *, n_a, q_block, gate_block0, q_scale):
    i = pl.program_id(0)
    j = pl.program_id(1)

    @pl.when((j == 0) & (i < n_a))
    def _():
        xb_ref[...] = xa_ref[...].astype(BF16)

    @pl.when((j == 0) & (i >= n_a))
    def _():
        xb_ref[...] = xs_ref[...].astype(BF16)

    acc = jnp.dot(xb_ref[...], w_ref[...], preferred_element_type=F32)
    is_gate = j >= gate_block0

    @pl.when(is_gate)
    def _():
        o_ref[...] = jax.nn.sigmoid(acc).astype(BF16)

    @pl.when(jnp.logical_not(is_gate))
    def _():
        scale = jnp.where(j == q_block, q_scale, 1.0).astype(F32)
        o_ref[...] = (acc * scale).astype(BF16)


def _two_array_maps(n_a):
    return (lambda i, *_: (jnp.minimum(i, n_a - 1), 0)), (lambda i, *_: (jnp.maximum(i - n_a, 0), 0))


def _inproj(xa, xs, w_bf16, *, q_block, gate_block0, q_scale):
    d = xa.shape[1]
    t = xa.shape[0] + xs.shape[0]
    n = w_bf16.shape[1]
    tm, tn = INPROJ_TM, INPROJ_TN
    assert xa.shape[0] % tm == 0 and xs.shape[0] % tm == 0 and xa.shape[0] > 0 and xs.shape[0] > 0
    n_a = xa.shape[0] // tm
    map_a, map_s = _two_array_maps(n_a)
    return pl.pallas_call(
        functools.partial(_inproj_body, n_a=n_a, q_block=q_block, gate_block0=gate_block0, q_scale=q_scale),
        grid=(t // tm, n // tn),
        in_specs=[pl.BlockSpec((tm, d), map_a),
                  pl.BlockSpec((tm, d), map_s),
                  pl.BlockSpec((d, tn), lambda i, j: (0, j))],
        out_specs=pl.BlockSpec((tm, tn), lambda i, j: (i, j)),
        out_shape=jax.ShapeDtypeStruct((t, n), BF16),
        scratch_shapes=[pltpu.VMEM((tm, d), BF16)],
        compiler_params=_cparams(("arbitrary", "arbitrary")),
        name="inproj",
    )(xa, xs, w_bf16)


def _attn_body(prev_ref, next_ref, kst_ref, off_ref,
               q_ref, kp_ref, kc_ref, kn_ref, vp_ref, vc_ref, vn_ref, bias_ref, o_ref,
               kbuf, vbuf, *, heads_per_group, head_dim):
    del prev_ref, next_ref
    b = pl.program_id(0)
    blk = ATT_ROWS * GRID_W
    kbuf[0:blk] = kp_ref[...]
    kbuf[blk:2 * blk] = kc_ref[...]
    kbuf[2 * blk:3 * blk] = kn_ref[...]
    vbuf[0:blk] = vp_ref[...]
    vbuf[blk:2 * blk] = vc_ref[...]
    vbuf[2 * blk:3 * blk] = vn_ref[...]

    gw = heads_per_group * head_dim
    n_groups = q_ref.shape[1] // gw
    win = WIN_R * GRID_W
    nq = heads_per_group * GRID_W
    row_head = lax.broadcasted_iota(I32, (nq, gw), 0) // GRID_W
    lane_head = lax.broadcasted_iota(I32, (nq, gw), 1) // head_dim
    head_mask = row_head == lane_head
    out_lane_head = lax.broadcasted_iota(I32, (GRID_W, gw), 1) // head_dim

    def row_step(i, carry):
        st = pl.multiple_of(kst_ref[b * ATT_ROWS + i] * GRID_W, GRID_W)
        dr0 = (WIN_R - 1) - off_ref[b * ATT_ROWS + i]
        q0 = pl.multiple_of(i * GRID_W, GRID_W)
        for g in range(n_groups):
            qg = q_ref[pl.ds(q0, GRID_W), g * gw:(g + 1) * gw]
            kg = kbuf[pl.ds(st, win), g * gw:(g + 1) * gw]
            vg = vbuf[pl.ds(st, win), g * gw:(g + 1) * gw]
            wt = jnp.concatenate([qg] * heads_per_group, axis=0)
            wt = jnp.where(head_mask, wt, jnp.zeros_like(wt))
            s = lax.dot_general(wt, kg, (((1,), (1,)), ((), ())),
                                preferred_element_type=F32)
            n_chunk = win // LANES
            sj = [s[:, j * LANES:(j + 1) * LANES] + bias_ref[dr0 + 2 * j, g] for j in range(n_chunk)]
            m = sj[0]
            for j in range(1, n_chunk):
                m = jnp.maximum(m, sj[j])
            m = jnp.max(m, axis=1, keepdims=True)
            ej = [jnp.exp(x - m) for x in sj]
            l = ej[0]
            for j in range(1, n_chunk):
                l = l + ej[j]
            inv = 1.0 / jnp.sum(l, axis=1, keepdims=True)
            p = jnp.concatenate([(x * inv).astype(BF16) for x in ej], axis=1)
            of = jnp.dot(p, vg, preferred_element_type=F32)
            out = jnp.zeros((GRID_W, gw), F32)
            for h in range(heads_per_group):
                out = out + jnp.where(out_lane_head == h, of[h * GRID_W:(h + 1) * GRID_W, :], 0.0)
            o_ref[pl.ds(q0, GRID_W), g * gw:(g + 1) * gw] = out.astype(o_ref.dtype)
        return carry

    lax.fori_loop(0, ATT_ROWS, row_step, 0)


def _attn_tables(images):
    prev, nxt, kst, off = [], [], [], []
    for s, n in images:
        rows = n * ATT_ROWS
        kr = min(WIN_R, rows)
        assert kr == WIN_R
        for bl in range(n):
            b = s + bl
            prev.append(max(b - 1, s))
            nxt.append(min(b + 1, s + n - 1))
            for i in range(ATT_ROWS):
                r = bl * ATT_ROWS + i
                rs = min(max(r - kr // 2, 0), rows - kr)
                kst.append(rs - bl * ATT_ROWS + ATT_ROWS)
                off.append(r - rs)
    return (np.asarray(prev, np.int32), np.asarray(nxt, np.int32),
            np.asarray(kst, np.int32), np.asarray(off, np.int32))


def _attn_bias_table(rpb, heads_per_group):
    n_heads, n_dr, n_dc = rpb.shape
    n_pair = n_dr - 1
    c = np.arange(GRID_W)
    cs = np.clip(c - WIN_C // 2, 0, GRID_W - WIN_C)
    kc = np.arange(GRID_W)
    valid = (kc[None, :] >= cs[:, None]) & (kc[None, :] < cs[:, None] + WIN_C)
    dc = np.clip(kc[None, :] - c[:, None] + (WIN_C - 1), 0, n_dc - 1)
    sel = np.zeros((2, n_dc, GRID_W, 2, GRID_W), np.float32)
    ci, ki = np.nonzero(valid)
    for p in range(2):
        sel[p, dc[ci, ki], ci, p, ki] = 1.0
    sel = jnp.asarray(sel.reshape(2 * n_dc, GRID_W * 2 * GRID_W))
    r = rpb.astype(F32)
    pair = jnp.stack([r[:, :n_pair], r[:, 1:]], axis=2)
    pair = jnp.transpose(pair, (1, 0, 2, 3)).reshape(n_pair * n_heads, 2 * n_dc)
    tab = jnp.dot(pair, sel, precision=lax.Precision.HIGHEST)
    tab = tab.reshape(n_pair, n_heads, GRID_W, 2 * GRID_W)
    valid2 = jnp.asarray(np.tile(valid[:, None, :], (1, 2, 1)).reshape(GRID_W, 2 * GRID_W))
    tab = jnp.where(valid2[None, None], tab, NEG_BIG)
    return tab.reshape(n_pair, n_heads // heads_per_group, heads_per_group * GRID_W, 2 * GRID_W)


def _attention(h, bias_tab, images, *, q_block, n_heads, head_dim):
    t = h.shape[0]
    blk = ATT_ROWS * GRID_W
    aw = n_heads * head_dim
    heads_per_group = 2 * LANES // head_dim
    prev, nxt, kst, off = _attn_tables(images)
    n_blocks = t // blk
    assert prev.shape[0] == n_blocks
    kcol, vcol = q_block + 1, q_block + 2
    spec = lambda col, which: pl.BlockSpec(
        (blk, aw), {"cur": lambda b, p, n, k, o: (b, col),
                    "prev": lambda b, p, n, k, o: (p[b], col),
                    "next": lambda b, p, n, k, o: (n[b], col)}[which])
    grid_spec = pltpu.PrefetchScalarGridSpec(
        num_scalar_prefetch=4,
        grid=(n_blocks,),
        in_specs=[spec(q_block, "cur"),
                  spec(kcol, "prev"), spec(kcol, "cur"), spec(kcol, "next"),
                  spec(vcol, "prev"), spec(vcol, "cur"), spec(vcol, "next"),
                  pl.BlockSpec(bias_tab.shape, lambda b, p, n, k, o: (0, 0, 0, 0),
                               pipeline_mode=pl.Buffered(1))],
        out_specs=pl.BlockSpec((blk, aw), lambda b, p, n, k, o: (b, 0)),
        scratch_shapes=[pltpu.VMEM((3 * blk, aw), BF16), pltpu.VMEM((3 * blk, aw), BF16)],
    )
    return pl.pallas_call(
        functools.partial(_attn_body, heads_per_group=heads_per_group, head_dim=head_dim),
        grid_spec=grid_spec,
        out_shape=jax.ShapeDtypeStruct((t, aw), BF16),
        compiler_params=_cparams(("arbitrary",)),
        name="natten",
    )(jnp.asarray(prev), jnp.asarray(nxt), jnp.asarray(kst), jnp.asarray(off),
      h, h, h, h, h, h, h, bias_tab)


def _layer_norm(r, g, b):
    mu = jnp.mean(r, axis=-1, keepdims=True)
    c = r - mu
    var = jnp.mean(c * c, axis=-1, keepdims=True)
    return c * lax.rsqrt(var + LN_EPS) * g + b


HI_MASK = -65536


def _pack_bf16_pair(lo, hi):
    lo_bits = lax.bitcast_convert_type(lo.astype(BF16).astype(F32), I32)
    hi_bits = lax.bitcast_convert_type(hi.astype(BF16).astype(F32), I32)
    return (hi_bits & HI_MASK) | lax.shift_right_logical(lo_bits, 16)


def _unpack_bf16_pair(word):
    lo = lax.bitcast_convert_type(lax.shift_left(word, 16), F32)
    hi = lax.bitcast_convert_type(word & HI_MASK, F32)
    return lo, hi


def _first_index_of(mask, idx, sentinel, axis):
    return jnp.min(jnp.where(mask, idx, sentinel), axis=axis, keepdims=True)


def _mixer_body(seq_lo_ref, seq_hi_ref,
                xa_ref, xs_ref, up_ref, uc_ref, un_ref, gp_ref, ga_ref, at_ref,
                pw_ref, pb_ref, ps_ref, wpb_ref, wab_ref, wo_ref, g1_ref, b1_ref,
                s13_ref, s2_ref, rwh_ref, rwl_ref, rb_ref,
                x1_ref, base_ref, idx_ref, gate_ref, rank_ref, cnt_ref,
                carry_ref, *, n_a, alpha, n_experts):
    i = pl.program_id(0)
    tm = xa_ref.shape[0]
    t0 = i * tm
    lo_seq = seq_lo_ref[i]
    hi_seq = seq_hi_ref[i]

    @pl.when(i == 0)
    def _():
        carry_ref[...] = jnp.zeros_like(carry_ref)

    u_ext = jnp.concatenate([up_ref[...], uc_ref[...], un_ref[...]], axis=0)
    ext = tm + 2 * POOL_HALO
    tok_r = t0 + lax.broadcasted_iota(I32, (tm, ext), 0)
    tok_c = t0 - POOL_HALO + lax.broadcasted_iota(I32, (tm, ext), 1)
    tok_q = t0 + lax.broadcasted_iota(I32, (tm, 1), 0)
    n_pg = pw_ref.shape[0]
    pg = pw_ref.shape[1]
    parts = []
    for g in range(n_pg):
        w = POOL_WINDOWS[g]
        lo = jnp.maximum(tok_r - w // 2, lo_seq)
        hi = jnp.minimum(tok_r - w // 2 + w, hi_seq)
        band = jnp.where((tok_c >= lo) & (tok_c < hi), 1.0, 0.0).astype(BF16)
        cnt = (jnp.minimum(tok_q - w // 2 + w, hi_seq) - jnp.maximum(tok_q - w // 2, lo_seq)).astype(F32)
        ug = u_ext[:, g * pg:(g + 1) * pg]
        sums = jnp.dot(band, ug, preferred_element_type=F32)
        z = sums / cnt - uc_ref[:, g * pg:(g + 1) * pg].astype(F32)
        y = jnp.dot(z.astype(BF16), pw_ref[g], preferred_element_type=F32)
        y = (y + pb_ref[:, g * pg:(g + 1) * pg]) * ps_ref[:, g * pg:(g + 1) * pg]
        parts.append(y.astype(BF16))
    a = jnp.concatenate(parts, axis=1)
    a_pool = jnp.dot(a, wpb_ref[...], preferred_element_type=F32)
    a_attn = jnp.dot(at_ref[...], wab_ref[...], preferred_element_type=F32)
    merged = gp_ref[...].astype(F32) * a_pool + ga_ref[...].astype(F32) * a_attn
    mix = jnp.dot(merged.astype(BF16), wo_ref[...], preferred_element_type=F32)
    x_in = jnp.where(i < n_a, xa_ref[...], xs_ref[...])
    x1 = _layer_norm(alpha * x_in + mix, g1_ref[...], b1_ref[...])
    half = x1.shape[1] // 2
    n_pk = half // LANES
    word = _pack_bf16_pair(x1[:, :half], x1[:, half:])
    for c in range(n_pk):
        x1_ref[pl.ds(c, tm, stride=n_pk), :] = word[:, c * LANES:(c + 1) * LANES]
    x1b = x1.astype(BF16)

    sd = s2_ref.shape[0]
    hc = jnp.dot(x1b, s13_ref[...], preferred_element_type=F32)
    hdn = jax.nn.silu(hc[:, :sd]) * hc[:, sd:]
    shared = jnp.dot(hdn.astype(BF16), s2_ref[...], preferred_element_type=F32)
    base_ref[...] = alpha * x1 + shared

    x1l = (x1 - x1b.astype(F32)).astype(BF16)
    nt = (((1,), (1,)), ((), ()))
    logits = (lax.dot_general(rwh_ref[...], x1b, nt, preferred_element_type=F32)
              + lax.dot_general(rwh_ref[...], x1l, nt, preferred_element_type=F32)
              + lax.dot_general(rwl_ref[...], x1b, nt, preferred_element_type=F32))
    scores = jax.nn.sigmoid(logits)
    biased = scores + rb_ref[...]
    gsz = n_experts // N_GROUPS
    neg_inf = -jnp.inf

    mem = lax.broadcasted_iota(I32, (gsz, tm), 0)
    gs_rows = []
    for g in range(N_GROUPS):
        bg = biased[g * gsz:(g + 1) * gsz, :]
        m1 = jnp.max(bg, axis=0, keepdims=True)
        first = _first_index_of(bg == m1, mem, gsz, 0)
        m2 = jnp.max(jnp.where(mem == first, neg_inf, bg), axis=0, keepdims=True)
        gs_rows.append(m1 + m2)
    gwork = jnp.concatenate(gs_rows, axis=0)

    gidx = lax.broadcasted_iota(I32, (N_GROUPS, tm), 0)
    egrp = lax.broadcasted_iota(I32, (n_experts, tm), 0) // gsz
    eidx = lax.broadcasted_iota(I32, (n_experts, tm), 0)
    work = jnp.full((n_experts, tm), neg_inf, F32)
    for _ in range(TOPK_GROUPS):
        gm = jnp.max(gwork, axis=0, keepdims=True)
        gf = _first_index_of(gwork == gm, gidx, N_GROUPS, 0)
        gwork = jnp.where(gidx == gf, neg_inf, gwork)
        work = jnp.where(egrp == gf, biased, work)

    hot = jnp.zeros((n_experts, tm), F32)
    sel_idx, sel_w, sel_hit = [], [], []
    for _ in range(TOP_K):
        em = jnp.max(work, axis=0, keepdims=True)
        ef = _first_index_of(work == em, eidx, n_experts, 0)
        hit = eidx == ef
        sel_idx.append(ef)
        sel_w.append(jnp.sum(jnp.where(hit, scores, 0.0), axis=0, keepdims=True))
        sel_hit.append(hit)
        hot = jnp.where(hit, 1.0, hot)
        work = jnp.where(hit, neg_inf, work)
    wsel = jnp.concatenate(sel_w, axis=0)
    gate_ref[...] = wsel / jnp.sum(wsel, axis=0, keepdims=True) * ROUTED_SCALE
    idx_ref[...] = jnp.concatenate(sel_idx, axis=0)

    tri = jnp.where(lax.broadcasted_iota(I32, (tm, tm), 0) < lax.broadcasted_iota(I32, (tm, tm), 1),
                    1.0, 0.0).astype(BF16)
    before = carry_ref[:, 0:1] + jnp.dot(hot.astype(BF16), tri, preferred_element_type=F32)
    rank = [jnp.sum(jnp.where(h, before, 0.0), axis=0, keepdims=True) for h in sel_hit]
    rank_ref[...] = jnp.concatenate(rank, axis=0).astype(I32)
    carry_ref[...] = carry_ref[...] + jnp.sum(hot, axis=1, keepdims=True)
    cnt_ref[...] = carry_ref[...]


def _mixer(xa, xs, h, attn, seq_lo, seq_hi, p, *, alpha, u_block, gate_block0):
    d = xa.shape[1]
    t = xa.shape[0] + xs.shape[0]
    tm = MIX_TM
    assert xa.shape[0] % tm == 0 and xs.shape[0] % tm == 0
    n_a = xa.shape[0] // tm
    map_a, map_s = _two_array_maps(n_a)
    pw = p["wpb"].shape[0]
    aw = attn.shape[1]
    n_experts = p["rwh"].shape[0]
    halo_per_tile = tm // POOL_HALO
    n_halo_blocks = t // POOL_HALO
    gp_col = gate_block0 * INPROJ_TN // d
    in_specs = [
        pl.BlockSpec((tm, d), map_a),
        pl.BlockSpec((tm, d), map_s),
        pl.BlockSpec((POOL_HALO, pw), lambda i, lo, hi: (jnp.maximum(i * halo_per_tile - 1, 0), u_block)),
        pl.BlockSpec((tm, pw), lambda i, lo, hi: (i, u_block)),
        pl.BlockSpec((POOL_HALO, pw),
                     lambda i, lo, hi: (jnp.minimum((i + 1) * halo_per_tile, n_halo_blocks - 1), u_block)),
        pl.BlockSpec((tm, d), lambda i, lo, hi: (i, gp_col)),
        pl.BlockSpec((tm, d), lambda i, lo, hi: (i, gp_col + 1)),
        pl.BlockSpec((tm, aw), lambda i, lo, hi: (i, 0)),
    ]
    weights = [p["pool_w"], p["pool_b"], p["pool_scale"], p["wpb"], p["wab"], p["wo"], p["ln1_g"], p["ln1_b"],
               p["s13"], p["s2"], p["rwh"], p["rwl"], p["rb"]]
    in_specs += [pl.BlockSpec(w.shape, (lambda nd: (lambda i, lo, hi: (0,) * nd))(w.ndim),
                              pipeline_mode=pl.Buffered(1)) for w in weights]
    n_pk = d // (2 * LANES)
    out_shape = (jax.ShapeDtypeStruct((t * n_pk, LANES), I32),
                 jax.ShapeDtypeStruct((t, d), F32),
                 jax.ShapeDtypeStruct((TOP_K, t), I32),
                 jax.ShapeDtypeStruct((TOP_K, t), F32),
                 jax.ShapeDtypeStruct((TOP_K, t), I32),
                 jax.ShapeDtypeStruct((n_experts, LANES), F32))
    out_specs = (pl.BlockSpec((tm * n_pk, LANES), lambda i, lo, hi: (i, 0)),
                 pl.BlockSpec((tm, d), lambda i, lo, hi: (i, 0)),
                 pl.BlockSpec((TOP_K, tm), lambda i, lo, hi: (0, i)),
                 pl.BlockSpec((TOP_K, tm), lambda i, lo, hi: (0, i)),
                 pl.BlockSpec((TOP_K, tm), lambda i, lo, hi: (0, i)),
                 pl.BlockSpec((n_experts, LANES), lambda i, lo, hi: (0, 0)))
    grid_spec = pltpu.PrefetchScalarGridSpec(
        num_scalar_prefetch=2, grid=(t // tm,), in_specs=in_specs, out_specs=out_specs,
        scratch_shapes=[pltpu.VMEM((n_experts, LANES), F32)])
    return pl.pallas_call(
        functools.partial(_mixer_body, n_a=n_a, alpha=alpha, n_experts=n_experts),
        grid_spec=grid_spec, out_shape=out_shape,
        compiler_params=_cparams(("arbitrary",)),
        name="mixer",
    )(seq_lo, seq_hi, xa, xs, h, h, h, h, h, attn, *weights)


SUBLANES = 8
MXU_N = 256


SC_CORES = 2
SC_SUBCORES = 16
SC_CHUNK = 32


def _sc_dispatch(x1t3, dest3, n_rows):
    t, n_lt, _ = x1t3.shape
    n_workers = SC_CORES * SC_SUBCORES
    assert t % (n_workers * SC_CHUNK) == 0
    chunks_per_worker = t // (n_workers * SC_CHUNK)
    mesh = plsc.VectorSubcoreMesh(core_axis_name="c", subcore_axis_name="s",
                                  num_cores=SC_CORES, num_subcores=SC_SUBCORES)

    @functools.partial(
        pl.kernel, mesh=mesh,
        out_type=jax.ShapeDtypeStruct((n_rows, n_lt, LANES), x1t3.dtype),
        scratch_types=[pltpu.VMEM((TOP_K, SC_CHUNK), I32),
                       pltpu.VMEM((SC_CHUNK, n_lt, LANES), x1t3.dtype),
                       pltpu.SemaphoreType.DMA],
        name="sc_dispatch")
    def dispatch(x_hbm, dest_hbm, out_hbm, idx_v, rows_v, sem):
        wid = lax.axis_index("s") * SC_CORES + lax.axis_index("c")

        @pl.loop(0, chunks_per_worker)
        def _(j):
            chunk = wid * chunks_per_worker + j
            base = pl.multiple_of(chunk * SC_CHUNK, SC_CHUNK)
            pltpu.sync_copy(x_hbm.at[pl.ds(base, SC_CHUNK)], rows_v)
            pltpu.sync_copy(dest_hbm.at[chunk], idx_v)
            copies = [pltpu.make_async_copy(rows_v, out_hbm.at[idx_v.at[k]], sem) for k in range(TOP_K)]
            for cp in copies:
                cp.start()
            for cp in copies:
                cp.wait()

    return dispatch(x1t3, dest3)


SC_RING = 3


def _sc_gather(yb3, dest3, t):
    _, n_lt, _ = yb3.shape
    n_workers = SC_CORES * SC_SUBCORES
    chunks_per_worker = t // (n_workers * SC_CHUNK)
    mesh = plsc.VectorSubcoreMesh(core_axis_name="c", subcore_axis_name="s",
                                  num_cores=SC_CORES, num_subcores=SC_SUBCORES)

    @functools.partial(
        pl.kernel, mesh=mesh,
        out_type=jax.ShapeDtypeStruct((TOP_K * t, n_lt, LANES), yb3.dtype),
        scratch_types=[pltpu.VMEM((TOP_K, SC_CHUNK), I32),
                       pltpu.VMEM((SC_RING, SC_CHUNK, n_lt, LANES), yb3.dtype)]
                      + [pltpu.SemaphoreType.DMA] * (2 * SC_RING),
        name="sc_gather")
    def gather(y_hbm, dest_hbm, out_hbm, idx_v, rows_v, *sems):
        gsem, wsem = sems[:SC_RING], sems[SC_RING:]
        wid = lax.axis_index("s") * SC_CORES + lax.axis_index("c")

        @pl.loop(0, chunks_per_worker)
        def _(j):
            chunk = wid * chunks_per_worker + j
            base = pl.multiple_of(chunk * SC_CHUNK, SC_CHUNK)
            pltpu.sync_copy(dest_hbm.at[chunk], idx_v)
            gets = [pltpu.make_async_copy(y_hbm.at[idx_v.at[k]], rows_v.at[k % SC_RING], gsem[k % SC_RING])
                    for k in range(TOP_K)]
            puts = [pltpu.make_async_copy(rows_v.at[k % SC_RING], out_hbm.at[pl.ds(k * t + base, SC_CHUNK)],
                                          wsem[k % SC_RING]) for k in range(TOP_K)]
            put_waited = set()
            for k in range(SC_RING - 1):
                gets[k].start()
            for k in range(TOP_K):
                gets[k].wait()
                puts[k].start()
                nxt = k + SC_RING - 1
                if nxt < TOP_K:
                    if k >= 1:
                        puts[k - 1].wait()
                        put_waited.add(k - 1)
                    gets[nxt].start()
            for k in range(TOP_K):
                if k not in put_waited:
                    puts[k].wait()

    return gather(yb3, dest3)


def _experts_body(be_ref, nact_ref, first_ref, nxte_ref, wslot_ref, valid_ref,
                  xs_ref, w1_hbm, w3_hbm, w2_hbm, o_ref,
                  wst1, wst3, wst2, w13b, w2b, sem_w):
    b = pl.program_id(0)
    nact = nact_ref[0]
    ed = wst2.shape[1]
    d = wst2.shape[2]
    n_pk = d // (2 * LANES)
    bm = xs_ref.shape[0] // n_pk
    active = b < nact

    def weight_copies(e, s):
        return (pltpu.make_async_copy(w1_hbm.at[e], wst1.at[s], sem_w.at[s]),
                pltpu.make_async_copy(w3_hbm.at[e], wst3.at[s], sem_w.at[s]),
                pltpu.make_async_copy(w2_hbm.at[e], wst2.at[s], sem_w.at[s]))

    @pl.when(b == 0)
    def _():
        for cp in weight_copies(be_ref[0], wslot_ref[0]):
            cp.start()

    @pl.when(first_ref[b] == 1)
    def _():
        ws = wslot_ref[b]
        for cp in weight_copies(be_ref[b], ws):
            cp.wait()

        @pl.when(nxte_ref[b] >= 0)
        def _():
            for cp in weight_copies(nxte_ref[b], 1 - ws):
                cp.start()

        for c in range(ed // MXU_N):
            w13b[:, 2 * c * MXU_N:(2 * c + 1) * MXU_N] = wst1[ws, :, c * MXU_N:(c + 1) * MXU_N].astype(BF16)
            w13b[:, (2 * c + 1) * MXU_N:(2 * c + 2) * MXU_N] = wst3[ws, :, c * MXU_N:(c + 1) * MXU_N].astype(BF16)
        w2b[...] = wst2[ws].astype(BF16)

    @pl.when(active)
    def _():
        keep = lax.broadcasted_iota(I32, (bm, LANES), 0) < valid_ref[b]
        los, his = [], []
        for c in range(n_pk):
            lo, hi = _unpack_bf16_pair(jnp.where(keep, xs_ref[pl.ds(c, bm, stride=n_pk), :], 0))
            los.append(lo.astype(BF16))
            his.append(hi.astype(BF16))
        xb = jnp.concatenate(los + his, axis=1)
        parts = []
        for c in range(ed // MXU_N):
            hc = jnp.dot(xb, w13b[:, 2 * c * MXU_N:(2 * c + 2) * MXU_N], preferred_element_type=F32)
            parts.append((jax.nn.silu(hc[:, :MXU_N]) * hc[:, MXU_N:]).astype(BF16))
        hdn = jnp.concatenate(parts, axis=1)
        n_down = d // MXU_N
        ys = [jnp.dot(hdn, w2b[:, c * MXU_N:(c + 1) * MXU_N], preferred_element_type=F32) for c in range(n_down)]
        tiles_per_chunk = MXU_N // LANES
        for c in range(n_down // 2):
            word = _pack_bf16_pair(ys[c], ys[c + n_down // 2])
            for h in range(tiles_per_chunk):
                o_ref[pl.ds(c * tiles_per_chunk + h, bm, stride=n_pk), :] = word[:, h * LANES:(h + 1) * LANES]

    @pl.when(jnp.logical_not(active))
    def _():
        o_ref[...] = jnp.zeros_like(o_ref)


def _experts(xs2d, block_e, nact, first, nxt_e, wslot, valid, w1, w3, w2, bm):
    d, ed = w1.shape[1], w1.shape[2]
    n_pk = d // (2 * LANES)
    n_blk = block_e.shape[0]
    assert xs2d.shape == (n_blk * bm * n_pk, LANES)
    assert ed % MXU_N == 0 and d % MXU_N == 0 and bm % SUBLANES == 0
    grid_spec = pltpu.PrefetchScalarGridSpec(
        num_scalar_prefetch=6,
        grid=(n_blk,),
        in_specs=[pl.BlockSpec((bm * n_pk, LANES),
                               lambda b, be, na, *_: (jnp.maximum(jnp.minimum(b, na[0] - 1), 0), 0)),
                  pl.BlockSpec(memory_space=pl.ANY),
                  pl.BlockSpec(memory_space=pl.ANY),
                  pl.BlockSpec(memory_space=pl.ANY)],
        out_specs=pl.BlockSpec((bm * n_pk, LANES), lambda b, *_: (b, 0)),
        scratch_shapes=[pltpu.VMEM((2, d, ed), F32),
                        pltpu.VMEM((2, d, ed), F32),
                        pltpu.VMEM((2, ed, d), F32),
                        pltpu.VMEM((d, 2 * ed), BF16),
                        pltpu.VMEM((ed, d), BF16),
                        pltpu.SemaphoreType.DMA((2,))],
    )
    return pl.pallas_call(
        _experts_body,
        grid_spec=grid_spec,
        out_shape=jax.ShapeDtypeStruct((n_blk * bm * n_pk, LANES), I32),
        compiler_params=_cparams(("arbitrary",)),
        name="experts",
    )(block_e, nact, first, nxt_e, wslot, valid, xs2d, w1, w3, w2)


def _finalize_body(base_ref, gate_ref, g2_ref, b2_ref, *refs, n_a):
    slabs, (oa_ref, os_ref) = refs[:TOP_K], refs[TOP_K:]
    i = pl.program_id(0)
    tm, d = base_ref.shape
    half = d // 2
    n_pk = half // LANES
    acc_lo = [base_ref[:, c * LANES:(c + 1) * LANES] for c in range(n_pk)]
    acc_hi = [base_ref[:, half + c * LANES:half + (c + 1) * LANES] for c in range(n_pk)]
    for k in range(TOP_K):
        g = gate_ref[:, k:k + 1]
        for c in range(n_pk):
            lo, hi = _unpack_bf16_pair(slabs[k][pl.ds(c, tm, stride=n_pk), :])
            acc_lo[c] = acc_lo[c] + g * lo
            acc_hi[c] = acc_hi[c] + g * hi
    y = _layer_norm(jnp.concatenate(acc_lo + acc_hi, axis=1), g2_ref[...], b2_ref[...])

    @pl.when(i < n_a)
    def _():
        oa_ref[...] = y

    @pl.when(i >= n_a)
    def _():
        os_ref[...] = y


def _finalize(base, gate_t, slabs2d, ln_g, ln_b, t_a):
    t, d = base.shape
    tm = FINAL_TM
    n_pk = d // (2 * LANES)
    assert t_a % tm == 0 and 0 < t_a < t and slabs2d.shape == (TOP_K * t * n_pk, LANES)
    n_a = t_a // tm
    n_steps = t // tm
    map_a, map_s = _two_array_maps(n_a)
    slab_specs = [pl.BlockSpec((tm * n_pk, LANES), (lambda k: (lambda i: (k * n_steps + i, 0)))(k))
                  for k in range(TOP_K)]
    return pl.pallas_call(
        functools.partial(_finalize_body, n_a=n_a),
        grid=(n_steps,),
        in_specs=[pl.BlockSpec((tm, d), lambda i: (i, 0)),
                  pl.BlockSpec((tm, TOP_K), lambda i: (i, 0)),
                  pl.BlockSpec((1, d), lambda i: (0, 0)),
                  pl.BlockSpec((1, d), lambda i: (0, 0))] + slab_specs,
        out_specs=(pl.BlockSpec((tm, d), map_a), pl.BlockSpec((tm, d), map_s)),
        out_shape=(jax.ShapeDtypeStruct((t_a, d), F32), jax.ShapeDtypeStruct((t - t_a, d), F32)),
        compiler_params=_cparams(("arbitrary",)),
        name="finalize",
    )(base, gate_t, ln_g, ln_b, *([slabs2d] * TOP_K))


def _split_hi_lo(w):
    hi = w.astype(BF16)
    lo = (w - hi.astype(F32)).astype(BF16)
    return hi, lo


def _encoder_layer(xa, xs, seqs, lp, alpha):
    (w_in, pool_w, pool_b, pool_scale, w_pool_br, w_attn_br, attn_rpb, w_out, ln1_g, ln1_b,
     router_w, router_bias, exp_w1, exp_w3, exp_w2, sh_w1, sh_w3, sh_w2, ln2_g, ln2_b) = lp
    d = xa.shape[1]
    t = xa.shape[0] + xs.shape[0]
    pw = w_pool_br.shape[0]
    aw = w_attn_br.shape[0]
    n_heads = attn_rpb.shape[0]
    head_dim = aw // n_heads
    n_experts = router_w.shape[1]
    assert pw == INPROJ_TN and aw == INPROJ_TN and d % INPROJ_TN == 0
    u_block, q_block = 0, pw // INPROJ_TN
    gate_block0 = (pw + 3 * aw) // INPROJ_TN

    h = _inproj(xa, xs, w_in.astype(BF16), q_block=q_block, gate_block0=gate_block0, q_scale=head_dim ** -0.5)

    blk = ATT_ROWS * GRID_W
    images, seq_lo, seq_hi = [], [], []
    for s0, s1 in seqs:
        images.append((s0 // blk, (s1 - s0) // blk))
        for _ in range((s1 - s0) // MIX_TM):
            seq_lo.append(s0)
            seq_hi.append(s1)
    bias_tab = _attn_bias_table(attn_rpb, 2 * LANES // head_dim)
    attn = _attention(h, bias_tab, images, q_block=q_block, n_heads=n_heads, head_dim=head_dim)

    rwh, rwl = _split_hi_lo(router_w.T)
    p = dict(pool_w=pool_w.astype(BF16), pool_b=pool_b.reshape(1, pw), pool_scale=pool_scale.reshape(1, pw),
             wpb=w_pool_br.astype(BF16), wab=w_attn_br.astype(BF16), wo=w_out.astype(BF16),
             ln1_g=ln1_g.reshape(1, d), ln1_b=ln1_b.reshape(1, d),
             s13=jnp.concatenate([sh_w1, sh_w3], axis=1).astype(BF16), s2=sh_w2.astype(BF16),
             rwh=rwh, rwl=rwl, rb=router_bias.reshape(n_experts, 1).astype(F32))
    x1, base, idx_t, gate_t, rank_t, counts = _mixer(
        xa, xs, h, attn, jnp.asarray(seq_lo, I32), jnp.asarray(seq_hi, I32), p,
        alpha=alpha, u_block=u_block, gate_block0=gate_block0)

    bm = EXPERT_BM
    cnt = counts[:, 0].astype(I32)
    pcnt = (cnt + bm - 1) // bm * bm
    pend = jnp.cumsum(pcnt)
    pstart = pend - pcnt
    eids = jnp.arange(n_experts, dtype=I32)[:, None, None]
    dest = jnp.sum(jnp.where(idx_t[None] == eids, pstart[:, None, None], 0), axis=0) + rank_t
    n_blk = (t * TOP_K + n_experts * (bm - 1)) // bm
    blk_row = jnp.arange(n_blk, dtype=I32) * bm
    block_e = jnp.minimum(jnp.sum((pend[None, :] <= blk_row[:, None]).astype(I32), axis=1), n_experts - 1)
    nact = (pend[-1] // bm).astype(I32).reshape(1)
    block_e = jnp.where(jnp.arange(n_blk) < nact[0], block_e, block_e[jnp.maximum(nact[0] - 1, 0)])
    eids1 = jnp.arange(n_experts, dtype=I32)
    present = cnt > 0
    later = present[None, :] & (eids1[None, :] > eids1[:, None])
    nxt_of_e = jnp.min(jnp.where(later, eids1[None, :], n_experts), axis=1)
    nxt_of_e = jnp.where(nxt_of_e >= n_experts, -1, nxt_of_e).astype(I32)
    ord_of_e = (jnp.cumsum(present.astype(I32)) - 1).astype(I32)
    onehot_be = (block_e[:, None] == eids1[None, :]).astype(I32)
    blk_ids = jnp.arange(n_blk, dtype=I32)
    is_act = blk_ids < nact[0]
    prev_e = jnp.concatenate([jnp.full((1,), -1, I32), block_e[:-1]])
    first = (is_act & (block_e != prev_e)).astype(I32)
    nxt_e = jnp.sum(onehot_be * nxt_of_e[None, :], axis=1).astype(I32)
    wslot = (jnp.sum(onehot_be * ord_of_e[None, :], axis=1) % 2).astype(I32)
    start_of_e = jnp.sum(onehot_be * pstart[None, :], axis=1)
    cnt_of_e = jnp.sum(onehot_be * cnt[None, :], axis=1)
    valid = jnp.where(is_act, jnp.clip(cnt_of_e - (blk_row - start_of_e), 0, bm), 0).astype(I32)
    n_pk = d // (2 * LANES)
    dest3 = jnp.transpose(dest.reshape(TOP_K, t // SC_CHUNK, SC_CHUNK), (1, 0, 2))
    xs = _sc_dispatch(x1.reshape(t, n_pk, LANES), dest3, n_blk * bm)
    yb = _experts(xs.reshape(n_blk * bm * n_pk, LANES), block_e, nact, first, nxt_e, wslot, valid,
                  exp_w1, exp_w3, exp_w2, bm)

    slabs = _sc_gather(yb.reshape(n_blk * bm, n_pk, LANES), dest3, t)
    return _finalize(base, gate_t.T, slabs.reshape(TOP_K * t * n_pk, LANES),
                     ln2_g.reshape(1, d), ln2_b.reshape(1, d), xa.shape[0])


def kernel(x_prompt, x_sample, w_in, pool_w, pool_b, pool_scale, w_pool_br, w_attn_br, attn_rpb, w_out,
           ln1_g, ln1_b, router_w, router_bias, exp_w1, exp_w3, exp_w2, sh_w1, sh_w3, sh_w2, ln2_g, ln2_b):
    depth = w_in.shape[0]
    alpha = (2.0 * depth) ** 0.25
    bp, sp, d = x_prompt.shape
    bs, ss, _ = x_sample.shape
    seqs = [(i * sp, (i + 1) * sp) for i in range(bp)]
    seqs += [(bp * sp + i * ss, bp * sp + (i + 1) * ss) for i in range(bs)]
    xa, xs = x_prompt.reshape(bp * sp, d), x_sample.reshape(bs * ss, d)
    for l in range(depth):
        lp = (w_in[l], pool_w[l], pool_b[l], pool_scale[l], w_pool_br[l], w_attn_br[l], attn_rpb[l], w_out[l],
              ln1_g[l], ln1_b[l], router_w[l], router_bias[l], exp_w1[l], exp_w3[l], exp_w2[l],
              sh_w1[l], sh_w3[l], sh_w2[l], ln2_g[l], ln2_b[l])
        xa, xs = _encoder_layer(xa, xs, seqs, lp, alpha)
    return (xa.reshape(bp, sp, d), xs.reshape(bs, ss, d))
```

```python
import functools

import numpy as np
import jax
import jax.numpy as jnp
from jax import lax
from jax.experimental import pallas as pl
from jax.experimental.pallas import tpu as pltpu
from jax.experimental.pallas import tpu_sc as plsc

F32 = jnp.float32
BF16 = jnp.bfloat16
I32 = jnp.int32

GRID_W = 64
POOL_WINDOWS = (2, 4, 8, 16)
WIN_R = 8
WIN_C = 16
TOP_K = 8
N_GROUPS = 8
TOPK_GROUPS = 4
ROUTED_SCALE = 2.5
LN_EPS = 1e-5
NEG_BIG = -1e30

LANES = 128
SUBLANES = 8
MXU_N = 256
VMEM_LIMIT = 56 * 1024 * 1024

INPROJ_TM = 1024
INPROJ_TN = 1024
ATT_ROWS = 8
MIX_TM = 256
POOL_HALO = 16
EXPERT_BM = 256
FINAL_TM = 256


def _cparams(sem):
    return pltpu.CompilerParams(dimension_semantics=sem, vmem_limit_bytes=VMEM_LIMIT)


def _inproj_body(xa_ref, xs_ref, w_ref, o_ref, xb_ref, *, n_a, q_block, gate_block0, q_scale):
    i = pl.program_id(0)
    j = pl.program_id(1)

    @pl.when((j == 0) & (i < n_a))
    def _():
        xb_ref[...] = xa_ref[...].astype(BF16)

    @pl.when((j == 0) & (i >= n_a))
    def _():
        xb_ref[...] = xs_ref[...].astype(BF16)

    is_gate = j >= gate_block0
    scale = jnp.where(j == q_block, q_scale, 1.0).astype(F32)
    for c in range(w_ref.shape[1] // MXU_N):
        acc = jnp.dot(xb_ref[...], w_ref[:, c * MXU_N:(c + 1) * MXU_N], preferred_element_type=F32)
        out = jnp.where(is_gate, jax.nn.sigmoid(acc), acc * scale)
        o_ref[:, c * MXU_N:(c + 1) * MXU_N] = out.astype(BF16)


def _two_array_maps(n_a):
    return (lambda i, *_: (jnp.minimum(i, n_a - 1), 0)), (lambda i, *_: (jnp.maximum(i - n_a, 0), 0))


def _inproj(xa, xs, w_bf16, *, q_block, gate_block0, q_scale):
    d = xa.shape[1]
    t = xa.shape[0] + xs.shape[0]
    n = w_bf16.shape[1]
    tm, tn = INPROJ_TM, INPROJ_TN
    assert xa.shape[0] % tm == 0 and xs.shape[0] % tm == 0 and xa.shape[0] > 0 and xs.shape[0] > 0
    n_a = xa.shape[0] // tm
    map_a, map_s = _two_array_maps(n_a)
    return pl.pallas_call(
        functools.partial(_inproj_body, n_a=n_a, q_block=q_block, gate_block0=gate_block0, q_scale=q_scale),
        grid=(t // tm, n // tn),
        in_specs=[pl.BlockSpec((tm, d), map_a),
                  pl.BlockSpec((tm, d), map_s),
                  pl.BlockSpec((d, tn), lambda i, j: (0, j))],
        out_specs=pl.BlockSpec((tm, tn), lambda i, j: (i, j)),
        out_shape=jax.ShapeDtypeStruct((t, n), BF16),
        scratch_shapes=[pltpu.VMEM((tm, d), BF16)],
        compiler_params=_cparams(("arbitrary", "arbitrary")),
        name="inproj",
    )(xa, xs, w_bf16)


def _attn_body(prev_ref, next_ref, kst_ref, off_ref,
               q_ref, kp_ref, kc_ref, kn_ref, vp_ref, vc_ref, vn_ref, bias_ref, o_ref,
               kbuf, vbuf, *, heads_per_group, head_dim):
    del prev_ref, next_ref
    b = pl.program_id(0)
    blk = ATT_ROWS * GRID_W
    kbuf[0:blk] = kp_ref[...]
    kbuf[blk:2 * blk] = kc_ref[...]
    kbuf[2 * blk:3 * blk] = kn_ref[...]
    vbuf[0:blk] = vp_ref[...]
    vbuf[blk:2 * blk] = vc_ref[...]
    vbuf[2 * blk:3 * blk] = vn_ref[...]

    gw = heads_per_group * head_dim
    n_groups = q_ref.shape[1] // gw
    win = WIN_R * GRID_W
    nq = heads_per_group * GRID_W
    row_head = lax.broadcasted_iota(I32, (nq, gw), 0) // GRID_W
    lane_head = lax.broadcasted_iota(I32, (nq, gw), 1) // head_dim
    head_mask = row_head == lane_head
    out_lane_head = lax.broadcasted_iota(I32, (GRID_W, gw), 1) // head_dim

    def row_step(i, carry):
        st = pl.multiple_of(kst_ref[b * ATT_ROWS + i] * GRID_W, GRID_W)
        dr0 = (WIN_R - 1) - off_ref[b * ATT_ROWS + i]
        q0 = pl.multiple_of(i * GRID_W, GRID_W)
        for g in range(n_groups):
            qg = q_ref[pl.ds(q0, GRID_W), g * gw:(g + 1) * gw]
            kg = kbuf[pl.ds(st, win), g * gw:(g + 1) * gw]
            vg = vbuf[pl.ds(st, win), g * gw:(g + 1) * gw]
            wt = jnp.concatenate([qg] * heads_per_group, axis=0)
            wt = jnp.where(head_mask, wt, jnp.zeros_like(wt))
            s = lax.dot_general(wt, kg, (((1,), (1,)), ((), ())),
                                preferred_element_type=F32)
            n_chunk = win // LANES
            sj = [s[:, j * LANES:(j + 1) * LANES] + bias_ref[dr0 + 2 * j, g] for j in range(n_chunk)]
            m = sj[0]
            for j in range(1, n_chunk):
                m = jnp.maximum(m, sj[j])
            m = jnp.max(m, axis=1, keepdims=True)
            ej = [jnp.exp(x - m) for x in sj]
            l = ej[0]
            for j in range(1, n_chunk):
                l = l + ej[j]
            inv = 1.0 / jnp.sum(l, axis=1, keepdims=True)
            p = jnp.concatenate([(x * inv).astype(BF16) for x in ej], axis=1)
            of = jnp.dot(p, vg, preferred_element_type=F32)
            out = jnp.zeros((GRID_W, gw), F32)
            for h in range(heads_per_group):
                out = out + jnp.where(out_lane_head == h, of[h * GRID_W:(h + 1) * GRID_W, :], 0.0)
            o_ref[pl.ds(q0, GRID_W), g * gw:(g + 1) * gw] = out.astype(o_ref.dtype)
        return carry

    lax.fori_loop(0, ATT_ROWS, row_step, 0, unroll=4)


def _attn_tables(images):
    prev, nxt, kst, off = [], [], [], []
    for s, n in images:
        rows = n * ATT_ROWS
        kr = min(WIN_R, rows)
        assert kr == WIN_R
        for bl in range(n):
            b = s + bl
            prev.append(max(b - 1, s))
            nxt.append(min(b + 1, s + n - 1))
            for i in range(ATT_ROWS):
                r = bl * ATT_ROWS + i
                rs = min(max(r - kr // 2, 0), rows - kr)
                kst.append(rs - bl * ATT_ROWS + ATT_ROWS)
                off.append(r - rs)
    return (np.asarray(prev, np.int32), np.asarray(nxt, np.int32),
            np.asarray(kst, np.int32), np.asarray(off, np.int32))


def _attn_bias_table(rpb, heads_per_group):
    n_heads, n_dr, n_dc = rpb.shape
    n_pair = n_dr - 1
    c = np.arange(GRID_W)
    cs = np.clip(c - WIN_C // 2, 0, GRID_W - WIN_C)
    kc = np.arange(GRID_W)
    valid = (kc[None, :] >= cs[:, None]) & (kc[None, :] < cs[:, None] + WIN_C)
    dc = np.clip(kc[None, :] - c[:, None] + (WIN_C - 1), 0, n_dc - 1)
    sel = np.zeros((2, n_dc, GRID_W, 2, GRID_W), np.float32)
    ci, ki = np.nonzero(valid)
    for p in range(2):
        sel[p, dc[ci, ki], ci, p, ki] = 1.0
    sel = jnp.asarray(sel.reshape(2 * n_dc, GRID_W * 2 * GRID_W))
    r = rpb.astype(F32)
    pair = jnp.stack([r[:, :n_pair], r[:, 1:]], axis=2)
    pair = jnp.transpose(pair, (1, 0, 2, 3)).reshape(n_pair * n_heads, 2 * n_dc)
    tab = jnp.dot(pair, sel, precision=lax.Precision.HIGHEST)
    tab = tab.reshape(n_pair, n_heads, GRID_W, 2 * GRID_W)
    valid2 = jnp.asarray(np.tile(valid[:, None, :], (1, 2, 1)).reshape(GRID_W, 2 * GRID_W))
    tab = jnp.where(valid2[None, None], tab, NEG_BIG)
    return tab.reshape(n_pair, n_heads // heads_per_group, heads_per_group * GRID_W, 2 * GRID_W)


def _attention(h, bias_tab, images, *, q_block, n_heads, head_dim):
    t = h.shape[0]
    blk = ATT_ROWS * GRID_W
    aw = n_heads * head_dim
    heads_per_group = 2 * LANES // head_dim
    prev, nxt, kst, off = _attn_tables(images)
    n_blocks = t // blk
    assert prev.shape[0] == n_blocks
    kcol, vcol = q_block + 1, q_block + 2
    spec = lambda col, which: pl.BlockSpec(
        (blk, aw), {"cur": lambda b, p, n, k, o: (b, col),
                    "prev": lambda b, p, n, k, o: (p[b], col),
                    "next": lambda b, p, n, k, o: (n[b], col)}[which])
    grid_spec = pltpu.PrefetchScalarGridSpec(
        num_scalar_prefetch=4,
        grid=(n_blocks,),
        in_specs=[spec(q_block, "cur"),
                  spec(kcol, "prev"), spec(kcol, "cur"), spec(kcol, "next"),
                  spec(vcol, "prev"), spec(vcol, "cur"), spec(vcol, "next"),
                  pl.BlockSpec(bias_tab.shape, lambda b, p, n, k, o: (0, 0, 0, 0),
                               pipeline_mode=pl.Buffered(1))],
        out_specs=pl.BlockSpec((blk, aw), lambda b, p, n, k, o: (b, 0)),
        scratch_shapes=[pltpu.VMEM((3 * blk, aw), BF16), pltpu.VMEM((3 * blk, aw), BF16)],
    )
    return pl.pallas_call(
        functools.partial(_attn_body, heads_per_group=heads_per_group, head_dim=head_dim),
        grid_spec=grid_spec,
        out_shape=jax.ShapeDtypeStruct((t, aw), BF16),
        compiler_params=_cparams(("arbitrary",)),
        name="natten",
    )(jnp.asarray(prev), jnp.asarray(nxt), jnp.asarray(kst), jnp.asarray(off),
      h, h, h, h, h, h, h, bias_tab)


def _layer_norm(r, g, b):
    mu = jnp.mean(r, axis=-1, keepdims=True)
    c = r - mu
    var = jnp.mean(c * c, axis=-1, keepdims=True)
    return c * lax.rsqrt(var + LN_EPS) * g + b


HI_MASK = -65536


def _pack_bf16_pair(lo, hi):
    lo_bits = lax.bitcast_convert_type(lo.astype(BF16).astype(F32), I32)
    hi_bits = lax.bitcast_convert_type(hi.astype(BF16).astype(F32), I32)
    return (hi_bits & HI_MASK) | lax.shift_right_logical(lo_bits, 16)


def _unpack_bf16_pair(word):
    lo = lax.bitcast_convert_type(lax.shift_left(word, 16), F32)
    hi = lax.bitcast_convert_type(word & HI_MASK, F32)
    return lo, hi


def _first_index_of(mask, idx, sentinel, axis):
    return jnp.min(jnp.where(mask, idx, sentinel), axis=axis, keepdims=True)


def _mixer_body(seq_lo_ref, seq_hi_ref,
                xa_ref, xs_ref, up_ref, uc_ref, un_ref, gp_ref, ga_ref, at_ref,
                pw_ref, pb_ref, ps_ref, wpb_ref, wab_ref, wo_ref, g1_ref, b1_ref,
                s13_ref, s2_ref, rwh_ref, rwl_ref, rb_ref,
                x1_ref, base_ref, idx_ref, gate_ref, rank_ref, cnt_ref,
                carry_ref, *, n_a, alpha, n_experts):
    i = pl.program_id(0)
    tm = xa_ref.shape[0]
    t0 = i * tm
    lo_seq = seq_lo_ref[i]
    hi_seq = seq_hi_ref[i]

    @pl.when(i == 0)
    def _():
        carry_ref[...] = jnp.zeros_like(carry_ref)

    u_ext = jnp.concatenate([up_ref[...], uc_ref[...], un_ref[...]], axis=0)
    ext = tm + 2 * POOL_HALO
    tok_r = t0 + lax.broadcasted_iota(I32, (tm, ext), 0)
    tok_c = t0 - POOL_HALO + lax.broadcasted_iota(I32, (tm, ext), 1)
    tok_q = t0 + lax.broadcasted_iota(I32, (tm, 1), 0)
    n_pg = pw_ref.shape[0]
    pg = pw_ref.shape[1]
    parts = []
    for g in range(n_pg):
        w = POOL_WINDOWS[g]
        lo = jnp.maximum(tok_r - w // 2, lo_seq)
        hi = jnp.minimum(tok_r - w // 2 + w, hi_seq)
        band = jnp.where((tok_c >= lo) & (tok_c < hi), 1.0, 0.0).astype(BF16)
        cnt = (jnp.minimum(tok_q - w // 2 + w, hi_seq) - jnp.maximum(tok_q - w // 2, lo_seq)).astype(F32)
        ug = u_ext[:, g * pg:(g + 1) * pg]
        sums = jnp.dot(band, ug, preferred_element_type=F32)
        z = sums / cnt - uc_ref[:, g * pg:(g + 1) * pg].astype(F32)
        y = jnp.dot(z.astype(BF16), pw_ref[g], preferred_element_type=F32)
        y = (y + pb_ref[:, g * pg:(g + 1) * pg]) * ps_ref[:, g * pg:(g + 1) * pg]
        parts.append(y.astype(BF16))
    a = jnp.concatenate(parts, axis=1)
    a_pool = jnp.dot(a, wpb_ref[...], preferred_element_type=F32)
    a_attn = jnp.dot(at_ref[...], wab_ref[...], preferred_element_type=F32)
    merged = gp_ref[...].astype(F32) * a_pool + ga_ref[...].astype(F32) * a_attn
    mix = jnp.dot(merged.astype(BF16), wo_ref[...], preferred_element_type=F32)
    x_in = jnp.where(i < n_a, xa_ref[...], xs_ref[...])
    x1 = _layer_norm(alpha * x_in + mix, g1_ref[...], b1_ref[...])
    half = x1.shape[1] // 2
    n_pk = half // LANES
    word = _pack_bf16_pair(x1[:, :half], x1[:, half:])
    for c in range(n_pk):
        x1_ref[pl.ds(c, tm, stride=n_pk), :] = word[:, c * LANES:(c + 1) * LANES]
    x1b = x1.astype(BF16)

    sd = s2_ref.shape[0]
    hc = jnp.dot(x1b, s13_ref[...], preferred_element_type=F32)
    hdn = jax.nn.silu(hc[:, :sd]) * hc[:, sd:]
    shared = jnp.dot(hdn.astype(BF16), s2_ref[...], preferred_element_type=F32)
    base_ref[...] = alpha * x1 + shared

    x1l = (x1 - x1b.astype(F32)).astype(BF16)
    nt = (((1,), (1,)), ((), ()))
    logits = (lax.dot_general(rwh_ref[...], x1b, nt, preferred_element_type=F32)
              + lax.dot_general(rwh_ref[...], x1l, nt, preferred_element_type=F32)
              + lax.dot_general(rwl_ref[...], x1b, nt, preferred_element_type=F32))
    scores = jax.nn.sigmoid(logits)
    biased = scores + rb_ref[...]
    gsz = n_experts // N_GROUPS
    neg_inf = -jnp.inf

    mem = lax.broadcasted_iota(I32, (gsz, tm), 0)
    gs_rows = []
    for g in range(N_GROUPS):
        bg = biased[g * gsz:(g + 1) * gsz, :]
        m1 = jnp.max(bg, axis=0, keepdims=True)
        first = _first_index_of(bg == m1, mem, gsz, 0)
        m2 = jnp.max(jnp.where(mem == first, neg_inf, bg), axis=0, keepdims=True)
        gs_rows.append(m1 + m2)
    gwork = jnp.concatenate(gs_rows, axis=0)

    gidx = lax.broadcasted_iota(I32, (N_GROUPS, tm), 0)
    egrp = lax.broadcasted_iota(I32, (n_experts, tm), 0) // gsz
    eidx = lax.broadcasted_iota(I32, (n_experts, tm), 0)
    work = jnp.full((n_experts, tm), neg_inf, F32)
    for _ in range(TOPK_GROUPS):
        gm = jnp.max(gwork, axis=0, keepdims=True)
        gf = _first_index_of(gwork == gm, gidx, N_GROUPS, 0)
        gwork = jnp.where(gidx == gf, neg_inf, gwork)
        work = jnp.where(egrp == gf, biased, work)

    hot = jnp.zeros((n_experts, tm), F32)
    sel_idx, sel_w, sel_hit = [], [], []
    for _ in range(TOP_K):
        em = jnp.max(work, axis=0, keepdims=True)
        ef = _first_index_of(work == em, eidx, n_experts, 0)
        hit = eidx == ef
        sel_idx.append(ef)
        sel_w.append(jnp.sum(jnp.where(hit, scores, 0.0), axis=0, keepdims=True))
        sel_hit.append(hit)
        hot = jnp.where(hit, 1.0, hot)
        work = jnp.where(hit, neg_inf, work)
    wsel = jnp.concatenate(sel_w, axis=0)
    gate_ref[...] = wsel / jnp.sum(wsel, axis=0, keepdims=True) * ROUTED_SCALE
    idx_ref[...] = jnp.concatenate(sel_idx, axis=0)

    tri = jnp.where(lax.broadcasted_iota(I32, (tm, tm), 0) < lax.broadcasted_iota(I32, (tm, tm), 1),
                    1.0, 0.0).astype(BF16)
    before = carry_ref[:, 0:1] + jnp.dot(hot.astype(BF16), tri, preferred_element_type=F32)
    rank = [jnp.sum(jnp.where(h, before, 0.0), axis=0, keepdims=True) for h in sel_hit]
    rank_ref[...] = jnp.concatenate(rank, axis=0).astype(I32)
    carry_ref[...] = carry_ref[...] + jnp.sum(hot, axis=1, keepdims=True)
    cnt_ref[...] = carry_ref[...]


def _mixer(xa, xs, h, attn, seq_lo, seq_hi, p, *, alpha, u_block, gate_block0):
    d = xa.shape[1]
    t = xa.shape[0] + xs.shape[0]
    tm = MIX_TM
    assert xa.shape[0] % tm == 0 and xs.shape[0] % tm == 0
    n_a = xa.shape[0] // tm
    map_a, map_s = _two_array_maps(n_a)
    pw = p["wpb"].shape[0]
    aw = attn.shape[1]
    n_experts = p["rwh"].shape[0]
    halo_per_tile = tm // POOL_HALO
    n_halo_blocks = t // POOL_HALO
    gp_col = gate_block0 * INPROJ_TN // d
    in_specs = [
        pl.BlockSpec((tm, d), map_a),
        pl.BlockSpec((tm, d), map_s),
        pl.BlockSpec((POOL_HALO, pw), lambda i, lo, hi: (jnp.maximum(i * halo_per_tile - 1, 0), u_block)),
        pl.BlockSpec((tm, pw), lambda i, lo, hi: (i, u_block)),
        pl.BlockSpec((POOL_HALO, pw),
                     lambda i, lo, hi: (jnp.minimum((i + 1) * halo_per_tile, n_halo_blocks - 1), u_block)),
        pl.BlockSpec((tm, d), lambda i, lo, hi: (i, gp_col)),
        pl.BlockSpec((tm, d), lambda i, lo, hi: (i, gp_col + 1)),
        pl.BlockSpec((tm, aw), lambda i, lo, hi: (i, 0)),
    ]
    weights = [p["pool_w"], p["pool_b"], p["pool_scale"], p["wpb"], p["wab"], p["wo"], p["ln1_g"], p["ln1_b"],
               p["s13"], p["s2"], p["rwh"], p["rwl"], p["rb"]]
    in_specs += [pl.BlockSpec(w.shape, (lambda nd: (lambda i, lo, hi: (0,) * nd))(w.ndim),
                              pipeline_mode=pl.Buffered(1)) for w in weights]
    n_pk = d // (2 * LANES)
    out_shape = (jax.ShapeDtypeStruct((t * n_pk, LANES), I32),
                 jax.ShapeDtypeStruct((t, d), F32),
                 jax.ShapeDtypeStruct((TOP_K, t), I32),
                 jax.ShapeDtypeStruct((TOP_K, t), F32),
                 jax.ShapeDtypeStruct((TOP_K, t), I32),
                 jax.ShapeDtypeStruct((n_experts, LANES), F32))
    out_specs = (pl.BlockSpec((tm * n_pk, LANES), lambda i, lo, hi: (i, 0)),
                 pl.BlockSpec((tm, d), lambda i, lo, hi: (i, 0)),
                 pl.BlockSpec((TOP_K, tm), lambda i, lo, hi: (0, i)),
                 pl.BlockSpec((TOP_K, tm), lambda i, lo, hi: (0, i)),
                 pl.BlockSpec((TOP_K, tm), lambda i, lo, hi: (0, i)),
                 pl.BlockSpec((n_experts, LANES), lambda i, lo, hi: (0, 0)))
    grid_spec = pltpu.PrefetchScalarGridSpec(
        num_scalar_prefetch=2, grid=(t // tm,), in_specs=in_specs, out_specs=out_specs,
        scratch_shapes=[pltpu.VMEM((n_experts, LANES), F32)])
    return pl.pallas_call(
        functools.partial(_mixer_body, n_a=n_a, alpha=alpha, n_experts=n_experts),
        grid_spec=grid_spec, out_shape=out_shape,
        compiler_params=_cparams(("arbitrary",)),
        name="mixer",
    )(seq_lo, seq_hi, xa, xs, h, h, h, h, h, attn, *weights)


SC_CORES = 2
SC_SUBCORES = 16
SC_CHUNK = 32


def _sc_dispatch(x1t3, dest3, n_rows):
    t, n_lt, _ = x1t3.shape
    n_workers = SC_CORES * SC_SUBCORES
    assert t % (n_workers * SC_CHUNK) == 0
    chunks_per_worker = t // (n_workers * SC_CHUNK)
    mesh = plsc.VectorSubcoreMesh(core_axis_name="c", subcore_axis_name="s",
                                  num_cores=SC_CORES, num_subcores=SC_SUBCORES)

    @functools.partial(
        pl.kernel, mesh=mesh,
        out_type=jax.ShapeDtypeStruct((n_rows, n_lt, LANES), x1t3.dtype),
        scratch_types=[pltpu.VMEM((TOP_K, SC_CHUNK), I32),
                       pltpu.VMEM((SC_CHUNK, n_lt, LANES), x1t3.dtype),
                       pltpu.SemaphoreType.DMA],
        name="sc_dispatch")
    def dispatch(x_hbm, dest_hbm, out_hbm, idx_v, rows_v, sem):
        wid = lax.axis_index("s") * SC_CORES + lax.axis_index("c")

        @pl.loop(0, chunks_per_worker)
        def _(j):
            chunk = wid * chunks_per_worker + j
            base = pl.multiple_of(chunk * SC_CHUNK, SC_CHUNK)
            pltpu.sync_copy(x_hbm.at[pl.ds(base, SC_CHUNK)], rows_v)
            pltpu.sync_copy(dest_hbm.at[chunk], idx_v)
            copies = [pltpu.make_async_copy(rows_v, out_hbm.at[idx_v.at[k]], sem) for k in range(TOP_K)]
            for cp in copies:
                cp.start()
            for cp in copies:
                cp.wait()

    return dispatch(x1t3, dest3)


SC_RING = 3


def _sc_gather(yb3, dest3, t):
    _, n_lt, _ = yb3.shape
    n_workers = SC_CORES * SC_SUBCORES
    chunks_per_worker = t // (n_workers * SC_CHUNK)
    mesh = plsc.VectorSubcoreMesh(core_axis_name="c", subcore_axis_name="s",
                                  num_cores=SC_CORES, num_subcores=SC_SUBCORES)

    @functools.partial(
        pl.kernel, mesh=mesh,
        out_type=jax.ShapeDtypeStruct((TOP_K * t, n_lt, LANES), yb3.dtype),
        scratch_types=[pltpu.VMEM((TOP_K, SC_CHUNK), I32),
                       pltpu.VMEM((SC_RING, SC_CHUNK, n_lt, LANES), yb3.dtype)]
                      + [pltpu.SemaphoreType.DMA] * (2 * SC_RING),
        name="sc_gather")
    def gather(y_hbm, dest_hbm, out_hbm, idx_v, rows_v, *sems):
        gsem, wsem = sems[:SC_RING], sems[SC_RING:]
        wid = lax.axis_index("s") * SC_CORES + lax.axis_index("c")

        @pl.loop(0, chunks_per_worker)
        def _(j):
            chunk = wid * chunks_per_worker + j
            base = pl.multiple_of(chunk * SC_CHUNK, SC_CHUNK)
            pltpu.sync_copy(dest_hbm.at[chunk], idx_v)
            gets = [pltpu.make_async_copy(y_hbm.at[idx_v.at[k]], rows_v.at[k % SC_RING], gsem[k % SC_RING])
                    for k in range(TOP_K)]
            puts = [pltpu.make_async_copy(rows_v.at[k % SC_RING], out_hbm.at[pl.ds(k * t + base, SC_CHUNK)],
                                          wsem[k % SC_RING]) for k in range(TOP_K)]
            put_waited = set()
            for k in range(SC_RING - 1):
                gets[k].start()
            for k in range(TOP_K):
                gets[k].wait()
                puts[k].start()
                nxt = k + SC_RING - 1
                if nxt < TOP_K:
                    if k >= 1:
                        puts[k - 1].wait()
                        put_waited.add(k - 1)
                    gets[nxt].start()
            for k in range(TOP_K):
                if k not in put_waited:
                    puts[k].wait()

    return gather(yb3, dest3)


def _experts_body(be_ref, nact_ref, first_ref, nxte_ref, wslot_ref, valid_ref,
                  xs_ref, w1_hbm, w3_hbm, w2_hbm, o_ref,
                  wst1, wst3, wst2, w13b, w2b, sem_w):
    b = pl.program_id(0)
    nact = nact_ref[0]
    ed = wst2.shape[1]
    d = wst2.shape[2]
    n_pk = d // (2 * LANES)
    bm = xs_ref.shape[0] // n_pk
    active = b < nact

    def weight_copies(e, s):
        return (pltpu.make_async_copy(w1_hbm.at[e], wst1.at[s], sem_w.at[s]),
                pltpu.make_async_copy(w3_hbm.at[e], wst3.at[s], sem_w.at[s]),
                pltpu.make_async_copy(w2_hbm.at[e], wst2.at[s], sem_w.at[s]))

    @pl.when(b == 0)
    def _():
        for cp in weight_copies(be_ref[0], wslot_ref[0]):
            cp.start()

    @pl.when(first_ref[b] == 1)
    def _():
        ws = wslot_ref[b]
        for cp in weight_copies(be_ref[b], ws):
            cp.wait()

        @pl.when(nxte_ref[b] >= 0)
        def _():
            for cp in weight_copies(nxte_ref[b], 1 - ws):
                cp.start()

        for c in range(ed // MXU_N):
            w13b[:, 2 * c * MXU_N:(2 * c + 1) * MXU_N] = wst1[ws, :, c * MXU_N:(c + 1) * MXU_N].astype(BF16)
            w13b[:, (2 * c + 1) * MXU_N:(2 * c + 2) * MXU_N] = wst3[ws, :, c * MXU_N:(c + 1) * MXU_N].astype(BF16)
        w2b[...] = wst2[ws].astype(BF16)

    @pl.when(active)
    def _():
        keep = lax.broadcasted_iota(I32, (bm, LANES), 0) < valid_ref[b]
        los, his = [], []
        for c in range(n_pk):
            lo, hi = _unpack_bf16_pair(jnp.where(keep, xs_ref[pl.ds(c, bm, stride=n_pk), :], 0))
            los.append(lo.astype(BF16))
            his.append(hi.astype(BF16))
        xb = jnp.concatenate(los + his, axis=1)
        parts = []
        for c in range(ed // MXU_N):
            hc = jnp.dot(xb, w13b[:, 2 * c * MXU_N:(2 * c + 2) * MXU_N], preferred_element_type=F32)
            parts.append((jax.nn.silu(hc[:, :MXU_N]) * hc[:, MXU_N:]).astype(BF16))
        hdn = jnp.concatenate(parts, axis=1)
        n_down = d // MXU_N
        ys = [jnp.dot(hdn, w2b[:, c * MXU_N:(c + 1) * MXU_N], preferred_element_type=F32) for c in range(n_down)]
        tiles_per_chunk = MXU_N // LANES
        for c in range(n_down // 2):
            word = _pack_bf16_pair(ys[c], ys[c + n_down // 2])
            for h in range(tiles_per_chunk):
                o_ref[pl.ds(c * tiles_per_chunk + h, bm, stride=n_pk), :] = word[:, h * LANES:(h + 1) * LANES]

    @pl.when(jnp.logical_not(active))
    def _():
        o_ref[...] = jnp.zeros_like(o_ref)


def _experts(xs2d, block_e, nact, first, nxt_e, wslot, valid, w1, w3, w2, bm):
    d, ed = w1.shape[1], w1.shape[2]
    n_pk = d // (2 * LANES)
    n_blk = block_e.shape[0]
    assert xs2d.shape == (n_blk * bm * n_pk, LANES)
    assert ed % MXU_N == 0 and d % MXU_N == 0 and bm % SUBLANES == 0
    grid_spec = pltpu.PrefetchScalarGridSpec(
        num_scalar_prefetch=6,
        grid=(n_blk,),
        in_specs=[pl.BlockSpec((bm * n_pk, LANES),
                               lambda b, be, na, *_: (jnp.maximum(jnp.minimum(b, na[0] - 1), 0), 0)),
                  pl.BlockSpec(memory_space=pl.ANY),
                  pl.BlockSpec(memory_space=pl.ANY),
                  pl.BlockSpec(memory_space=pl.ANY)],
        out_specs=pl.BlockSpec((bm * n_pk, LANES), lambda b, *_: (b, 0)),
        scratch_shapes=[pltpu.VMEM((2, d, ed), F32),
                        pltpu.VMEM((2, d, ed), F32),
                        pltpu.VMEM((2, ed, d), F32),
                        pltpu.VMEM((d, 2 * ed), BF16),
                        pltpu.VMEM((ed, d), BF16),
                        pltpu.SemaphoreType.DMA((2,))],
    )
    return pl.pallas_call(
        _experts_body,
        grid_spec=grid_spec,
        out_shape=jax.ShapeDtypeStruct((n_blk * bm * n_pk, LANES), I32),
        compiler_params=_cparams(("arbitrary",)),
        name="experts",
    )(block_e, nact, first, nxt_e, wslot, valid, xs2d, w1, w3, w2)


def _finalize_body(base_ref, gate_ref, g2_ref, b2_ref, *refs, n_a):
    slabs, (oa_ref, os_ref) = refs[:TOP_K], refs[TOP_K:]
    i = pl.program_id(0)
    tm, d = base_ref.shape
    half = d // 2
    n_pk = half // LANES
    acc_lo = [base_ref[:, c * LANES:(c + 1) * LANES] for c in range(n_pk)]
    acc_hi = [base_ref[:, half + c * LANES:half + (c + 1) * LANES] for c in range(n_pk)]
    for k in range(TOP_K):
        g = gate_ref[:, k:k + 1]
        for c in range(n_pk):
            lo, hi = _unpack_bf16_pair(slabs[k][pl.ds(c, tm, stride=n_pk), :])
            acc_lo[c] = acc_lo[c] + g * lo
            acc_hi[c] = acc_hi[c] + g * hi
    y = _layer_norm(jnp.concatenate(acc_lo + acc_hi, axis=1), g2_ref[...], b2_ref[...])

    @pl.when(i < n_a)
    def _():
        oa_ref[...] = y

    @pl.when(i >= n_a)
    def _():
        os_ref[...] = y


def _finalize(base, gate_t, slabs2d, ln_g, ln_b, t_a):
    t, d = base.shape
    tm = FINAL_TM
    n_pk = d // (2 * LANES)
    assert t_a % tm == 0 and 0 < t_a < t and slabs2d.shape == (TOP_K * t * n_pk, LANES)
    n_a = t_a // tm
    n_steps = t // tm
    map_a, map_s = _two_array_maps(n_a)
    slab_specs = [pl.BlockSpec((tm * n_pk, LANES), (lambda k: (lambda i: (k * n_steps + i, 0)))(k))
                  for k in range(TOP_K)]
    return pl.pallas_call(
        functools.partial(_finalize_body, n_a=n_a),
        grid=(n_steps,),
        in_specs=[pl.BlockSpec((tm, d), lambda i: (i, 0)),
                  pl.BlockSpec((tm, TOP_K), lambda i: (i, 0)),
                  pl.BlockSpec((1, d), lambda i: (0, 0)),
                  pl.BlockSpec((1, d), lambda i: (0, 0))] + slab_specs,
        out_specs=(pl.BlockSpec((tm, d), map_a), pl.BlockSpec((tm, d), map_s)),
        out_shape=(jax.ShapeDtypeStruct((t_a, d), F32), jax.ShapeDtypeStruct((t - t_a, d), F32)),
        compiler_params=_cparams(("arbitrary",)),
        name="finalize",
    )(base, gate_t, ln_g, ln_b, *([slabs2d] * TOP_K))


def _split_hi_lo(w):
    hi = w.astype(BF16)
    lo = (w - hi.astype(F32)).astype(BF16)
    return hi, lo


def _encoder_layer(xa, xs, seqs, lp, alpha):
    (w_in, pool_w, pool_b, pool_scale, w_pool_br, w_attn_br, attn_rpb, w_out, ln1_g, ln1_b,
     router_w, router_bias, exp_w1, exp_w3, exp_w2, sh_w1, sh_w3, sh_w2, ln2_g, ln2_b) = lp
    d = xa.shape[1]
    t = xa.shape[0] + xs.shape[0]
    pw = w_pool_br.shape[0]
    aw = w_attn_br.shape[0]
    n_heads = attn_rpb.shape[0]
    head_dim = aw // n_heads
    n_experts = router_w.shape[1]
    assert pw == INPROJ_TN and aw == INPROJ_TN and d % INPROJ_TN == 0
    u_block, q_block = 0, pw // INPROJ_TN
    gate_block0 = (pw + 3 * aw) // INPROJ_TN

    h = _inproj(xa, xs, w_in.astype(BF16), q_block=q_block, gate_block0=gate_block0, q_scale=head_dim ** -0.5)

    blk = ATT_ROWS * GRID_W
    images, seq_lo, seq_hi = [], [], []
    for s0, s1 in seqs:
        images.append((s0 // blk, (s1 - s0) // blk))
        for _ in range((s1 - s0) // MIX_TM):
            seq_lo.append(s0)
            seq_hi.append(s1)
    bias_tab = _attn_bias_table(attn_rpb, 2 * LANES // head_dim)
    attn = _attention(h, bias_tab, images, q_block=q_block, n_heads=n_heads, head_dim=head_dim)

    rwh, rwl = _split_hi_lo(router_w.T)
    p = dict(pool_w=pool_w.astype(BF16), pool_b=pool_b.reshape(1, pw), pool_scale=pool_scale.reshape(1, pw),
             wpb=w_pool_br.astype(BF16), wab=w_attn_br.astype(BF16), wo=w_out.astype(BF16),
             ln1_g=ln1_g.reshape(1, d), ln1_b=ln1_b.reshape(1, d),
             s13=jnp.concatenate([sh_w1, sh_w3], axis=1).astype(BF16), s2=sh_w2.astype(BF16),
             rwh=rwh, rwl=rwl, rb=router_bias.reshape(n_experts, 1).astype(F32))
    x1, base, idx_t, gate_t, rank_t, counts = _mixer(
        xa, xs, h, attn, jnp.asarray(seq_lo, I32), jnp.asarray(seq_hi, I32), p,
        alpha=alpha, u_block=u_block, gate_block0=gate_block0)

    bm = EXPERT_BM
    cnt = counts[:, 0].astype(I32)
    pcnt = (cnt + bm - 1) // bm * bm
    pend = jnp.cumsum(pcnt)
    pstart = pend - pcnt
    eids = jnp.arange(n_experts, dtype=I32)[:, None, None]
    dest = jnp.sum(jnp.where(idx_t[None] == eids, pstart[:, None, None], 0), axis=0) + rank_t
    n_blk = (t * TOP_K + n_experts * (bm - 1)) // bm
    blk_row = jnp.arange(n_blk, dtype=I32) * bm
    block_e = jnp.minimum(jnp.sum((pend[None, :] <= blk_row[:, None]).astype(I32), axis=1), n_experts - 1)
    nact = (pend[-1] // bm).astype(I32).reshape(1)
    block_e = jnp.where(jnp.arange(n_blk) < nact[0], block_e, block_e[jnp.maximum(nact[0] - 1, 0)])
    eids1 = jnp.arange(n_experts, dtype=I32)
    present = cnt > 0
    later = present[None, :] & (eids1[None, :] > eids1[:, None])
    nxt_of_e = jnp.min(jnp.where(later, eids1[None, :], n_experts), axis=1)
    nxt_of_e = jnp.where(nxt_of_e >= n_experts, -1, nxt_of_e).astype(I32)
    ord_of_e = (jnp.cumsum(present.astype(I32)) - 1).astype(I32)
    onehot_be = (block_e[:, None] == eids1[None, :]).astype(I32)
    blk_ids = jnp.arange(n_blk, dtype=I32)
    is_act = blk_ids < nact[0]
    prev_e = jnp.concatenate([jnp.full((1,), -1, I32), block_e[:-1]])
    first = (is_act & (block_e != prev_e)).astype(I32)
    nxt_e = jnp.sum(onehot_be * nxt_of_e[None, :], axis=1).astype(I32)
    wslot = (jnp.sum(onehot_be * ord_of_e[None, :], axis=1) % 2).astype(I32)
    start_of_e = jnp.sum(onehot_be * pstart[None, :], axis=1)
    cnt_of_e = jnp.sum(onehot_be * cnt[None, :], axis=1)
    valid = jnp.where(is_act, jnp.clip(cnt_of_e - (blk_row - start_of_e), 0, bm), 0).astype(I32)
    n_pk = d // (2 * LANES)
    dest3 = jnp.transpose(dest.reshape(TOP_K, t // SC_CHUNK, SC_CHUNK), (1, 0, 2))
    xs = _sc_dispatch(x1.reshape(t, n_pk, LANES), dest3, n_blk * bm)
    yb = _experts(xs.reshape(n_blk * bm * n_pk, LANES), block_e, nact, first, nxt_e, wslot, valid,
                  exp_w1, exp_w3, exp_w2, bm)

    slabs = _sc_gather(yb.reshape(n_blk * bm, n_pk, LANES), dest3, t)
    return _finalize(base, gate_t.T, slabs.reshape(TOP_K * t * n_pk, LANES),
                     ln2_g.reshape(1, d), ln2_b.reshape(1, d), xa.shape[0])


def kernel(x_prompt, x_sample, w_in, pool_w, pool_b, pool_scale, w_pool_br, w_attn_br, attn_rpb, w_out,
           ln1_g, ln1_b, router_w, router_bias, exp_w1, exp_w3, exp_w2, sh_w1, sh_w3, sh_w2, ln2_g, ln2_b):
    depth = w_in.shape[0]
    alpha = (2.0 * depth) ** 0.25
    bp, sp, d = x_prompt.shape
    bs, ss, _ = x_sample.shape
    seqs = [(i * sp, (i + 1) * sp) for i in range(bp)]
    seqs += [(bp * sp + i * ss, bp * sp + (i + 1) * ss) for i in range(bs)]
    xa, xs = x_prompt.reshape(bp * sp, d), x_sample.reshape(bs * ss, d)
    for l in range(depth):
        lp = (w_in[l], pool_w[l], pool_b[l], pool_scale[l], w_pool_br[l], w_attn_br[l], attn_rpb[l], w_out[l],
              ln1_g[l], ln1_b[l], router_w[l], router_bias[l], exp_w1[l], exp_w3[l], exp_w2[l],
              sh_w1[l], sh_w3[l], sh_w2[l], ln2_g[l], ln2_b[l])
        xa, xs = _encoder_layer(xa, xs, seqs, lp, alpha)
    return (xa.reshape(bp, sp, d), xs.reshape(bs, ss, d))
```

```python
import functools

import numpy as np
import jax
import jax.numpy as jnp
from jax import lax
from jax.experimental import pallas as pl
from jax.experimental.pallas import tpu as pltpu
from jax.experimental.pallas import tpu_sc as plsc

F32 = jnp.float32
BF16 = jnp.bfloat16
I32 = jnp.int32

GRID_W = 64
POOL_WINDOWS = (2, 4, 8, 16)
WIN_R = 8
WIN_C = 16
TOP_K = 8
N_GROUPS = 8
TOPK_GROUPS = 4
ROUTED_SCALE = 2.5
LN_EPS = 1e-5
NEG_BIG = -1e30

LANES = 128
SUBLANES = 8
MXU_N = 256
VMEM_LIMIT = 56 * 1024 * 1024

INPROJ_TM = 1024
INPROJ_TN = 1024
ATT_ROWS = 8
MIX_TM = 256
POOL_HALO = 16
EXPERT_BM = 256
FINAL_TM = 256
SHARED_TM = 512


def _cparams(sem):
    return pltpu.CompilerParams(dimension_semantics=sem, vmem_limit_bytes=VMEM_LIMIT)


def _inproj_body(xa_ref, xs_ref, w_ref, o_ref, xb_ref, *, n_a, q_block, gate_block0, q_scale):
    i = pl.program_id(0)
    j = pl.program_id(1)

    @pl.when((j == 0) & (i < n_a))
    def _():
        xb_ref[...] = xa_ref[...].astype(BF16)

    @pl.when((j == 0) & (i >= n_a))
    def _():
        xb_ref[...] = xs_ref[...].astype(BF16)

    is_gate = j >= gate_block0
    scale = jnp.where(j == q_block, q_scale, 1.0).astype(F32)
    for c in range(w_ref.shape[1] // MXU_N):
        acc = jnp.dot(xb_ref[...], w_ref[:, c * MXU_N:(c + 1) * MXU_N], preferred_element_type=F32)
        out = jnp.where(is_gate, jax.nn.sigmoid(acc), acc * scale)
        o_ref[:, c * MXU_N:(c + 1) * MXU_N] = out.astype(BF16)


def _two_array_maps(n_a):
    return (lambda i, *_: (jnp.minimum(i, n_a - 1), 0)), (lambda i, *_: (jnp.maximum(i - n_a, 0), 0))


def _inproj(xa, xs, w_bf16, *, q_block, gate_block0, q_scale):
    d = xa.shape[1]
    t = xa.shape[0] + xs.shape[0]
    n = w_bf16.shape[1]
    tm, tn = INPROJ_TM, INPROJ_TN
    assert xa.shape[0] % tm == 0 and xs.shape[0] % tm == 0 and xa.shape[0] > 0 and xs.shape[0] > 0
    n_a = xa.shape[0] // tm
    map_a, map_s = _two_array_maps(n_a)
    return pl.pallas_call(
        functools.partial(_inproj_body, n_a=n_a, q_block=q_block, gate_block0=gate_block0, q_scale=q_scale),
        grid=(t // tm, n // tn),
        in_specs=[pl.BlockSpec((tm, d), map_a),
                  pl.BlockSpec((tm, d), map_s),
                  pl.BlockSpec((d, tn), lambda i, j: (0, j))],
        out_specs=pl.BlockSpec((tm, tn), lambda i, j: (i, j)),
        out_shape=jax.ShapeDtypeStruct((t, n), BF16),
        scratch_shapes=[pltpu.VMEM((tm, d), BF16)],
        compiler_params=_cparams(("arbitrary", "arbitrary")),
        name="inproj",
    )(xa, xs, w_bf16)


def _attn_body(prev_ref, next_ref, kst_ref, off_ref,
               q_ref, kp_ref, kc_ref, kn_ref, vp_ref, vc_ref, vn_ref, bias_ref, o_ref,
               kbuf, vbuf, *, heads_per_group, head_dim):
    del prev_ref, next_ref
    b = pl.program_id(0)
    blk = ATT_ROWS * GRID_W
    kbuf[0:blk] = kp_ref[...]
    kbuf[blk:2 * blk] = kc_ref[...]
    kbuf[2 * blk:3 * blk] = kn_ref[...]
    vbuf[0:blk] = vp_ref[...]
    vbuf[blk:2 * blk] = vc_ref[...]
    vbuf[2 * blk:3 * blk] = vn_ref[...]

    gw = heads_per_group * head_dim
    n_groups = q_ref.shape[1] // gw
    win = WIN_R * GRID_W
    nq = heads_per_group * GRID_W
    row_head = lax.broadcasted_iota(I32, (nq, gw), 0) // GRID_W
    lane_head = lax.broadcasted_iota(I32, (nq, gw), 1) // head_dim
    head_mask = row_head == lane_head
    out_lane_head = lax.broadcasted_iota(I32, (GRID_W, gw), 1) // head_dim

    def row_step(i, carry):
        st = pl.multiple_of(kst_ref[b * ATT_ROWS + i] * GRID_W, GRID_W)
        dr0 = (WIN_R - 1) - off_ref[b * ATT_ROWS + i]
        q0 = pl.multiple_of(i * GRID_W, GRID_W)
        for g in range(n_groups):
            qg = q_ref[pl.ds(q0, GRID_W), g * gw:(g + 1) * gw]
            kg = kbuf[pl.ds(st, win), g * gw:(g + 1) * gw]
            vg = vbuf[pl.ds(st, win), g * gw:(g + 1) * gw]
            wt = jnp.concatenate([qg] * heads_per_group, axis=0)
            wt = jnp.where(head_mask, wt, jnp.zeros_like(wt))
            s = lax.dot_general(wt, kg, (((1,), (1,)), ((), ())),
                                preferred_element_type=F32)
            n_chunk = win // LANES
            sj = [s[:, j * LANES:(j + 1) * LANES] + bias_ref[dr0 + 2 * j, g] for j in range(n_chunk)]
            m = sj[0]
            for j in range(1, n_chunk):
                m = jnp.maximum(m, sj[j])
            m = jnp.max(m, axis=1, keepdims=True)
            ej = [jnp.exp(x - m) for x in sj]
            l = ej[0]
            for j in range(1, n_chunk):
                l = l + ej[j]
            inv = 1.0 / jnp.sum(l, axis=1, keepdims=True)
            p = jnp.concatenate([(x * inv).astype(BF16) for x in ej], axis=1)
            of = jnp.dot(p, vg, preferred_element_type=F32)
            out = jnp.zeros((GRID_W, gw), F32)
            for h in range(heads_per_group):
                out = out + jnp.where(out_lane_head == h, of[h * GRID_W:(h + 1) * GRID_W, :], 0.0)
            o_ref[pl.ds(q0, GRID_W), g * gw:(g + 1) * gw] = out.astype(o_ref.dtype)
        return carry

    lax.fori_loop(0, ATT_ROWS, row_step, 0, unroll=4)


def _attn_tables(images):
    prev, nxt, kst, off = [], [], [], []
    for s, n in images:
        rows = n * ATT_ROWS
        kr = min(WIN_R, rows)
        assert kr == WIN_R
        for bl in range(n):
            b = s + bl
            prev.append(max(b - 1, s))
            nxt.append(min(b + 1, s + n - 1))
            for i in range(ATT_ROWS):
                r = bl * ATT_ROWS + i
                rs = min(max(r - kr // 2, 0), rows - kr)
                kst.append(rs - bl * ATT_ROWS + ATT_ROWS)
                off.append(r - rs)
    return (np.asarray(prev, np.int32), np.asarray(nxt, np.int32),
            np.asarray(kst, np.int32), np.asarray(off, np.int32))


def _attn_bias_table(rpb, heads_per_group):
    n_heads, n_dr, n_dc = rpb.shape
    n_pair = n_dr - 1
    c = np.arange(GRID_W)
    cs = np.clip(c - WIN_C // 2, 0, GRID_W - WIN_C)
    kc = np.arange(GRID_W)
    valid = (kc[None, :] >= cs[:, None]) & (kc[None, :] < cs[:, None] + WIN_C)
    dc = np.clip(kc[None, :] - c[:, None] + (WIN_C - 1), 0, n_dc - 1)
    sel = np.zeros((2, n_dc, GRID_W, 2, GRID_W), np.float32)
    ci, ki = np.nonzero(valid)
    for p in range(2):
        sel[p, dc[ci, ki], ci, p, ki] = 1.0
    sel = jnp.asarray(sel.reshape(2 * n_dc, GRID_W * 2 * GRID_W))
    r = rpb.astype(F32)
    pair = jnp.stack([r[:, :n_pair], r[:, 1:]], axis=2)
    pair = jnp.transpose(pair, (1, 0, 2, 3)).reshape(n_pair * n_heads, 2 * n_dc)
    tab = jnp.dot(pair, sel, precision=lax.Precision.HIGHEST)
    tab = tab.reshape(n_pair, n_heads, GRID_W, 2 * GRID_W)
    valid2 = jnp.asarray(np.tile(valid[:, None, :], (1, 2, 1)).reshape(GRID_W, 2 * GRID_W))
    tab = jnp.where(valid2[None, None], tab, NEG_BIG)
    return tab.reshape(n_pair, n_heads // heads_per_group, heads_per_group * GRID_W, 2 * GRID_W)


def _attention(h, bias_tab, images, *, q_block, n_heads, head_dim):
    t = h.shape[0]
    blk = ATT_ROWS * GRID_W
    aw = n_heads * head_dim
    heads_per_group = 2 * LANES // head_dim
    prev, nxt, kst, off = _attn_tables(images)
    n_blocks = t // blk
    assert prev.shape[0] == n_blocks
    kcol, vcol = q_block + 1, q_block + 2
    spec = lambda col, which: pl.BlockSpec(
        (blk, aw), {"cur": lambda b, p, n, k, o: (b, col),
                    "prev": lambda b, p, n, k, o: (p[b], col),
                    "next": lambda b, p, n, k, o: (n[b], col)}[which])
    grid_spec = pltpu.PrefetchScalarGridSpec(
        num_scalar_prefetch=4,
        grid=(n_blocks,),
        in_specs=[spec(q_block, "cur"),
                  spec(kcol, "prev"), spec(kcol, "cur"), spec(kcol, "next"),
                  spec(vcol, "prev"), spec(vcol, "cur"), spec(vcol, "next"),
                  pl.BlockSpec(bias_tab.shape, lambda b, p, n, k, o: (0, 0, 0, 0),
                               pipeline_mode=pl.Buffered(1))],
        out_specs=pl.BlockSpec((blk, aw), lambda b, p, n, k, o: (b, 0)),
        scratch_shapes=[pltpu.VMEM((3 * blk, aw), BF16), pltpu.VMEM((3 * blk, aw), BF16)],
    )
    return pl.pallas_call(
        functools.partial(_attn_body, heads_per_group=heads_per_group, head_dim=head_dim),
        grid_spec=grid_spec,
        out_shape=jax.ShapeDtypeStruct((t, aw), BF16),
        compiler_params=_cparams(("arbitrary",)),
        name="natten",
    )(jnp.asarray(prev), jnp.asarray(nxt), jnp.asarray(kst), jnp.asarray(off),
      h, h, h, h, h, h, h, bias_tab)


def _layer_norm(r, g, b):
    mu = jnp.mean(r, axis=-1, keepdims=True)
    c = r - mu
    var = jnp.mean(c * c, axis=-1, keepdims=True)
    return c * lax.rsqrt(var + LN_EPS) * g + b


HI_MASK = -65536


def _pack_bf16_pair(lo, hi):
    lo_bits = lax.bitcast_convert_type(lo.astype(BF16).astype(F32), I32)
    hi_bits = lax.bitcast_convert_type(hi.astype(BF16).astype(F32), I32)
    return (hi_bits & HI_MASK) | lax.shift_right_logical(lo_bits, 16)


def _unpack_bf16_pair(word):
    lo = lax.bitcast_convert_type(lax.shift_left(word, 16), F32)
    hi = lax.bitcast_convert_type(word & HI_MASK, F32)
    return lo, hi


def _first_index_of(mask, idx, sentinel, axis):
    return jnp.min(jnp.where(mask, idx, sentinel), axis=axis, keepdims=True)


def _mixer_body(seq_lo_ref, seq_hi_ref,
                xa_ref, xs_ref, up_ref, uc_ref, un_ref, gp_ref, ga_ref, at_ref,
                pw_ref, pb_ref, ps_ref, wpb_ref, wab_ref, wo_ref, g1_ref, b1_ref,
                rwh_ref, rwl_ref, rb_ref,
                x1_ref, x1f_ref, idx_ref, gate_ref, rank_ref, cnt_ref,
                carry_ref, *, n_a, alpha, n_experts):
    i = pl.program_id(0)
    tm = xa_ref.shape[0]
    t0 = i * tm
    lo_seq = seq_lo_ref[i]
    hi_seq = seq_hi_ref[i]

    @pl.when(i == 0)
    def _():
        carry_ref[...] = jnp.zeros_like(carry_ref)

    u_ext = jnp.concatenate([up_ref[...], uc_ref[...], un_ref[...]], axis=0)
    ext = tm + 2 * POOL_HALO
    tok_r = t0 + lax.broadcasted_iota(I32, (tm, ext), 0)
    tok_c = t0 - POOL_HALO + lax.broadcasted_iota(I32, (tm, ext), 1)
    tok_q = t0 + lax.broadcasted_iota(I32, (tm, 1), 0)
    n_pg = pw_ref.shape[0]
    pg = pw_ref.shape[1]
    parts = []
    for g in range(n_pg):
        w = POOL_WINDOWS[g]
        lo = jnp.maximum(tok_r - w // 2, lo_seq)
        hi = jnp.minimum(tok_r - w // 2 + w, hi_seq)
        band = jnp.where((tok_c >= lo) & (tok_c < hi), 1.0, 0.0).astype(BF16)
        cnt = (jnp.minimum(tok_q - w // 2 + w, hi_seq) - jnp.maximum(tok_q - w // 2, lo_seq)).astype(F32)
        ug = u_ext[:, g * pg:(g + 1) * pg]
        sums = jnp.dot(band, ug, preferred_element_type=F32)
        z = sums / cnt - uc_ref[:, g * pg:(g + 1) * pg].astype(F32)
        y = jnp.dot(z.astype(BF16), pw_ref[g], preferred_element_type=F32)
        y = (y + pb_ref[:, g * pg:(g + 1) * pg]) * ps_ref[:, g * pg:(g + 1) * pg]
        parts.append(y.astype(BF16))
    a = jnp.concatenate(parts, axis=1)
    a_pool = jnp.dot(a, wpb_ref[...], preferred_element_type=F32)
    a_attn = jnp.dot(at_ref[...], wab_ref[...], preferred_element_type=F32)
    merged = gp_ref[...].astype(F32) * a_pool + ga_ref[...].astype(F32) * a_attn
    mix = jnp.dot(merged.astype(BF16), wo_ref[...], preferred_element_type=F32)
    x_in = jnp.where(i < n_a, xa_ref[...], xs_ref[...])
    x1 = _layer_norm(alpha * x_in + mix, g1_ref[...], b1_ref[...])
    half = x1.shape[1] // 2
    n_pk = half // LANES
    word = _pack_bf16_pair(x1[:, :half], x1[:, half:])
    for c in range(n_pk):
        x1_ref[pl.ds(c, tm, stride=n_pk), :] = word[:, c * LANES:(c + 1) * LANES]
    x1f_ref[...] = x1
    x1b = x1.astype(BF16)

    x1l = (x1 - x1b.astype(F32)).astype(BF16)
    nt = (((1,), (1,)), ((), ()))
    logits = (lax.dot_general(rwh_ref[...], x1b, nt, preferred_element_type=F32)
              + lax.dot_general(rwh_ref[...], x1l, nt, preferred_element_type=F32)
              + lax.dot_general(rwl_ref[...], x1b, nt, preferred_element_type=F32))
    scores = jax.nn.sigmoid(logits)
    biased = scores + rb_ref[...]
    gsz = n_experts // N_GROUPS
    neg_inf = -jnp.inf

    mem = lax.broadcasted_iota(I32, (gsz, tm), 0)
    gs_rows = []
    for g in range(N_GROUPS):
        bg = biased[g * gsz:(g + 1) * gsz, :]
        m1 = jnp.max(bg, axis=0, keepdims=True)
        first = _first_index_of(bg == m1, mem, gsz, 0)
        m2 = jnp.max(jnp.where(mem == first, neg_inf, bg), axis=0, keepdims=True)
        gs_rows.append(m1 + m2)
    gwork = jnp.concatenate(gs_rows, axis=0)

    gidx = lax.broadcasted_iota(I32, (N_GROUPS, tm), 0)
    egrp = lax.broadcasted_iota(I32, (n_experts, tm), 0) // gsz
    eidx = lax.broadcasted_iota(I32, (n_experts, tm), 0)
    work = jnp.full((n_experts, tm), neg_inf, F32)
    for _ in range(TOPK_GROUPS):
        gm = jnp.max(gwork, axis=0, keepdims=True)
        gf = _first_index_of(gwork == gm, gidx, N_GROUPS, 0)
        gwork = jnp.where(gidx == gf, neg_inf, gwork)
        work = jnp.where(egrp == gf, biased, work)

    hot = jnp.zeros((n_experts, tm), F32)
    sel_idx, sel_w, sel_hit = [], [], []
    for _ in range(TOP_K):
        em = jnp.max(work, axis=0, keepdims=True)
        ef = _first_index_of(work == em, eidx, n_experts, 0)
        hit = eidx == ef
        sel_idx.append(ef)
        sel_w.append(jnp.sum(jnp.where(hit, scores, 0.0), axis=0, keepdims=True))
        sel_hit.append(hit)
        hot = jnp.where(hit, 1.0, hot)
        work = jnp.where(hit, neg_inf, work)
    wsel = jnp.concatenate(sel_w, axis=0)
    gate_ref[...] = wsel / jnp.sum(wsel, axis=0, keepdims=True) * ROUTED_SCALE
    idx_ref[...] = jnp.concatenate(sel_idx, axis=0)

    tri = jnp.where(lax.broadcasted_iota(I32, (tm, tm), 0) < lax.broadcasted_iota(I32, (tm, tm), 1),
                    1.0, 0.0).astype(BF16)
    before = carry_ref[:, 0:1] + jnp.dot(hot.astype(BF16), tri, preferred_element_type=F32)
    rank = [jnp.sum(jnp.where(h, before, 0.0), axis=0, keepdims=True) for h in sel_hit]
    rank_ref[...] = jnp.concatenate(rank, axis=0).astype(I32)
    carry_ref[...] = carry_ref[...] + jnp.sum(hot, axis=1, keepdims=True)
    cnt_ref[...] = carry_ref[...]


def _mixer(xa, xs, h, attn, seq_lo, seq_hi, p, *, alpha, u_block, gate_block0):
    d = xa.shape[1]
    t = xa.shape[0] + xs.shape[0]
    tm = MIX_TM
    assert xa.shape[0] % tm == 0 and xs.shape[0] % tm == 0
    n_a = xa.shape[0] // tm
    map_a, map_s = _two_array_maps(n_a)
    pw = p["wpb"].shape[0]
    aw = attn.shape[1]
    n_experts = p["rwh"].shape[0]
    halo_per_tile = tm // POOL_HALO
    n_halo_blocks = t // POOL_HALO
    gp_col = gate_block0 * INPROJ_TN // d
    in_specs = [
        pl.BlockSpec((tm, d), map_a),
        pl.BlockSpec((tm, d), map_s),
        pl.BlockSpec((POOL_HALO, pw), lambda i, lo, hi: (jnp.maximum(i * halo_per_tile - 1, 0), u_block)),
        pl.BlockSpec((tm, pw), lambda i, lo, hi: (i, u_block)),
        pl.BlockSpec((POOL_HALO, pw),
                     lambda i, lo, hi: (jnp.minimum((i + 1) * halo_per_tile, n_halo_blocks - 1), u_block)),
        pl.BlockSpec((tm, d), lambda i, lo, hi: (i, gp_col)),
        pl.BlockSpec((tm, d), lambda i, lo, hi: (i, gp_col + 1)),
        pl.BlockSpec((tm, aw), lambda i, lo, hi: (i, 0)),
    ]
    weights = [p["pool_w"], p["pool_b"], p["pool_scale"], p["wpb"], p["wab"], p["wo"], p["ln1_g"], p["ln1_b"],
               p["rwh"], p["rwl"], p["rb"]]
    in_specs += [pl.BlockSpec(w.shape, (lambda nd: (lambda i, lo, hi: (0,) * nd))(w.ndim),
                              pipeline_mode=pl.Buffered(1)) for w in weights]
    n_pk = d // (2 * LANES)
    out_shape = (jax.ShapeDtypeStruct((t * n_pk, LANES), I32),
                 jax.ShapeDtypeStruct((t, d), F32),
                 jax.ShapeDtypeStruct((TOP_K, t), I32),
                 jax.ShapeDtypeStruct((TOP_K, t), F32),
                 jax.ShapeDtypeStruct((TOP_K, t), I32),
                 jax.ShapeDtypeStruct((n_experts, LANES), F32))
    out_specs = (pl.BlockSpec((tm * n_pk, LANES), lambda i, lo, hi: (i, 0)),
                 pl.BlockSpec((tm, d), lambda i, lo, hi: (i, 0)),
                 pl.BlockSpec((TOP_K, tm), lambda i, lo, hi: (0, i)),
                 pl.BlockSpec((TOP_K, tm), lambda i, lo, hi: (0, i)),
                 pl.BlockSpec((TOP_K, tm), lambda i, lo, hi: (0, i)),
                 pl.BlockSpec((n_experts, LANES), lambda i, lo, hi: (0, 0)))
    grid_spec = pltpu.PrefetchScalarGridSpec(
        num_scalar_prefetch=2, grid=(t // tm,), in_specs=in_specs, out_specs=out_specs,
        scratch_shapes=[pltpu.VMEM((n_experts, LANES), F32)])
    return pl.pallas_call(
        functools.partial(_mixer_body, n_a=n_a, alpha=alpha, n_experts=n_experts),
        grid_spec=grid_spec, out_shape=out_shape,
        compiler_params=_cparams(("arbitrary",)),
        name="mixer",
    )(seq_lo, seq_hi, xa, xs, h, h, h, h, h, attn, *weights)


def _shared_ffn_body(x_ref, s13_ref, s2_ref, o_ref, *, alpha):
    x = x_ref[...]
    sd = s2_ref.shape[0]
    hc = jnp.dot(x.astype(BF16), s13_ref[...], preferred_element_type=F32)
    hdn = jax.nn.silu(hc[:, :sd]) * hc[:, sd:]
    o_ref[...] = alpha * x + jnp.dot(hdn.astype(BF16), s2_ref[...], preferred_element_type=F32)


def _shared_ffn(x1f, s13, s2, *, alpha):
    t, d = x1f.shape
    tm = SHARED_TM
    assert t % tm == 0
    return pl.pallas_call(
        functools.partial(_shared_ffn_body, alpha=alpha),
        grid=(t // tm,),
        in_specs=[pl.BlockSpec((tm, d), lambda i: (i, 0)),
                  pl.BlockSpec(s13.shape, lambda i: (0, 0), pipeline_mode=pl.Buffered(1)),
                  pl.BlockSpec(s2.shape, lambda i: (0, 0), pipeline_mode=pl.Buffered(1))],
        out_specs=pl.BlockSpec((tm, d), lambda i: (i, 0)),
        out_shape=jax.ShapeDtypeStruct((t, d), F32),
        compiler_params=_cparams(("arbitrary",)),
        name="shared_ffn",
    )(x1f, s13, s2)


SC_CORES = 2
SC_SUBCORES = 16
SC_CHUNK = 32


def _sc_dispatch(x1t3, dest3, n_rows):
    t, n_lt, _ = x1t3.shape
    n_workers = SC_CORES * SC_SUBCORES
    assert t % (n_workers * SC_CHUNK) == 0
    chunks_per_worker = t // (n_workers * SC_CHUNK)
    mesh = plsc.VectorSubcoreMesh(core_axis_name="c", subcore_axis_name="s",
                                  num_cores=SC_CORES, num_subcores=SC_SUBCORES)

    @functools.partial(
        pl.kernel, mesh=mesh,
        out_type=jax.ShapeDtypeStruct((n_rows, n_lt, LANES), x1t3.dtype),
        scratch_types=[pltpu.VMEM((TOP_K, SC_CHUNK), I32),
                       pltpu.VMEM((SC_CHUNK, n_lt, LANES), x1t3.dtype),
                       pltpu.SemaphoreType.DMA],
        name="sc_dispatch")
    def dispatch(x_hbm, dest_hbm, out_hbm, idx_v, rows_v, sem):
        wid = lax.axis_index("s") * SC_CORES + lax.axis_index("c")

        @pl.loop(0, chunks_per_worker)
        def _(j):
            chunk = wid * chunks_per_worker + j
            base = pl.multiple_of(chunk * SC_CHUNK, SC_CHUNK)
            pltpu.sync_copy(x_hbm.at[pl.ds(base, SC_CHUNK)], rows_v)
            pltpu.sync_copy(dest_hbm.at[chunk], idx_v)
            copies = [pltpu.make_async_copy(rows_v, out_hbm.at[idx_v.at[k]], sem) for k in range(TOP_K)]
            for cp in copies:
                cp.start()
            for cp in copies:
                cp.wait()

    return dispatch(x1t3, dest3)


SC_RING = 3


def _sc_gather(yb3, dest3, t):
    _, n_lt, _ = yb3.shape
    n_workers = SC_CORES * SC_SUBCORES
    chunks_per_worker = t // (n_workers * SC_CHUNK)
    mesh = plsc.VectorSubcoreMesh(core_axis_name="c", subcore_axis_name="s",
                                  num_cores=SC_CORES, num_subcores=SC_SUBCORES)

    @functools.partial(
        pl.kernel, mesh=mesh,
        out_type=jax.ShapeDtypeStruct((TOP_K * t, n_lt, LANES), yb3.dtype),
        scratch_types=[pltpu.VMEM((TOP_K, SC_CHUNK), I32),
                       pltpu.VMEM((SC_RING, SC_CHUNK, n_lt, LANES), yb3.dtype)]
                      + [pltpu.SemaphoreType.DMA] * (2 * SC_RING),
        name="sc_gather")
    def gather(y_hbm, dest_hbm, out_hbm, idx_v, rows_v, *sems):
        gsem, wsem = sems[:SC_RING], sems[SC_RING:]
        wid = lax.axis_index("s") * SC_CORES + lax.axis_index("c")

        @pl.loop(0, chunks_per_worker)
        def _(j):
            chunk = wid * chunks_per_worker + j
            base = pl.multiple_of(chunk * SC_CHUNK, SC_CHUNK)
            pltpu.sync_copy(dest_hbm.at[chunk], idx_v)
            gets = [pltpu.make_async_copy(y_hbm.at[idx_v.at[k]], rows_v.at[k % SC_RING], gsem[k % SC_RING])
                    for k in range(TOP_K)]
            puts = [pltpu.make_async_copy(rows_v.at[k % SC_RING], out_hbm.at[pl.ds(k * t + base, SC_CHUNK)],
                                          wsem[k % SC_RING]) for k in range(TOP_K)]
            put_waited = set()
            for k in range(SC_RING - 1):
                gets[k].start()
            for k in range(TOP_K):
                gets[k].wait()
                puts[k].start()
                nxt = k + SC_RING - 1
                if nxt < TOP_K:
                    if k >= 1:
                        puts[k - 1].wait()
                        put_waited.add(k - 1)
                    gets[nxt].start()
            for k in range(TOP_K):
                if k not in put_waited:
                    puts[k].wait()

    return gather(yb3, dest3)


def _experts_body(be_ref, nact_ref, first_ref, nxte_ref, wslot_ref, valid_ref,
                  xs_ref, w1_hbm, w3_hbm, w2_hbm, o_ref,
                  wst1, wst3, wst2, w13b, w2b, sem_w):
    b = pl.program_id(0)
    nact = nact_ref[0]
    ed = wst2.shape[1]
    d = wst2.shape[2]
    n_pk = d // (2 * LANES)
    bm = xs_ref.shape[0] // n_pk
    active = b < nact

    def weight_copies(e, s):
        return (pltpu.make_async_copy(w1_hbm.at[e], wst1.at[s], sem_w.at[s]),
                pltpu.make_async_copy(w3_hbm.at[e], wst3.at[s], sem_w.at[s]),
                pltpu.make_async_copy(w2_hbm.at[e], wst2.at[s], sem_w.at[s]))

    @pl.when(b == 0)
    def _():
        for cp in weight_copies(be_ref[0], wslot_ref[0]):
            cp.start()

    @pl.when(first_ref[b] == 1)
    def _():
        ws = wslot_ref[b]
        for cp in weight_copies(be_ref[b], ws):
            cp.wait()

        @pl.when(nxte_ref[b] >= 0)
        def _():
            for cp in weight_copies(nxte_ref[b], 1 - ws):
                cp.start()

        for c in range(ed // MXU_N):
            w13b[:, 2 * c * MXU_N:(2 * c + 1) * MXU_N] = wst1[ws, :, c * MXU_N:(c + 1) * MXU_N].astype(BF16)
            w13b[:, (2 * c + 1) * MXU_N:(2 * c + 2) * MXU_N] = wst3[ws, :, c * MXU_N:(c + 1) * MXU_N].astype(BF16)
        w2b[...] = wst2[ws].astype(BF16)

    @pl.when(active)
    def _():
        keep = lax.broadcasted_iota(I32, (bm, LANES), 0) < valid_ref[b]
        los, his = [], []
        for c in range(n_pk):
            lo, hi = _unpack_bf16_pair(jnp.where(keep, xs_ref[pl.ds(c, bm, stride=n_pk), :], 0))
            los.append(lo.astype(BF16))
            his.append(hi.astype(BF16))
        xb = jnp.concatenate(los + his, axis=1)
        parts = []
        for c in range(ed // MXU_N):
            hc = jnp.dot(xb, w13b[:, 2 * c * MXU_N:(2 * c + 2) * MXU_N], preferred_element_type=F32)
            parts.append((jax.nn.silu(hc[:, :MXU_N]) * hc[:, MXU_N:]).astype(BF16))
        hdn = jnp.concatenate(parts, axis=1)
        n_down = d // MXU_N
        ys = [jnp.dot(hdn, w2b[:, c * MXU_N:(c + 1) * MXU_N], preferred_element_type=F32) for c in range(n_down)]
        tiles_per_chunk = MXU_N // LANES
        for c in range(n_down // 2):
            word = _pack_bf16_pair(ys[c], ys[c + n_down // 2])
            for h in range(tiles_per_chunk):
                o_ref[pl.ds(c * tiles_per_chunk + h, bm, stride=n_pk), :] = word[:, h * LANES:(h + 1) * LANES]

    @pl.when(jnp.logical_not(active))
    def _():
        o_ref[...] = jnp.zeros_like(o_ref)


def _experts(xs2d, block_e, nact, first, nxt_e, wslot, valid, w1, w3, w2, bm):
    d, ed = w1.shape[1], w1.shape[2]
    n_pk = d // (2 * LANES)
    n_blk = block_e.shape[0]
    assert xs2d.shape == (n_blk * bm * n_pk, LANES)
    assert ed % MXU_N == 0 and d % MXU_N == 0 and bm % SUBLANES == 0
    grid_spec = pltpu.PrefetchScalarGridSpec(
        num_scalar_prefetch=6,
        grid=(n_blk,),
        in_specs=[pl.BlockSpec((bm * n_pk, LANES),
                               lambda b, be, na, *_: (jnp.maximum(jnp.minimum(b, na[0] - 1), 0), 0)),
                  pl.BlockSpec(memory_space=pl.ANY),
                  pl.BlockSpec(memory_space=pl.ANY),
                  pl.BlockSpec(memory_space=pl.ANY)],
        out_specs=pl.BlockSpec((bm * n_pk, LANES), lambda b, *_: (b, 0)),
        scratch_shapes=[pltpu.VMEM((2, d, ed), F32),
                        pltpu.VMEM((2, d, ed), F32),
                        pltpu.VMEM((2, ed, d), F32),
                        pltpu.VMEM((d, 2 * ed), BF16),
                        pltpu.VMEM((ed, d), BF16),
                        pltpu.SemaphoreType.DMA((2,))],
    )
    return pl.pallas_call(
        _experts_body,
        grid_spec=grid_spec,
        out_shape=jax.ShapeDtypeStruct((n_blk * bm * n_pk, LANES), I32),
        compiler_params=_cparams(("arbitrary",)),
        name="experts",
    )(block_e, nact, first, nxt_e, wslot, valid, xs2d, w1, w3, w2)


def _finalize_body(base_ref, gate_ref, g2_ref, b2_ref, *refs, n_a):
    slabs, (oa_ref, os_ref) = refs[:TOP_K], refs[TOP_K:]
    i = pl.program_id(0)
    tm, d = base_ref.shape
    half = d // 2
    n_pk = half // LANES
    acc_lo = [base_ref[:, c * LANES:(c + 1) * LANES] for c in range(n_pk)]
    acc_hi = [base_ref[:, half + c * LANES:half + (c + 1) * LANES] for c in range(n_pk)]
    for k in range(TOP_K):
        g = gate_ref[:, k:k + 1]
        for c in range(n_pk):
            lo, hi = _unpack_bf16_pair(slabs[k][pl.ds(c, tm, stride=n_pk), :])
            acc_lo[c] = acc_lo[c] + g * lo
            acc_hi[c] = acc_hi[c] + g * hi
    y = _layer_norm(jnp.concatenate(acc_lo + acc_hi, axis=1), g2_ref[...], b2_ref[...])

    @pl.when(i < n_a)
    def _():
        oa_ref[...] = y

    @pl.when(i >= n_a)
    def _():
        os_ref[...] = y


def _finalize(base, gate_t, slabs2d, ln_g, ln_b, t_a):
    t, d = base.shape
    tm = FINAL_TM
    n_pk = d // (2 * LANES)
    assert t_a % tm == 0 and 0 < t_a < t and slabs2d.shape == (TOP_K * t * n_pk, LANES)
    n_a = t_a // tm
    n_steps = t // tm
    map_a, map_s = _two_array_maps(n_a)
    slab_specs = [pl.BlockSpec((tm * n_pk, LANES), (lambda k: (lambda i: (k * n_steps + i, 0)))(k))
                  for k in range(TOP_K)]
    return pl.pallas_call(
        functools.partial(_finalize_body, n_a=n_a),
        grid=(n_steps,),
        in_specs=[pl.BlockSpec((tm, d), lambda i: (i, 0)),
                  pl.BlockSpec((tm, TOP_K), lambda i: (i, 0)),
                  pl.BlockSpec((1, d), lambda i: (0, 0)),
                  pl.BlockSpec((1, d), lambda i: (0, 0))] + slab_specs,
        out_specs=(pl.BlockSpec((tm, d), map_a), pl.BlockSpec((tm, d), map_s)),
        out_shape=(jax.ShapeDtypeStruct((t_a, d), F32), jax.ShapeDtypeStruct((t - t_a, d), F32)),
        compiler_params=_cparams(("arbitrary",)),
        name="finalize",
    )(base, gate_t, ln_g, ln_b, *([slabs2d] * TOP_K))


def _split_hi_lo(w):
    hi = w.astype(BF16)
    lo = (w - hi.astype(F32)).astype(BF16)
    return hi, lo


def _encoder_layer(xa, xs, seqs, lp, alpha):
    (w_in, pool_w, pool_b, pool_scale, w_pool_br, w_attn_br, attn_rpb, w_out, ln1_g, ln1_b,
     router_w, router_bias, exp_w1, exp_w3, exp_w2, sh_w1, sh_w3, sh_w2, ln2_g, ln2_b) = lp
    d = xa.shape[1]
    t = xa.shape[0] + xs.shape[0]
    pw = w_pool_br.shape[0]
    aw = w_attn_br.shape[0]
    n_heads = attn_rpb.shape[0]
    head_dim = aw // n_heads
    n_experts = router_w.shape[1]
    assert pw == INPROJ_TN and aw == INPROJ_TN and d % INPROJ_TN == 0
    u_block, q_block = 0, pw // INPROJ_TN
    gate_block0 = (pw + 3 * aw) // INPROJ_TN

    h = _inproj(xa, xs, w_in.astype(BF16), q_block=q_block, gate_block0=gate_block0, q_scale=head_dim ** -0.5)

    blk = ATT_ROWS * GRID_W
    images, seq_lo, seq_hi = [], [], []
    for s0, s1 in seqs:
        images.append((s0 // blk, (s1 - s0) // blk))
        for _ in range((s1 - s0) // MIX_TM):
            seq_lo.append(s0)
            seq_hi.append(s1)
    bias_tab = _attn_bias_table(attn_rpb, 2 * LANES // head_dim)
    attn = _attention(h, bias_tab, images, q_block=q_block, n_heads=n_heads, head_dim=head_dim)

    rwh, rwl = _split_hi_lo(router_w.T)
    p = dict(pool_w=pool_w.astype(BF16), pool_b=pool_b.reshape(1, pw), pool_scale=pool_scale.reshape(1, pw),
             wpb=w_pool_br.astype(BF16), wab=w_attn_br.astype(BF16), wo=w_out.astype(BF16),
             ln1_g=ln1_g.reshape(1, d), ln1_b=ln1_b.reshape(1, d),
             rwh=rwh, rwl=rwl, rb=router_bias.reshape(n_experts, 1).astype(F32))
    x1, x1f, idx_t, gate_t, rank_t, counts = _mixer(
        xa, xs, h, attn, jnp.asarray(seq_lo, I32), jnp.asarray(seq_hi, I32), p,
        alpha=alpha, u_block=u_block, gate_block0=gate_block0)

    bm = EXPERT_BM
    cnt = counts[:, 0].astype(I32)
    pcnt = (cnt + bm - 1) // bm * bm
    pend = jnp.cumsum(pcnt)
    pstart = pend - pcnt
    eids = jnp.arange(n_experts, dtype=I32)[:, None, None]
    dest = jnp.sum(jnp.where(idx_t[None] == eids, pstart[:, None, None], 0), axis=0) + rank_t
    n_blk = (t * TOP_K + n_experts * (bm - 1)) // bm
    blk_row = jnp.arange(n_blk, dtype=I32) * bm
    block_e = jnp.minimum(jnp.sum((pend[None, :] <= blk_row[:, None]).astype(I32), axis=1), n_experts - 1)
    nact = (pend[-1] // bm).astype(I32).reshape(1)
    block_e = jnp.where(jnp.arange(n_blk) < nact[0], block_e, block_e[jnp.maximum(nact[0] - 1, 0)])
    eids1 = jnp.arange(n_experts, dtype=I32)
    present = cnt > 0
    later = present[None, :] & (eids1[None, :] > eids1[:, None])
    nxt_of_e = jnp.min(jnp.where(later, eids1[None, :], n_experts), axis=1)
    nxt_of_e = jnp.where(nxt_of_e >= n_experts, -1, nxt_of_e).astype(I32)
    ord_of_e = (jnp.cumsum(present.astype(I32)) - 1).astype(I32)
    onehot_be = (block_e[:, None] == eids1[None, :]).astype(I32)
    blk_ids = jnp.arange(n_blk, dtype=I32)
    is_act = blk_ids < nact[0]
    prev_e = jnp.concatenate([jnp.full((1,), -1, I32), block_e[:-1]])
    first = (is_act & (block_e != prev_e)).astype(I32)
    nxt_e = jnp.sum(onehot_be * nxt_of_e[None, :], axis=1).astype(I32)
    wslot = (jnp.sum(onehot_be * ord_of_e[None, :], axis=1) % 2).astype(I32)
    start_of_e = jnp.sum(onehot_be * pstart[None, :], axis=1)
    cnt_of_e = jnp.sum(onehot_be * cnt[None, :], axis=1)
    valid = jnp.where(is_act, jnp.clip(cnt_of_e - (blk_row - start_of_e), 0, bm), 0).astype(I32)
    n_pk = d // (2 * LANES)
    dest3 = jnp.transpose(dest.reshape(TOP_K, t // SC_CHUNK, SC_CHUNK), (1, 0, 2))
    xs = _sc_dispatch(x1.reshape(t, n_pk, LANES), dest3, n_blk * bm)
    base = _shared_ffn(x1f, jnp.concatenate([sh_w1, sh_w3], axis=1).astype(BF16), sh_w2.astype(BF16), alpha=alpha)
    yb = _experts(xs.reshape(n_blk * bm * n_pk, LANES), block_e, nact, first, nxt_e, wslot, valid,
                  exp_w1, exp_w3, exp_w2, bm)

    slabs = _sc_gather(yb.reshape(n_blk * bm, n_pk, LANES), dest3, t)
    return _finalize(base, gate_t.T, slabs.reshape(TOP_K * t * n_pk, LANES),
                     ln2_g.reshape(1, d), ln2_b.reshape(1, d), xa.shape[0])


def kernel(x_prompt, x_sample, w_in, pool_w, pool_b, pool_scale, w_pool_br, w_attn_br, attn_rpb, w_out,
           ln1_g, ln1_b, router_w, router_bias, exp_w1, exp_w3, exp_w2, sh_w1, sh_w3, sh_w2, ln2_g, ln2_b):
    depth = w_in.shape[0]
    alpha = (2.0 * depth) ** 0.25
    bp, sp, d = x_prompt.shape
    bs, ss, _ = x_sample.shape
    seqs = [(i * sp, (i + 1) * sp) for i in range(bp)]
    seqs += [(bp * sp + i * ss, bp * sp + (i + 1) * ss) for i in range(bs)]
    xa, xs = x_prompt.reshape(bp * sp, d), x_sample.reshape(bs * ss, d)
    for l in range(depth):
        lp = (w_in[l], pool_w[l], pool_b[l], pool_scale[l], w_pool_br[l], w_attn_br[l], attn_rpb[l], w_out[l],
              ln1_g[l], ln1_b[l], router_w[l], router_bias[l], exp_w1[l], exp_w3[l], exp_w2[l],
              sh_w1[l], sh_w3[l], sh_w2[l], ln2_g[l], ln2_b[l])
        xa, xs = _encoder_layer(xa, xs, seqs, lp, alpha)
    return (xa.reshape(bp, sp, d), xs.reshape(bs, ss, d))
```

```python
import functools

import numpy as np
import jax
import jax.numpy as jnp
from jax import lax
from jax.experimental import pallas as pl
from jax.experimental.pallas import tpu as pltpu
from jax.experimental.pallas import tpu_sc as plsc

F32 = jnp.float32
BF16 = jnp.bfloat16
I32 = jnp.int32

GRID_W = 64
POOL_WINDOWS = (2, 4, 8, 16)
WIN_R = 8
WIN_C = 16
TOP_K = 8
N_GROUPS = 8
TOPK_GROUPS = 4
ROUTED_SCALE = 2.5
LN_EPS = 1e-5
NEG_BIG = -1e30

LANES = 128
SUBLANES = 8
MXU_N = 256
VMEM_LIMIT = 56 * 1024 * 1024

INPROJ_TM = 1024
INPROJ_TN = 1024
ATT_ROWS = 8
MIX_TM = 256
POOL_HALO = 16
EXPERT_BM = 256
FINAL_TM = 256
SHARED_TM = 512


def _cparams(sem):
    return pltpu.CompilerParams(dimension_semantics=sem, vmem_limit_bytes=VMEM_LIMIT)


def _inproj_body(xa_ref, xs_ref, w_ref, o_ref, xb_ref, *, n_a, q_block, gate_block0, q_scale):
    i = pl.program_id(0)
    j = pl.program_id(1)

    @pl.when((j == 0) & (i < n_a))
    def _():
        xb_ref[...] = xa_ref[...].astype(BF16)

    @pl.when((j == 0) & (i >= n_a))
    def _():
        xb_ref[...] = xs_ref[...].astype(BF16)

    is_gate = j >= gate_block0
    scale = jnp.where(j == q_block, q_scale, 1.0).astype(F32)
    for c in range(w_ref.shape[1] // MXU_N):
        acc = jnp.dot(xb_ref[...], w_ref[:, c * MXU_N:(c + 1) * MXU_N], preferred_element_type=F32)
        out = jnp.where(is_gate, jax.nn.sigmoid(acc), acc * scale)
        o_ref[:, c * MXU_N:(c + 1) * MXU_N] = out.astype(BF16)


def _two_array_maps(n_a):
    return (lambda i, *_: (jnp.minimum(i, n_a - 1), 0)), (lambda i, *_: (jnp.maximum(i - n_a, 0), 0))


def _inproj(xa, xs, w_bf16, *, q_block, gate_block0, q_scale):
    d = xa.shape[1]
    t = xa.shape[0] + xs.shape[0]
    n = w_bf16.shape[1]
    tm, tn = INPROJ_TM, INPROJ_TN
    assert xa.shape[0] % tm == 0 and xs.shape[0] % tm == 0 and xa.shape[0] > 0 and xs.shape[0] > 0
    n_a = xa.shape[0] // tm
    map_a, map_s = _two_array_maps(n_a)
    return pl.pallas_call(
        functools.partial(_inproj_body, n_a=n_a, q_block=q_block, gate_block0=gate_block0, q_scale=q_scale),
        grid=(t // tm, n // tn),
        in_specs=[pl.BlockSpec((tm, d), map_a),
                  pl.BlockSpec((tm, d), map_s),
                  pl.BlockSpec((d, tn), lambda i, j: (0, j))],
        out_specs=pl.BlockSpec((tm, tn), lambda i, j: (i, j)),
        out_shape=jax.ShapeDtypeStruct((t, n), BF16),
        scratch_shapes=[pltpu.VMEM((tm, d), BF16)],
        compiler_params=_cparams(("arbitrary", "arbitrary")),
        name="inproj",
    )(xa, xs, w_bf16)


def _attn_body(prev_ref, next_ref, kst_ref, off_ref,
               q_ref, kp_ref, kc_ref, kn_ref, vp_ref, vc_ref, vn_ref, bias_ref, o_ref,
               kbuf, vbuf, *, heads_per_group, head_dim):
    del prev_ref, next_ref
    b = pl.program_id(0)
    blk = ATT_ROWS * GRID_W
    kbuf[0:blk] = kp_ref[...]
    kbuf[blk:2 * blk] = kc_ref[...]
    kbuf[2 * blk:3 * blk] = kn_ref[...]
    vbuf[0:blk] = vp_ref[...]
    vbuf[blk:2 * blk] = vc_ref[...]
    vbuf[2 * blk:3 * blk] = vn_ref[...]

    gw = heads_per_group * head_dim
    n_groups = q_ref.shape[1] // gw
    win = WIN_R * GRID_W
    nq = heads_per_group * GRID_W
    row_head = lax.broadcasted_iota(I32, (nq, gw), 0) // GRID_W
    lane_head = lax.broadcasted_iota(I32, (nq, gw), 1) // head_dim
    head_mask = row_head == lane_head
    out_lane_head = lax.broadcasted_iota(I32, (GRID_W, gw), 1) // head_dim

    def row_step(i, carry):
        st = pl.multiple_of(kst_ref[b * ATT_ROWS + i] * GRID_W, GRID_W)
        dr0 = (WIN_R - 1) - off_ref[b * ATT_ROWS + i]
        q0 = pl.multiple_of(i * GRID_W, GRID_W)
        for g in range(n_groups):
            qg = q_ref[pl.ds(q0, GRID_W), g * gw:(g + 1) * gw]
            kg = kbuf[pl.ds(st, win), g * gw:(g + 1) * gw]
            vg = vbuf[pl.ds(st, win), g * gw:(g + 1) * gw]
            wt = jnp.concatenate([qg] * heads_per_group, axis=0)
            wt = jnp.where(head_mask, wt, jnp.zeros_like(wt))
            s = lax.dot_general(wt, kg, (((1,), (1,)), ((), ())),
                                preferred_element_type=F32)
            n_chunk = win // LANES
            sj = [s[:, j * LANES:(j + 1) * LANES] + bias_ref[dr0 + 2 * j, g] for j in range(n_chunk)]
            m = sj[0]
            for j in range(1, n_chunk):
                m = jnp.maximum(m, sj[j])
            m = jnp.max(m, axis=1, keepdims=True)
            ej = [jnp.exp(x - m) for x in sj]
            l = ej[0]
            for j in range(1, n_chunk):
                l = l + ej[j]
            inv = 1.0 / jnp.sum(l, axis=1, keepdims=True)
            p = jnp.concatenate([(x * inv).astype(BF16) for x in ej], axis=1)
            of = jnp.dot(p, vg, preferred_element_type=F32)
            out = jnp.zeros((GRID_W, gw), F32)
            for h in range(heads_per_group):
                out = out + jnp.where(out_lane_head == h, of[h * GRID_W:(h + 1) * GRID_W, :], 0.0)
            o_ref[pl.ds(q0, GRID_W), g * gw:(g + 1) * gw] = out.astype(o_ref.dtype)
        return carry

    lax.fori_loop(0, ATT_ROWS, row_step, 0, unroll=4)


def _attn_tables(images):
    prev, nxt, kst, off = [], [], [], []
    for s, n in images:
        rows = n * ATT_ROWS
        kr = min(WIN_R, rows)
        assert kr == WIN_R
        for bl in range(n):
            b = s + bl
            prev.append(max(b - 1, s))
            nxt.append(min(b + 1, s + n - 1))
            for i in range(ATT_ROWS):
                r = bl * ATT_ROWS + i
                rs = min(max(r - kr // 2, 0), rows - kr)
                kst.append(rs - bl * ATT_ROWS + ATT_ROWS)
                off.append(r - rs)
    return (np.asarray(prev, np.int32), np.asarray(nxt, np.int32),
            np.asarray(kst, np.int32), np.asarray(off, np.int32))


def _attn_bias_table(rpb, heads_per_group):
    n_heads, n_dr, n_dc = rpb.shape
    n_pair = n_dr - 1
    c = np.arange(GRID_W)
    cs = np.clip(c - WIN_C // 2, 0, GRID_W - WIN_C)
    kc = np.arange(GRID_W)
    valid = (kc[None, :] >= cs[:, None]) & (kc[None, :] < cs[:, None] + WIN_C)
    dc = np.clip(kc[None, :] - c[:, None] + (WIN_C - 1), 0, n_dc - 1)
    sel = np.zeros((2, n_dc, GRID_W, 2, GRID_W), np.float32)
    ci, ki = np.nonzero(valid)
    for p in range(2):
        sel[p, dc[ci, ki], ci, p, ki] = 1.0
    sel = jnp.asarray(sel.reshape(2 * n_dc, GRID_W * 2 * GRID_W))
    r = rpb.astype(F32)
    pair = jnp.stack([r[:, :n_pair], r[:, 1:]], axis=2)
    pair = jnp.transpose(pair, (1, 0, 2, 3)).reshape(n_pair * n_heads, 2 * n_dc)
    tab = jnp.dot(pair, sel, precision=lax.Precision.HIGHEST)
    tab = tab.reshape(n_pair, n_heads, GRID_W, 2 * GRID_W)
    valid2 = jnp.asarray(np.tile(valid[:, None, :], (1, 2, 1)).reshape(GRID_W, 2 * GRID_W))
    tab = jnp.where(valid2[None, None], tab, NEG_BIG)
    return tab.reshape(n_pair, n_heads // heads_per_group, heads_per_group * GRID_W, 2 * GRID_W)


def _attention(h, bias_tab, images, *, q_block, n_heads, head_dim):
    t = h.shape[0]
    blk = ATT_ROWS * GRID_W
    aw = n_heads * head_dim
    heads_per_group = 2 * LANES // head_dim
    prev, nxt, kst, off = _attn_tables(images)
    n_blocks = t // blk
    assert prev.shape[0] == n_blocks
    kcol, vcol = q_block + 1, q_block + 2
    spec = lambda col, which: pl.BlockSpec(
        (blk, aw), {"cur": lambda b, p, n, k, o: (b, col),
                    "prev": lambda b, p, n, k, o: (p[b], col),
                    "next": lambda b, p, n, k, o: (n[b], col)}[which])
    grid_spec = pltpu.PrefetchScalarGridSpec(
        num_scalar_prefetch=4,
        grid=(n_blocks,),
        in_specs=[spec(q_block, "cur"),
                  spec(kcol, "prev"), spec(kcol, "cur"), spec(kcol, "next"),
                  spec(vcol, "prev"), spec(vcol, "cur"), spec(vcol, "next"),
                  pl.BlockSpec(bias_tab.shape, lambda b, p, n, k, o: (0, 0, 0, 0),
                               pipeline_mode=pl.Buffered(1))],
        out_specs=pl.BlockSpec((blk, aw), lambda b, p, n, k, o: (b, 0)),
        scratch_shapes=[pltpu.VMEM((3 * blk, aw), BF16), pltpu.VMEM((3 * blk, aw), BF16)],
    )
    return pl.pallas_call(
        functools.partial(_attn_body, heads_per_group=heads_per_group, head_dim=head_dim),
        grid_spec=grid_spec,
        out_shape=jax.ShapeDtypeStruct((t, aw), BF16),
        compiler_params=_cparams(("arbitrary",)),
        name="natten",
    )(jnp.asarray(prev), jnp.asarray(nxt), jnp.asarray(kst), jnp.asarray(off),
      h, h, h, h, h, h, h, bias_tab)


def _layer_norm(r, g, b):
    mu = jnp.mean(r, axis=-1, keepdims=True)
    c = r - mu
    var = jnp.mean(c * c, axis=-1, keepdims=True)
    return c * lax.rsqrt(var + LN_EPS) * g + b


HI_MASK = -65536


def _pack_bf16_pair(lo, hi):
    lo_bits = lax.bitcast_convert_type(lo.astype(BF16).astype(F32), I32)
    hi_bits = lax.bitcast_convert_type(hi.astype(BF16).astype(F32), I32)
    return (hi_bits & HI_MASK) | lax.shift_right_logical(lo_bits, 16)


def _unpack_bf16_pair(word):
    lo = lax.bitcast_convert_type(lax.shift_left(word, 16), F32)
    hi = lax.bitcast_convert_type(word & HI_MASK, F32)
    return lo, hi


def _first_index_of(mask, idx, sentinel, axis):
    return jnp.min(jnp.where(mask, idx, sentinel), axis=axis, keepdims=True)


def _mixer_body(seq_lo_ref, seq_hi_ref,
                xa_ref, xs_ref, up_ref, uc_ref, un_ref, gp_ref, ga_ref, at_ref,
                pw_ref, pb_ref, ps_ref, wpb_ref, wab_ref, wo_ref, g1_ref, b1_ref,
                rwh_ref, rwl_ref, rb_ref,
                x1_ref, x1f_ref, idx_ref, gate_ref, rank_ref, cnt_ref,
                carry_ref, *, n_a, alpha, n_experts):
    i = pl.program_id(0)
    tm = xa_ref.shape[0]
    t0 = i * tm
    lo_seq = seq_lo_ref[i]
    hi_seq = seq_hi_ref[i]

    @pl.when(i == 0)
    def _():
        carry_ref[...] = jnp.zeros_like(carry_ref)

    u_ext = jnp.concatenate([up_ref[...], uc_ref[...], un_ref[...]], axis=0)
    ext = tm + 2 * POOL_HALO
    tok_r = t0 + lax.broadcasted_iota(I32, (tm, ext), 0)
    tok_c = t0 - POOL_HALO + lax.broadcasted_iota(I32, (tm, ext), 1)
    tok_q = t0 + lax.broadcasted_iota(I32, (tm, 1), 0)
    n_pg = pw_ref.shape[0]
    pg = pw_ref.shape[1]
    parts = []
    for g in range(n_pg):
        w = POOL_WINDOWS[g]
        lo = jnp.maximum(tok_r - w // 2, lo_seq)
        hi = jnp.minimum(tok_r - w // 2 + w, hi_seq)
        band = jnp.where((tok_c >= lo) & (tok_c < hi), 1.0, 0.0).astype(BF16)
        cnt = (jnp.minimum(tok_q - w // 2 + w, hi_seq) - jnp.maximum(tok_q - w // 2, lo_seq)).astype(F32)
        ug = u_ext[:, g * pg:(g + 1) * pg]
        sums = jnp.dot(band, ug, preferred_element_type=F32)
        z = sums / cnt - uc_ref[:, g * pg:(g + 1) * pg].astype(F32)
        y = jnp.dot(z.astype(BF16), pw_ref[g], preferred_element_type=F32)
        y = (y + pb_ref[:, g * pg:(g + 1) * pg]) * ps_ref[:, g * pg:(g + 1) * pg]
        parts.append(y.astype(BF16))
    a = jnp.concatenate(parts, axis=1)
    a_pool = jnp.dot(a, wpb_ref[...], preferred_element_type=F32)
    a_attn = jnp.dot(at_ref[...], wab_ref[...], preferred_element_type=F32)
    merged = gp_ref[...].astype(F32) * a_pool + ga_ref[...].astype(F32) * a_attn
    mix = jnp.dot(merged.astype(BF16), wo_ref[...], preferred_element_type=F32)
    x_in = jnp.where(i < n_a, xa_ref[...], xs_ref[...])
    x1 = _layer_norm(alpha * x_in + mix, g1_ref[...], b1_ref[...])
    half = x1.shape[1] // 2
    n_pk = half // LANES
    word = _pack_bf16_pair(x1[:, :half], x1[:, half:])
    for c in range(n_pk):
        x1_ref[pl.ds(c, tm, stride=n_pk), :] = word[:, c * LANES:(c + 1) * LANES]
    x1f_ref[...] = x1
    x1b = x1.astype(BF16)

    x1l = (x1 - x1b.astype(F32)).astype(BF16)
    nt = (((1,), (1,)), ((), ()))
    logits = (lax.dot_general(rwh_ref[...], x1b, nt, preferred_element_type=F32)
              + lax.dot_general(rwh_ref[...], x1l, nt, preferred_element_type=F32)
              + lax.dot_general(rwl_ref[...], x1b, nt, preferred_element_type=F32))
    scores = jax.nn.sigmoid(logits)
    biased = scores + rb_ref[...]
    gsz = n_experts // N_GROUPS
    neg_inf = -jnp.inf

    mem = lax.broadcasted_iota(I32, (gsz, tm), 0)
    gs_rows = []
    for g in range(N_GROUPS):
        bg = biased[g * gsz:(g + 1) * gsz, :]
        m1 = jnp.max(bg, axis=0, keepdims=True)
        first = _first_index_of(bg == m1, mem, gsz, 0)
        m2 = jnp.max(jnp.where(mem == first, neg_inf, bg), axis=0, keepdims=True)
        gs_rows.append(m1 + m2)
    gwork = jnp.concatenate(gs_rows, axis=0)

    gidx = lax.broadcasted_iota(I32, (N_GROUPS, tm), 0)
    egrp = lax.broadcasted_iota(I32, (n_experts, tm), 0) // gsz
    eidx = lax.broadcasted_iota(I32, (n_experts, tm), 0)
    work = jnp.full((n_experts, tm), neg_inf, F32)
    for _ in range(TOPK_GROUPS):
        gm = jnp.max(gwork, axis=0, keepdims=True)
        gf = _first_index_of(gwork == gm, gidx, N_GROUPS, 0)
        gwork = jnp.where(gidx == gf, neg_inf, gwork)
        work = jnp.where(egrp == gf, biased, work)

    hot = jnp.zeros((n_experts, tm), F32)
    sel_idx, sel_w, sel_hit = [], [], []
    for _ in range(TOP_K):
        em = jnp.max(work, axis=0, keepdims=True)
        ef = _first_index_of(work == em, eidx, n_experts, 0)
        hit = eidx == ef
        sel_idx.append(ef)
        sel_w.append(jnp.sum(jnp.where(hit, scores, 0.0), axis=0, keepdims=True))
        sel_hit.append(hit)
        hot = jnp.where(hit, 1.0, hot)
        work = jnp.where(hit, neg_inf, work)
    wsel = jnp.concatenate(sel_w, axis=0)
    gate_ref[...] = wsel / jnp.sum(wsel, axis=0, keepdims=True) * ROUTED_SCALE
    idx_ref[...] = jnp.concatenate(sel_idx, axis=0)

    tri = jnp.where(lax.broadcasted_iota(I32, (tm, tm), 0) < lax.broadcasted_iota(I32, (tm, tm), 1),
                    1.0, 0.0).astype(BF16)
    before = carry_ref[:, 0:1] + jnp.dot(hot.astype(BF16), tri, preferred_element_type=F32)
    rank = [jnp.sum(jnp.where(h, before, 0.0), axis=0, keepdims=True) for h in sel_hit]
    rank_ref[...] = jnp.concatenate(rank, axis=0).astype(I32)
    carry_ref[...] = carry_ref[...] + jnp.sum(hot, axis=1, keepdims=True)
    cnt_ref[...] = carry_ref[...]


def _mixer(xa, xs, h, attn, seq_lo, seq_hi, p, *, alpha, u_block, gate_block0):
    d = xa.shape[1]
    t = xa.shape[0] + xs.shape[0]
    tm = MIX_TM
    assert xa.shape[0] % tm == 0 and xs.shape[0] % tm == 0
    n_a = xa.shape[0] // tm
    map_a, map_s = _two_array_maps(n_a)
    pw = p["wpb"].shape[0]
    aw = attn.shape[1]
    n_experts = p["rwh"].shape[0]
    halo_per_tile = tm // POOL_HALO
    n_halo_blocks = t // POOL_HALO
    gp_col = gate_block0 * INPROJ_TN // d
    in_specs = [
        pl.BlockSpec((tm, d), map_a),
        pl.BlockSpec((tm, d), map_s),
        pl.BlockSpec((POOL_HALO, pw), lambda i, lo, hi: (jnp.maximum(i * halo_per_tile - 1, 0), u_block)),
        pl.BlockSpec((tm, pw), lambda i, lo, hi: (i, u_block)),
        pl.BlockSpec((POOL_HALO, pw),
                     lambda i, lo, hi: (jnp.minimum((i + 1) * halo_per_tile, n_halo_blocks - 1), u_block)),
        pl.BlockSpec((tm, d), lambda i, lo, hi: (i, gp_col)),
        pl.BlockSpec((tm, d), lambda i, lo, hi: (i, gp_col + 1)),
        pl.BlockSpec((tm, aw), lambda i, lo, hi: (i, 0)),
    ]
    weights = [p["pool_w"], p["pool_b"], p["pool_scale"], p["wpb"], p["wab"], p["wo"], p["ln1_g"], p["ln1_b"],
               p["rwh"], p["rwl"], p["rb"]]
    in_specs += [pl.BlockSpec(w.shape, (lambda nd: (lambda i, lo, hi: (0,) * nd))(w.ndim),
                              pipeline_mode=pl.Buffered(1)) for w in weights]
    n_pk = d // (2 * LANES)
    out_shape = (jax.ShapeDtypeStruct((t * n_pk, LANES), I32),
                 jax.ShapeDtypeStruct((t, d), F32),
                 jax.ShapeDtypeStruct((TOP_K, t), I32),
                 jax.ShapeDtypeStruct((TOP_K, t), F32),
                 jax.ShapeDtypeStruct((TOP_K, t), I32),
                 jax.ShapeDtypeStruct((n_experts, LANES), F32))
    out_specs = (pl.BlockSpec((tm * n_pk, LANES), lambda i, lo, hi: (i, 0)),
                 pl.BlockSpec((tm, d), lambda i, lo, hi: (i, 0)),
                 pl.BlockSpec((TOP_K, tm), lambda i, lo, hi: (0, i)),
                 pl.BlockSpec((TOP_K, tm), lambda i, lo, hi: (0, i)),
                 pl.BlockSpec((TOP_K, tm), lambda i, lo, hi: (0, i)),
                 pl.BlockSpec((n_experts, LANES), lambda i, lo, hi: (0, 0)))
    grid_spec = pltpu.PrefetchScalarGridSpec(
        num_scalar_prefetch=2, grid=(t // tm,), in_specs=in_specs, out_specs=out_specs,
        scratch_shapes=[pltpu.VMEM((n_experts, LANES), F32)])
    return pl.pallas_call(
        functools.partial(_mixer_body, n_a=n_a, alpha=alpha, n_experts=n_experts),
        grid_spec=grid_spec, out_shape=out_shape,
        compiler_params=_cparams(("arbitrary",)),
        name="mixer",
    )(seq_lo, seq_hi, xa, xs, h, h, h, h, h, attn, *weights)


def _shared_ffn_body(x_ref, s13_ref, s2_ref, o_ref, *, alpha):
    x = x_ref[...]
    sd = s2_ref.shape[0]
    hc = jnp.dot(x.astype(BF16), s13_ref[...], preferred_element_type=F32)
    hdn = jax.nn.silu(hc[:, :sd]) * hc[:, sd:]
    o_ref[...] = alpha * x + jnp.dot(hdn.astype(BF16), s2_ref[...], preferred_element_type=F32)


def _shared_ffn(x1f, s13, s2, *, alpha):
    t, d = x1f.shape
    tm = SHARED_TM
    assert t % tm == 0
    return pl.pallas_call(
        functools.partial(_shared_ffn_body, alpha=alpha),
        grid=(t // tm,),
        in_specs=[pl.BlockSpec((tm, d), lambda i: (i, 0)),
                  pl.BlockSpec(s13.shape, lambda i: (0, 0), pipeline_mode=pl.Buffered(1)),
                  pl.BlockSpec(s2.shape, lambda i: (0, 0), pipeline_mode=pl.Buffered(1))],
        out_specs=pl.BlockSpec((tm, d), lambda i: (i, 0)),
        out_shape=jax.ShapeDtypeStruct((t, d), F32),
        compiler_params=_cparams(("arbitrary",)),
        name="shared_ffn",
    )(x1f, s13, s2)


SC_CORES = 2
SC_SUBCORES = 16
SC_CHUNK = 32


def _sc_dispatch(x1t3, dest3, n_rows):
    t, n_lt, _ = x1t3.shape
    n_workers = SC_CORES * SC_SUBCORES
    assert t % (n_workers * SC_CHUNK) == 0
    chunks_per_worker = t // (n_workers * SC_CHUNK)
    mesh = plsc.VectorSubcoreMesh(core_axis_name="c", subcore_axis_name="s",
                                  num_cores=SC_CORES, num_subcores=SC_SUBCORES)

    @functools.partial(
        pl.kernel, mesh=mesh,
        out_type=jax.ShapeDtypeStruct((n_rows, n_lt, LANES), x1t3.dtype),
        scratch_types=[pltpu.VMEM((TOP_K, SC_CHUNK), I32),
                       pltpu.VMEM((SC_CHUNK, n_lt, LANES), x1t3.dtype),
                       pltpu.SemaphoreType.DMA],
        name="sc_dispatch")
    def dispatch(x_hbm, dest_hbm, out_hbm, idx_v, rows_v, sem):
        wid = lax.axis_index("s") * SC_CORES + lax.axis_index("c")

        @pl.loop(0, chunks_per_worker)
        def _(j):
            chunk = wid * chunks_per_worker + j
            base = pl.multiple_of(chunk * SC_CHUNK, SC_CHUNK)
            pltpu.sync_copy(x_hbm.at[pl.ds(base, SC_CHUNK)], rows_v)
            pltpu.sync_copy(dest_hbm.at[chunk], idx_v)
            copies = [pltpu.make_async_copy(rows_v, out_hbm.at[idx_v.at[k]], sem) for k in range(TOP_K)]
            for cp in copies:
                cp.start()
            for cp in copies:
                cp.wait()

    return dispatch(x1t3, dest3)


SC_RING = 3


def _sc_gather(yb3, dest3, t):
    _, n_lt, _ = yb3.shape
    n_workers = SC_CORES * SC_SUBCORES
    chunks_per_worker = t // (n_workers * SC_CHUNK)
    mesh = plsc.VectorSubcoreMesh(core_axis_name="c", subcore_axis_name="s",
                                  num_cores=SC_CORES, num_subcores=SC_SUBCORES)

    @functools.partial(
        pl.kernel, mesh=mesh,
        out_type=jax.ShapeDtypeStruct((TOP_K * t, n_lt, LANES), yb3.dtype),
        scratch_types=[pltpu.VMEM((TOP_K, SC_CHUNK), I32),
                       pltpu.VMEM((SC_RING, SC_CHUNK, n_lt, LANES), yb3.dtype)]
                      + [pltpu.SemaphoreType.DMA] * (2 * SC_RING),
        name="sc_gather")
    def gather(y_hbm, dest_hbm, out_hbm, idx_v, rows_v, *sems):
        gsem, wsem = sems[:SC_RING], sems[SC_RING:]
        wid = lax.axis_index("s") * SC_CORES + lax.axis_index("c")

        @pl.loop(0, chunks_per_worker)
        def _(j):
            chunk = wid * chunks_per_worker + j
            base = pl.multiple_of(chunk * SC_CHUNK, SC_CHUNK)
            pltpu.sync_copy(dest_hbm.at[chunk], idx_v)
            gets = [pltpu.make_async_copy(y_hbm.at[idx_v.at[k]], rows_v.at[k % SC_RING], gsem[k % SC_RING])
                    for k in range(TOP_K)]
            puts = [pltpu.make_async_copy(rows_v.at[k % SC_RING], out_hbm.at[pl.ds(k * t + base, SC_CHUNK)],
                                          wsem[k % SC_RING]) for k in range(TOP_K)]
            put_waited = set()
            for k in range(SC_RING - 1):
                gets[k].start()
            for k in range(TOP_K):
                gets[k].wait()
                puts[k].start()
                nxt = k + SC_RING - 1
                if nxt < TOP_K:
                    if k >= 1:
                        puts[k - 1].wait()
                        put_waited.add(k - 1)
                    gets[nxt].start()
            for k in range(TOP_K):
                if k not in put_waited:
                    puts[k].wait()

    return gather(yb3, dest3)


def _experts_body(be_ref, nact_ref, first_ref, nxte_ref, wslot_ref, valid_ref,
                  xs_ref, w1_hbm, w3_hbm, w2_hbm, after_hbm, o_ref,
                  wst1, wst3, wst2, w13b, w2b, sem_w):
    del after_hbm
    b = pl.program_id(0)
    nact = nact_ref[0]
    ed = wst2.shape[1]
    d = wst2.shape[2]
    n_pk = d // (2 * LANES)
    bm = xs_ref.shape[0] // n_pk
    active = b < nact

    def weight_copies(e, s):
        return (pltpu.make_async_copy(w1_hbm.at[e], wst1.at[s], sem_w.at[s]),
                pltpu.make_async_copy(w3_hbm.at[e], wst3.at[s], sem_w.at[s]),
                pltpu.make_async_copy(w2_hbm.at[e], wst2.at[s], sem_w.at[s]))

    @pl.when(b == 0)
    def _():
        for cp in weight_copies(be_ref[0], wslot_ref[0]):
            cp.start()

    @pl.when(first_ref[b] == 1)
    def _():
        ws = wslot_ref[b]
        for cp in weight_copies(be_ref[b], ws):
            cp.wait()

        @pl.when(nxte_ref[b] >= 0)
        def _():
            for cp in weight_copies(nxte_ref[b], 1 - ws):
                cp.start()

        for c in range(ed // MXU_N):
            w13b[:, 2 * c * MXU_N:(2 * c + 1) * MXU_N] = wst1[ws, :, c * MXU_N:(c + 1) * MXU_N].astype(BF16)
            w13b[:, (2 * c + 1) * MXU_N:(2 * c + 2) * MXU_N] = wst3[ws, :, c * MXU_N:(c + 1) * MXU_N].astype(BF16)
        w2b[...] = wst2[ws].astype(BF16)

    @pl.when(active)
    def _():
        keep = lax.broadcasted_iota(I32, (bm, LANES), 0) < valid_ref[b]
        los, his = [], []
        for c in range(n_pk):
            lo, hi = _unpack_bf16_pair(jnp.where(keep, xs_ref[pl.ds(c, bm, stride=n_pk), :], 0))
            los.append(lo.astype(BF16))
            his.append(hi.astype(BF16))
        xb = jnp.concatenate(los + his, axis=1)
        parts = []
        for c in range(ed // MXU_N):
            hc = jnp.dot(xb, w13b[:, 2 * c * MXU_N:(2 * c + 2) * MXU_N], preferred_element_type=F32)
            parts.append((jax.nn.silu(hc[:, :MXU_N]) * hc[:, MXU_N:]).astype(BF16))
        hdn = jnp.concatenate(parts, axis=1)
        n_down = d // MXU_N
        ys = [jnp.dot(hdn, w2b[:, c * MXU_N:(c + 1) * MXU_N], preferred_element_type=F32) for c in range(n_down)]
        tiles_per_chunk = MXU_N // LANES
        for c in range(n_down // 2):
            word = _pack_bf16_pair(ys[c], ys[c + n_down // 2])
            for h in range(tiles_per_chunk):
                o_ref[pl.ds(c * tiles_per_chunk + h, bm, stride=n_pk), :] = word[:, h * LANES:(h + 1) * LANES]

    @pl.when(jnp.logical_not(active))
    def _():
        o_ref[...] = jnp.zeros_like(o_ref)


def _experts(xs2d, block_e, nact, first, nxt_e, wslot, valid, w1, w3, w2, bm, after):
    d, ed = w1.shape[1], w1.shape[2]
    n_pk = d // (2 * LANES)
    n_blk = block_e.shape[0]
    assert xs2d.shape == (n_blk * bm * n_pk, LANES)
    assert ed % MXU_N == 0 and d % MXU_N == 0 and bm % SUBLANES == 0
    grid_spec = pltpu.PrefetchScalarGridSpec(
        num_scalar_prefetch=6,
        grid=(n_blk,),
        in_specs=[pl.BlockSpec((bm * n_pk, LANES),
                               lambda b, be, na, *_: (jnp.maximum(jnp.minimum(b, na[0] - 1), 0), 0)),
                  pl.BlockSpec(memory_space=pl.ANY),
                  pl.BlockSpec(memory_space=pl.ANY),
                  pl.BlockSpec(memory_space=pl.ANY),
                  pl.BlockSpec(memory_space=pl.ANY)],
        out_specs=pl.BlockSpec((bm * n_pk, LANES), lambda b, *_: (b, 0)),
        scratch_shapes=[pltpu.VMEM((2, d, ed), F32),
                        pltpu.VMEM((2, d, ed), F32),
                        pltpu.VMEM((2, ed, d), F32),
                        pltpu.VMEM((d, 2 * ed), BF16),
                        pltpu.VMEM((ed, d), BF16),
                        pltpu.SemaphoreType.DMA((2,))],
    )
    return pl.pallas_call(
        _experts_body,
        grid_spec=grid_spec,
        out_shape=jax.ShapeDtypeStruct((n_blk * bm * n_pk, LANES), I32),
        compiler_params=_cparams(("arbitrary",)),
        name="experts",
    )(block_e, nact, first, nxt_e, wslot, valid, xs2d, w1, w3, w2, after)


def _finalize_body(base_ref, gate_ref, g2_ref, b2_ref, *refs, n_a):
    slabs, (oa_ref, os_ref) = refs[:TOP_K], refs[TOP_K:]
    i = pl.program_id(0)
    tm, d = base_ref.shape
    half = d // 2
    n_pk = half // LANES
    acc_lo = [base_ref[:, c * LANES:(c + 1) * LANES] for c in range(n_pk)]
    acc_hi = [base_ref[:, half + c * LANES:half + (c + 1) * LANES] for c in range(n_pk)]
    for k in range(TOP_K):
        g = gate_ref[:, k:k + 1]
        for c in range(n_pk):
            lo, hi = _unpack_bf16_pair(slabs[k][pl.ds(c, tm, stride=n_pk), :])
            acc_lo[c] = acc_lo[c] + g * lo
            acc_hi[c] = acc_hi[c] + g * hi
    y = _layer_norm(jnp.concatenate(acc_lo + acc_hi, axis=1), g2_ref[...], b2_ref[...])

    @pl.when(i < n_a)
    def _():
        oa_ref[...] = y

    @pl.when(i >= n_a)
    def _():
        os_ref[...] = y


def _finalize(base, gate_t, slabs2d, ln_g, ln_b, t_a):
    t, d = base.shape
    tm = FINAL_TM
    n_pk = d // (2 * LANES)
    assert t_a % tm == 0 and 0 < t_a < t and slabs2d.shape == (TOP_K * t * n_pk, LANES)
    n_a = t_a // tm
    n_steps = t // tm
    map_a, map_s = _two_array_maps(n_a)
    slab_specs = [pl.BlockSpec((tm * n_pk, LANES), (lambda k: (lambda i: (k * n_steps + i, 0)))(k))
                  for k in range(TOP_K)]
    return pl.pallas_call(
        functools.partial(_finalize_body, n_a=n_a),
        grid=(n_steps,),
        in_specs=[pl.BlockSpec((tm, d), lambda i: (i, 0)),
                  pl.BlockSpec((tm, TOP_K), lambda i: (i, 0)),
                  pl.BlockSpec((1, d), lambda i: (0, 0)),
                  pl.BlockSpec((1, d), lambda i: (0, 0))] + slab_specs,
        out_specs=(pl.BlockSpec((tm, d), map_a), pl.BlockSpec((tm, d), map_s)),
        out_shape=(jax.ShapeDtypeStruct((t_a, d), F32), jax.ShapeDtypeStruct((t - t_a, d), F32)),
        compiler_params=_cparams(("arbitrary",)),
        name="finalize",
    )(base, gate_t, ln_g, ln_b, *([slabs2d] * TOP_K))


def _split_hi_lo(w):
    hi = w.astype(BF16)
    lo = (w - hi.astype(F32)).astype(BF16)
    return hi, lo


def _encoder_layer(xa, xs, seqs, lp, alpha):
    (w_in, pool_w, pool_b, pool_scale, w_pool_br, w_attn_br, attn_rpb, w_out, ln1_g, ln1_b,
     router_w, router_bias, exp_w1, exp_w3, exp_w2, sh_w1, sh_w3, sh_w2, ln2_g, ln2_b) = lp
    d = xa.shape[1]
    t = xa.shape[0] + xs.shape[0]
    pw = w_pool_br.shape[0]
    aw = w_attn_br.shape[0]
    n_heads = attn_rpb.shape[0]
    head_dim = aw // n_heads
    n_experts = router_w.shape[1]
    assert pw == INPROJ_TN and aw == INPROJ_TN and d % INPROJ_TN == 0
    u_block, q_block = 0, pw // INPROJ_TN
    gate_block0 = (pw + 3 * aw) // INPROJ_TN

    h = _inproj(xa, xs, w_in.astype(BF16), q_block=q_block, gate_block0=gate_block0, q_scale=head_dim ** -0.5)

    blk = ATT_ROWS * GRID_W
    images, seq_lo, seq_hi = [], [], []
    for s0, s1 in seqs:
        images.append((s0 // blk, (s1 - s0) // blk))
        for _ in range((s1 - s0) // MIX_TM):
            seq_lo.append(s0)
            seq_hi.append(s1)
    bias_tab = _attn_bias_table(attn_rpb, 2 * LANES // head_dim)
    attn = _attention(h, bias_tab, images, q_block=q_block, n_heads=n_heads, head_dim=head_dim)

    rwh, rwl = _split_hi_lo(router_w.T)
    p = dict(pool_w=pool_w.astype(BF16), pool_b=pool_b.reshape(1, pw), pool_scale=pool_scale.reshape(1, pw),
             wpb=w_pool_br.astype(BF16), wab=w_attn_br.astype(BF16), wo=w_out.astype(BF16),
             ln1_g=ln1_g.reshape(1, d), ln1_b=ln1_b.reshape(1, d),
             rwh=rwh, rwl=rwl, rb=router_bias.reshape(n_experts, 1).astype(F32))
    x1, x1f, idx_t, gate_t, rank_t, counts = _mixer(
        xa, xs, h, attn, jnp.asarray(seq_lo, I32), jnp.asarray(seq_hi, I32), p,
        alpha=alpha, u_block=u_block, gate_block0=gate_block0)

    bm = EXPERT_BM
    cnt = counts[:, 0].astype(I32)
    pcnt = (cnt + bm - 1) // bm * bm
    pend = jnp.cumsum(pcnt)
    pstart = pend - pcnt
    eids = jnp.arange(n_experts, dtype=I32)[:, None, None]
    dest = jnp.sum(jnp.where(idx_t[None] == eids, pstart[:, None, None], 0), axis=0) + rank_t
    n_blk = (t * TOP_K + n_experts * (bm - 1)) // bm
    blk_row = jnp.arange(n_blk, dtype=I32) * bm
    block_e = jnp.minimum(jnp.sum((pend[None, :] <= blk_row[:, None]).astype(I32), axis=1), n_experts - 1)
    nact = (pend[-1] // bm).astype(I32).reshape(1)
    block_e = jnp.where(jnp.arange(n_blk) < nact[0], block_e, block_e[jnp.maximum(nact[0] - 1, 0)])
    eids1 = jnp.arange(n_experts, dtype=I32)
    present = cnt > 0
    later = present[None, :] & (eids1[None, :] > eids1[:, None])
    nxt_of_e = jnp.min(jnp.where(later, eids1[None, :], n_experts), axis=1)
    nxt_of_e = jnp.where(nxt_of_e >= n_experts, -1, nxt_of_e).astype(I32)
    ord_of_e = (jnp.cumsum(present.astype(I32)) - 1).astype(I32)
    onehot_be = (block_e[:, None] == eids1[None, :]).astype(I32)
    blk_ids = jnp.arange(n_blk, dtype=I32)
    is_act = blk_ids < nact[0]
    prev_e = jnp.concatenate([jnp.full((1,), -1, I32), block_e[:-1]])
    first = (is_act & (block_e != prev_e)).astype(I32)
    nxt_e = jnp.sum(onehot_be * nxt_of_e[None, :], axis=1).astype(I32)
    wslot = (jnp.sum(onehot_be * ord_of_e[None, :], axis=1) % 2).astype(I32)
    start_of_e = jnp.sum(onehot_be * pstart[None, :], axis=1)
    cnt_of_e = jnp.sum(onehot_be * cnt[None, :], axis=1)
    valid = jnp.where(is_act, jnp.clip(cnt_of_e - (blk_row - start_of_e), 0, bm), 0).astype(I32)
    n_pk = d // (2 * LANES)
    dest3 = jnp.transpose(dest.reshape(TOP_K, t // SC_CHUNK, SC_CHUNK), (1, 0, 2))
    xs = _sc_dispatch(x1.reshape(t, n_pk, LANES), dest3, n_blk * bm)
    base = _shared_ffn(x1f, jnp.concatenate([sh_w1, sh_w3], axis=1).astype(BF16), sh_w2.astype(BF16), alpha=alpha)
    yb = _experts(xs.reshape(n_blk * bm * n_pk, LANES), block_e, nact, first, nxt_e, wslot, valid,
                  exp_w1, exp_w3, exp_w2, bm, after=base)

    slabs = _sc_gather(yb.reshape(n_blk * bm, n_pk, LANES), dest3, t)
    return _finalize(base, gate_t.T, slabs.reshape(TOP_K * t * n_pk, LANES),
                     ln2_g.reshape(1, d), ln2_b.reshape(1, d), xa.shape[0])


def kernel(x_prompt, x_sample, w_in, pool_w, pool_b, pool_scale, w_pool_br, w_attn_br, attn_rpb, w_out,
           ln1_g, ln1_b, router_w, router_bias, exp_w1, exp_w3, exp_w2, sh_w1, sh_w3, sh_w2, ln2_g, ln2_b):
    depth = w_in.shape[0]
    alpha = (2.0 * depth) ** 0.25
    bp, sp, d = x_prompt.shape
    bs, ss, _ = x_sample.shape
    seqs = [(i * sp, (i + 1) * sp) for i in range(bp)]
    seqs += [(bp * sp + i * ss, bp * sp + (i + 1) * ss) for i in range(bs)]
    xa, xs = x_prompt.reshape(bp * sp, d), x_sample.reshape(bs * ss, d)
    for l in range(depth):
        lp = (w_in[l], pool_w[l], pool_b[l], pool_scale[l], w_pool_br[l], w_attn_br[l], attn_rpb[l], w_out[l],
              ln1_g[l], ln1_b[l], router_w[l], router_bias[l], exp_w1[l], exp_w3[l], exp_w2[l],
              sh_w1[l], sh_w3[l], sh_w2[l], ln2_g[l], ln2_b[l])
        xa, xs = _encoder_layer(xa, xs, seqs, lp, alpha)
    return (xa.reshape(bp, sp, d), xs.reshape(bs, ss, d))
```

```python
import functools

import numpy as np
import jax
import jax.numpy as jnp
from jax import lax
from jax.experimental import pallas as pl
from jax.experimental.pallas import tpu as pltpu
from jax.experimental.pallas import tpu_sc as plsc

F32 = jnp.float32
BF16 = jnp.bfloat16
I32 = jnp.int32

GRID_W = 64
POOL_WINDOWS = (2, 4, 8, 16)
WIN_R = 8
WIN_C = 16
TOP_K = 8
N_GROUPS = 8
TOPK_GROUPS = 4
ROUTED_SCALE = 2.5
LN_EPS = 1e-5
NEG_BIG = -1e30

LANES = 128
SUBLANES = 8
MXU_N = 256
VMEM_LIMIT = 56 * 1024 * 1024

INPROJ_TM = 1024
INPROJ_TN = 1024
ATT_ROWS = 8
MIX_TM = 256
POOL_HALO = 16
EXPERT_BM = 256
FINAL_TM = 256
SHARED_TM = 512


def _cparams(sem):
    return pltpu.CompilerParams(dimension_semantics=sem, vmem_limit_bytes=VMEM_LIMIT)


def _inproj_body(xa_ref, xs_ref, w_ref, o_ref, xb_ref, *, n_a, q_block, gate_block0, q_scale):
    i = pl.program_id(0)
    j = pl.program_id(1)

    @pl.when((j == 0) & (i < n_a))
    def _():
        xb_ref[...] = xa_ref[...].astype(BF16)

    @pl.when((j == 0) & (i >= n_a))
    def _():
        xb_ref[...] = xs_ref[...].astype(BF16)

    is_gate = j >= gate_block0
    scale = jnp.where(j == q_block, q_scale, 1.0).astype(F32)
    for c in range(w_ref.shape[1] // MXU_N):
        acc = jnp.dot(xb_ref[...], w_ref[:, c * MXU_N:(c + 1) * MXU_N], preferred_element_type=F32)
        out = jnp.where(is_gate, jax.nn.sigmoid(acc), acc * scale)
        o_ref[:, c * MXU_N:(c + 1) * MXU_N] = out.astype(BF16)


def _two_array_maps(n_a):
    return (lambda i, *_: (jnp.minimum(i, n_a - 1), 0)), (lambda i, *_: (jnp.maximum(i - n_a, 0), 0))


def _inproj(xa, xs, w_bf16, *, q_block, gate_block0, q_scale):
    d = xa.shape[1]
    t = xa.shape[0] + xs.shape[0]
    n = w_bf16.shape[1]
    tm, tn = INPROJ_TM, INPROJ_TN
    assert xa.shape[0] % tm == 0 and xs.shape[0] % tm == 0 and xa.shape[0] > 0 and xs.shape[0] > 0
    n_a = xa.shape[0] // tm
    map_a, map_s = _two_array_maps(n_a)
    return pl.pallas_call(
        functools.partial(_inproj_body, n_a=n_a, q_block=q_block, gate_block0=gate_block0, q_scale=q_scale),
        grid=(t // tm, n // tn),
        in_specs=[pl.BlockSpec((tm, d), map_a),
                  pl.BlockSpec((tm, d), map_s),
                  pl.BlockSpec((d, tn), lambda i, j: (0, j))],
        out_specs=pl.BlockSpec((tm, tn), lambda i, j: (i, j)),
        out_shape=jax.ShapeDtypeStruct((t, n), BF16),
        scratch_shapes=[pltpu.VMEM((tm, d), BF16)],
        compiler_params=_cparams(("arbitrary", "arbitrary")),
        name="inproj",
    )(xa, xs, w_bf16)


def _attn_body(prev_ref, next_ref, kst_ref, off_ref,
               q_ref, kp_ref, kc_ref, kn_ref, vp_ref, vc_ref, vn_ref, bias_ref, o_ref,
               kbuf, vbuf, *, heads_per_group, head_dim):
    del prev_ref, next_ref
    b = pl.program_id(0)
    blk = ATT_ROWS * GRID_W
    kbuf[0:blk] = kp_ref[...]
    kbuf[blk:2 * blk] = kc_ref[...]
    kbuf[2 * blk:3 * blk] = kn_ref[...]
    vbuf[0:blk] = vp_ref[...]
    vbuf[blk:2 * blk] = vc_ref[...]
    vbuf[2 * blk:3 * blk] = vn_ref[...]

    gw = heads_per_group * head_dim
    n_groups = q_ref.shape[1] // gw
    win = WIN_R * GRID_W
    nq = heads_per_group * GRID_W
    row_head = lax.broadcasted_iota(I32, (nq, gw), 0) // GRID_W
    lane_head = lax.broadcasted_iota(I32, (nq, gw), 1) // head_dim
    head_mask = row_head == lane_head
    out_lane_head = lax.broadcasted_iota(I32, (GRID_W, gw), 1) // head_dim

    def row_step(i, carry):
        st = pl.multiple_of(kst_ref[b * ATT_ROWS + i] * GRID_W, GRID_W)
        dr0 = (WIN_R - 1) - off_ref[b * ATT_ROWS + i]
        q0 = pl.multiple_of(i * GRID_W, GRID_W)
        for g in range(n_groups):
            qg = q_ref[pl.ds(q0, GRID_W), g * gw:(g + 1) * gw]
            kg = kbuf[pl.ds(st, win), g * gw:(g + 1) * gw]
            vg = vbuf[pl.ds(st, win), g * gw:(g + 1) * gw]
            wt = jnp.concatenate([qg] * heads_per_group, axis=0)
            wt = jnp.where(head_mask, wt, jnp.zeros_like(wt))
            s = lax.dot_general(wt, kg, (((1,), (1,)), ((), ())),
                                preferred_element_type=F32)
            n_chunk = win // LANES
            sj = [s[:, j * LANES:(j + 1) * LANES] + bias_ref[dr0 + 2 * j, g] for j in range(n_chunk)]
            m = sj[0]
            for j in range(1, n_chunk):
                m = jnp.maximum(m, sj[j])
            m = jnp.max(m, axis=1, keepdims=True)
            ej = [jnp.exp(x - m) for x in sj]
            l = ej[0]
            for j in range(1, n_chunk):
                l = l + ej[j]
            inv = 1.0 / jnp.sum(l, axis=1, keepdims=True)
            p = jnp.concatenate([(x * inv).astype(BF16) for x in ej], axis=1)
            of = jnp.dot(p, vg, preferred_element_type=F32)
            out = jnp.zeros((GRID_W, gw), F32)
            for h in range(heads_per_group):
                out = out + jnp.where(out_lane_head == h, of[h * GRID_W:(h + 1) * GRID_W, :], 0.0)
            o_ref[pl.ds(q0, GRID_W), g * gw:(g + 1) * gw] = out.astype(o_ref.dtype)
        return carry

    lax.fori_loop(0, ATT_ROWS, row_step, 0, unroll=4)


def _attn_tables(images):
    prev, nxt, kst, off = [], [], [], []
    for s, n in images:
        rows = n * ATT_ROWS
        kr = min(WIN_R, rows)
        assert kr == WIN_R
        for bl in range(n):
            b = s + bl
            prev.append(max(b - 1, s))
            nxt.append(min(b + 1, s + n - 1))
            for i in range(ATT_ROWS):
                r = bl * ATT_ROWS + i
                rs = min(max(r - kr // 2, 0), rows - kr)
                kst.append(rs - bl * ATT_ROWS + ATT_ROWS)
                off.append(r - rs)
    return (np.asarray(prev, np.int32), np.asarray(nxt, np.int32),
            np.asarray(kst, np.int32), np.asarray(off, np.int32))


def _attn_bias_table(rpb, heads_per_group):
    n_heads, n_dr, n_dc = rpb.shape
    n_pair = n_dr - 1
    c = np.arange(GRID_W)
    cs = np.clip(c - WIN_C // 2, 0, GRID_W - WIN_C)
    kc = np.arange(GRID_W)
    valid = (kc[None, :] >= cs[:, None]) & (kc[None, :] < cs[:, None] + WIN_C)
    dc = np.clip(kc[None, :] - c[:, None] + (WIN_C - 1), 0, n_dc - 1)
    sel = np.zeros((2, n_dc, GRID_W, 2, GRID_W), np.float32)
    ci, ki = np.nonzero(valid)
    for p in range(2):
        sel[p, dc[ci, ki], ci, p, ki] = 1.0
    sel = jnp.asarray(sel.reshape(2 * n_dc, GRID_W * 2 * GRID_W))
    r = rpb.astype(F32)
    pair = jnp.stack([r[:, :n_pair], r[:, 1:]], axis=2)
    pair = jnp.transpose(pair, (1, 0, 2, 3)).reshape(n_pair * n_heads, 2 * n_dc)
    tab = jnp.dot(pair, sel, precision=lax.Precision.HIGHEST)
    tab = tab.reshape(n_pair, n_heads, GRID_W, 2 * GRID_W)
    valid2 = jnp.asarray(np.tile(valid[:, None, :], (1, 2, 1)).reshape(GRID_W, 2 * GRID_W))
    tab = jnp.where(valid2[None, None], tab, NEG_BIG)
    return tab.reshape(n_pair, n_heads // heads_per_group, heads_per_group * GRID_W, 2 * GRID_W)


def _attention(h, bias_tab, images, *, q_block, n_heads, head_dim):
    t = h.shape[0]
    blk = ATT_ROWS * GRID_W
    aw = n_heads * head_dim
    heads_per_group = 2 * LANES // head_dim
    prev, nxt, kst, off = _attn_tables(images)
    n_blocks = t // blk
    assert prev.shape[0] == n_blocks
    kcol, vcol = q_block + 1, q_block + 2
    spec = lambda col, which: pl.BlockSpec(
        (blk, aw), {"cur": lambda b, p, n, k, o: (b, col),
                    "prev": lambda b, p, n, k, o: (p[b], col),
                    "next": lambda b, p, n, k, o: (n[b], col)}[which])
    grid_spec = pltpu.PrefetchScalarGridSpec(
        num_scalar_prefetch=4,
        grid=(n_blocks,),
        in_specs=[spec(q_block, "cur"),
                  spec(kcol, "prev"), spec(kcol, "cur"), spec(kcol, "next"),
                  spec(vcol, "prev"), spec(vcol, "cur"), spec(vcol, "next"),
                  pl.BlockSpec(bias_tab.shape, lambda b, p, n, k, o: (0, 0, 0, 0),
                               pipeline_mode=pl.Buffered(1))],
        out_specs=pl.BlockSpec((blk, aw), lambda b, p, n, k, o: (b, 0)),
        scratch_shapes=[pltpu.VMEM((3 * blk, aw), BF16), pltpu.VMEM((3 * blk, aw), BF16)],
    )
    return pl.pallas_call(
        functools.partial(_attn_body, heads_per_group=heads_per_group, head_dim=head_dim),
        grid_spec=grid_spec,
        out_shape=jax.ShapeDtypeStruct((t, aw), BF16),
        compiler_params=_cparams(("arbitrary",)),
        name="natten",
    )(jnp.asarray(prev), jnp.asarray(nxt), jnp.asarray(kst), jnp.asarray(off),
      h, h, h, h, h, h, h, bias_tab)


def _layer_norm(r, g, b):
    mu = jnp.mean(r, axis=-1, keepdims=True)
    c = r - mu
    var = jnp.mean(c * c, axis=-1, keepdims=True)
    return c * lax.rsqrt(var + LN_EPS) * g + b


HI_MASK = -65536


def _pack_bf16_pair(lo, hi):
    lo_bits = lax.bitcast_convert_type(lo.astype(BF16).astype(F32), I32)
    hi_bits = lax.bitcast_convert_type(hi.astype(BF16).astype(F32), I32)
    return (hi_bits & HI_MASK) | lax.shift_right_logical(lo_bits, 16)


def _unpack_bf16_pair(word):
    lo = lax.bitcast_convert_type(lax.shift_left(word, 16), F32)
    hi = lax.bitcast_convert_type(word & HI_MASK, F32)
    return lo, hi


def _first_index_of(mask, idx, sentinel, axis):
    return jnp.min(jnp.where(mask, idx, sentinel), axis=axis, keepdims=True)


def _mixer_body(seq_lo_ref, seq_hi_ref,
                xa_ref, xs_ref, up_ref, uc_ref, un_ref, gp_ref, ga_ref, at_ref,
                pw_ref, pb_ref, ps_ref, wpb_ref, wab_ref, wo_ref, g1_ref, b1_ref,
                rwh_ref, rwl_ref, rb_ref,
                x1_ref, x1f_ref, idx_ref, gate_ref, rank_ref, cnt_ref,
                carry_ref, *, n_a, alpha, n_experts):
    i = pl.program_id(0)
    tm = xa_ref.shape[0]
    t0 = i * tm
    lo_seq = seq_lo_ref[i]
    hi_seq = seq_hi_ref[i]

    @pl.when(i == 0)
    def _():
        carry_ref[...] = jnp.zeros_like(carry_ref)

    u_ext = jnp.concatenate([up_ref[...], uc_ref[...], un_ref[...]], axis=0)
    ext = tm + 2 * POOL_HALO
    tok_r = t0 + lax.broadcasted_iota(I32, (tm, ext), 0)
    tok_c = t0 - POOL_HALO + lax.broadcasted_iota(I32, (tm, ext), 1)
    tok_q = t0 + lax.broadcasted_iota(I32, (tm, 1), 0)
    n_pg = pw_ref.shape[0]
    pg = pw_ref.shape[1]
    parts = []
    for g in range(n_pg):
        w = POOL_WINDOWS[g]
        lo = jnp.maximum(tok_r - w // 2, lo_seq)
        hi = jnp.minimum(tok_r - w // 2 + w, hi_seq)
        band = jnp.where((tok_c >= lo) & (tok_c < hi), 1.0, 0.0).astype(BF16)
        cnt = (jnp.minimum(tok_q - w // 2 + w, hi_seq) - jnp.maximum(tok_q - w // 2, lo_seq)).astype(F32)
        ug = u_ext[:, g * pg:(g + 1) * pg]
        sums = jnp.dot(band, ug, preferred_element_type=F32)
        z = sums / cnt - uc_ref[:, g * pg:(g + 1) * pg].astype(F32)
        y = jnp.dot(z.astype(BF16), pw_ref[g], preferred_element_type=F32)
        y = (y + pb_ref[:, g * pg:(g + 1) * pg]) * ps_ref[:, g * pg:(g + 1) * pg]
        parts.append(y.astype(BF16))
    a = jnp.concatenate(parts, axis=1)
    a_pool = jnp.dot(a, wpb_ref[...], preferred_element_type=F32)
    a_attn = jnp.dot(at_ref[...], wab_ref[...], preferred_element_type=F32)
    merged = gp_ref[...].astype(F32) * a_pool + ga_ref[...].astype(F32) * a_attn
    mix = jnp.dot(merged.astype(BF16), wo_ref[...], preferred_element_type=F32)
    x_in = jnp.where(i < n_a, xa_ref[...], xs_ref[...])
    x1 = _layer_norm(alpha * x_in + mix, g1_ref[...], b1_ref[...])
    half = x1.shape[1] // 2
    n_pk = half // LANES
    word = _pack_bf16_pair(x1[:, :half], x1[:, half:])
    for c in range(n_pk):
        x1_ref[pl.ds(c, tm, stride=n_pk), :] = word[:, c * LANES:(c + 1) * LANES]
    x1f_ref[...] = x1
    x1b = x1.astype(BF16)

    x1l = (x1 - x1b.astype(F32)).astype(BF16)
    nt = (((1,), (1,)), ((), ()))
    logits = (lax.dot_general(rwh_ref[...], x1b, nt, preferred_element_type=F32)
              + lax.dot_general(rwh_ref[...], x1l, nt, preferred_element_type=F32)
              + lax.dot_general(rwl_ref[...], x1b, nt, preferred_element_type=F32))
    scores = jax.nn.sigmoid(logits)
    biased = scores + rb_ref[...]
    gsz = n_experts // N_GROUPS
    neg_inf = -jnp.inf

    mem = lax.broadcasted_iota(I32, (gsz, tm), 0)
    gs_rows = []
    for g in range(N_GROUPS):
        bg = biased[g * gsz:(g + 1) * gsz, :]
        m1 = jnp.max(bg, axis=0, keepdims=True)
        first = _first_index_of(bg == m1, mem, gsz, 0)
        m2 = jnp.max(jnp.where(mem == first, neg_inf, bg), axis=0, keepdims=True)
        gs_rows.append(m1 + m2)
    gwork = jnp.concatenate(gs_rows, axis=0)

    gidx = lax.broadcasted_iota(I32, (N_GROUPS, tm), 0)
    egrp = lax.broadcasted_iota(I32, (n_experts, tm), 0) // gsz
    eidx = lax.broadcasted_iota(I32, (n_experts, tm), 0)
    work = jnp.full((n_experts, tm), neg_inf, F32)
    for _ in range(TOPK_GROUPS):
        gm = jnp.max(gwork, axis=0, keepdims=True)
        gf = _first_index_of(gwork == gm, gidx, N_GROUPS, 0)
        gwork = jnp.where(gidx == gf, neg_inf, gwork)
        work = jnp.where(egrp == gf, biased, work)

    hot = jnp.zeros((n_experts, tm), F32)
    sel_idx, sel_w, sel_hit = [], [], []
    for _ in range(TOP_K):
        em = jnp.max(work, axis=0, keepdims=True)
        ef = _first_index_of(work == em, eidx, n_experts, 0)
        hit = eidx == ef
        sel_idx.append(ef)
        sel_w.append(jnp.sum(jnp.where(hit, scores, 0.0), axis=0, keepdims=True))
        sel_hit.append(hit)
        hot = jnp.where(hit, 1.0, hot)
        work = jnp.where(hit, neg_inf, work)
    wsel = jnp.concatenate(sel_w, axis=0)
    gate_ref[...] = wsel / jnp.sum(wsel, axis=0, keepdims=True) * ROUTED_SCALE
    idx_ref[...] = jnp.concatenate(sel_idx, axis=0)

    tri = jnp.where(lax.broadcasted_iota(I32, (tm, tm), 0) < lax.broadcasted_iota(I32, (tm, tm), 1),
                    1.0, 0.0).astype(BF16)
    before = carry_ref[:, 0:1] + jnp.dot(hot.astype(BF16), tri, preferred_element_type=F32)
    rank = [jnp.sum(jnp.where(h, before, 0.0), axis=0, keepdims=True) for h in sel_hit]
    rank_ref[...] = jnp.concatenate(rank, axis=0).astype(I32)
    carry_ref[...] = carry_ref[...] + jnp.sum(hot, axis=1, keepdims=True)
    cnt_ref[...] = carry_ref[...]


def _mixer(xa, xs, h, attn, seq_lo, seq_hi, p, *, alpha, u_block, gate_block0):
    d = xa.shape[1]
    t = xa.shape[0] + xs.shape[0]
    tm = MIX_TM
    assert xa.shape[0] % tm == 0 and xs.shape[0] % tm == 0
    n_a = xa.shape[0] // tm
    map_a, map_s = _two_array_maps(n_a)
    pw = p["wpb"].shape[0]
    aw = attn.shape[1]
    n_experts = p["rwh"].shape[0]
    halo_per_tile = tm // POOL_HALO
    n_halo_blocks = t // POOL_HALO
    gp_col = gate_block0 * INPROJ_TN // d
    in_specs = [
        pl.BlockSpec((tm, d), map_a),
        pl.BlockSpec((tm, d), map_s),
        pl.BlockSpec((POOL_HALO, pw), lambda i, lo, hi: (jnp.maximum(i * halo_per_tile - 1, 0), u_block)),
        pl.BlockSpec((tm, pw), lambda i, lo, hi: (i, u_block)),
        pl.BlockSpec((POOL_HALO, pw),
                     lambda i, lo, hi: (jnp.minimum((i + 1) * halo_per_tile, n_halo_blocks - 1), u_block)),
        pl.BlockSpec((tm, d), lambda i, lo, hi: (i, gp_col)),
        pl.BlockSpec((tm, d), lambda i, lo, hi: (i, gp_col + 1)),
        pl.BlockSpec((tm, aw), lambda i, lo, hi: (i, 0)),
    ]
    weights = [p["pool_w"], p["pool_b"], p["pool_scale"], p["wpb"], p["wab"], p["wo"], p["ln1_g"], p["ln1_b"],
               p["rwh"], p["rwl"], p["rb"]]
    in_specs += [pl.BlockSpec(w.shape, (lambda nd: (lambda i, lo, hi: (0,) * nd))(w.ndim),
                              pipeline_mode=pl.Buffered(1)) for w in weights]
    n_pk = d // (2 * LANES)
    out_shape = (jax.ShapeDtypeStruct((t * n_pk, LANES), I32),
                 jax.ShapeDtypeStruct((t, d), F32),
                 jax.ShapeDtypeStruct((TOP_K, t), I32),
                 jax.ShapeDtypeStruct((TOP_K, t), F32),
                 jax.ShapeDtypeStruct((TOP_K, t), I32),
                 jax.ShapeDtypeStruct((n_experts, LANES), F32))
    out_specs = (pl.BlockSpec((tm * n_pk, LANES), lambda i, lo, hi: (i, 0)),
                 pl.BlockSpec((tm, d), lambda i, lo, hi: (i, 0)),
                 pl.BlockSpec((TOP_K, tm), lambda i, lo, hi: (0, i)),
                 pl.BlockSpec((TOP_K, tm), lambda i, lo, hi: (0, i)),
                 pl.BlockSpec((TOP_K, tm), lambda i, lo, hi: (0, i)),
                 pl.BlockSpec((n_experts, LANES), lambda i, lo, hi: (0, 0)))
    grid_spec = pltpu.PrefetchScalarGridSpec(
        num_scalar_prefetch=2, grid=(t // tm,), in_specs=in_specs, out_specs=out_specs,
        scratch_shapes=[pltpu.VMEM((n_experts, LANES), F32)])
    return pl.pallas_call(
        functools.partial(_mixer_body, n_a=n_a, alpha=alpha, n_experts=n_experts),
        grid_spec=grid_spec, out_shape=out_shape,
        compiler_params=_cparams(("arbitrary",)),
        name="mixer",
    )(seq_lo, seq_hi, xa, xs, h, h, h, h, h, attn, *weights)


def _shared_ffn_body(x_ref, s13_ref, s2_ref, o_ref, *, alpha):
    x = x_ref[...]
    sd = s2_ref.shape[0]
    hc = jnp.dot(x.astype(BF16), s13_ref[...], preferred_element_type=F32)
    hdn = jax.nn.silu(hc[:, :sd]) * hc[:, sd:]
    o_ref[...] = alpha * x + jnp.dot(hdn.astype(BF16), s2_ref[...], preferred_element_type=F32)


def _shared_ffn(x1f, s13, s2, *, alpha):
    t, d = x1f.shape
    tm = SHARED_TM
    assert t % tm == 0
    return pl.pallas_call(
        functools.partial(_shared_ffn_body, alpha=alpha),
        grid=(t // tm,),
        in_specs=[pl.BlockSpec((tm, d), lambda i: (i, 0)),
                  pl.BlockSpec(s13.shape, lambda i: (0, 0), pipeline_mode=pl.Buffered(1)),
                  pl.BlockSpec(s2.shape, lambda i: (0, 0), pipeline_mode=pl.Buffered(1))],
        out_specs=pl.BlockSpec((tm, d), lambda i: (i, 0)),
        out_shape=jax.ShapeDtypeStruct((t, d), F32),
        compiler_params=_cparams(("arbitrary",)),
        name="shared_ffn",
    )(x1f, s13, s2)


SC_CORES = 2
SC_SUBCORES = 16
SC_CHUNK = 32


def _sc_dispatch(x1t3, dest3, n_rows):
    t, n_lt, _ = x1t3.shape
    n_workers = SC_CORES * SC_SUBCORES
    assert t % (n_workers * SC_CHUNK) == 0
    chunks_per_worker = t // (n_workers * SC_CHUNK)
    mesh = plsc.VectorSubcoreMesh(core_axis_name="c", subcore_axis_name="s",
                                  num_cores=SC_CORES, num_subcores=SC_SUBCORES)

    @functools.partial(
        pl.kernel, mesh=mesh,
        out_type=jax.ShapeDtypeStruct((n_rows, n_lt, LANES), x1t3.dtype),
        scratch_types=[pltpu.VMEM((TOP_K, SC_CHUNK), I32),
                       pltpu.VMEM((SC_CHUNK, n_lt, LANES), x1t3.dtype),
                       pltpu.SemaphoreType.DMA],
        name="sc_dispatch")
    def dispatch(x_hbm, dest_hbm, out_hbm, idx_v, rows_v, sem):
        wid = lax.axis_index("s") * SC_CORES + lax.axis_index("c")

        @pl.loop(0, chunks_per_worker)
        def _(j):
            chunk = wid * chunks_per_worker + j
            base = pl.multiple_of(chunk * SC_CHUNK, SC_CHUNK)
            pltpu.sync_copy(x_hbm.at[pl.ds(base, SC_CHUNK)], rows_v)
            pltpu.sync_copy(dest_hbm.at[chunk], idx_v)
            copies = [pltpu.make_async_copy(rows_v, out_hbm.at[idx_v.at[k]], sem) for k in range(TOP_K)]
            for cp in copies:
                cp.start()
            for cp in copies:
                cp.wait()

    return dispatch(x1t3, dest3)


SC_RING = 3


def _sc_gather(yb3, dest3, t):
    _, n_lt, _ = yb3.shape
    n_workers = SC_CORES * SC_SUBCORES
    chunks_per_worker = t // (n_workers * SC_CHUNK)
    mesh = plsc.VectorSubcoreMesh(core_axis_name="c", subcore_axis_name="s",
                                  num_cores=SC_CORES, num_subcores=SC_SUBCORES)

    @functools.partial(
        pl.kernel, mesh=mesh,
        out_type=jax.ShapeDtypeStruct((TOP_K * t, n_lt, LANES), yb3.dtype),
        scratch_types=[pltpu.VMEM((TOP_K, SC_CHUNK), I32),
                       pltpu.VMEM((SC_RING, SC_CHUNK, n_lt, LANES), yb3.dtype)]
                      + [pltpu.SemaphoreType.DMA] * (2 * SC_RING),
        name="sc_gather")
    def gather(y_hbm, dest_hbm, out_hbm, idx_v, rows_v, *sems):
        gsem, wsem = sems[:SC_RING], sems[SC_RING:]
        wid = lax.axis_index("s") * SC_CORES + lax.axis_index("c")

        @pl.loop(0, chunks_per_worker)
        def _(j):
            chunk = wid * chunks_per_worker + j
            base = pl.multiple_of(chunk * SC_CHUNK, SC_CHUNK)
            pltpu.sync_copy(dest_hbm.at[chunk], idx_v)
            gets = [pltpu.make_async_copy(y_hbm.at[idx_v.at[k]], rows_v.at[k % SC_RING], gsem[k % SC_RING])
                    for k in range(TOP_K)]
            puts = [pltpu.make_async_copy(rows_v.at[k % SC_RING], out_hbm.at[pl.ds(k * t + base, SC_CHUNK)],
                                          wsem[k % SC_RING]) for k in range(TOP_K)]
            put_waited = set()
            for k in range(SC_RING - 1):
                gets[k].start()
            for k in range(TOP_K):
                gets[k].wait()
                puts[k].start()
                nxt = k + SC_RING - 1
                if nxt < TOP_K:
                    if k >= 1:
                        puts[k - 1].wait()
                        put_waited.add(k - 1)
                    gets[nxt].start()
            for k in range(TOP_K):
                if k not in put_waited:
                    puts[k].wait()

    return gather(yb3, dest3)


def _experts_body(be_ref, nact_ref, first_ref, nxte_ref, wslot_ref, valid_ref,
                  xs_ref, w1_hbm, w3_hbm, w2_hbm, o_ref,
                  wst1, wst3, wst2, w13b, w2b, sem_w):
    b = pl.program_id(0)
    nact = nact_ref[0]
    ed = wst2.shape[1]
    d = wst2.shape[2]
    n_pk = d // (2 * LANES)
    bm = xs_ref.shape[0] // n_pk
    active = b < nact

    def weight_copies(e, s):
        return (pltpu.make_async_copy(w1_hbm.at[e], wst1.at[s], sem_w.at[s]),
                pltpu.make_async_copy(w3_hbm.at[e], wst3.at[s], sem_w.at[s]),
                pltpu.make_async_copy(w2_hbm.at[e], wst2.at[s], sem_w.at[s]))

    @pl.when(b == 0)
    def _():
        for cp in weight_copies(be_ref[0], wslot_ref[0]):
            cp.start()

    @pl.when(first_ref[b] == 1)
    def _():
        ws = wslot_ref[b]
        for cp in weight_copies(be_ref[b], ws):
            cp.wait()

        @pl.when(nxte_ref[b] >= 0)
        def _():
            for cp in weight_copies(nxte_ref[b], 1 - ws):
                cp.start(priority=1)

        for c in range(ed // MXU_N):
            w13b[:, 2 * c * MXU_N:(2 * c + 1) * MXU_N] = wst1[ws, :, c * MXU_N:(c + 1) * MXU_N].astype(BF16)
            w13b[:, (2 * c + 1) * MXU_N:(2 * c + 2) * MXU_N] = wst3[ws, :, c * MXU_N:(c + 1) * MXU_N].astype(BF16)
        w2b[...] = wst2[ws].astype(BF16)

    @pl.when(active)
    def _():
        keep = lax.broadcasted_iota(I32, (bm, LANES), 0) < valid_ref[b]
        los, his = [], []
        for c in range(n_pk):
            lo, hi = _unpack_bf16_pair(jnp.where(keep, xs_ref[pl.ds(c, bm, stride=n_pk), :], 0))
            los.append(lo.astype(BF16))
            his.append(hi.astype(BF16))
        xb = jnp.concatenate(los + his, axis=1)
        parts = []
        for c in range(ed // MXU_N):
            hc = jnp.dot(xb, w13b[:, 2 * c * MXU_N:(2 * c + 2) * MXU_N], preferred_element_type=F32)
            parts.append((jax.nn.silu(hc[:, :MXU_N]) * hc[:, MXU_N:]).astype(BF16))
        hdn = jnp.concatenate(parts, axis=1)
        n_down = d // MXU_N
        ys = [jnp.dot(hdn, w2b[:, c * MXU_N:(c + 1) * MXU_N], preferred_element_type=F32) for c in range(n_down)]
        tiles_per_chunk = MXU_N // LANES
        for c in range(n_down // 2):
            word = _pack_bf16_pair(ys[c], ys[c + n_down // 2])
            for h in range(tiles_per_chunk):
                o_ref[pl.ds(c * tiles_per_chunk + h, bm, stride=n_pk), :] = word[:, h * LANES:(h + 1) * LANES]

    @pl.when(jnp.logical_not(active))
    def _():
        o_ref[...] = jnp.zeros_like(o_ref)


def _experts(xs2d, block_e, nact, first, nxt_e, wslot, valid, w1, w3, w2, bm):
    d, ed = w1.shape[1], w1.shape[2]
    n_pk = d // (2 * LANES)
    n_blk = block_e.shape[0]
    assert xs2d.shape == (n_blk * bm * n_pk, LANES)
    assert ed % MXU_N == 0 and d % MXU_N == 0 and bm % SUBLANES == 0
    grid_spec = pltpu.PrefetchScalarGridSpec(
        num_scalar_prefetch=6,
        grid=(n_blk,),
        in_specs=[pl.BlockSpec((bm * n_pk, LANES),
                               lambda b, be, na, *_: (jnp.maximum(jnp.minimum(b, na[0] - 1), 0), 0)),
                  pl.BlockSpec(memory_space=pl.ANY),
                  pl.BlockSpec(memory_space=pl.ANY),
                  pl.BlockSpec(memory_space=pl.ANY)],
        out_specs=pl.BlockSpec((bm * n_pk, LANES), lambda b, *_: (b, 0)),
        scratch_shapes=[pltpu.VMEM((2, d, ed), F32),
                        pltpu.VMEM((2, d, ed), F32),
                        pltpu.VMEM((2, ed, d), F32),
                        pltpu.VMEM((d, 2 * ed), BF16),
                        pltpu.VMEM((ed, d), BF16),
                        pltpu.SemaphoreType.DMA((2,))],
    )
    return pl.pallas_call(
        _experts_body,
        grid_spec=grid_spec,
        out_shape=jax.ShapeDtypeStruct((n_blk * bm * n_pk, LANES), I32),
        compiler_params=_cparams(("arbitrary",)),
        name="experts",
    )(block_e, nact, first, nxt_e, wslot, valid, xs2d, w1, w3, w2)


def _finalize_body(base_ref, gate_ref, g2_ref, b2_ref, *refs, n_a):
    slabs, (oa_ref, os_ref) = refs[:TOP_K], refs[TOP_K:]
    i = pl.program_id(0)
    tm, d = base_ref.shape
    half = d // 2
    n_pk = half // LANES
    acc_lo = [base_ref[:, c * LANES:(c + 1) * LANES] for c in range(n_pk)]
    acc_hi = [base_ref[:, half + c * LANES:half + (c + 1) * LANES] for c in range(n_pk)]
    for k in range(TOP_K):
        g = gate_ref[:, k:k + 1]
        for c in range(n_pk):
            lo, hi = _unpack_bf16_pair(slabs[k][pl.ds(c, tm, stride=n_pk), :])
            acc_lo[c] = acc_lo[c] + g * lo
            acc_hi[c] = acc_hi[c] + g * hi
    y = _layer_norm(jnp.concatenate(acc_lo + acc_hi, axis=1), g2_ref[...], b2_ref[...])

    @pl.when(i < n_a)
    def _():
        oa_ref[...] = y

    @pl.when(i >= n_a)
    def _():
        os_ref[...] = y


def _finalize(base, gate_t, slabs2d, ln_g, ln_b, t_a):
    t, d = base.shape
    tm = FINAL_TM
    n_pk = d // (2 * LANES)
    assert t_a % tm == 0 and 0 < t_a < t and slabs2d.shape == (TOP_K * t * n_pk, LANES)
    n_a = t_a // tm
    n_steps = t // tm
    map_a, map_s = _two_array_maps(n_a)
    slab_specs = [pl.BlockSpec((tm * n_pk, LANES), (lambda k: (lambda i: (k * n_steps + i, 0)))(k))
                  for k in range(TOP_K)]
    return pl.pallas_call(
        functools.partial(_finalize_body, n_a=n_a),
        grid=(n_steps,),
        in_specs=[pl.BlockSpec((tm, d), lambda i: (i, 0)),
                  pl.BlockSpec((tm, TOP_K), lambda i: (i, 0)),
                  pl.BlockSpec((1, d), lambda i: (0, 0)),
                  pl.BlockSpec((1, d), lambda i: (0, 0))] + slab_specs,
        out_specs=(pl.BlockSpec((tm, d), map_a), pl.BlockSpec((tm, d), map_s)),
        out_shape=(jax.ShapeDtypeStruct((t_a, d), F32), jax.ShapeDtypeStruct((t - t_a, d), F32)),
        compiler_params=_cparams(("arbitrary",)),
        name="finalize",
    )(base, gate_t, ln_g, ln_b, *([slabs2d] * TOP_K))


def _split_hi_lo(w):
    hi = w.astype(BF16)
    lo = (w - hi.astype(F32)).astype(BF16)
    return hi, lo


def _encoder_layer(xa, xs, seqs, lp, alpha):
    (w_in, pool_w, pool_b, pool_scale, w_pool_br, w_attn_br, attn_rpb, w_out, ln1_g, ln1_b,
     router_w, router_bias, exp_w1, exp_w3, exp_w2, sh_w1, sh_w3, sh_w2, ln2_g, ln2_b) = lp
    d = xa.shape[1]
    t = xa.shape[0] + xs.shape[0]
    pw = w_pool_br.shape[0]
    aw = w_attn_br.shape[0]
    n_heads = attn_rpb.shape[0]
    head_dim = aw // n_heads
    n_experts = router_w.shape[1]
    assert pw == INPROJ_TN and aw == INPROJ_TN and d % INPROJ_TN == 0
    u_block, q_block = 0, pw // INPROJ_TN
    gate_block0 = (pw + 3 * aw) // INPROJ_TN

    h = _inproj(xa, xs, w_in.astype(BF16), q_block=q_block, gate_block0=gate_block0, q_scale=head_dim ** -0.5)

    blk = ATT_ROWS * GRID_W
    images, seq_lo, seq_hi = [], [], []
    for s0, s1 in seqs:
        images.append((s0 // blk, (s1 - s0) // blk))
        for _ in range((s1 - s0) // MIX_TM):
            seq_lo.append(s0)
            seq_hi.append(s1)
    bias_tab = _attn_bias_table(attn_rpb, 2 * LANES // head_dim)
    attn = _attention(h, bias_tab, images, q_block=q_block, n_heads=n_heads, head_dim=head_dim)

    rwh, rwl = _split_hi_lo(router_w.T)
    p = dict(pool_w=pool_w.astype(BF16), pool_b=pool_b.reshape(1, pw), pool_scale=pool_scale.reshape(1, pw),
             wpb=w_pool_br.astype(BF16), wab=w_attn_br.astype(BF16), wo=w_out.astype(BF16),
             ln1_g=ln1_g.reshape(1, d), ln1_b=ln1_b.reshape(1, d),
             rwh=rwh, rwl=rwl, rb=router_bias.reshape(n_experts, 1).astype(F32))
    x1, x1f, idx_t, gate_t, rank_t, counts = _mixer(
        xa, xs, h, attn, jnp.asarray(seq_lo, I32), jnp.asarray(seq_hi, I32), p,
        alpha=alpha, u_block=u_block, gate_block0=gate_block0)

    bm = EXPERT_BM
    cnt = counts[:, 0].astype(I32)
    pcnt = (cnt + bm - 1) // bm * bm
    pend = jnp.cumsum(pcnt)
    pstart = pend - pcnt
    eids = jnp.arange(n_experts, dtype=I32)[:, None, None]
    dest = jnp.sum(jnp.where(idx_t[None] == eids, pstart[:, None, None], 0), axis=0) + rank_t
    n_blk = (t * TOP_K + n_experts * (bm - 1)) // bm
    blk_row = jnp.arange(n_blk, dtype=I32) * bm
    block_e = jnp.minimum(jnp.sum((pend[None, :] <= blk_row[:, None]).astype(I32), axis=1), n_experts - 1)
    nact = (pend[-1] // bm).astype(I32).reshape(1)
    block_e = jnp.where(jnp.arange(n_blk) < nact[0], block_e, block_e[jnp.maximum(nact[0] - 1, 0)])
    eids1 = jnp.arange(n_experts, dtype=I32)
    present = cnt > 0
    later = present[None, :] & (eids1[None, :] > eids1[:, None])
    nxt_of_e = jnp.min(jnp.where(later, eids1[None, :], n_experts), axis=1)
    nxt_of_e = jnp.where(nxt_of_e >= n_experts, -1, nxt_of_e).astype(I32)
    ord_of_e = (jnp.cumsum(present.astype(I32)) - 1).astype(I32)
    onehot_be = (block_e[:, None] == eids1[None, :]).astype(I32)
    blk_ids = jnp.arange(n_blk, dtype=I32)
    is_act = blk_ids < nact[0]
    prev_e = jnp.concatenate([jnp.full((1,), -1, I32), block_e[:-1]])
    first = (is_act & (block_e != prev_e)).astype(I32)
    nxt_e = jnp.sum(onehot_be * nxt_of_e[None, :], axis=1).astype(I32)
    wslot = (jnp.sum(onehot_be * ord_of_e[None, :], axis=1) % 2).astype(I32)
    start_of_e = jnp.sum(onehot_be * pstart[None, :], axis=1)
    cnt_of_e = jnp.sum(onehot_be * cnt[None, :], axis=1)
    valid = jnp.where(is_act, jnp.clip(cnt_of_e - (blk_row - start_of_e), 0, bm), 0).astype(I32)
    n_pk = d // (2 * LANES)
    dest3 = jnp.transpose(dest.reshape(TOP_K, t // SC_CHUNK, SC_CHUNK), (1, 0, 2))
    xs = _sc_dispatch(x1.reshape(t, n_pk, LANES), dest3, n_blk * bm)
    base = _shared_ffn(x1f, jnp.concatenate([sh_w1, sh_w3], axis=1).astype(BF16), sh_w2.astype(BF16), alpha=alpha)
    yb = _experts(xs.reshape(n_blk * bm * n_pk, LANES), block_e, nact, first, nxt_e, wslot, valid,
                  exp_w1, exp_w3, exp_w2, bm)

    slabs = _sc_gather(yb.reshape(n_blk * bm, n_pk, LANES), dest3, t)
    return _finalize(base, gate_t.T, slabs.reshape(TOP_K * t * n_pk, LANES),
                     ln2_g.reshape(1, d), ln2_b.reshape(1, d), xa.shape[0])


def kernel(x_prompt, x_sample, w_in, pool_w, pool_b, pool_scale, w_pool_br, w_attn_br, attn_rpb, w_out,
           ln1_g, ln1_b, router_w, router_bias, exp_w1, exp_w3, exp_w2, sh_w1, sh_w3, sh_w2, ln2_g, ln2_b):
    depth = w_in.shape[0]
    alpha = (2.0 * depth) ** 0.25
    bp, sp, d = x_prompt.shape
    bs, ss, _ = x_sample.shape
    seqs = [(i * sp, (i + 1) * sp) for i in range(bp)]
    seqs += [(bp * sp + i * ss, bp * sp + (i + 1) * ss) for i in range(bs)]
    xa, xs = x_prompt.reshape(bp * sp, d), x_sample.reshape(bs * ss, d)
    for l in range(depth):
        lp = (w_in[l], pool_w[l], pool_b[l], pool_scale[l], w_pool_br[l], w_attn_br[l], attn_rpb[l], w_out[l],
              ln1_g[l], ln1_b[l], router_w[l], router_bias[l], exp_w1[l], exp_w3[l], exp_w2[l],
              sh_w1[l], sh_w3[l], sh_w2[l], ln2_g[l], ln2_b[l])
        xa, xs = _encoder_layer(xa, xs, seqs, lp, alpha)
    return (xa.reshape(bp, sp, d), xs.reshape(bs, ss, d))
```

```python
import functools

import numpy as np
import jax
import jax.numpy as jnp
from jax import lax
from jax.experimental import pallas as pl
from jax.experimental.pallas import tpu as pltpu
from jax.experimental.pallas import tpu_sc as plsc

F32 = jnp.float32
BF16 = jnp.bfloat16
I32 = jnp.int32

GRID_W = 64
POOL_WINDOWS = (2, 4, 8, 16)
WIN_R = 8
WIN_C = 16
TOP_K = 8
N_GROUPS = 8
TOPK_GROUPS = 4
ROUTED_SCALE = 2.5
LN_EPS = 1e-5
NEG_BIG = -1e30

LANES = 128
SUBLANES = 8
MXU_N = 256
VMEM_LIMIT = 56 * 1024 * 1024

INPROJ_TM = 1024
INPROJ_TN = 1024
ATT_ROWS = 8
MIX_TM = 256
POOL_HALO = 16
EXPERT_BM = 512
FINAL_TM = 256
SHARED_TM = 512


def _cparams(sem):
    return pltpu.CompilerParams(dimension_semantics=sem, vmem_limit_bytes=VMEM_LIMIT)


def _inproj_body(xa_ref, xs_ref, w_ref, o_ref, xb_ref, *, n_a, q_block, gate_block0, q_scale):
    i = pl.program_id(0)
    j = pl.program_id(1)

    @pl.when((j == 0) & (i < n_a))
    def _():
        xb_ref[...] = xa_ref[...].astype(BF16)

    @pl.when((j == 0) & (i >= n_a))
    def _():
        xb_ref[...] = xs_ref[...].astype(BF16)

    is_gate = j >= gate_block0
    scale = jnp.where(j == q_block, q_scale, 1.0).astype(F32)
    for c in range(w_ref.shape[1] // MXU_N):
        acc = jnp.dot(xb_ref[...], w_ref[:, c * MXU_N:(c + 1) * MXU_N], preferred_element_type=F32)
        out = jnp.where(is_gate, jax.nn.sigmoid(acc), acc * scale)
        o_ref[:, c * MXU_N:(c + 1) * MXU_N] = out.astype(BF16)


def _two_array_maps(n_a):
    return (lambda i, *_: (jnp.minimum(i, n_a - 1), 0)), (lambda i, *_: (jnp.maximum(i - n_a, 0), 0))


def _inproj(xa, xs, w_bf16, *, q_block, gate_block0, q_scale):
    d = xa.shape[1]
    t = xa.shape[0] + xs.shape[0]
    n = w_bf16.shape[1]
    tm, tn = INPROJ_TM, INPROJ_TN
    assert xa.shape[0] % tm == 0 and xs.shape[0] % tm == 0 and xa.shape[0] > 0 and xs.shape[0] > 0
    n_a = xa.shape[0] // tm
    map_a, map_s = _two_array_maps(n_a)
    return pl.pallas_call(
        functools.partial(_inproj_body, n_a=n_a, q_block=q_block, gate_block0=gate_block0, q_scale=q_scale),
        grid=(t // tm, n // tn),
        in_specs=[pl.BlockSpec((tm, d), map_a),
                  pl.BlockSpec((tm, d), map_s),
                  pl.BlockSpec((d, tn), lambda i, j: (0, j))],
        out_specs=pl.BlockSpec((tm, tn), lambda i, j: (i, j)),
        out_shape=jax.ShapeDtypeStruct((t, n), BF16),
        scratch_shapes=[pltpu.VMEM((tm, d), BF16)],
        compiler_params=_cparams(("arbitrary", "arbitrary")),
        name="inproj",
    )(xa, xs, w_bf16)


def _attn_body(prev_ref, next_ref, kst_ref, off_ref,
               q_ref, kp_ref, kc_ref, kn_ref, vp_ref, vc_ref, vn_ref, bias_ref, o_ref,
               kbuf, vbuf, *, heads_per_group, head_dim):
    del prev_ref, next_ref
    b = pl.program_id(0)
    blk = ATT_ROWS * GRID_W
    kbuf[0:blk] = kp_ref[...]
    kbuf[blk:2 * blk] = kc_ref[...]
    kbuf[2 * blk:3 * blk] = kn_ref[...]
    vbuf[0:blk] = vp_ref[...]
    vbuf[blk:2 * blk] = vc_ref[...]
    vbuf[2 * blk:3 * blk] = vn_ref[...]

    gw = heads_per_group * head_dim
    n_groups = q_ref.shape[1] // gw
    win = WIN_R * GRID_W
    nq = heads_per_group * GRID_W
    row_head = lax.broadcasted_iota(I32, (nq, gw), 0) // GRID_W
    lane_head = lax.broadcasted_iota(I32, (nq, gw), 1) // head_dim
    head_mask = row_head == lane_head
    out_lane_head = lax.broadcasted_iota(I32, (GRID_W, gw), 1) // head_dim

    def row_step(i, carry):
        st = pl.multiple_of(kst_ref[b * ATT_ROWS + i] * GRID_W, GRID_W)
        dr0 = (WIN_R - 1) - off_ref[b * ATT_ROWS + i]
        q0 = pl.multiple_of(i * GRID_W, GRID_W)
        for g in range(n_groups):
            qg = q_ref[pl.ds(q0, GRID_W), g * gw:(g + 1) * gw]
            kg = kbuf[pl.ds(st, win), g * gw:(g + 1) * gw]
            vg = vbuf[pl.ds(st, win), g * gw:(g + 1) * gw]
            wt = jnp.concatenate([qg] * heads_per_group, axis=0)
            wt = jnp.where(head_mask, wt, jnp.zeros_like(wt))
            s = lax.dot_general(wt, kg, (((1,), (1,)), ((), ())),
                                preferred_element_type=F32)
            n_chunk = win // LANES
            sj = [s[:, j * LANES:(j + 1) * LANES] + bias_ref[dr0 + 2 * j, g] for j in range(n_chunk)]
            m = sj[0]
            for j in range(1, n_chunk):
                m = jnp.maximum(m, sj[j])
            m = jnp.max(m, axis=1, keepdims=True)
            ej = [jnp.exp(x - m) for x in sj]
            l = ej[0]
            for j in range(1, n_chunk):
                l = l + ej[j]
            inv = 1.0 / jnp.sum(l, axis=1, keepdims=True)
            p = jnp.concatenate([(x * inv).astype(BF16) for x in ej], axis=1)
            of = jnp.dot(p, vg, preferred_element_type=F32)
            out = jnp.zeros((GRID_W, gw), F32)
            for h in range(heads_per_group):
                out = out + jnp.where(out_lane_head == h, of[h * GRID_W:(h + 1) * GRID_W, :], 0.0)
            o_ref[pl.ds(q0, GRID_W), g * gw:(g + 1) * gw] = out.astype(o_ref.dtype)
        return carry

    lax.fori_loop(0, ATT_ROWS, row_step, 0, unroll=4)


def _attn_tables(images):
    prev, nxt, kst, off = [], [], [], []
    for s, n in images:
        rows = n * ATT_ROWS
        kr = min(WIN_R, rows)
        assert kr == WIN_R
        for bl in range(n):
            b = s + bl
            prev.append(max(b - 1, s))
            nxt.append(min(b + 1, s + n - 1))
            for i in range(ATT_ROWS):
                r = bl * ATT_ROWS + i
                rs = min(max(r - kr // 2, 0), rows - kr)
                kst.append(rs - bl * ATT_ROWS + ATT_ROWS)
                off.append(r - rs)
    return (np.asarray(prev, np.int32), np.asarray(nxt, np.int32),
            np.asarray(kst, np.int32), np.asarray(off, np.int32))


def _attn_bias_table(rpb, heads_per_group):
    n_heads, n_dr, n_dc = rpb.shape
    n_pair = n_dr - 1
    c = np.arange(GRID_W)
    cs = np.clip(c - WIN_C // 2, 0, GRID_W - WIN_C)
    kc = np.arange(GRID_W)
    valid = (kc[None, :] >= cs[:, None]) & (kc[None, :] < cs[:, None] + WIN_C)
    dc = np.clip(kc[None, :] - c[:, None] + (WIN_C - 1), 0, n_dc - 1)
    sel = np.zeros((2, n_dc, GRID_W, 2, GRID_W), np.float32)
    ci, ki = np.nonzero(valid)
    for p in range(2):
        sel[p, dc[ci, ki], ci, p, ki] = 1.0
    sel = jnp.asarray(sel.reshape(2 * n_dc, GRID_W * 2 * GRID_W))
    r = rpb.astype(F32)
    pair = jnp.stack([r[:, :n_pair], r[:, 1:]], axis=2)
    pair = jnp.transpose(pair, (1, 0, 2, 3)).reshape(n_pair * n_heads, 2 * n_dc)
    tab = jnp.dot(pair, sel, precision=lax.Precision.HIGHEST)
    tab = tab.reshape(n_pair, n_heads, GRID_W, 2 * GRID_W)
    valid2 = jnp.asarray(np.tile(valid[:, None, :], (1, 2, 1)).reshape(GRID_W, 2 * GRID_W))
    tab = jnp.where(valid2[None, None], tab, NEG_BIG)
    return tab.reshape(n_pair, n_heads // heads_per_group, heads_per_group * GRID_W, 2 * GRID_W)


def _attention(h, bias_tab, images, *, q_block, n_heads, head_dim):
    t = h.shape[0]
    blk = ATT_ROWS * GRID_W
    aw = n_heads * head_dim
    heads_per_group = 2 * LANES // head_dim
    prev, nxt, kst, off = _attn_tables(images)
    n_blocks = t // blk
    assert prev.shape[0] == n_blocks
    kcol, vcol = q_block + 1, q_block + 2
    spec = lambda col, which: pl.BlockSpec(
        (blk, aw), {"cur": lambda b, p, n, k, o: (b, col),
                    "prev": lambda b, p, n, k, o: (p[b], col),
                    "next": lambda b, p, n, k, o: (n[b], col)}[which])
    grid_spec = pltpu.PrefetchScalarGridSpec(
        num_scalar_prefetch=4,
        grid=(n_blocks,),
        in_specs=[spec(q_block, "cur"),
                  spec(kcol, "prev"), spec(kcol, "cur"), spec(kcol, "next"),
                  spec(vcol, "prev"), spec(vcol, "cur"), spec(vcol, "next"),
                  pl.BlockSpec(bias_tab.shape, lambda b, p, n, k, o: (0, 0, 0, 0),
                               pipeline_mode=pl.Buffered(1))],
        out_specs=pl.BlockSpec((blk, aw), lambda b, p, n, k, o: (b, 0)),
        scratch_shapes=[pltpu.VMEM((3 * blk, aw), BF16), pltpu.VMEM((3 * blk, aw), BF16)],
    )
    return pl.pallas_call(
        functools.partial(_attn_body, heads_per_group=heads_per_group, head_dim=head_dim),
        grid_spec=grid_spec,
        out_shape=jax.ShapeDtypeStruct((t, aw), BF16),
        compiler_params=_cparams(("arbitrary",)),
        name="natten",
    )(jnp.asarray(prev), jnp.asarray(nxt), jnp.asarray(kst), jnp.asarray(off),
      h, h, h, h, h, h, h, bias_tab)


def _layer_norm(r, g, b):
    mu = jnp.mean(r, axis=-1, keepdims=True)
    c = r - mu
    var = jnp.mean(c * c, axis=-1, keepdims=True)
    return c * lax.rsqrt(var + LN_EPS) * g + b


HI_MASK = -65536


def _pack_bf16_pair(lo, hi):
    lo_bits = lax.bitcast_convert_type(lo.astype(BF16).astype(F32), I32)
    hi_bits = lax.bitcast_convert_type(hi.astype(BF16).astype(F32), I32)
    return (hi_bits & HI_MASK) | lax.shift_right_logical(lo_bits, 16)


def _unpack_bf16_pair(word):
    lo = lax.bitcast_convert_type(lax.shift_left(word, 16), F32)
    hi = lax.bitcast_convert_type(word & HI_MASK, F32)
    return lo, hi


def _first_index_of(mask, idx, sentinel, axis):
    return jnp.min(jnp.where(mask, idx, sentinel), axis=axis, keepdims=True)


def _mixer_body(seq_lo_ref, seq_hi_ref,
                xa_ref, xs_ref, up_ref, uc_ref, un_ref, gp_ref, ga_ref, at_ref,
                pw_ref, pb_ref, ps_ref, wpb_ref, wab_ref, wo_ref, g1_ref, b1_ref,
                rwh_ref, rwl_ref, rb_ref,
                x1_ref, x1f_ref, idx_ref, gate_ref, rank_ref, cnt_ref,
                carry_ref, *, n_a, alpha, n_experts):
    i = pl.program_id(0)
    tm = xa_ref.shape[0]
    t0 = i * tm
    lo_seq = seq_lo_ref[i]
    hi_seq = seq_hi_ref[i]

    @pl.when(i == 0)
    def _():
        carry_ref[...] = jnp.zeros_like(carry_ref)

    u_ext = jnp.concatenate([up_ref[...], uc_ref[...], un_ref[...]], axis=0)
    ext = tm + 2 * POOL_HALO
    tok_r = t0 + lax.broadcasted_iota(I32, (tm, ext), 0)
    tok_c = t0 - POOL_HALO + lax.broadcasted_iota(I32, (tm, ext), 1)
    tok_q = t0 + lax.broadcasted_iota(I32, (tm, 1), 0)
    n_pg = pw_ref.shape[0]
    pg = pw_ref.shape[1]
    parts = []
    for g in range(n_pg):
        w = POOL_WINDOWS[g]
        lo = jnp.maximum(tok_r - w // 2, lo_seq)
        hi = jnp.minimum(tok_r - w // 2 + w, hi_seq)
        band = jnp.where((tok_c >= lo) & (tok_c < hi), 1.0, 0.0).astype(BF16)
        cnt = (jnp.minimum(tok_q - w // 2 + w, hi_seq) - jnp.maximum(tok_q - w // 2, lo_seq)).astype(F32)
        ug = u_ext[:, g * pg:(g + 1) * pg]
        sums = jnp.dot(band, ug, preferred_element_type=F32)
        z = sums / cnt - uc_ref[:, g * pg:(g + 1) * pg].astype(F32)
        y = jnp.dot(z.astype(BF16), pw_ref[g], preferred_element_type=F32)
        y = (y + pb_ref[:, g * pg:(g + 1) * pg]) * ps_ref[:, g * pg:(g + 1) * pg]
        parts.append(y.astype(BF16))
    a = jnp.concatenate(parts, axis=1)
    a_pool = jnp.dot(a, wpb_ref[...], preferred_element_type=F32)
    a_attn = jnp.dot(at_ref[...], wab_ref[...], preferred_element_type=F32)
    merged = gp_ref[...].astype(F32) * a_pool + ga_ref[...].astype(F32) * a_attn
    mix = jnp.dot(merged.astype(BF16), wo_ref[...], preferred_element_type=F32)
    x_in = jnp.where(i < n_a, xa_ref[...], xs_ref[...])
    x1 = _layer_norm(alpha * x_in + mix, g1_ref[...], b1_ref[...])
    half = x1.shape[1] // 2
    n_pk = half // LANES
    word = _pack_bf16_pair(x1[:, :half], x1[:, half:])
    for c in range(n_pk):
        x1_ref[pl.ds(c, tm, stride=n_pk), :] = word[:, c * LANES:(c + 1) * LANES]
    x1f_ref[...] = x1
    x1b = x1.astype(BF16)

    x1l = (x1 - x1b.astype(F32)).astype(BF16)
    nt = (((1,), (1,)), ((), ()))
    logits = (lax.dot_general(rwh_ref[...], x1b, nt, preferred_element_type=F32)
              + lax.dot_general(rwh_ref[...], x1l, nt, preferred_element_type=F32)
              + lax.dot_general(rwl_ref[...], x1b, nt, preferred_element_type=F32))
    scores = jax.nn.sigmoid(logits)
    biased = scores + rb_ref[...]
    gsz = n_experts // N_GROUPS
    neg_inf = -jnp.inf

    mem = lax.broadcasted_iota(I32, (gsz, tm), 0)
    gs_rows = []
    for g in range(N_GROUPS):
        bg = biased[g * gsz:(g + 1) * gsz, :]
        m1 = jnp.max(bg, axis=0, keepdims=True)
        first = _first_index_of(bg == m1, mem, gsz, 0)
        m2 = jnp.max(jnp.where(mem == first, neg_inf, bg), axis=0, keepdims=True)
        gs_rows.append(m1 + m2)
    gwork = jnp.concatenate(gs_rows, axis=0)

    gidx = lax.broadcasted_iota(I32, (N_GROUPS, tm), 0)
    egrp = lax.broadcasted_iota(I32, (n_experts, tm), 0) // gsz
    eidx = lax.broadcasted_iota(I32, (n_experts, tm), 0)
    work = jnp.full((n_experts, tm), neg_inf, F32)
    for _ in range(TOPK_GROUPS):
        gm = jnp.max(gwork, axis=0, keepdims=True)
        gf = _first_index_of(gwork == gm, gidx, N_GROUPS, 0)
        gwork = jnp.where(gidx == gf, neg_inf, gwork)
        work = jnp.where(egrp == gf, biased, work)

    hot = jnp.zeros((n_experts, tm), F32)
    sel_idx, sel_w, sel_hit = [], [], []
    for _ in range(TOP_K):
        em = jnp.max(work, axis=0, keepdims=True)
        ef = _first_index_of(work == em, eidx, n_experts, 0)
        hit = eidx == ef
        sel_idx.append(ef)
        sel_w.append(jnp.sum(jnp.where(hit, scores, 0.0), axis=0, keepdims=True))
        sel_hit.append(hit)
        hot = jnp.where(hit, 1.0, hot)
        work = jnp.where(hit, neg_inf, work)
    wsel = jnp.concatenate(sel_w, axis=0)
    gate_ref[...] = wsel / jnp.sum(wsel, axis=0, keepdims=True) * ROUTED_SCALE
    idx_ref[...] = jnp.concatenate(sel_idx, axis=0)

    tri = jnp.where(lax.broadcasted_iota(I32, (tm, tm), 0) < lax.broadcasted_iota(I32, (tm, tm), 1),
                    1.0, 0.0).astype(BF16)
    before = carry_ref[:, 0:1] + jnp.dot(hot.astype(BF16), tri, preferred_element_type=F32)
    rank = [jnp.sum(jnp.where(h, before, 0.0), axis=0, keepdims=True) for h in sel_hit]
    rank_ref[...] = jnp.concatenate(rank, axis=0).astype(I32)
    carry_ref[...] = carry_ref[...] + jnp.sum(hot, axis=1, keepdims=True)
    cnt_ref[...] = carry_ref[...]


def _mixer(xa, xs, h, attn, seq_lo, seq_hi, p, *, alpha, u_block, gate_block0):
    d = xa.shape[1]
    t = xa.shape[0] + xs.shape[0]
    tm = MIX_TM
    assert xa.shape[0] % tm == 0 and xs.shape[0] % tm == 0
    n_a = xa.shape[0] // tm
    map_a, map_s = _two_array_maps(n_a)
    pw = p["wpb"].shape[0]
    aw = attn.shape[1]
    n_experts = p["rwh"].shape[0]
    halo_per_tile = tm // POOL_HALO
    n_halo_blocks = t // POOL_HALO
    gp_col = gate_block0 * INPROJ_TN // d
    in_specs = [
        pl.BlockSpec((tm, d), map_a),
        pl.BlockSpec((tm, d), map_s),
        pl.BlockSpec((POOL_HALO, pw), lambda i, lo, hi: (jnp.maximum(i * halo_per_tile - 1, 0), u_block)),
        pl.BlockSpec((tm, pw), lambda i, lo, hi: (i, u_block)),
        pl.BlockSpec((POOL_HALO, pw),
                     lambda i, lo, hi: (jnp.minimum((i + 1) * halo_per_tile, n_halo_blocks - 1), u_block)),
        pl.BlockSpec((tm, d), lambda i, lo, hi: (i, gp_col)),
        pl.BlockSpec((tm, d), lambda i, lo, hi: (i, gp_col + 1)),
        pl.BlockSpec((tm, aw), lambda i, lo, hi: (i, 0)),
    ]
    weights = [p["pool_w"], p["pool_b"], p["pool_scale"], p["wpb"], p["wab"], p["wo"], p["ln1_g"], p["ln1_b"],
               p["rwh"], p["rwl"], p["rb"]]
    in_specs += [pl.BlockSpec(w.shape, (lambda nd: (lambda i, lo, hi: (0,) * nd))(w.ndim),
                              pipeline_mode=pl.Buffered(1)) for w in weights]
    n_pk = d // (2 * LANES)
    out_shape = (jax.ShapeDtypeStruct((t * n_pk, LANES), I32),
                 jax.ShapeDtypeStruct((t, d), F32),
                 jax.ShapeDtypeStruct((TOP_K, t), I32),
                 jax.ShapeDtypeStruct((TOP_K, t), F32),
                 jax.ShapeDtypeStruct((TOP_K, t), I32),
                 jax.ShapeDtypeStruct((n_experts, LANES), F32))
    out_specs = (pl.BlockSpec((tm * n_pk, LANES), lambda i, lo, hi: (i, 0)),
                 pl.BlockSpec((tm, d), lambda i, lo, hi: (i, 0)),
                 pl.BlockSpec((TOP_K, tm), lambda i, lo, hi: (0, i)),
                 pl.BlockSpec((TOP_K, tm), lambda i, lo, hi: (0, i)),
                 pl.BlockSpec((TOP_K, tm), lambda i, lo, hi: (0, i)),
                 pl.BlockSpec((n_experts, LANES), lambda i, lo, hi: (0, 0)))
    grid_spec = pltpu.PrefetchScalarGridSpec(
        num_scalar_prefetch=2, grid=(t // tm,), in_specs=in_specs, out_specs=out_specs,
        scratch_shapes=[pltpu.VMEM((n_experts, LANES), F32)])
    return pl.pallas_call(
        functools.partial(_mixer_body, n_a=n_a, alpha=alpha, n_experts=n_experts),
        grid_spec=grid_spec, out_shape=out_shape,
        compiler_params=_cparams(("arbitrary",)),
        name="mixer",
    )(seq_lo, seq_hi, xa, xs, h, h, h, h, h, attn, *weights)


def _shared_ffn_body(x_ref, s13_ref, s2_ref, o_ref, *, alpha):
    x = x_ref[...]
    sd = s2_ref.shape[0]
    hc = jnp.dot(x.astype(BF16), s13_ref[...], preferred_element_type=F32)
    hdn = jax.nn.silu(hc[:, :sd]) * hc[:, sd:]
    o_ref[...] = alpha * x + jnp.dot(hdn.astype(BF16), s2_ref[...], preferred_element_type=F32)


def _shared_ffn(x1f, s13, s2, *, alpha):
    t, d = x1f.shape
    tm = SHARED_TM
    assert t % tm == 0
    return pl.pallas_call(
        functools.partial(_shared_ffn_body, alpha=alpha),
        grid=(t // tm,),
        in_specs=[pl.BlockSpec((tm, d), lambda i: (i, 0)),
                  pl.BlockSpec(s13.shape, lambda i: (0, 0), pipeline_mode=pl.Buffered(1)),
                  pl.BlockSpec(s2.shape, lambda i: (0, 0), pipeline_mode=pl.Buffered(1))],
        out_specs=pl.BlockSpec((tm, d), lambda i: (i, 0)),
        out_shape=jax.ShapeDtypeStruct((t, d), F32),
        compiler_params=_cparams(("arbitrary",)),
        name="shared_ffn",
    )(x1f, s13, s2)


SC_CORES = 2
SC_SUBCORES = 16
SC_CHUNK = 32


def _sc_dispatch(x1t3, dest3, n_rows):
    t, n_lt, _ = x1t3.shape
    n_workers = SC_CORES * SC_SUBCORES
    assert t % (n_workers * SC_CHUNK) == 0
    chunks_per_worker = t // (n_workers * SC_CHUNK)
    mesh = plsc.VectorSubcoreMesh(core_axis_name="c", subcore_axis_name="s",
                                  num_cores=SC_CORES, num_subcores=SC_SUBCORES)

    @functools.partial(
        pl.kernel, mesh=mesh,
        out_type=jax.ShapeDtypeStruct((n_rows, n_lt, LANES), x1t3.dtype),
        scratch_types=[pltpu.VMEM((TOP_K, SC_CHUNK), I32),
                       pltpu.VMEM((SC_CHUNK, n_lt, LANES), x1t3.dtype),
                       pltpu.SemaphoreType.DMA],
        name="sc_dispatch")
    def dispatch(x_hbm, dest_hbm, out_hbm, idx_v, rows_v, sem):
        wid = lax.axis_index("s") * SC_CORES + lax.axis_index("c")

        @pl.loop(0, chunks_per_worker)
        def _(j):
            chunk = wid * chunks_per_worker + j
            base = pl.multiple_of(chunk * SC_CHUNK, SC_CHUNK)
            pltpu.sync_copy(x_hbm.at[pl.ds(base, SC_CHUNK)], rows_v)
            pltpu.sync_copy(dest_hbm.at[chunk], idx_v)
            copies = [pltpu.make_async_copy(rows_v, out_hbm.at[idx_v.at[k]], sem) for k in range(TOP_K)]
            for cp in copies:
                cp.start()
            for cp in copies:
                cp.wait()

    return dispatch(x1t3, dest3)


SC_RING = 3


def _sc_gather(yb3, dest3, t):
    _, n_lt, _ = yb3.shape
    n_workers = SC_CORES * SC_SUBCORES
    chunks_per_worker = t // (n_workers * SC_CHUNK)
    mesh = plsc.VectorSubcoreMesh(core_axis_name="c", subcore_axis_name="s",
                                  num_cores=SC_CORES, num_subcores=SC_SUBCORES)

    @functools.partial(
        pl.kernel, mesh=mesh,
        out_type=jax.ShapeDtypeStruct((TOP_K * t, n_lt, LANES), yb3.dtype),
        scratch_types=[pltpu.VMEM((TOP_K, SC_CHUNK), I32),
                       pltpu.VMEM((SC_RING, SC_CHUNK, n_lt, LANES), yb3.dtype)]
                      + [pltpu.SemaphoreType.DMA] * (2 * SC_RING),
        name="sc_gather")
    def gather(y_hbm, dest_hbm, out_hbm, idx_v, rows_v, *sems):
        gsem, wsem = sems[:SC_RING], sems[SC_RING:]
        wid = lax.axis_index("s") * SC_CORES + lax.axis_index("c")

        @pl.loop(0, chunks_per_worker)
        def _(j):
            chunk = wid * chunks_per_worker + j
            base = pl.multiple_of(chunk * SC_CHUNK, SC_CHUNK)
            pltpu.sync_copy(dest_hbm.at[chunk], idx_v)
            gets = [pltpu.make_async_copy(y_hbm.at[idx_v.at[k]], rows_v.at[k % SC_RING], gsem[k % SC_RING])
                    for k in range(TOP_K)]
            puts = [pltpu.make_async_copy(rows_v.at[k % SC_RING], out_hbm.at[pl.ds(k * t + base, SC_CHUNK)],
                                          wsem[k % SC_RING]) for k in range(TOP_K)]
            put_waited = set()
            for k in range(SC_RING - 1):
                gets[k].start()
            for k in range(TOP_K):
                gets[k].wait()
                puts[k].start()
                nxt = k + SC_RING - 1
                if nxt < TOP_K:
                    if k >= 1:
                        puts[k - 1].wait()
                        put_waited.add(k - 1)
                    gets[nxt].start()
            for k in range(TOP_K):
                if k not in put_waited:
                    puts[k].wait()

    return gather(yb3, dest3)


def _experts_body(be_ref, nact_ref, first_ref, nxte_ref, wslot_ref, valid_ref,
                  xs_ref, w1_hbm, w3_hbm, w2_hbm, o_ref,
                  wst1, wst3, wst2, w13b, w2b, sem_w):
    b = pl.program_id(0)
    nact = nact_ref[0]
    ed = wst2.shape[1]
    d = wst2.shape[2]
    n_pk = d // (2 * LANES)
    bm = xs_ref.shape[0] // n_pk
    active = b < nact

    def weight_copies(e, s):
        return (pltpu.make_async_copy(w1_hbm.at[e], wst1.at[s], sem_w.at[s]),
                pltpu.make_async_copy(w3_hbm.at[e], wst3.at[s], sem_w.at[s]),
                pltpu.make_async_copy(w2_hbm.at[e], wst2.at[s], sem_w.at[s]))

    @pl.when(b == 0)
    def _():
        for cp in weight_copies(be_ref[0], wslot_ref[0]):
            cp.start()

    @pl.when(first_ref[b] == 1)
    def _():
        ws = wslot_ref[b]
        for cp in weight_copies(be_ref[b], ws):
            cp.wait()

        @pl.when(nxte_ref[b] >= 0)
        def _():
            for cp in weight_copies(nxte_ref[b], 1 - ws):
                cp.start(priority=1)

        for c in range(ed // MXU_N):
            w13b[:, 2 * c * MXU_N:(2 * c + 1) * MXU_N] = wst1[ws, :, c * MXU_N:(c + 1) * MXU_N].astype(BF16)
            w13b[:, (2 * c + 1) * MXU_N:(2 * c + 2) * MXU_N] = wst3[ws, :, c * MXU_N:(c + 1) * MXU_N].astype(BF16)
        w2b[...] = wst2[ws].astype(BF16)

    @pl.when(active)
    def _():
        keep = lax.broadcasted_iota(I32, (bm, LANES), 0) < valid_ref[b]
        los, his = [], []
        for c in range(n_pk):
            lo, hi = _unpack_bf16_pair(jnp.where(keep, xs_ref[pl.ds(c, bm, stride=n_pk), :], 0))
            los.append(lo.astype(BF16))
            his.append(hi.astype(BF16))
        xb = jnp.concatenate(los + his, axis=1)
        parts = []
        for c in range(ed // MXU_N):
            hc = jnp.dot(xb, w13b[:, 2 * c * MXU_N:(2 * c + 2) * MXU_N], preferred_element_type=F32)
            parts.append((jax.nn.silu(hc[:, :MXU_N]) * hc[:, MXU_N:]).astype(BF16))
        hdn = jnp.concatenate(parts, axis=1)
        n_down = d // MXU_N
        ys = [jnp.dot(hdn, w2b[:, c * MXU_N:(c + 1) * MXU_N], preferred_element_type=F32) for c in range(n_down)]
        tiles_per_chunk = MXU_N // LANES
        for c in range(n_down // 2):
            word = _pack_bf16_pair(ys[c], ys[c + n_down // 2])
            for h in range(tiles_per_chunk):
                o_ref[pl.ds(c * tiles_per_chunk + h, bm, stride=n_pk), :] = word[:, h * LANES:(h + 1) * LANES]

    @pl.when(jnp.logical_not(active))
    def _():
        o_ref[...] = jnp.zeros_like(o_ref)


def _experts(xs2d, block_e, nact, first, nxt_e, wslot, valid, w1, w3, w2, bm):
    d, ed = w1.shape[1], w1.shape[2]
    n_pk = d // (2 * LANES)
    n_blk = block_e.shape[0]
    assert xs2d.shape == (n_blk * bm * n_pk, LANES)
    assert ed % MXU_N == 0 and d % MXU_N == 0 and bm % SUBLANES == 0
    grid_spec = pltpu.PrefetchScalarGridSpec(
        num_scalar_prefetch=6,
        grid=(n_blk,),
        in_specs=[pl.BlockSpec((bm * n_pk, LANES),
                               lambda b, be, na, *_: (jnp.maximum(jnp.minimum(b, na[0] - 1), 0), 0)),
                  pl.BlockSpec(memory_space=pl.ANY),
                  pl.BlockSpec(memory_space=pl.ANY),
                  pl.BlockSpec(memory_space=pl.ANY)],
        out_specs=pl.BlockSpec((bm * n_pk, LANES), lambda b, *_: (b, 0)),
        scratch_shapes=[pltpu.VMEM((2, d, ed), F32),
                        pltpu.VMEM((2, d, ed), F32),
                        pltpu.VMEM((2, ed, d), F32),
                        pltpu.VMEM((d, 2 * ed), BF16),
                        pltpu.VMEM((ed, d), BF16),
                        pltpu.SemaphoreType.DMA((2,))],
    )
    return pl.pallas_call(
        _experts_body,
        grid_spec=grid_spec,
        out_shape=jax.ShapeDtypeStruct((n_blk * bm * n_pk, LANES), I32),
        compiler_params=_cparams(("arbitrary",)),
        name="experts",
    )(block_e, nact, first, nxt_e, wslot, valid, xs2d, w1, w3, w2)


def _finalize_body(base_ref, gate_ref, g2_ref, b2_ref, *refs, n_a):
    slabs, (oa_ref, os_ref) = refs[:TOP_K], refs[TOP_K:]
    i = pl.program_id(0)
    tm, d = base_ref.shape
    half = d // 2
    n_pk = half // LANES
    acc_lo = [base_ref[:, c * LANES:(c + 1) * LANES] for c in range(n_pk)]
    acc_hi = [base_ref[:, half + c * LANES:half + (c + 1) * LANES] for c in range(n_pk)]
    for k in range(TOP_K):
        g = gate_ref[:, k:k + 1]
        for c in range(n_pk):
            lo, hi = _unpack_bf16_pair(slabs[k][pl.ds(c, tm, stride=n_pk), :])
            acc_lo[c] = acc_lo[c] + g * lo
            acc_hi[c] = acc_hi[c] + g * hi
    y = _layer_norm(jnp.concatenate(acc_lo + acc_hi, axis=1), g2_ref[...], b2_ref[...])

    @pl.when(i < n_a)
    def _():
        oa_ref[...] = y

    @pl.when(i >= n_a)
    def _():
        os_ref[...] = y


def _finalize(base, gate_t, slabs2d, ln_g, ln_b, t_a):
    t, d = base.shape
    tm = FINAL_TM
    n_pk = d // (2 * LANES)
    assert t_a % tm == 0 and 0 < t_a < t and slabs2d.shape == (TOP_K * t * n_pk, LANES)
    n_a = t_a // tm
    n_steps = t // tm
    map_a, map_s = _two_array_maps(n_a)
    slab_specs = [pl.BlockSpec((tm * n_pk, LANES), (lambda k: (lambda i: (k * n_steps + i, 0)))(k))
                  for k in range(TOP_K)]
    return pl.pallas_call(
        functools.partial(_finalize_body, n_a=n_a),
        grid=(n_steps,),
        in_specs=[pl.BlockSpec((tm, d), lambda i: (i, 0)),
                  pl.BlockSpec((tm, TOP_K), lambda i: (i, 0)),
                  pl.BlockSpec((1, d), lambda i: (0, 0)),
                  pl.BlockSpec((1, d), lambda i: (0, 0))] + slab_specs,
        out_specs=(pl.BlockSpec((tm, d), map_a), pl.BlockSpec((tm, d), map_s)),
        out_shape=(jax.ShapeDtypeStruct((t_a, d), F32), jax.ShapeDtypeStruct((t - t_a, d), F32)),
        compiler_params=_cparams(("arbitrary",)),
        name="finalize",
    )(base, gate_t, ln_g, ln_b, *([slabs2d] * TOP_K))


def _split_hi_lo(w):
    hi = w.astype(BF16)
    lo = (w - hi.astype(F32)).astype(BF16)
    return hi, lo


def _encoder_layer(xa, xs, seqs, lp, alpha):
    (w_in, pool_w, pool_b, pool_scale, w_pool_br, w_attn_br, attn_rpb, w_out, ln1_g, ln1_b,
     router_w, router_bias, exp_w1, exp_w3, exp_w2, sh_w1, sh_w3, sh_w2, ln2_g, ln2_b) = lp
    d = xa.shape[1]
    t = xa.shape[0] + xs.shape[0]
    pw = w_pool_br.shape[0]
    aw = w_attn_br.shape[0]
    n_heads = attn_rpb.shape[0]
    head_dim = aw // n_heads
    n_experts = router_w.shape[1]
    assert pw == INPROJ_TN and aw == INPROJ_TN and d % INPROJ_TN == 0
    u_block, q_block = 0, pw // INPROJ_TN
    gate_block0 = (pw + 3 * aw) // INPROJ_TN

    h = _inproj(xa, xs, w_in.astype(BF16), q_block=q_block, gate_block0=gate_block0, q_scale=head_dim ** -0.5)

    blk = ATT_ROWS * GRID_W
    images, seq_lo, seq_hi = [], [], []
    for s0, s1 in seqs:
        images.append((s0 // blk, (s1 - s0) // blk))
        for _ in range((s1 - s0) // MIX_TM):
            seq_lo.append(s0)
            seq_hi.append(s1)
    bias_tab = _attn_bias_table(attn_rpb, 2 * LANES // head_dim)
    attn = _attention(h, bias_tab, images, q_block=q_block, n_heads=n_heads, head_dim=head_dim)

    rwh, rwl = _split_hi_lo(router_w.T)
    p = dict(pool_w=pool_w.astype(BF16), pool_b=pool_b.reshape(1, pw), pool_scale=pool_scale.reshape(1, pw),
             wpb=w_pool_br.astype(BF16), wab=w_attn_br.astype(BF16), wo=w_out.astype(BF16),
             ln1_g=ln1_g.reshape(1, d), ln1_b=ln1_b.reshape(1, d),
             rwh=rwh, rwl=rwl, rb=router_bias.reshape(n_experts, 1).astype(F32))
    x1, x1f, idx_t, gate_t, rank_t, counts = _mixer(
        xa, xs, h, attn, jnp.asarray(seq_lo, I32), jnp.asarray(seq_hi, I32), p,
        alpha=alpha, u_block=u_block, gate_block0=gate_block0)

    bm = EXPERT_BM
    cnt = counts[:, 0].astype(I32)
    pcnt = (cnt + bm - 1) // bm * bm
    pend = jnp.cumsum(pcnt)
    pstart = pend - pcnt
    eids = jnp.arange(n_experts, dtype=I32)[:, None, None]
    dest = jnp.sum(jnp.where(idx_t[None] == eids, pstart[:, None, None], 0), axis=0) + rank_t
    n_blk = (t * TOP_K + n_experts * (bm - 1)) // bm
    blk_row = jnp.arange(n_blk, dtype=I32) * bm
    block_e = jnp.minimum(jnp.sum((pend[None, :] <= blk_row[:, None]).astype(I32), axis=1), n_experts - 1)
    nact = (pend[-1] // bm).astype(I32).reshape(1)
    block_e = jnp.where(jnp.arange(n_blk) < nact[0], block_e, block_e[jnp.maximum(nact[0] - 1, 0)])
    eids1 = jnp.arange(n_experts, dtype=I32)
    present = cnt > 0
    later = present[None, :] & (eids1[None, :] > eids1[:, None])
    nxt_of_e = jnp.min(jnp.where(later, eids1[None, :], n_experts), axis=1)
    nxt_of_e = jnp.where(nxt_of_e >= n_experts, -1, nxt_of_e).astype(I32)
    ord_of_e = (jnp.cumsum(present.astype(I32)) - 1).astype(I32)
    onehot_be = (block_e[:, None] == eids1[None, :]).astype(I32)
    blk_ids = jnp.arange(n_blk, dtype=I32)
    is_act = blk_ids < nact[0]
    prev_e = jnp.concatenate([jnp.full((1,), -1, I32), block_e[:-1]])
    first = (is_act & (block_e != prev_e)).astype(I32)
    nxt_e = jnp.sum(onehot_be * nxt_of_e[None, :], axis=1).astype(I32)
    wslot = (jnp.sum(onehot_be * ord_of_e[None, :], axis=1) % 2).astype(I32)
    start_of_e = jnp.sum(onehot_be * pstart[None, :], axis=1)
    cnt_of_e = jnp.sum(onehot_be * cnt[None, :], axis=1)
    valid = jnp.where(is_act, jnp.clip(cnt_of_e - (blk_row - start_of_e), 0, bm), 0).astype(I32)
    n_pk = d // (2 * LANES)
    dest3 = jnp.transpose(dest.reshape(TOP_K, t // SC_CHUNK, SC_CHUNK), (1, 0, 2))
    xs = _sc_dispatch(x1.reshape(t, n_pk, LANES), dest3, n_blk * bm)
    base = _shared_ffn(x1f, jnp.concatenate([sh_w1, sh_w3], axis=1).astype(BF16), sh_w2.astype(BF16), alpha=alpha)
    yb = _experts(xs.reshape(n_blk * bm * n_pk, LANES), block_e, nact, first, nxt_e, wslot, valid,
                  exp_w1, exp_w3, exp_w2, bm)

    slabs = _sc_gather(yb.reshape(n_blk * bm, n_pk, LANES), dest3, t)
    return _finalize(base, gate_t.T, slabs.reshape(TOP_K * t * n_pk, LANES),
                     ln2_g.reshape(1, d), ln2_b.reshape(1, d), xa.shape[0])


def kernel(x_prompt, x_sample, w_in, pool_w, pool_b, pool_scale, w_pool_br, w_attn_br, attn_rpb, w_out,
           ln1_g, ln1_b, router_w, router_bias, exp_w1, exp_w3, exp_w2, sh_w1, sh_w3, sh_w2, ln2_g, ln2_b):
    depth = w_in.shape[0]
    alpha = (2.0 * depth) ** 0.25
    bp, sp, d = x_prompt.shape
    bs, ss, _ = x_sample.shape
    seqs = [(i * sp, (i + 1) * sp) for i in range(bp)]
    seqs += [(bp * sp + i * ss, bp * sp + (i + 1) * ss) for i in range(bs)]
    xa, xs = x_prompt.reshape(bp * sp, d), x_sample.reshape(bs * ss, d)
    for l in range(depth):
        lp = (w_in[l], pool_w[l], pool_b[l], pool_scale[l], w_pool_br[l], w_attn_br[l], attn_rpb[l], w_out[l],
              ln1_g[l], ln1_b[l], router_w[l], router_bias[l], exp_w1[l], exp_w3[l], exp_w2[l],
              sh_w1[l], sh_w3[l], sh_w2[l], ln2_g[l], ln2_b[l])
        xa, xs = _encoder_layer(xa, xs, seqs, lp, alpha)
    return (xa.reshape(bp, sp, d), xs.reshape(bs, ss, d))
```

```python
import functools

import numpy as np
import jax
import jax.numpy as jnp
from jax import lax
from jax.experimental import pallas as pl
from jax.experimental.pallas import tpu as pltpu
from jax.experimental.pallas import tpu_sc as plsc

F32 = jnp.float32
BF16 = jnp.bfloat16
I32 = jnp.int32

GRID_W = 64
POOL_WINDOWS = (2, 4, 8, 16)
WIN_R = 8
WIN_C = 16
TOP_K = 8
N_GROUPS = 8
TOPK_GROUPS = 4
ROUTED_SCALE = 2.5
LN_EPS = 1e-5
NEG_BIG = -1e30

LANES = 128
SUBLANES = 8
MXU_N = 256
VMEM_LIMIT = 56 * 1024 * 1024

INPROJ_TM = 1024
INPROJ_TN = 1024
ATT_ROWS = 8
MIX_TM = 256
POOL_HALO = 16
EXPERT_BM = 512
FINAL_TM = 256
SHARED_TM = 512


def _cparams(sem):
    return pltpu.CompilerParams(dimension_semantics=sem, vmem_limit_bytes=VMEM_LIMIT)


def _inproj_body(xa_ref, xs_ref, w_ref, o_ref, xb_ref, *, n_a, q_block, gate_block0, q_scale):
    i = pl.program_id(0)
    j = pl.program_id(1)

    @pl.when((j == 0) & (i < n_a))
    def _():
        xb_ref[...] = xa_ref[...].astype(BF16)

    @pl.when((j == 0) & (i >= n_a))
    def _():
        xb_ref[...] = xs_ref[...].astype(BF16)

    is_gate = j >= gate_block0
    scale = jnp.where(j == q_block, q_scale, 1.0).astype(F32)
    for c in range(w_ref.shape[1] // MXU_N):
        acc = jnp.dot(xb_ref[...], w_ref[:, c * MXU_N:(c + 1) * MXU_N], preferred_element_type=F32)
        out = jnp.where(is_gate, jax.nn.sigmoid(acc), acc * scale)
        o_ref[:, c * MXU_N:(c + 1) * MXU_N] = out.astype(BF16)


def _two_array_maps(n_a):
    return (lambda i, *_: (jnp.minimum(i, n_a - 1), 0)), (lambda i, *_: (jnp.maximum(i - n_a, 0), 0))


def _inproj(xa, xs, w_bf16, *, q_block, gate_block0, q_scale):
    d = xa.shape[1]
    t = xa.shape[0] + xs.shape[0]
    n = w_bf16.shape[1]
    tm, tn = INPROJ_TM, INPROJ_TN
    assert xa.shape[0] % tm == 0 and xs.shape[0] % tm == 0 and xa.shape[0] > 0 and xs.shape[0] > 0
    n_a = xa.shape[0] // tm
    map_a, map_s = _two_array_maps(n_a)
    return pl.pallas_call(
        functools.partial(_inproj_body, n_a=n_a, q_block=q_block, gate_block0=gate_block0, q_scale=q_scale),
        grid=(t // tm, n // tn),
        in_specs=[pl.BlockSpec((tm, d), map_a),
                  pl.BlockSpec((tm, d), map_s),
                  pl.BlockSpec((d, tn), lambda i, j: (0, j))],
        out_specs=pl.BlockSpec((tm, tn), lambda i, j: (i, j)),
        out_shape=jax.ShapeDtypeStruct((t, n), BF16),
        scratch_shapes=[pltpu.VMEM((tm, d), BF16)],
        compiler_params=_cparams(("arbitrary", "arbitrary")),
        name="inproj",
    )(xa, xs, w_bf16)


def _attn_body(prev_ref, next_ref, kst_ref, off_ref,
               q_ref, kp_ref, kc_ref, kn_ref, vp_ref, vc_ref, vn_ref, bias_ref, o_ref,
               kbuf, vbuf, *, heads_per_group, head_dim):
    del prev_ref, next_ref
    b = pl.program_id(0)
    blk = ATT_ROWS * GRID_W
    kbuf[0:blk] = kp_ref[...]
    kbuf[blk:2 * blk] = kc_ref[...]
    kbuf[2 * blk:3 * blk] = kn_ref[...]
    vbuf[0:blk] = vp_ref[...]
    vbuf[blk:2 * blk] = vc_ref[...]
    vbuf[2 * blk:3 * blk] = vn_ref[...]

    gw = heads_per_group * head_dim
    n_groups = q_ref.shape[1] // gw
    win = WIN_R * GRID_W
    nq = heads_per_group * GRID_W
    row_head = lax.broadcasted_iota(I32, (nq, gw), 0) // GRID_W
    lane_head = lax.broadcasted_iota(I32, (nq, gw), 1) // head_dim
    head_mask = row_head == lane_head
    out_lane_head = lax.broadcasted_iota(I32, (GRID_W, gw), 1) // head_dim

    def row_step(i, carry):
        st = pl.multiple_of(kst_ref[b * ATT_ROWS + i] * GRID_W, GRID_W)
        dr0 = (WIN_R - 1) - off_ref[b * ATT_ROWS + i]
        q0 = pl.multiple_of(i * GRID_W, GRID_W)
        for g in range(n_groups):
            qg = q_ref[pl.ds(q0, GRID_W), g * gw:(g + 1) * gw]
            kg = kbuf[pl.ds(st, win), g * gw:(g + 1) * gw]
            vg = vbuf[pl.ds(st, win), g * gw:(g + 1) * gw]
            wt = jnp.concatenate([qg] * heads_per_group, axis=0)
            wt = jnp.where(head_mask, wt, jnp.zeros_like(wt))
            s = lax.dot_general(wt, kg, (((1,), (1,)), ((), ())),
                                preferred_element_type=F32)
            n_chunk = win // LANES
            sj = [s[:, j * LANES:(j + 1) * LANES] + bias_ref[dr0 + 2 * j, g] for j in range(n_chunk)]
            m = sj[0]
            for j in range(1, n_chunk):
                m = jnp.maximum(m, sj[j])
            m = jnp.max(m, axis=1, keepdims=True)
            ej = [jnp.exp(x - m) for x in sj]
            l = ej[0]
            for j in range(1, n_chunk):
                l = l + ej[j]
            inv = 1.0 / jnp.sum(l, axis=1, keepdims=True)
            p = jnp.concatenate([x.astype(BF16) for x in ej], axis=1)
            of = jnp.dot(p, vg, preferred_element_type=F32) * inv
            out = jnp.zeros((GRID_W, gw), F32)
            for h in range(heads_per_group):
                out = out + jnp.where(out_lane_head == h, of[h * GRID_W:(h + 1) * GRID_W, :], 0.0)
            o_ref[pl.ds(q0, GRID_W), g * gw:(g + 1) * gw] = out.astype(o_ref.dtype)
        return carry

    lax.fori_loop(0, ATT_ROWS, row_step, 0, unroll=4)


def _attn_tables(images):
    prev, nxt, kst, off = [], [], [], []
    for s, n in images:
        rows = n * ATT_ROWS
        kr = min(WIN_R, rows)
        assert kr == WIN_R
        for bl in range(n):
            b = s + bl
            prev.append(max(b - 1, s))
            nxt.append(min(b + 1, s + n - 1))
            for i in range(ATT_ROWS):
                r = bl * ATT_ROWS + i
                rs = min(max(r - kr // 2, 0), rows - kr)
                kst.append(rs - bl * ATT_ROWS + ATT_ROWS)
                off.append(r - rs)
    return (np.asarray(prev, np.int32), np.asarray(nxt, np.int32),
            np.asarray(kst, np.int32), np.asarray(off, np.int32))


def _attn_bias_table(rpb, heads_per_group):
    n_heads, n_dr, n_dc = rpb.shape
    n_pair = n_dr - 1
    c = np.arange(GRID_W)
    cs = np.clip(c - WIN_C // 2, 0, GRID_W - WIN_C)
    kc = np.arange(GRID_W)
    valid = (kc[None, :] >= cs[:, None]) & (kc[None, :] < cs[:, None] + WIN_C)
    dc = np.clip(kc[None, :] - c[:, None] + (WIN_C - 1), 0, n_dc - 1)
    sel = np.zeros((2, n_dc, GRID_W, 2, GRID_W), np.float32)
    ci, ki = np.nonzero(valid)
    for p in range(2):
        sel[p, dc[ci, ki], ci, p, ki] = 1.0
    sel = jnp.asarray(sel.reshape(2 * n_dc, GRID_W * 2 * GRID_W))
    r = rpb.astype(F32)
    pair = jnp.stack([r[:, :n_pair], r[:, 1:]], axis=2)
    pair = jnp.transpose(pair, (1, 0, 2, 3)).reshape(n_pair * n_heads, 2 * n_dc)
    tab = jnp.dot(pair, sel, precision=lax.Precision.HIGHEST)
    tab = tab.reshape(n_pair, n_heads, GRID_W, 2 * GRID_W)
    valid2 = jnp.asarray(np.tile(valid[:, None, :], (1, 2, 1)).reshape(GRID_W, 2 * GRID_W))
    tab = jnp.where(valid2[None, None], tab, NEG_BIG)
    return tab.reshape(n_pair, n_heads // heads_per_group, heads_per_group * GRID_W, 2 * GRID_W)


def _attention(h, bias_tab, images, *, q_block, n_heads, head_dim):
    t = h.shape[0]
    blk = ATT_ROWS * GRID_W
    aw = n_heads * head_dim
    heads_per_group = 2 * LANES // head_dim
    prev, nxt, kst, off = _attn_tables(images)
    n_blocks = t // blk
    assert prev.shape[0] == n_blocks
    kcol, vcol = q_block + 1, q_block + 2
    spec = lambda col, which: pl.BlockSpec(
        (blk, aw), {"cur": lambda b, p, n, k, o: (b, col),
                    "prev": lambda b, p, n, k, o: (p[b], col),
                    "next": lambda b, p, n, k, o: (n[b], col)}[which])
    grid_spec = pltpu.PrefetchScalarGridSpec(
        num_scalar_prefetch=4,
        grid=(n_blocks,),
        in_specs=[spec(q_block, "cur"),
                  spec(kcol, "prev"), spec(kcol, "cur"), spec(kcol, "next"),
                  spec(vcol, "prev"), spec(vcol, "cur"), spec(vcol, "next"),
                  pl.BlockSpec(bias_tab.shape, lambda b, p, n, k, o: (0, 0, 0, 0),
                               pipeline_mode=pl.Buffered(1))],
        out_specs=pl.BlockSpec((blk, aw), lambda b, p, n, k, o: (b, 0)),
        scratch_shapes=[pltpu.VMEM((3 * blk, aw), BF16), pltpu.VMEM((3 * blk, aw), BF16)],
    )
    return pl.pallas_call(
        functools.partial(_attn_body, heads_per_group=heads_per_group, head_dim=head_dim),
        grid_spec=grid_spec,
        out_shape=jax.ShapeDtypeStruct((t, aw), BF16),
        compiler_params=_cparams(("arbitrary",)),
        name="natten",
    )(jnp.asarray(prev), jnp.asarray(nxt), jnp.asarray(kst), jnp.asarray(off),
      h, h, h, h, h, h, h, bias_tab)


def _layer_norm(r, g, b):
    mu = jnp.mean(r, axis=-1, keepdims=True)
    c = r - mu
    var = jnp.mean(c * c, axis=-1, keepdims=True)
    return c * lax.rsqrt(var + LN_EPS) * g + b


HI_MASK = -65536


def _pack_bf16_pair(lo, hi):
    lo_bits = lax.bitcast_convert_type(lo.astype(BF16).astype(F32), I32)
    hi_bits = lax.bitcast_convert_type(hi.astype(BF16).astype(F32), I32)
    return (hi_bits & HI_MASK) | lax.shift_right_logical(lo_bits, 16)


def _unpack_bf16_pair(word):
    lo = lax.bitcast_convert_type(lax.shift_left(word, 16), F32)
    hi = lax.bitcast_convert_type(word & HI_MASK, F32)
    return lo, hi


def _first_index_of(mask, idx, sentinel, axis):
    return jnp.min(jnp.where(mask, idx, sentinel), axis=axis, keepdims=True)


def _mixer_body(seq_lo_ref, seq_hi_ref,
                xa_ref, xs_ref, up_ref, uc_ref, un_ref, gp_ref, ga_ref, at_ref,
                pw_ref, pb_ref, ps_ref, wpb_ref, wab_ref, wo_ref, g1_ref, b1_ref,
                rwh_ref, rwl_ref, rb_ref,
                x1_ref, x1f_ref, idx_ref, gate_ref, rank_ref, cnt_ref,
                carry_ref, *, n_a, alpha, n_experts):
    i = pl.program_id(0)
    tm = xa_ref.shape[0]
    t0 = i * tm
    lo_seq = seq_lo_ref[i]
    hi_seq = seq_hi_ref[i]

    @pl.when(i == 0)
    def _():
        carry_ref[...] = jnp.zeros_like(carry_ref)

    u_ext = jnp.concatenate([up_ref[...], uc_ref[...], un_ref[...]], axis=0)
    ext = tm + 2 * POOL_HALO
    tok_r = t0 + lax.broadcasted_iota(I32, (tm, ext), 0)
    tok_c = t0 - POOL_HALO + lax.broadcasted_iota(I32, (tm, ext), 1)
    tok_q = t0 + lax.broadcasted_iota(I32, (tm, 1), 0)
    n_pg = pw_ref.shape[0]
    pg = pw_ref.shape[1]
    parts = []
    for g in range(n_pg):
        w = POOL_WINDOWS[g]
        lo = jnp.maximum(tok_r - w // 2, lo_seq)
        hi = jnp.minimum(tok_r - w // 2 + w, hi_seq)
        band = jnp.where((tok_c >= lo) & (tok_c < hi), 1.0, 0.0).astype(BF16)
        cnt = (jnp.minimum(tok_q - w // 2 + w, hi_seq) - jnp.maximum(tok_q - w // 2, lo_seq)).astype(F32)
        ug = u_ext[:, g * pg:(g + 1) * pg]
        sums = jnp.dot(band, ug, preferred_element_type=F32)
        z = sums / cnt - uc_ref[:, g * pg:(g + 1) * pg].astype(F32)
        y = jnp.dot(z.astype(BF16), pw_ref[g], preferred_element_type=F32)
        y = (y + pb_ref[:, g * pg:(g + 1) * pg]) * ps_ref[:, g * pg:(g + 1) * pg]
        parts.append(y.astype(BF16))
    a = jnp.concatenate(parts, axis=1)
    a_pool = jnp.dot(a, wpb_ref[...], preferred_element_type=F32)
    a_attn = jnp.dot(at_ref[...], wab_ref[...], preferred_element_type=F32)
    merged = gp_ref[...].astype(F32) * a_pool + ga_ref[...].astype(F32) * a_attn
    mix = jnp.dot(merged.astype(BF16), wo_ref[...], preferred_element_type=F32)
    x_in = jnp.where(i < n_a, xa_ref[...], xs_ref[...])
    x1 = _layer_norm(alpha * x_in + mix, g1_ref[...], b1_ref[...])
    half = x1.shape[1] // 2
    n_pk = half // LANES
    word = _pack_bf16_pair(x1[:, :half], x1[:, half:])
    for c in range(n_pk):
        x1_ref[pl.ds(c, tm, stride=n_pk), :] = word[:, c * LANES:(c + 1) * LANES]
    x1f_ref[...] = x1
    x1b = x1.astype(BF16)

    x1l = (x1 - x1b.astype(F32)).astype(BF16)
    nt = (((1,), (1,)), ((), ()))
    logits = (lax.dot_general(rwh_ref[...], x1b, nt, preferred_element_type=F32)
              + lax.dot_general(rwh_ref[...], x1l, nt, preferred_element_type=F32)
              + lax.dot_general(rwl_ref[...], x1b, nt, preferred_element_type=F32))
    scores = jax.nn.sigmoid(logits)
    biased = scores + rb_ref[...]
    gsz = n_experts // N_GROUPS
    neg_inf = -jnp.inf

    mem = lax.broadcasted_iota(I32, (gsz, tm), 0)
    gs_rows = []
    for g in range(N_GROUPS):
        bg = biased[g * gsz:(g + 1) * gsz, :]
        m1 = jnp.max(bg, axis=0, keepdims=True)
        first = _first_index_of(bg == m1, mem, gsz, 0)
        m2 = jnp.max(jnp.where(mem == first, neg_inf, bg), axis=0, keepdims=True)
        gs_rows.append(m1 + m2)
    gwork = jnp.concatenate(gs_rows, axis=0)

    gidx = lax.broadcasted_iota(I32, (N_GROUPS, tm), 0)
    egrp = lax.broadcasted_iota(I32, (n_experts, tm), 0) // gsz
    eidx = lax.broadcasted_iota(I32, (n_experts, tm), 0)
    work = jnp.full((n_experts, tm), neg_inf, F32)
    for _ in range(TOPK_GROUPS):
        gm = jnp.max(gwork, axis=0, keepdims=True)
        gf = _first_index_of(gwork == gm, gidx, N_GROUPS, 0)
        gwork = jnp.where(gidx == gf, neg_inf, gwork)
        work = jnp.where(egrp == gf, biased, work)

    hot = jnp.zeros((n_experts, tm), F32)
    sel_idx, sel_w, sel_hit = [], [], []
    for _ in range(TOP_K):
        em = jnp.max(work, axis=0, keepdims=True)
        ef = _first_index_of(work == em, eidx, n_experts, 0)
        hit = eidx == ef
        sel_idx.append(ef)
        sel_w.append(jnp.sum(jnp.where(hit, scores, 0.0), axis=0, keepdims=True))
        sel_hit.append(hit)
        hot = jnp.where(hit, 1.0, hot)
        work = jnp.where(hit, neg_inf, work)
    wsel = jnp.concatenate(sel_w, axis=0)
    gate_ref[...] = wsel / jnp.sum(wsel, axis=0, keepdims=True) * ROUTED_SCALE
    idx_ref[...] = jnp.concatenate(sel_idx, axis=0)

    tri = jnp.where(lax.broadcasted_iota(I32, (tm, tm), 0) < lax.broadcasted_iota(I32, (tm, tm), 1),
                    1.0, 0.0).astype(BF16)
    before = carry_ref[:, 0:1] + jnp.dot(hot.astype(BF16), tri, preferred_element_type=F32)
    rank = [jnp.sum(jnp.where(h, before, 0.0), axis=0, keepdims=True) for h in sel_hit]
    rank_ref[...] = jnp.concatenate(rank, axis=0).astype(I32)
    carry_ref[...] = carry_ref[...] + jnp.sum(hot, axis=1, keepdims=True)
    cnt_ref[...] = carry_ref[...]


def _mixer(xa, xs, h, attn, seq_lo, seq_hi, p, *, alpha, u_block, gate_block0):
    d = xa.shape[1]
    t = xa.shape[0] + xs.shape[0]
    tm = MIX_TM
    assert xa.shape[0] % tm == 0 and xs.shape[0] % tm == 0
    n_a = xa.shape[0] // tm
    map_a, map_s = _two_array_maps(n_a)
    pw = p["wpb"].shape[0]
    aw = attn.shape[1]
    n_experts = p["rwh"].shape[0]
    halo_per_tile = tm // POOL_HALO
    n_halo_blocks = t // POOL_HALO
    gp_col = gate_block0 * INPROJ_TN // d
    in_specs = [
        pl.BlockSpec((tm, d), map_a),
        pl.BlockSpec((tm, d), map_s),
        pl.BlockSpec((POOL_HALO, pw), lambda i, lo, hi: (jnp.maximum(i * halo_per_tile - 1, 0), u_block)),
        pl.BlockSpec((tm, pw), lambda i, lo, hi: (i, u_block)),
        pl.BlockSpec((POOL_HALO, pw),
                     lambda i, lo, hi: (jnp.minimum((i + 1) * halo_per_tile, n_halo_blocks - 1), u_block)),
        pl.BlockSpec((tm, d), lambda i, lo, hi: (i, gp_col)),
        pl.BlockSpec((tm, d), lambda i, lo, hi: (i, gp_col + 1)),
        pl.BlockSpec((tm, aw), lambda i, lo, hi: (i, 0)),
    ]
    weights = [p["pool_w"], p["pool_b"], p["pool_scale"], p["wpb"], p["wab"], p["wo"], p["ln1_g"], p["ln1_b"],
               p["rwh"], p["rwl"], p["rb"]]
    in_specs += [pl.BlockSpec(w.shape, (lambda nd: (lambda i, lo, hi: (0,) * nd))(w.ndim),
                              pipeline_mode=pl.Buffered(1)) for w in weights]
    n_pk = d // (2 * LANES)
    out_shape = (jax.ShapeDtypeStruct((t * n_pk, LANES), I32),
                 jax.ShapeDtypeStruct((t, d), F32),
                 jax.ShapeDtypeStruct((TOP_K, t), I32),
                 jax.ShapeDtypeStruct((TOP_K, t), F32),
                 jax.ShapeDtypeStruct((TOP_K, t), I32),
                 jax.ShapeDtypeStruct((n_experts, LANES), F32))
    out_specs = (pl.BlockSpec((tm * n_pk, LANES), lambda i, lo, hi: (i, 0)),
                 pl.BlockSpec((tm, d), lambda i, lo, hi: (i, 0)),
                 pl.BlockSpec((TOP_K, tm), lambda i, lo, hi: (0, i)),
                 pl.BlockSpec((TOP_K, tm), lambda i, lo, hi: (0, i)),
                 pl.BlockSpec((TOP_K, tm), lambda i, lo, hi: (0, i)),
                 pl.BlockSpec((n_experts, LANES), lambda i, lo, hi: (0, 0)))
    grid_spec = pltpu.PrefetchScalarGridSpec(
        num_scalar_prefetch=2, grid=(t // tm,), in_specs=in_specs, out_specs=out_specs,
        scratch_shapes=[pltpu.VMEM((n_experts, LANES), F32)])
    return pl.pallas_call(
        functools.partial(_mixer_body, n_a=n_a, alpha=alpha, n_experts=n_experts),
        grid_spec=grid_spec, out_shape=out_shape,
        compiler_params=_cparams(("arbitrary",)),
        name="mixer",
    )(seq_lo, seq_hi, xa, xs, h, h, h, h, h, attn, *weights)


def _shared_ffn_body(x_ref, s13_ref, s2_ref, o_ref, *, alpha):
    x = x_ref[...]
    sd = s2_ref.shape[0]
    hc = jnp.dot(x.astype(BF16), s13_ref[...], preferred_element_type=F32)
    hdn = jax.nn.silu(hc[:, :sd]) * hc[:, sd:]
    o_ref[...] = alpha * x + jnp.dot(hdn.astype(BF16), s2_ref[...], preferred_element_type=F32)


def _shared_ffn(x1f, s13, s2, *, alpha):
    t, d = x1f.shape
    tm = SHARED_TM
    assert t % tm == 0
    return pl.pallas_call(
        functools.partial(_shared_ffn_body, alpha=alpha),
        grid=(t // tm,),
        in_specs=[pl.BlockSpec((tm, d), lambda i: (i, 0)),
                  pl.BlockSpec(s13.shape, lambda i: (0, 0), pipeline_mode=pl.Buffered(1)),
                  pl.BlockSpec(s2.shape, lambda i: (0, 0), pipeline_mode=pl.Buffered(1))],
        out_specs=pl.BlockSpec((tm, d), lambda i: (i, 0)),
        out_shape=jax.ShapeDtypeStruct((t, d), F32),
        compiler_params=_cparams(("arbitrary",)),
        name="shared_ffn",
    )(x1f, s13, s2)


SC_CORES = 2
SC_SUBCORES = 16
SC_CHUNK = 32


def _sc_dispatch(x1t3, dest3, n_rows):
    t, n_lt, _ = x1t3.shape
    n_workers = SC_CORES * SC_SUBCORES
    assert t % (n_workers * SC_CHUNK) == 0
    chunks_per_worker = t // (n_workers * SC_CHUNK)
    mesh = plsc.VectorSubcoreMesh(core_axis_name="c", subcore_axis_name="s",
                                  num_cores=SC_CORES, num_subcores=SC_SUBCORES)

    @functools.partial(
        pl.kernel, mesh=mesh,
        out_type=jax.ShapeDtypeStruct((n_rows, n_lt, LANES), x1t3.dtype),
        scratch_types=[pltpu.VMEM((TOP_K, SC_CHUNK), I32),
                       pltpu.VMEM((SC_CHUNK, n_lt, LANES), x1t3.dtype),
                       pltpu.SemaphoreType.DMA],
        name="sc_dispatch")
    def dispatch(x_hbm, dest_hbm, out_hbm, idx_v, rows_v, sem):
        wid = lax.axis_index("s") * SC_CORES + lax.axis_index("c")

        @pl.loop(0, chunks_per_worker)
        def _(j):
            chunk = wid * chunks_per_worker + j
            base = pl.multiple_of(chunk * SC_CHUNK, SC_CHUNK)
            pltpu.sync_copy(x_hbm.at[pl.ds(base, SC_CHUNK)], rows_v)
            pltpu.sync_copy(dest_hbm.at[chunk], idx_v)
            copies = [pltpu.make_async_copy(rows_v, out_hbm.at[idx_v.at[k]], sem) for k in range(TOP_K)]
            for cp in copies:
                cp.start()
            for cp in copies:
                cp.wait()

    return dispatch(x1t3, dest3)


SC_RING = 3


def _sc_gather(yb3, dest3, t):
    _, n_lt, _ = yb3.shape
    n_workers = SC_CORES * SC_SUBCORES
    chunks_per_worker = t // (n_workers * SC_CHUNK)
    mesh = plsc.VectorSubcoreMesh(core_axis_name="c", subcore_axis_name="s",
                                  num_cores=SC_CORES, num_subcores=SC_SUBCORES)

    @functools.partial(
        pl.kernel, mesh=mesh,
        out_type=jax.ShapeDtypeStruct((TOP_K * t, n_lt, LANES), yb3.dtype),
        scratch_types=[pltpu.VMEM((TOP_K, SC_CHUNK), I32),
                       pltpu.VMEM((SC_RING, SC_CHUNK, n_lt, LANES), yb3.dtype)]
                      + [pltpu.SemaphoreType.DMA] * (2 * SC_RING),
        name="sc_gather")
    def gather(y_hbm, dest_hbm, out_hbm, idx_v, rows_v, *sems):
        gsem, wsem = sems[:SC_RING], sems[SC_RING:]
        wid = lax.axis_index("s") * SC_CORES + lax.axis_index("c")

        @pl.loop(0, chunks_per_worker)
        def _(j):
            chunk = wid * chunks_per_worker + j
            base = pl.multiple_of(chunk * SC_CHUNK, SC_CHUNK)
            pltpu.sync_copy(dest_hbm.at[chunk], idx_v)
            gets = [pltpu.make_async_copy(y_hbm.at[idx_v.at[k]], rows_v.at[k % SC_RING], gsem[k % SC_RING])
                    for k in range(TOP_K)]
            puts = [pltpu.make_async_copy(rows_v.at[k % SC_RING], out_hbm.at[pl.ds(k * t + base, SC_CHUNK)],
                                          wsem[k % SC_RING]) for k in range(TOP_K)]
            put_waited = set()
            for k in range(SC_RING - 1):
                gets[k].start()
            for k in range(TOP_K):
                gets[k].wait()
                puts[k].start()
                nxt = k + SC_RING - 1
                if nxt < TOP_K:
                    if k >= 1:
                        puts[k - 1].wait()
                        put_waited.add(k - 1)
                    gets[nxt].start()
            for k in range(TOP_K):
                if k not in put_waited:
                    puts[k].wait()

    return gather(yb3, dest3)


def _experts_body(be_ref, nact_ref, first_ref, nxte_ref, wslot_ref, valid_ref,
                  xs_ref, w1_hbm, w3_hbm, w2_hbm, o_ref,
                  wst1, wst3, wst2, w13b, w2b, sem_w):
    b = pl.program_id(0)
    nact = nact_ref[0]
    ed = wst2.shape[1]
    d = wst2.shape[2]
    n_pk = d // (2 * LANES)
    bm = xs_ref.shape[0] // n_pk
    active = b < nact

    def weight_copies(e, s):
        return (pltpu.make_async_copy(w1_hbm.at[e], wst1.at[s], sem_w.at[s]),
                pltpu.make_async_copy(w3_hbm.at[e], wst3.at[s], sem_w.at[s]),
                pltpu.make_async_copy(w2_hbm.at[e], wst2.at[s], sem_w.at[s]))

    @pl.when(b == 0)
    def _():
        for cp in weight_copies(be_ref[0], wslot_ref[0]):
            cp.start()

    @pl.when(first_ref[b] == 1)
    def _():
        ws = wslot_ref[b]
        for cp in weight_copies(be_ref[b], ws):
            cp.wait()

        @pl.when(nxte_ref[b] >= 0)
        def _():
            for cp in weight_copies(nxte_ref[b], 1 - ws):
                cp.start(priority=1)

        for c in range(ed // MXU_N):
            w13b[:, 2 * c * MXU_N:(2 * c + 1) * MXU_N] = wst1[ws, :, c * MXU_N:(c + 1) * MXU_N].astype(BF16)
            w13b[:, (2 * c + 1) * MXU_N:(2 * c + 2) * MXU_N] = wst3[ws, :, c * MXU_N:(c + 1) * MXU_N].astype(BF16)
        w2b[...] = wst2[ws].astype(BF16)

    @pl.when(active)
    def _():
        keep = lax.broadcasted_iota(I32, (bm, LANES), 0) < valid_ref[b]
        los, his = [], []
        for c in range(n_pk):
            lo, hi = _unpack_bf16_pair(jnp.where(keep, xs_ref[pl.ds(c, bm, stride=n_pk), :], 0))
            los.append(lo.astype(BF16))
            his.append(hi.astype(BF16))
        xb = jnp.concatenate(los + his, axis=1)
        parts = []
        for c in range(ed // MXU_N):
            hc = jnp.dot(xb, w13b[:, 2 * c * MXU_N:(2 * c + 2) * MXU_N], preferred_element_type=F32)
            parts.append((jax.nn.silu(hc[:, :MXU_N]) * hc[:, MXU_N:]).astype(BF16))
        hdn = jnp.concatenate(parts, axis=1)
        n_down = d // MXU_N
        ys = [jnp.dot(hdn, w2b[:, c * MXU_N:(c + 1) * MXU_N], preferred_element_type=F32) for c in range(n_down)]
        tiles_per_chunk = MXU_N // LANES
        for c in range(n_down // 2):
            word = _pack_bf16_pair(ys[c], ys[c + n_down // 2])
            for h in range(tiles_per_chunk):
                o_ref[pl.ds(c * tiles_per_chunk + h, bm, stride=n_pk), :] = word[:, h * LANES:(h + 1) * LANES]

    @pl.when(jnp.logical_not(active))
    def _():
        o_ref[...] = jnp.zeros_like(o_ref)


def _experts(xs2d, block_e, nact, first, nxt_e, wslot, valid, w1, w3, w2, bm):
    d, ed = w1.shape[1], w1.shape[2]
    n_pk = d // (2 * LANES)
    n_blk = block_e.shape[0]
    assert xs2d.shape == (n_blk * bm * n_pk, LANES)
    assert ed % MXU_N == 0 and d % MXU_N == 0 and bm % SUBLANES == 0
    grid_spec = pltpu.PrefetchScalarGridSpec(
        num_scalar_prefetch=6,
        grid=(n_blk,),
        in_specs=[pl.BlockSpec((bm * n_pk, LANES),
                               lambda b, be, na, *_: (jnp.maximum(jnp.minimum(b, na[0] - 1), 0), 0)),
                  pl.BlockSpec(memory_space=pl.ANY),
                  pl.BlockSpec(memory_space=pl.ANY),
                  pl.BlockSpec(memory_space=pl.ANY)],
        out_specs=pl.BlockSpec((bm * n_pk, LANES), lambda b, *_: (b, 0)),
        scratch_shapes=[pltpu.VMEM((2, d, ed), F32),
                        pltpu.VMEM((2, d, ed), F32),
                        pltpu.VMEM((2, ed, d), F32),
                        pltpu.VMEM((d, 2 * ed), BF16),
                        pltpu.VMEM((ed, d), BF16),
                        pltpu.SemaphoreType.DMA((2,))],
    )
    return pl.pallas_call(
        _experts_body,
        grid_spec=grid_spec,
        out_shape=jax.ShapeDtypeStruct((n_blk * bm * n_pk, LANES), I32),
        compiler_params=_cparams(("arbitrary",)),
        name="experts",
    )(block_e, nact, first, nxt_e, wslot, valid, xs2d, w1, w3, w2)


def _finalize_body(base_ref, gate_ref, g2_ref, b2_ref, *refs, n_a):
    slabs, (oa_ref, os_ref) = refs[:TOP_K], refs[TOP_K:]
    i = pl.program_id(0)
    tm, d = base_ref.shape
    half = d // 2
    n_pk = half // LANES
    acc_lo = [base_ref[:, c * LANES:(c + 1) * LANES] for c in range(n_pk)]
    acc_hi = [base_ref[:, half + c * LANES:half + (c + 1) * LANES] for c in range(n_pk)]
    for k in range(TOP_K):
        g = gate_ref[:, k:k + 1]
        for c in range(n_pk):
            lo, hi = _unpack_bf16_pair(slabs[k][pl.ds(c, tm, stride=n_pk), :])
            acc_lo[c] = acc_lo[c] + g * lo
            acc_hi[c] = acc_hi[c] + g * hi
    y = _layer_norm(jnp.concatenate(acc_lo + acc_hi, axis=1), g2_ref[...], b2_ref[...])

    @pl.when(i < n_a)
    def _():
        oa_ref[...] = y

    @pl.when(i >= n_a)
    def _():
        os_ref[...] = y


def _finalize(base, gate_t, slabs2d, ln_g, ln_b, t_a):
    t, d = base.shape
    tm = FINAL_TM
    n_pk = d // (2 * LANES)
    assert t_a % tm == 0 and 0 < t_a < t and slabs2d.shape == (TOP_K * t * n_pk, LANES)
    n_a = t_a // tm
    n_steps = t // tm
    map_a, map_s = _two_array_maps(n_a)
    slab_specs = [pl.BlockSpec((tm * n_pk, LANES), (lambda k: (lambda i: (k * n_steps + i, 0)))(k))
                  for k in range(TOP_K)]
    return pl.pallas_call(
        functools.partial(_finalize_body, n_a=n_a),
        grid=(n_steps,),
        in_specs=[pl.BlockSpec((tm, d), lambda i: (i, 0)),
                  pl.BlockSpec((tm, TOP_K), lambda i: (i, 0)),
                  pl.BlockSpec((1, d), lambda i: (0, 0)),
                  pl.BlockSpec((1, d), lambda i: (0, 0))] + slab_specs,
        out_specs=(pl.BlockSpec((tm, d), map_a), pl.BlockSpec((tm, d), map_s)),
        out_shape=(jax.ShapeDtypeStruct((t_a, d), F32), jax.ShapeDtypeStruct((t - t_a, d), F32)),
        compiler_params=_cparams(("arbitrary",)),
        name="finalize",
    )(base, gate_t, ln_g, ln_b, *([slabs2d] * TOP_K))


def _split_hi_lo(w):
    hi = w.astype(BF16)
    lo = (w - hi.astype(F32)).astype(BF16)
    return hi, lo


def _encoder_layer(xa, xs, seqs, lp, alpha):
    (w_in, pool_w, pool_b, pool_scale, w_pool_br, w_attn_br, attn_rpb, w_out, ln1_g, ln1_b,
     router_w, router_bias, exp_w1, exp_w3, exp_w2, sh_w1, sh_w3, sh_w2, ln2_g, ln2_b) = lp
    d = xa.shape[1]
    t = xa.shape[0] + xs.shape[0]
    pw = w_pool_br.shape[0]
    aw = w_attn_br.shape[0]
    n_heads = attn_rpb.shape[0]
    head_dim = aw // n_heads
    n_experts = router_w.shape[1]
    assert pw == INPROJ_TN and aw == INPROJ_TN and d % INPROJ_TN == 0
    u_block, q_block = 0, pw // INPROJ_TN
    gate_block0 = (pw + 3 * aw) // INPROJ_TN

    h = _inproj(xa, xs, w_in.astype(BF16), q_block=q_block, gate_block0=gate_block0, q_scale=head_dim ** -0.5)

    blk = ATT_ROWS * GRID_W
    images, seq_lo, seq_hi = [], [], []
    for s0, s1 in seqs:
        images.append((s0 // blk, (s1 - s0) // blk))
        for _ in range((s1 - s0) // MIX_TM):
            seq_lo.append(s0)
            seq_hi.append(s1)
    bias_tab = _attn_bias_table(attn_rpb, 2 * LANES // head_dim)
    attn = _attention(h, bias_tab, images, q_block=q_block, n_heads=n_heads, head_dim=head_dim)

    rwh, rwl = _split_hi_lo(router_w.T)
    p = dict(pool_w=pool_w.astype(BF16), pool_b=pool_b.reshape(1, pw), pool_scale=pool_scale.reshape(1, pw),
             wpb=w_pool_br.astype(BF16), wab=w_attn_br.astype(BF16), wo=w_out.astype(BF16),
             ln1_g=ln1_g.reshape(1, d), ln1_b=ln1_b.reshape(1, d),
             rwh=rwh, rwl=rwl, rb=router_bias.reshape(n_experts, 1).astype(F32))
    x1, x1f, idx_t, gate_t, rank_t, counts = _mixer(
        xa, xs, h, attn, jnp.asarray(seq_lo, I32), jnp.asarray(seq_hi, I32), p,
        alpha=alpha, u_block=u_block, gate_block0=gate_block0)

    bm = EXPERT_BM
    cnt = counts[:, 0].astype(I32)
    pcnt = (cnt + bm - 1) // bm * bm
    pend = jnp.cumsum(pcnt)
    pstart = pend - pcnt
    eids = jnp.arange(n_experts, dtype=I32)[:, None, None]
    dest = jnp.sum(jnp.where(idx_t[None] == eids, pstart[:, None, None], 0), axis=0) + rank_t
    n_blk = (t * TOP_K + n_experts * (bm - 1)) // bm
    blk_row = jnp.arange(n_blk, dtype=I32) * bm
    block_e = jnp.minimum(jnp.sum((pend[None, :] <= blk_row[:, None]).astype(I32), axis=1), n_experts - 1)
    nact = (pend[-1] // bm).astype(I32).reshape(1)
    block_e = jnp.where(jnp.arange(n_blk) < nact[0], block_e, block_e[jnp.maximum(nact[0] - 1, 0)])
    eids1 = jnp.arange(n_experts, dtype=I32)
    present = cnt > 0
    later = present[None, :] & (eids1[None, :] > eids1[:, None])
    nxt_of_e = jnp.min(jnp.where(later, eids1[None, :], n_experts), axis=1)
    nxt_of_e = jnp.where(nxt_of_e >= n_experts, -1, nxt_of_e).astype(I32)
    ord_of_e = (jnp.cumsum(present.astype(I32)) - 1).astype(I32)
    onehot_be = (block_e[:, None] == eids1[None, :]).astype(I32)
    blk_ids = jnp.arange(n_blk, dtype=I32)
    is_act = blk_ids < nact[0]
    prev_e = jnp.concatenate([jnp.full((1,), -1, I32), block_e[:-1]])
    first = (is_act & (block_e != prev_e)).astype(I32)
    nxt_e = jnp.sum(onehot_be * nxt_of_e[None, :], axis=1).astype(I32)
    wslot = (jnp.sum(onehot_be * ord_of_e[None, :], axis=1) % 2).astype(I32)
    start_of_e = jnp.sum(onehot_be * pstart[None, :], axis=1)
    cnt_of_e = jnp.sum(onehot_be * cnt[None, :], axis=1)
    valid = jnp.where(is_act, jnp.clip(cnt_of_e - (blk_row - start_of_e), 0, bm), 0).astype(I32)
    n_pk = d // (2 * LANES)
    dest3 = jnp.transpose(dest.reshape(TOP_K, t // SC_CHUNK, SC_CHUNK), (1, 0, 2))
    xs = _sc_dispatch(x1.reshape(t, n_pk, LANES), dest3, n_blk * bm)
    base = _shared_ffn(x1f, jnp.concatenate([sh_w1, sh_w3], axis=1).astype(BF16), sh_w2.astype(BF16), alpha=alpha)
    yb = _experts(xs.reshape(n_blk * bm * n_pk, LANES), block_e, nact, first, nxt_e, wslot, valid,
                  exp_w1, exp_w3, exp_w2, bm)

    slabs = _sc_gather(yb.reshape(n_blk * bm, n_pk, LANES), dest3, t)
    return _finalize(base, gate_t.T, slabs.reshape(TOP_K * t * n_pk, LANES),
                     ln2_g.reshape(1, d), ln2_b.reshape(1, d), xa.shape[0])


def kernel(x_prompt, x_sample, w_in, pool_w, pool_b, pool_scale, w_pool_br, w_attn_br, attn_rpb, w_out,
           ln1_g, ln1_b, router_w, router_bias, exp_w1, exp_w3, exp_w2, sh_w1, sh_w3, sh_w2, ln2_g, ln2_b):
    depth = w_in.shape[0]
    alpha = (2.0 * depth) ** 0.25
    bp, sp, d = x_prompt.shape
    bs, ss, _ = x_sample.shape
    seqs = [(i * sp, (i + 1) * sp) for i in range(bp)]
    seqs += [(bp * sp + i * ss, bp * sp + (i + 1) * ss) for i in range(bs)]
    xa, xs = x_prompt.reshape(bp * sp, d), x_sample.reshape(bs * ss, d)
    for l in range(depth):
        lp = (w_in[l], pool_w[l], pool_b[l], pool_scale[l], w_pool_br[l], w_attn_br[l], attn_rpb[l], w_out[l],
              ln1_g[l], ln1_b[l], router_w[l], router_bias[l], exp_w1[l], exp_w3[l], exp_w2[l],
              sh_w1[l], sh_w3[l], sh_w2[l], ln2_g[l], ln2_b[l])
        xa, xs = _encoder_layer(xa, xs, seqs, lp, alpha)
    return (xa.reshape(bp, sp, d), xs.reshape(bs, ss, d))
```

```python
import functools

import numpy as np
import jax
import jax.numpy as jnp
from jax import lax
from jax.experimental import pallas as pl
from jax.experimental.pallas import tpu as pltpu
from jax.experimental.pallas import tpu_sc as plsc

F32 = jnp.float32
BF16 = jnp.bfloat16
I32 = jnp.int32

GRID_W = 64
POOL_WINDOWS = (2, 4, 8, 16)
WIN_R = 8
WIN_C = 16
TOP_K = 8
N_GROUPS = 8
TOPK_GROUPS = 4
ROUTED_SCALE = 2.5
LN_EPS = 1e-5
NEG_BIG = -1e30

LANES = 128
SUBLANES = 8
MXU_N = 256
VMEM_LIMIT = 56 * 1024 * 1024

INPROJ_TM = 1024
INPROJ_TN = 1024
ATT_ROWS = 8
MIX_TM = 256
POOL_HALO = 16
EXPERT_BM = 512
FINAL_TM = 256
SHARED_TM = 256


def _cparams(sem):
    return pltpu.CompilerParams(dimension_semantics=sem, vmem_limit_bytes=VMEM_LIMIT)


def _inproj_body(xa_ref, xs_ref, w_ref, o_ref, xb_ref, *, n_a, q_block, gate_block0, q_scale):
    i = pl.program_id(0)
    j = pl.program_id(1)

    @pl.when((j == 0) & (i < n_a))
    def _():
        xb_ref[...] = xa_ref[...].astype(BF16)

    @pl.when((j == 0) & (i >= n_a))
    def _():
        xb_ref[...] = xs_ref[...].astype(BF16)

    is_gate = j >= gate_block0
    scale = jnp.where(j == q_block, q_scale, 1.0).astype(F32)
    for c in range(w_ref.shape[1] // MXU_N):
        acc = jnp.dot(xb_ref[...], w_ref[:, c * MXU_N:(c + 1) * MXU_N], preferred_element_type=F32)
        out = jnp.where(is_gate, jax.nn.sigmoid(acc), acc * scale)
        o_ref[:, c * MXU_N:(c + 1) * MXU_N] = out.astype(BF16)


def _two_array_maps(n_a):
    return (lambda i, *_: (jnp.minimum(i, n_a - 1), 0)), (lambda i, *_: (jnp.maximum(i - n_a, 0), 0))


def _inproj(xa, xs, w_bf16, *, q_block, gate_block0, q_scale):
    d = xa.shape[1]
    t = xa.shape[0] + xs.shape[0]
    n = w_bf16.shape[1]
    tm, tn = INPROJ_TM, INPROJ_TN
    assert xa.shape[0] % tm == 0 and xs.shape[0] % tm == 0 and xa.shape[0] > 0 and xs.shape[0] > 0
    n_a = xa.shape[0] // tm
    map_a, map_s = _two_array_maps(n_a)
    return pl.pallas_call(
        functools.partial(_inproj_body, n_a=n_a, q_block=q_block, gate_block0=gate_block0, q_scale=q_scale),
        grid=(t // tm, n // tn),
        in_specs=[pl.BlockSpec((tm, d), map_a),
                  pl.BlockSpec((tm, d), map_s),
                  pl.BlockSpec((d, tn), lambda i, j: (0, j))],
        out_specs=pl.BlockSpec((tm, tn), lambda i, j: (i, j)),
        out_shape=jax.ShapeDtypeStruct((t, n), BF16),
        scratch_shapes=[pltpu.VMEM((tm, d), BF16)],
        compiler_params=_cparams(("arbitrary", "arbitrary")),
        name="inproj",
    )(xa, xs, w_bf16)


def _attn_body(prev_ref, next_ref, kst_ref, off_ref,
               q_ref, kp_ref, kc_ref, kn_ref, vp_ref, vc_ref, vn_ref, bias_ref, o_ref,
               kbuf, vbuf, *, heads_per_group, head_dim):
    del prev_ref, next_ref
    b = pl.program_id(0)
    blk = ATT_ROWS * GRID_W
    kbuf[0:blk] = kp_ref[...]
    kbuf[blk:2 * blk] = kc_ref[...]
    kbuf[2 * blk:3 * blk] = kn_ref[...]
    vbuf[0:blk] = vp_ref[...]
    vbuf[blk:2 * blk] = vc_ref[...]
    vbuf[2 * blk:3 * blk] = vn_ref[...]

    gw = heads_per_group * head_dim
    n_groups = q_ref.shape[1] // gw
    win = WIN_R * GRID_W
    nq = heads_per_group * GRID_W
    row_head = lax.broadcasted_iota(I32, (nq, gw), 0) // GRID_W
    lane_head = lax.broadcasted_iota(I32, (nq, gw), 1) // head_dim
    head_mask = row_head == lane_head
    out_lane_head = lax.broadcasted_iota(I32, (GRID_W, gw), 1) // head_dim

    def row_step(i, carry):
        st = pl.multiple_of(kst_ref[b * ATT_ROWS + i] * GRID_W, GRID_W)
        dr0 = (WIN_R - 1) - off_ref[b * ATT_ROWS + i]
        q0 = pl.multiple_of(i * GRID_W, GRID_W)
        for g in range(n_groups):
            qg = q_ref[pl.ds(q0, GRID_W), g * gw:(g + 1) * gw]
            kg = kbuf[pl.ds(st, win), g * gw:(g + 1) * gw]
            vg = vbuf[pl.ds(st, win), g * gw:(g + 1) * gw]
            wt = jnp.concatenate([qg] * heads_per_group, axis=0)
            wt = jnp.where(head_mask, wt, jnp.zeros_like(wt))
            s = lax.dot_general(wt, kg, (((1,), (1,)), ((), ())),
                                preferred_element_type=F32)
            n_chunk = win // LANES
            sj = [s[:, j * LANES:(j + 1) * LANES] + bias_ref[dr0 + 2 * j, g] for j in range(n_chunk)]
            m = sj[0]
            for j in range(1, n_chunk):
                m = jnp.maximum(m, sj[j])
            m = jnp.max(m, axis=1, keepdims=True)
            ej = [jnp.exp(x - m) for x in sj]
            l = ej[0]
            for j in range(1, n_chunk):
                l = l + ej[j]
            inv = 1.0 / jnp.sum(l, axis=1, keepdims=True)
            p = jnp.concatenate([x.astype(BF16) for x in ej], axis=1)
            of = jnp.dot(p, vg, preferred_element_type=F32) * inv
            out = jnp.zeros((GRID_W, gw), F32)
            for h in range(heads_per_group):
                out = out + jnp.where(out_lane_head == h, of[h * GRID_W:(h + 1) * GRID_W, :], 0.0)
            o_ref[pl.ds(q0, GRID_W), g * gw:(g + 1) * gw] = out.astype(o_ref.dtype)
        return carry

    lax.fori_loop(0, ATT_ROWS, row_step, 0, unroll=4)


def _attn_tables(images):
    prev, nxt, kst, off = [], [], [], []
    for s, n in images:
        rows = n * ATT_ROWS
        kr = min(WIN_R, rows)
        assert kr == WIN_R
        for bl in range(n):
            b = s + bl
            prev.append(max(b - 1, s))
            nxt.append(min(b + 1, s + n - 1))
            for i in range(ATT_ROWS):
                r = bl * ATT_ROWS + i
                rs = min(max(r - kr // 2, 0), rows - kr)
                kst.append(rs - bl * ATT_ROWS + ATT_ROWS)
                off.append(r - rs)
    return (np.asarray(prev, np.int32), np.asarray(nxt, np.int32),
            np.asarray(kst, np.int32), np.asarray(off, np.int32))


def _attn_bias_table(rpb, heads_per_group):
    n_heads, n_dr, n_dc = rpb.shape
    n_pair = n_dr - 1
    c = np.arange(GRID_W)
    cs = np.clip(c - WIN_C // 2, 0, GRID_W - WIN_C)
    kc = np.arange(GRID_W)
    valid = (kc[None, :] >= cs[:, None]) & (kc[None, :] < cs[:, None] + WIN_C)
    dc = np.clip(kc[None, :] - c[:, None] + (WIN_C - 1), 0, n_dc - 1)
    sel = np.zeros((2, n_dc, GRID_W, 2, GRID_W), np.float32)
    ci, ki = np.nonzero(valid)
    for p in range(2):
        sel[p, dc[ci, ki], ci, p, ki] = 1.0
    sel = jnp.asarray(sel.reshape(2 * n_dc, GRID_W * 2 * GRID_W))
    r = rpb.astype(F32)
    pair = jnp.stack([r[:, :n_pair], r[:, 1:]], axis=2)
    pair = jnp.transpose(pair, (1, 0, 2, 3)).reshape(n_pair * n_heads, 2 * n_dc)
    tab = jnp.dot(pair, sel, precision=lax.Precision.HIGHEST)
    tab = tab.reshape(n_pair, n_heads, GRID_W, 2 * GRID_W)
    valid2 = jnp.asarray(np.tile(valid[:, None, :], (1, 2, 1)).reshape(GRID_W, 2 * GRID_W))
    tab = jnp.where(valid2[None, None], tab, NEG_BIG)
    return tab.reshape(n_pair, n_heads // heads_per_group, heads_per_group * GRID_W, 2 * GRID_W)


def _attention(h, bias_tab, images, *, q_block, n_heads, head_dim):
    t = h.shape[0]
    blk = ATT_ROWS * GRID_W
    aw = n_heads * head_dim
    heads_per_group = 2 * LANES // head_dim
    prev, nxt, kst, off = _attn_tables(images)
    n_blocks = t // blk
    assert prev.shape[0] == n_blocks
    kcol, vcol = q_block + 1, q_block + 2
    spec = lambda col, which: pl.BlockSpec(
        (blk, aw), {"cur": lambda b, p, n, k, o: (b, col),
                    "prev": lambda b, p, n, k, o: (p[b], col),
                    "next": lambda b, p, n, k, o: (n[b], col)}[which])
    grid_spec = pltpu.PrefetchScalarGridSpec(
        num_scalar_prefetch=4,
        grid=(n_blocks,),
        in_specs=[spec(q_block, "cur"),
                  spec(kcol, "prev"), spec(kcol, "cur"), spec(kcol, "next"),
                  spec(vcol, "prev"), spec(vcol, "cur"), spec(vcol, "next"),
                  pl.BlockSpec(bias_tab.shape, lambda b, p, n, k, o: (0, 0, 0, 0),
                               pipeline_mode=pl.Buffered(1))],
        out_specs=pl.BlockSpec((blk, aw), lambda b, p, n, k, o: (b, 0)),
        scratch_shapes=[pltpu.VMEM((3 * blk, aw), BF16), pltpu.VMEM((3 * blk, aw), BF16)],
    )
    return pl.pallas_call(
        functools.partial(_attn_body, heads_per_group=heads_per_group, head_dim=head_dim),
        grid_spec=grid_spec,
        out_shape=jax.ShapeDtypeStruct((t, aw), BF16),
        compiler_params=_cparams(("arbitrary",)),
        name="natten",
    )(jnp.asarray(prev), jnp.asarray(nxt), jnp.asarray(kst), jnp.asarray(off),
      h, h, h, h, h, h, h, bias_tab)


def _layer_norm(r, g, b):
    mu = jnp.mean(r, axis=-1, keepdims=True)
    c = r - mu
    var = jnp.mean(c * c, axis=-1, keepdims=True)
    return c * lax.rsqrt(var + LN_EPS) * g + b


HI_MASK = -65536


def _pack_bf16_pair(lo, hi):
    lo_bits = lax.bitcast_convert_type(lo.astype(BF16).astype(F32), I32)
    hi_bits = lax.bitcast_convert_type(hi.astype(BF16).astype(F32), I32)
    return (hi_bits & HI_MASK) | lax.shift_right_logical(lo_bits, 16)


def _unpack_bf16_pair(word):
    lo = lax.bitcast_convert_type(lax.shift_left(word, 16), F32)
    hi = lax.bitcast_convert_type(word & HI_MASK, F32)
    return lo, hi


def _first_index_of(mask, idx, sentinel, axis):
    return jnp.min(jnp.where(mask, idx, sentinel), axis=axis, keepdims=True)


def _mixer_body(seq_lo_ref, seq_hi_ref,
                xa_ref, xs_ref, up_ref, uc_ref, un_ref, gp_ref, ga_ref, at_ref,
                pw_ref, pb_ref, ps_ref, wpb_ref, wab_ref, wo_ref, g1_ref, b1_ref,
                rwh_ref, rwl_ref, rb_ref,
                x1_ref, x1f_ref, idx_ref, gate_ref, rank_ref, cnt_ref,
                carry_ref, *, n_a, alpha, n_experts):
    i = pl.program_id(0)
    tm = xa_ref.shape[0]
    t0 = i * tm
    lo_seq = seq_lo_ref[i]
    hi_seq = seq_hi_ref[i]

    @pl.when(i == 0)
    def _():
        carry_ref[...] = jnp.zeros_like(carry_ref)

    u_ext = jnp.concatenate([up_ref[...], uc_ref[...], un_ref[...]], axis=0)
    ext = tm + 2 * POOL_HALO
    tok_r = t0 + lax.broadcasted_iota(I32, (tm, ext), 0)
    tok_c = t0 - POOL_HALO + lax.broadcasted_iota(I32, (tm, ext), 1)
    tok_q = t0 + lax.broadcasted_iota(I32, (tm, 1), 0)
    n_pg = pw_ref.shape[0]
    pg = pw_ref.shape[1]
    parts = []
    for g in range(n_pg):
        w = POOL_WINDOWS[g]
        lo = jnp.maximum(tok_r - w // 2, lo_seq)
        hi = jnp.minimum(tok_r - w // 2 + w, hi_seq)
        band = jnp.where((tok_c >= lo) & (tok_c < hi), 1.0, 0.0).astype(BF16)
        cnt = (jnp.minimum(tok_q - w // 2 + w, hi_seq) - jnp.maximum(tok_q - w // 2, lo_seq)).astype(F32)
        ug = u_ext[:, g * pg:(g + 1) * pg]
        sums = jnp.dot(band, ug, preferred_element_type=F32)
        z = sums / cnt - uc_ref[:, g * pg:(g + 1) * pg].astype(F32)
        y = jnp.dot(z.astype(BF16), pw_ref[g], preferred_element_type=F32)
        y = (y + pb_ref[:, g * pg:(g + 1) * pg]) * ps_ref[:, g * pg:(g + 1) * pg]
        parts.append(y.astype(BF16))
    a = jnp.concatenate(parts, axis=1)
    a_pool = jnp.dot(a, wpb_ref[...], preferred_element_type=F32)
    a_attn = jnp.dot(at_ref[...], wab_ref[...], preferred_element_type=F32)
    merged = gp_ref[...].astype(F32) * a_pool + ga_ref[...].astype(F32) * a_attn
    mix = jnp.dot(merged.astype(BF16), wo_ref[...], preferred_element_type=F32)
    x_in = jnp.where(i < n_a, xa_ref[...], xs_ref[...])
    x1 = _layer_norm(alpha * x_in + mix, g1_ref[...], b1_ref[...])
    half = x1.shape[1] // 2
    n_pk = half // LANES
    word = _pack_bf16_pair(x1[:, :half], x1[:, half:])
    for c in range(n_pk):
        x1_ref[pl.ds(c, tm, stride=n_pk), :] = word[:, c * LANES:(c + 1) * LANES]
    x1f_ref[...] = x1
    x1b = x1.astype(BF16)

    x1l = (x1 - x1b.astype(F32)).astype(BF16)
    nt = (((1,), (1,)), ((), ()))
    logits = (lax.dot_general(rwh_ref[...], x1b, nt, preferred_element_type=F32)
              + lax.dot_general(rwh_ref[...], x1l, nt, preferred_element_type=F32)
              + lax.dot_general(rwl_ref[...], x1b, nt, preferred_element_type=F32))
    scores = jax.nn.sigmoid(logits)
    biased = scores + rb_ref[...]
    gsz = n_experts // N_GROUPS
    neg_inf = -jnp.inf

    mem = lax.broadcasted_iota(I32, (gsz, tm), 0)
    gs_rows = []
    for g in range(N_GROUPS):
        bg = biased[g * gsz:(g + 1) * gsz, :]
        m1 = jnp.max(bg, axis=0, keepdims=True)
        first = _first_index_of(bg == m1, mem, gsz, 0)
        m2 = jnp.max(jnp.where(mem == first, neg_inf, bg), axis=0, keepdims=True)
        gs_rows.append(m1 + m2)
    gwork = jnp.concatenate(gs_rows, axis=0)

    gidx = lax.broadcasted_iota(I32, (N_GROUPS, tm), 0)
    egrp = lax.broadcasted_iota(I32, (n_experts, tm), 0) // gsz
    eidx = lax.broadcasted_iota(I32, (n_experts, tm), 0)
    work = jnp.full((n_experts, tm), neg_inf, F32)
    for _ in range(TOPK_GROUPS):
        gm = jnp.max(gwork, axis=0, keepdims=True)
        gf = _first_index_of(gwork == gm, gidx, N_GROUPS, 0)
        gwork = jnp.where(gidx == gf, neg_inf, gwork)
        work = jnp.where(egrp == gf, biased, work)

    hot = jnp.zeros((n_experts, tm), F32)
    sel_idx, sel_w, sel_hit = [], [], []
    for _ in range(TOP_K):
        em = jnp.max(work, axis=0, keepdims=True)
        ef = _first_index_of(work == em, eidx, n_experts, 0)
        hit = eidx == ef
        sel_idx.append(ef)
        sel_w.append(jnp.sum(jnp.where(hit, scores, 0.0), axis=0, keepdims=True))
        sel_hit.append(hit)
        hot = jnp.where(hit, 1.0, hot)
        work = jnp.where(hit, neg_inf, work)
    wsel = jnp.concatenate(sel_w, axis=0)
    gate_ref[...] = wsel / jnp.sum(wsel, axis=0, keepdims=True) * ROUTED_SCALE
    idx_ref[...] = jnp.concatenate(sel_idx, axis=0)

    tri = jnp.where(lax.broadcasted_iota(I32, (tm, tm), 0) < lax.broadcasted_iota(I32, (tm, tm), 1),
                    1.0, 0.0).astype(BF16)
    before = carry_ref[:, 0:1] + jnp.dot(hot.astype(BF16), tri, preferred_element_type=F32)
    rank = [jnp.sum(jnp.where(h, before, 0.0), axis=0, keepdims=True) for h in sel_hit]
    rank_ref[...] = jnp.concatenate(rank, axis=0).astype(I32)
    carry_ref[...] = carry_ref[...] + jnp.sum(hot, axis=1, keepdims=True)
    cnt_ref[...] = carry_ref[...]


def _mixer(xa, xs, h, attn, seq_lo, seq_hi, p, *, alpha, u_block, gate_block0):
    d = xa.shape[1]
    t = xa.shape[0] + xs.shape[0]
    tm = MIX_TM
    assert xa.shape[0] % tm == 0 and xs.shape[0] % tm == 0
    n_a = xa.shape[0] // tm
    map_a, map_s = _two_array_maps(n_a)
    pw = p["wpb"].shape[0]
    aw = attn.shape[1]
    n_experts = p["rwh"].shape[0]
    halo_per_tile = tm // POOL_HALO
    n_halo_blocks = t // POOL_HALO
    gp_col = gate_block0 * INPROJ_TN // d
    in_specs = [
        pl.BlockSpec((tm, d), map_a),
        pl.BlockSpec((tm, d), map_s),
        pl.BlockSpec((POOL_HALO, pw), lambda i, lo, hi: (jnp.maximum(i * halo_per_tile - 1, 0), u_block)),
        pl.BlockSpec((tm, pw), lambda i, lo, hi: (i, u_block)),
        pl.BlockSpec((POOL_HALO, pw),
                     lambda i, lo, hi: (jnp.minimum((i + 1) * halo_per_tile, n_halo_blocks - 1), u_block)),
        pl.BlockSpec((tm, d), lambda i, lo, hi: (i, gp_col)),
        pl.BlockSpec((tm, d), lambda i, lo, hi: (i, gp_col + 1)),
        pl.BlockSpec((tm, aw), lambda i, lo, hi: (i, 0)),
    ]
    weights = [p["pool_w"], p["pool_b"], p["pool_scale"], p["wpb"], p["wab"], p["wo"], p["ln1_g"], p["ln1_b"],
               p["rwh"], p["rwl"], p["rb"]]
    in_specs += [pl.BlockSpec(w.shape, (lambda nd: (lambda i, lo, hi: (0,) * nd))(w.ndim),
                              pipeline_mode=pl.Buffered(1)) for w in weights]
    n_pk = d // (2 * LANES)
    out_shape = (jax.ShapeDtypeStruct((t * n_pk, LANES), I32),
                 jax.ShapeDtypeStruct((t, d), F32),
                 jax.ShapeDtypeStruct((TOP_K, t), I32),
                 jax.ShapeDtypeStruct((TOP_K, t), F32),
                 jax.ShapeDtypeStruct((TOP_K, t), I32),
                 jax.ShapeDtypeStruct((n_experts, LANES), F32))
    out_specs = (pl.BlockSpec((tm * n_pk, LANES), lambda i, lo, hi: (i, 0)),
                 pl.BlockSpec((tm, d), lambda i, lo, hi: (i, 0)),
                 pl.BlockSpec((TOP_K, tm), lambda i, lo, hi: (0, i)),
                 pl.BlockSpec((TOP_K, tm), lambda i, lo, hi: (0, i)),
                 pl.BlockSpec((TOP_K, tm), lambda i, lo, hi: (0, i)),
                 pl.BlockSpec((n_experts, LANES), lambda i, lo, hi: (0, 0)))
    grid_spec = pltpu.PrefetchScalarGridSpec(
        num_scalar_prefetch=2, grid=(t // tm,), in_specs=in_specs, out_specs=out_specs,
        scratch_shapes=[pltpu.VMEM((n_experts, LANES), F32)])
    return pl.pallas_call(
        functools.partial(_mixer_body, n_a=n_a, alpha=alpha, n_experts=n_experts),
        grid_spec=grid_spec, out_shape=out_shape,
        compiler_params=_cparams(("arbitrary",)),
        name="mixer",
    )(seq_lo, seq_hi, xa, xs, h, h, h, h, h, attn, *weights)


def _shared_ffn_body(x_ref, s13_ref, s2_ref, o_ref, *, alpha):
    x = x_ref[...]
    sd = s2_ref.shape[0]
    hc = jnp.dot(x.astype(BF16), s13_ref[...], preferred_element_type=F32)
    hdn = jax.nn.silu(hc[:, :sd]) * hc[:, sd:]
    o_ref[...] = alpha * x + jnp.dot(hdn.astype(BF16), s2_ref[...], preferred_element_type=F32)


def _shared_ffn(x1f, s13, s2, *, alpha):
    t, d = x1f.shape
    tm = SHARED_TM
    assert t % tm == 0
    return pl.pallas_call(
        functools.partial(_shared_ffn_body, alpha=alpha),
        grid=(t // tm,),
        in_specs=[pl.BlockSpec((tm, d), lambda i: (i, 0)),
                  pl.BlockSpec(s13.shape, lambda i: (0, 0), pipeline_mode=pl.Buffered(1)),
                  pl.BlockSpec(s2.shape, lambda i: (0, 0), pipeline_mode=pl.Buffered(1))],
        out_specs=pl.BlockSpec((tm, d), lambda i: (i, 0)),
        out_shape=jax.ShapeDtypeStruct((t, d), F32),
        compiler_params=_cparams(("arbitrary",)),
        name="shared_ffn",
    )(x1f, s13, s2)


SC_CORES = 2
SC_SUBCORES = 16
SC_CHUNK = 32


def _sc_dispatch(x1t3, dest3, n_rows):
    t, n_lt, _ = x1t3.shape
    n_workers = SC_CORES * SC_SUBCORES
    assert t % (n_workers * SC_CHUNK) == 0
    chunks_per_worker = t // (n_workers * SC_CHUNK)
    mesh = plsc.VectorSubcoreMesh(core_axis_name="c", subcore_axis_name="s",
                                  num_cores=SC_CORES, num_subcores=SC_SUBCORES)

    @functools.partial(
        pl.kernel, mesh=mesh,
        out_type=jax.ShapeDtypeStruct((n_rows, n_lt, LANES), x1t3.dtype),
        scratch_types=[pltpu.VMEM((TOP_K, SC_CHUNK), I32),
                       pltpu.VMEM((SC_CHUNK, n_lt, LANES), x1t3.dtype),
                       pltpu.SemaphoreType.DMA],
        name="sc_dispatch")
    def dispatch(x_hbm, dest_hbm, out_hbm, idx_v, rows_v, sem):
        wid = lax.axis_index("s") * SC_CORES + lax.axis_index("c")

        @pl.loop(0, chunks_per_worker)
        def _(j):
            chunk = wid * chunks_per_worker + j
            base = pl.multiple_of(chunk * SC_CHUNK, SC_CHUNK)
            pltpu.sync_copy(x_hbm.at[pl.ds(base, SC_CHUNK)], rows_v)
            pltpu.sync_copy(dest_hbm.at[chunk], idx_v)
            copies = [pltpu.make_async_copy(rows_v, out_hbm.at[idx_v.at[k]], sem) for k in range(TOP_K)]
            for cp in copies:
                cp.start()
            for cp in copies:
                cp.wait()

    return dispatch(x1t3, dest3)


SC_RING = 3


def _sc_gather(yb3, dest3, t):
    _, n_lt, _ = yb3.shape
    n_workers = SC_CORES * SC_SUBCORES
    chunks_per_worker = t // (n_workers * SC_CHUNK)
    mesh = plsc.VectorSubcoreMesh(core_axis_name="c", subcore_axis_name="s",
                                  num_cores=SC_CORES, num_subcores=SC_SUBCORES)

    @functools.partial(
        pl.kernel, mesh=mesh,
        out_type=jax.ShapeDtypeStruct((TOP_K * t, n_lt, LANES), yb3.dtype),
        scratch_types=[pltpu.VMEM((TOP_K, SC_CHUNK), I32),
                       pltpu.VMEM((SC_RING, SC_CHUNK, n_lt, LANES), yb3.dtype)]
                      + [pltpu.SemaphoreType.DMA] * (2 * SC_RING),
        name="sc_gather")
    def gather(y_hbm, dest_hbm, out_hbm, idx_v, rows_v, *sems):
        gsem, wsem = sems[:SC_RING], sems[SC_RING:]
        wid = lax.axis_index("s") * SC_CORES + lax.axis_index("c")

        @pl.loop(0, chunks_per_worker)
        def _(j):
            chunk = wid * chunks_per_worker + j
            base = pl.multiple_of(chunk * SC_CHUNK, SC_CHUNK)
            pltpu.sync_copy(dest_hbm.at[chunk], idx_v)
            gets = [pltpu.make_async_copy(y_hbm.at[idx_v.at[k]], rows_v.at[k % SC_RING], gsem[k % SC_RING])
                    for k in range(TOP_K)]
            puts = [pltpu.make_async_copy(rows_v.at[k % SC_RING], out_hbm.at[pl.ds(k * t + base, SC_CHUNK)],
                                          wsem[k % SC_RING]) for k in range(TOP_K)]
            put_waited = set()
            for k in range(SC_RING - 1):
                gets[k].start()
            for k in range(TOP_K):
                gets[k].wait()
                puts[k].start()
                nxt = k + SC_RING - 1
                if nxt < TOP_K:
                    if k >= 1:
                        puts[k - 1].wait()
                        put_waited.add(k - 1)
                    gets[nxt].start()
            for k in range(TOP_K):
                if k not in put_waited:
                    puts[k].wait()

    return gather(yb3, dest3)


def _experts_body(be_ref, nact_ref, first_ref, nxte_ref, wslot_ref, valid_ref,
                  xs_ref, w1_hbm, w3_hbm, w2_hbm, o_ref,
                  wst1, wst3, wst2, w13b, w2b, sem_w):
    b = pl.program_id(0)
    nact = nact_ref[0]
    ed = wst2.shape[1]
    d = wst2.shape[2]
    n_pk = d // (2 * LANES)
    bm = xs_ref.shape[0] // n_pk
    active = b < nact

    def weight_copies(e, s):
        return (pltpu.make_async_copy(w1_hbm.at[e], wst1.at[s], sem_w.at[s]),
                pltpu.make_async_copy(w3_hbm.at[e], wst3.at[s], sem_w.at[s]),
                pltpu.make_async_copy(w2_hbm.at[e], wst2.at[s], sem_w.at[s]))

    @pl.when(b == 0)
    def _():
        for cp in weight_copies(be_ref[0], wslot_ref[0]):
            cp.start()

    @pl.when(first_ref[b] == 1)
    def _():
        ws = wslot_ref[b]
        for cp in weight_copies(be_ref[b], ws):
            cp.wait()

        @pl.when(nxte_ref[b] >= 0)
        def _():
            for cp in weight_copies(nxte_ref[b], 1 - ws):
                cp.start(priority=1)

        for c in range(ed // MXU_N):
            w13b[:, 2 * c * MXU_N:(2 * c + 1) * MXU_N] = wst1[ws, :, c * MXU_N:(c + 1) * MXU_N].astype(BF16)
            w13b[:, (2 * c + 1) * MXU_N:(2 * c + 2) * MXU_N] = wst3[ws, :, c * MXU_N:(c + 1) * MXU_N].astype(BF16)
        w2b[...] = wst2[ws].astype(BF16)

    @pl.when(active)
    def _():
        keep = lax.broadcasted_iota(I32, (bm, LANES), 0) < valid_ref[b]
        los, his = [], []
        for c in range(n_pk):
            lo, hi = _unpack_bf16_pair(jnp.where(keep, xs_ref[pl.ds(c, bm, stride=n_pk), :], 0))
            los.append(lo.astype(BF16))
            his.append(hi.astype(BF16))
        xb = jnp.concatenate(los + his, axis=1)
        parts = []
        for c in range(ed // MXU_N):
            hc = jnp.dot(xb, w13b[:, 2 * c * MXU_N:(2 * c + 2) * MXU_N], preferred_element_type=F32)
            parts.append((jax.nn.silu(hc[:, :MXU_N]) * hc[:, MXU_N:]).astype(BF16))
        hdn = jnp.concatenate(parts, axis=1)
        n_down = d // MXU_N
        ys = [jnp.dot(hdn, w2b[:, c * MXU_N:(c + 1) * MXU_N], preferred_element_type=F32) for c in range(n_down)]
        tiles_per_chunk = MXU_N // LANES
        for c in range(n_down // 2):
            word = _pack_bf16_pair(ys[c], ys[c + n_down // 2])
            for h in range(tiles_per_chunk):
                o_ref[pl.ds(c * tiles_per_chunk + h, bm, stride=n_pk), :] = word[:, h * LANES:(h + 1) * LANES]

    @pl.when(jnp.logical_not(active))
    def _():
        o_ref[...] = jnp.zeros_like(o_ref)


def _experts(xs2d, block_e, nact, first, nxt_e, wslot, valid, w1, w3, w2, bm):
    d, ed = w1.shape[1], w1.shape[2]
    n_pk = d // (2 * LANES)
    n_blk = block_e.shape[0]
    assert xs2d.shape == (n_blk * bm * n_pk, LANES)
    assert ed % MXU_N == 0 and d % MXU_N == 0 and bm % SUBLANES == 0
    grid_spec = pltpu.PrefetchScalarGridSpec(
        num_scalar_prefetch=6,
        grid=(n_blk,),
        in_specs=[pl.BlockSpec((bm * n_pk, LANES),
                               lambda b, be, na, *_: (jnp.maximum(jnp.minimum(b, na[0] - 1), 0), 0)),
                  pl.BlockSpec(memory_space=pl.ANY),
                  pl.BlockSpec(memory_space=pl.ANY),
                  pl.BlockSpec(memory_space=pl.ANY)],
        out_specs=pl.BlockSpec((bm * n_pk, LANES), lambda b, *_: (b, 0)),
        scratch_shapes=[pltpu.VMEM((2, d, ed), F32),
                        pltpu.VMEM((2, d, ed), F32),
                        pltpu.VMEM((2, ed, d), F32),
                        pltpu.VMEM((d, 2 * ed), BF16),
                        pltpu.VMEM((ed, d), BF16),
                        pltpu.SemaphoreType.DMA((2,))],
    )
    return pl.pallas_call(
        _experts_body,
        grid_spec=grid_spec,
        out_shape=jax.ShapeDtypeStruct((n_blk * bm * n_pk, LANES), I32),
        compiler_params=_cparams(("arbitrary",)),
        name="experts",
    )(block_e, nact, first, nxt_e, wslot, valid, xs2d, w1, w3, w2)


def _finalize_body(base_ref, gate_ref, g2_ref, b2_ref, *refs, n_a):
    slabs, (oa_ref, os_ref) = refs[:TOP_K], refs[TOP_K:]
    i = pl.program_id(0)
    tm, d = base_ref.shape
    half = d // 2
    n_pk = half // LANES
    acc_lo = [base_ref[:, c * LANES:(c + 1) * LANES] for c in range(n_pk)]
    acc_hi = [base_ref[:, half + c * LANES:half + (c + 1) * LANES] for c in range(n_pk)]
    for k in range(TOP_K):
        g = gate_ref[:, k:k + 1]
        for c in range(n_pk):
            lo, hi = _unpack_bf16_pair(slabs[k][pl.ds(c, tm, stride=n_pk), :])
            acc_lo[c] = acc_lo[c] + g * lo
            acc_hi[c] = acc_hi[c] + g * hi
    y = _layer_norm(jnp.concatenate(acc_lo + acc_hi, axis=1), g2_ref[...], b2_ref[...])

    @pl.when(i < n_a)
    def _():
        oa_ref[...] = y

    @pl.when(i >= n_a)
    def _():
        os_ref[...] = y


def _finalize(base, gate_t, slabs2d, ln_g, ln_b, t_a):
    t, d = base.shape
    tm = FINAL_TM
    n_pk = d // (2 * LANES)
    assert t_a % tm == 0 and 0 < t_a < t and slabs2d.shape == (TOP_K * t * n_pk, LANES)
    n_a = t_a // tm
    n_steps = t // tm
    map_a, map_s = _two_array_maps(n_a)
    slab_specs = [pl.BlockSpec((tm * n_pk, LANES), (lambda k: (lambda i: (k * n_steps + i, 0)))(k))
                  for k in range(TOP_K)]
    return pl.pallas_call(
        functools.partial(_finalize_body, n_a=n_a),
        grid=(n_steps,),
        in_specs=[pl.BlockSpec((tm, d), lambda i: (i, 0)),
                  pl.BlockSpec((tm, TOP_K), lambda i: (i, 0)),
                  pl.BlockSpec((1, d), lambda i: (0, 0)),
                  pl.BlockSpec((1, d), lambda i: (0, 0))] + slab_specs,
        out_specs=(pl.BlockSpec((tm, d), map_a), pl.BlockSpec((tm, d), map_s)),
        out_shape=(jax.ShapeDtypeStruct((t_a, d), F32), jax.ShapeDtypeStruct((t - t_a, d), F32)),
        compiler_params=_cparams(("arbitrary",)),
        name="finalize",
    )(base, gate_t, ln_g, ln_b, *([slabs2d] * TOP_K))


def _split_hi_lo(w):
    hi = w.astype(BF16)
    lo = (w - hi.astype(F32)).astype(BF16)
    return hi, lo


def _encoder_layer(xa, xs, seqs, lp, alpha):
    (w_in, pool_w, pool_b, pool_scale, w_pool_br, w_attn_br, attn_rpb, w_out, ln1_g, ln1_b,
     router_w, router_bias, exp_w1, exp_w3, exp_w2, sh_w1, sh_w3, sh_w2, ln2_g, ln2_b) = lp
    d = xa.shape[1]
    t = xa.shape[0] + xs.shape[0]
    pw = w_pool_br.shape[0]
    aw = w_attn_br.shape[0]
    n_heads = attn_rpb.shape[0]
    head_dim = aw // n_heads
    n_experts = router_w.shape[1]
    assert pw == INPROJ_TN and aw == INPROJ_TN and d % INPROJ_TN == 0
    u_block, q_block = 0, pw // INPROJ_TN
    gate_block0 = (pw + 3 * aw) // INPROJ_TN

    h = _inproj(xa, xs, w_in.astype(BF16), q_block=q_block, gate_block0=gate_block0, q_scale=head_dim ** -0.5)

    blk = ATT_ROWS * GRID_W
    images, seq_lo, seq_hi = [], [], []
    for s0, s1 in seqs:
        images.append((s0 // blk, (s1 - s0) // blk))
        for _ in range((s1 - s0) // MIX_TM):
            seq_lo.append(s0)
            seq_hi.append(s1)
    bias_tab = _attn_bias_table(attn_rpb, 2 * LANES // head_dim)
    attn = _attention(h, bias_tab, images, q_block=q_block, n_heads=n_heads, head_dim=head_dim)

    rwh, rwl = _split_hi_lo(router_w.T)
    p = dict(pool_w=pool_w.astype(BF16), pool_b=pool_b.reshape(1, pw), pool_scale=pool_scale.reshape(1, pw),
             wpb=w_pool_br.astype(BF16), wab=w_attn_br.astype(BF16), wo=w_out.astype(BF16),
             ln1_g=ln1_g.reshape(1, d), ln1_b=ln1_b.reshape(1, d),
             rwh=rwh, rwl=rwl, rb=router_bias.reshape(n_experts, 1).astype(F32))
    x1, x1f, idx_t, gate_t, rank_t, counts = _mixer(
        xa, xs, h, attn, jnp.asarray(seq_lo, I32), jnp.asarray(seq_hi, I32), p,
        alpha=alpha, u_block=u_block, gate_block0=gate_block0)

    bm = EXPERT_BM
    cnt = counts[:, 0].astype(I32)
    pcnt = (cnt + bm - 1) // bm * bm
    pend = jnp.cumsum(pcnt)
    pstart = pend - pcnt
    eids = jnp.arange(n_experts, dtype=I32)[:, None, None]
    dest = jnp.sum(jnp.where(idx_t[None] == eids, pstart[:, None, None], 0), axis=0) + rank_t
    n_blk = (t * TOP_K + n_experts * (bm - 1)) // bm
    blk_row = jnp.arange(n_blk, dtype=I32) * bm
    block_e = jnp.minimum(jnp.sum((pend[None, :] <= blk_row[:, None]).astype(I32), axis=1), n_experts - 1)
    nact = (pend[-1] // bm).astype(I32).reshape(1)
    block_e = jnp.where(jnp.arange(n_blk) < nact[0], block_e, block_e[jnp.maximum(nact[0] - 1, 0)])
    eids1 = jnp.arange(n_experts, dtype=I32)
    present = cnt > 0
    later = present[None, :] & (eids1[None, :] > eids1[:, None])
    nxt_of_e = jnp.min(jnp.where(later, eids1[None, :], n_experts), axis=1)
    nxt_of_e = jnp.where(nxt_of_e >= n_experts, -1, nxt_of_e).astype(I32)
    ord_of_e = (jnp.cumsum(present.astype(I32)) - 1).astype(I32)
    onehot_be = (block_e[:, None] == eids1[None, :]).astype(I32)
    blk_ids = jnp.arange(n_blk, dtype=I32)
    is_act = blk_ids < nact[0]
    prev_e = jnp.concatenate([jnp.full((1,), -1, I32), block_e[:-1]])
    first = (is_act & (block_e != prev_e)).astype(I32)
    nxt_e = jnp.sum(onehot_be * nxt_of_e[None, :], axis=1).astype(I32)
    wslot = (jnp.sum(onehot_be * ord_of_e[None, :], axis=1) % 2).astype(I32)
    start_of_e = jnp.sum(onehot_be * pstart[None, :], axis=1)
    cnt_of_e = jnp.sum(onehot_be * cnt[None, :], axis=1)
    valid = jnp.where(is_act, jnp.clip(cnt_of_e - (blk_row - start_of_e), 0, bm), 0).astype(I32)
    n_pk = d // (2 * LANES)
    dest3 = jnp.transpose(dest.reshape(TOP_K, t // SC_CHUNK, SC_CHUNK), (1, 0, 2))
    xs = _sc_dispatch(x1.reshape(t, n_pk, LANES), dest3, n_blk * bm)
    base = _shared_ffn(x1f, jnp.concatenate([sh_w1, sh_w3], axis=1).astype(BF16), sh_w2.astype(BF16), alpha=alpha)
    yb = _experts(xs.reshape(n_blk * bm * n_pk, LANES), block_e, nact, first, nxt_e, wslot, valid,
                  exp_w1, exp_w3, exp_w2, bm)

    slabs = _sc_gather(yb.reshape(n_blk * bm, n_pk, LANES), dest3, t)
    return _finalize(base, gate_t.T, slabs.reshape(TOP_K * t * n_pk, LANES),
                     ln2_g.reshape(1, d), ln2_b.reshape(1, d), xa.shape[0])


def kernel(x_prompt, x_sample, w_in, pool_w, pool_b, pool_scale, w_pool_br, w_attn_br, attn_rpb, w_out,
           ln1_g, ln1_b, router_w, router_bias, exp_w1, exp_w3, exp_w2, sh_w1, sh_w3, sh_w2, ln2_g, ln2_b):
    depth = w_in.shape[0]
    alpha = (2.0 * depth) ** 0.25
    bp, sp, d = x_prompt.shape
    bs, ss, _ = x_sample.shape
    seqs = [(i * sp, (i + 1) * sp) for i in range(bp)]
    seqs += [(bp * sp + i * ss, bp * sp + (i + 1) * ss) for i in range(bs)]
    xa, xs = x_prompt.reshape(bp * sp, d), x_sample.reshape(bs * ss, d)
    for l in range(depth):
        lp = (w_in[l], pool_w[l], pool_b[l], pool_scale[l], w_pool_br[l], w_attn_br[l], attn_rpb[l], w_out[l],
              ln1_g[l], ln1_b[l], router_w[l], router_bias[l], exp_w1[l], exp_w3[l], exp_w2[l],
              sh_w1[l], sh_w3[l], sh_w2[l], ln2_g[l], ln2_b[l])
        xa, xs = _encoder_layer(xa, xs, seqs, lp, alpha)
    return (xa.reshape(bp, sp, d), xs.reshape(bs, ss, d))
```

```python
import functools

import numpy as np
import jax
import jax.numpy as jnp
from jax import lax
from jax.experimental import pallas as pl
from jax.experimental.pallas import tpu as pltpu
from jax.experimental.pallas import tpu_sc as plsc

F32 = jnp.float32
BF16 = jnp.bfloat16
I32 = jnp.int32

GRID_W = 64
POOL_WINDOWS = (2, 4, 8, 16)
WIN_R = 8
WIN_C = 16
TOP_K = 8
N_GROUPS = 8
TOPK_GROUPS = 4
ROUTED_SCALE = 2.5
LN_EPS = 1e-5
NEG_BIG = -1e30

LANES = 128
SUBLANES = 8
MXU_N = 256
VMEM_LIMIT = 56 * 1024 * 1024

INPROJ_TM = 1024
INPROJ_TN = 1024
ATT_ROWS = 8
MIX_TM = 256
POOL_HALO = 16
EXPERT_BM = 512
FINAL_TM = 256
SHARED_TM = 1024


def _cparams(sem):
    return pltpu.CompilerParams(dimension_semantics=sem, vmem_limit_bytes=VMEM_LIMIT)


def _inproj_body(xa_ref, xs_ref, w_ref, o_ref, xb_ref, *, n_a, q_block, gate_block0, q_scale):
    i = pl.program_id(0)
    j = pl.program_id(1)

    @pl.when((j == 0) & (i < n_a))
    def _():
        xb_ref[...] = xa_ref[...].astype(BF16)

    @pl.when((j == 0) & (i >= n_a))
    def _():
        xb_ref[...] = xs_ref[...].astype(BF16)

    is_gate = j >= gate_block0
    scale = jnp.where(j == q_block, q_scale, 1.0).astype(F32)
    for c in range(w_ref.shape[1] // MXU_N):
        acc = jnp.dot(xb_ref[...], w_ref[:, c * MXU_N:(c + 1) * MXU_N], preferred_element_type=F32)
        out = jnp.where(is_gate, jax.nn.sigmoid(acc), acc * scale)
        o_ref[:, c * MXU_N:(c + 1) * MXU_N] = out.astype(BF16)


def _two_array_maps(n_a):
    return (lambda i, *_: (jnp.minimum(i, n_a - 1), 0)), (lambda i, *_: (jnp.maximum(i - n_a, 0), 0))


def _inproj(xa, xs, w_bf16, *, q_block, gate_block0, q_scale):
    d = xa.shape[1]
    t = xa.shape[0] + xs.shape[0]
    n = w_bf16.shape[1]
    tm, tn = INPROJ_TM, INPROJ_TN
    assert xa.shape[0] % tm == 0 and xs.shape[0] % tm == 0 and xa.shape[0] > 0 and xs.shape[0] > 0
    n_a = xa.shape[0] // tm
    map_a, map_s = _two_array_maps(n_a)
    return pl.pallas_call(
        functools.partial(_inproj_body, n_a=n_a, q_block=q_block, gate_block0=gate_block0, q_scale=q_scale),
        grid=(t // tm, n // tn),
        in_specs=[pl.BlockSpec((tm, d), map_a),
                  pl.BlockSpec((tm, d), map_s),
                  pl.BlockSpec((d, tn), lambda i, j: (0, j))],
        out_specs=pl.BlockSpec((tm, tn), lambda i, j: (i, j)),
        out_shape=jax.ShapeDtypeStruct((t, n), BF16),
        scratch_shapes=[pltpu.VMEM((tm, d), BF16)],
        compiler_params=_cparams(("arbitrary", "arbitrary")),
        name="inproj",
    )(xa, xs, w_bf16)


def _attn_body(prev_ref, next_ref, kst_ref, off_ref,
               q_ref, kp_ref, kc_ref, kn_ref, vp_ref, vc_ref, vn_ref, bias_ref, o_ref,
               kbuf, vbuf, *, heads_per_group, head_dim):
    del prev_ref, next_ref
    b = pl.program_id(0)
    blk = ATT_ROWS * GRID_W
    kbuf[0:blk] = kp_ref[...]
    kbuf[blk:2 * blk] = kc_ref[...]
    kbuf[2 * blk:3 * blk] = kn_ref[...]
    vbuf[0:blk] = vp_ref[...]
    vbuf[blk:2 * blk] = vc_ref[...]
    vbuf[2 * blk:3 * blk] = vn_ref[...]

    gw = heads_per_group * head_dim
    n_groups = q_ref.shape[1] // gw
    win = WIN_R * GRID_W
    nq = heads_per_group * GRID_W
    row_head = lax.broadcasted_iota(I32, (nq, gw), 0) // GRID_W
    lane_head = lax.broadcasted_iota(I32, (nq, gw), 1) // head_dim
    head_mask = row_head == lane_head
    out_lane_head = lax.broadcasted_iota(I32, (GRID_W, gw), 1) // head_dim

    def row_step(i, carry):
        st = pl.multiple_of(kst_ref[b * ATT_ROWS + i] * GRID_W, GRID_W)
        dr0 = (WIN_R - 1) - off_ref[b * ATT_ROWS + i]
        q0 = pl.multiple_of(i * GRID_W, GRID_W)
        for g in range(n_groups):
            qg = q_ref[pl.ds(q0, GRID_W), g * gw:(g + 1) * gw]
            kg = kbuf[pl.ds(st, win), g * gw:(g + 1) * gw]
            vg = vbuf[pl.ds(st, win), g * gw:(g + 1) * gw]
            wt = jnp.concatenate([qg] * heads_per_group, axis=0)
            wt = jnp.where(head_mask, wt, jnp.zeros_like(wt))
            s = lax.dot_general(wt, kg, (((1,), (1,)), ((), ())),
                                preferred_element_type=F32)
            n_chunk = win // LANES
            sj = [s[:, j * LANES:(j + 1) * LANES] + bias_ref[dr0 + 2 * j, g] for j in range(n_chunk)]
            m = sj[0]
            for j in range(1, n_chunk):
                m = jnp.maximum(m, sj[j])
            m = jnp.max(m, axis=1, keepdims=True)
            ej = [jnp.exp(x - m) for x in sj]
            l = ej[0]
            for j in range(1, n_chunk):
                l = l + ej[j]
            inv = 1.0 / jnp.sum(l, axis=1, keepdims=True)
            p = jnp.concatenate([x.astype(BF16) for x in ej], axis=1)
            of = jnp.dot(p, vg, preferred_element_type=F32) * inv
            out = jnp.zeros((GRID_W, gw), F32)
            for h in range(heads_per_group):
                out = out + jnp.where(out_lane_head == h, of[h * GRID_W:(h + 1) * GRID_W, :], 0.0)
            o_ref[pl.ds(q0, GRID_W), g * gw:(g + 1) * gw] = out.astype(o_ref.dtype)
        return carry

    lax.fori_loop(0, ATT_ROWS, row_step, 0, unroll=4)


def _attn_tables(images):
    prev, nxt, kst, off = [], [], [], []
    for s, n in images:
        rows = n * ATT_ROWS
        kr = min(WIN_R, rows)
        assert kr == WIN_R
        for bl in range(n):
            b = s + bl
            prev.append(max(b - 1, s))
            nxt.append(min(b + 1, s + n - 1))
            for i in range(ATT_ROWS):
                r = bl * ATT_ROWS + i
                rs = min(max(r - kr // 2, 0), rows - kr)
                kst.append(rs - bl * ATT_ROWS + ATT_ROWS)
                off.append(r - rs)
    return (np.asarray(prev, np.int32), np.asarray(nxt, np.int32),
            np.asarray(kst, np.int32), np.asarray(off, np.int32))


def _attn_bias_table(rpb, heads_per_group):
    n_heads, n_dr, n_dc = rpb.shape
    n_pair = n_dr - 1
    c = np.arange(GRID_W)
    cs = np.clip(c - WIN_C // 2, 0, GRID_W - WIN_C)
    kc = np.arange(GRID_W)
    valid = (kc[None, :] >= cs[:, None]) & (kc[None, :] < cs[:, None] + WIN_C)
    dc = np.clip(kc[None, :] - c[:, None] + (WIN_C - 1), 0, n_dc - 1)
    sel = np.zeros((2, n_dc, GRID_W, 2, GRID_W), np.float32)
    ci, ki = np.nonzero(valid)
    for p in range(2):
        sel[p, dc[ci, ki], ci, p, ki] = 1.0
    sel = jnp.asarray(sel.reshape(2 * n_dc, GRID_W * 2 * GRID_W))
    r = rpb.astype(F32)
    pair = jnp.stack([r[:, :n_pair], r[:, 1:]], axis=2)
    pair = jnp.transpose(pair, (1, 0, 2, 3)).reshape(n_pair * n_heads, 2 * n_dc)
    tab = jnp.dot(pair, sel, precision=lax.Precision.HIGHEST)
    tab = tab.reshape(n_pair, n_heads, GRID_W, 2 * GRID_W)
    valid2 = jnp.asarray(np.tile(valid[:, None, :], (1, 2, 1)).reshape(GRID_W, 2 * GRID_W))
    tab = jnp.where(valid2[None, None], tab, NEG_BIG)
    return tab.reshape(n_pair, n_heads // heads_per_group, heads_per_group * GRID_W, 2 * GRID_W)


def _attention(h, bias_tab, images, *, q_block, n_heads, head_dim):
    t = h.shape[0]
    blk = ATT_ROWS * GRID_W
    aw = n_heads * head_dim
    heads_per_group = 2 * LANES // head_dim
    prev, nxt, kst, off = _attn_tables(images)
    n_blocks = t // blk
    assert prev.shape[0] == n_blocks
    kcol, vcol = q_block + 1, q_block + 2
    spec = lambda col, which: pl.BlockSpec(
        (blk, aw), {"cur": lambda b, p, n, k, o: (b, col),
                    "prev": lambda b, p, n, k, o: (p[b], col),
                    "next": lambda b, p, n, k, o: (n[b], col)}[which])
    grid_spec = pltpu.PrefetchScalarGridSpec(
        num_scalar_prefetch=4,
        grid=(n_blocks,),
        in_specs=[spec(q_block, "cur"),
                  spec(kcol, "prev"), spec(kcol, "cur"), spec(kcol, "next"),
                  spec(vcol, "prev"), spec(vcol, "cur"), spec(vcol, "next"),
                  pl.BlockSpec(bias_tab.shape, lambda b, p, n, k, o: (0, 0, 0, 0),
                               pipeline_mode=pl.Buffered(1))],
        out_specs=pl.BlockSpec((blk, aw), lambda b, p, n, k, o: (b, 0)),
        scratch_shapes=[pltpu.VMEM((3 * blk, aw), BF16), pltpu.VMEM((3 * blk, aw), BF16)],
    )
    return pl.pallas_call(
        functools.partial(_attn_body, heads_per_group=heads_per_group, head_dim=head_dim),
        grid_spec=grid_spec,
        out_shape=jax.ShapeDtypeStruct((t, aw), BF16),
        compiler_params=_cparams(("arbitrary",)),
        name="natten",
    )(jnp.asarray(prev), jnp.asarray(nxt), jnp.asarray(kst), jnp.asarray(off),
      h, h, h, h, h, h, h, bias_tab)


def _layer_norm(r, g, b):
    mu = jnp.mean(r, axis=-1, keepdims=True)
    c = r - mu
    var = jnp.mean(c * c, axis=-1, keepdims=True)
    return c * lax.rsqrt(var + LN_EPS) * g + b


HI_MASK = -65536


def _pack_bf16_pair(lo, hi):
    lo_bits = lax.bitcast_convert_type(lo.astype(BF16).astype(F32), I32)
    hi_bits = lax.bitcast_convert_type(hi.astype(BF16).astype(F32), I32)
    return (hi_bits & HI_MASK) | lax.shift_right_logical(lo_bits, 16)


def _unpack_bf16_pair(word):
    lo = lax.bitcast_convert_type(lax.shift_left(word, 16), F32)
    hi = lax.bitcast_convert_type(word & HI_MASK, F32)
    return lo, hi


def _first_index_of(mask, idx, sentinel, axis):
    return jnp.min(jnp.where(mask, idx, sentinel), axis=axis, keepdims=True)


def _mixer_body(seq_lo_ref, seq_hi_ref,
                xa_ref, xs_ref, up_ref, uc_ref, un_ref, gp_ref, ga_ref, at_ref,
                pw_ref, pb_ref, ps_ref, wpb_ref, wab_ref, wo_ref, g1_ref, b1_ref,
                rwh_ref, rwl_ref, rb_ref,
                x1_ref, x1f_ref, idx_ref, gate_ref, rank_ref, cnt_ref,
                carry_ref, *, n_a, alpha, n_experts):
    i = pl.program_id(0)
    tm = xa_ref.shape[0]
    t0 = i * tm
    lo_seq = seq_lo_ref[i]
    hi_seq = seq_hi_ref[i]

    @pl.when(i == 0)
    def _():
        carry_ref[...] = jnp.zeros_like(carry_ref)

    u_ext = jnp.concatenate([up_ref[...], uc_ref[...], un_ref[...]], axis=0)
    ext = tm + 2 * POOL_HALO
    tok_r = t0 + lax.broadcasted_iota(I32, (tm, ext), 0)
    tok_c = t0 - POOL_HALO + lax.broadcasted_iota(I32, (tm, ext), 1)
    tok_q = t0 + lax.broadcasted_iota(I32, (tm, 1), 0)
    n_pg = pw_ref.shape[0]
    pg = pw_ref.shape[1]
    parts = []
    for g in range(n_pg):
        w = POOL_WINDOWS[g]
        lo = jnp.maximum(tok_r - w // 2, lo_seq)
        hi = jnp.minimum(tok_r - w // 2 + w, hi_seq)
        band = jnp.where((tok_c >= lo) & (tok_c < hi), 1.0, 0.0).astype(BF16)
        cnt = (jnp.minimum(tok_q - w // 2 + w, hi_seq) - jnp.maximum(tok_q - w // 2, lo_seq)).astype(F32)
        ug = u_ext[:, g * pg:(g + 1) * pg]
        sums = jnp.dot(band, ug, preferred_element_type=F32)
        z = sums / cnt - uc_ref[:, g * pg:(g + 1) * pg].astype(F32)
        y = jnp.dot(z.astype(BF16), pw_ref[g], preferred_element_type=F32)
        y = (y + pb_ref[:, g * pg:(g + 1) * pg]) * ps_ref[:, g * pg:(g + 1) * pg]
        parts.append(y.astype(BF16))
    a = jnp.concatenate(parts, axis=1)
    a_pool = jnp.dot(a, wpb_ref[...], preferred_element_type=F32)
    a_attn = jnp.dot(at_ref[...], wab_ref[...], preferred_element_type=F32)
    merged = gp_ref[...].astype(F32) * a_pool + ga_ref[...].astype(F32) * a_attn
    mix = jnp.dot(merged.astype(BF16), wo_ref[...], preferred_element_type=F32)
    x_in = jnp.where(i < n_a, xa_ref[...], xs_ref[...])
    x1 = _layer_norm(alpha * x_in + mix, g1_ref[...], b1_ref[...])
    half = x1.shape[1] // 2
    n_pk = half // LANES
    word = _pack_bf16_pair(x1[:, :half], x1[:, half:])
    for c in range(n_pk):
        x1_ref[pl.ds(c, tm, stride=n_pk), :] = word[:, c * LANES:(c + 1) * LANES]
    x1f_ref[...] = x1
    x1b = x1.astype(BF16)

    x1l = (x1 - x1b.astype(F32)).astype(BF16)
    nt = (((1,), (1,)), ((), ()))
    logits = (lax.dot_general(rwh_ref[...], x1b, nt, preferred_element_type=F32)
              + lax.dot_general(rwh_ref[...], x1l, nt, preferred_element_type=F32)
              + lax.dot_general(rwl_ref[...], x1b, nt, preferred_element_type=F32))
    scores = jax.nn.sigmoid(logits)
    biased = scores + rb_ref[...]
    gsz = n_experts // N_GROUPS
    neg_inf = -jnp.inf

    mem = lax.broadcasted_iota(I32, (gsz, tm), 0)
    gs_rows = []
    for g in range(N_GROUPS):
        bg = biased[g * gsz:(g + 1) * gsz, :]
        m1 = jnp.max(bg, axis=0, keepdims=True)
        first = _first_index_of(bg == m1, mem, gsz, 0)
        m2 = jnp.max(jnp.where(mem == first, neg_inf, bg), axis=0, keepdims=True)
        gs_rows.append(m1 + m2)
    gwork = jnp.concatenate(gs_rows, axis=0)

    gidx = lax.broadcasted_iota(I32, (N_GROUPS, tm), 0)
    egrp = lax.broadcasted_iota(I32, (n_experts, tm), 0) // gsz
    eidx = lax.broadcasted_iota(I32, (n_experts, tm), 0)
    work = jnp.full((n_experts, tm), neg_inf, F32)
    for _ in range(TOPK_GROUPS):
        gm = jnp.max(gwork, axis=0, keepdims=True)
        gf = _first_index_of(gwork == gm, gidx, N_GROUPS, 0)
        gwork = jnp.where(gidx == gf, neg_inf, gwork)
        work = jnp.where(egrp == gf, biased, work)

    hot = jnp.zeros((n_experts, tm), F32)
    sel_idx, sel_w, sel_hit = [], [], []
    for _ in range(TOP_K):
        em = jnp.max(work, axis=0, keepdims=True)
        ef = _first_index_of(work == em, eidx, n_experts, 0)
        hit = eidx == ef
        sel_idx.append(ef)
        sel_w.append(jnp.sum(jnp.where(hit, scores, 0.0), axis=0, keepdims=True))
        sel_hit.append(hit)
        hot = jnp.where(hit, 1.0, hot)
        work = jnp.where(hit, neg_inf, work)
    wsel = jnp.concatenate(sel_w, axis=0)
    gate_ref[...] = wsel / jnp.sum(wsel, axis=0, keepdims=True) * ROUTED_SCALE
    idx_ref[...] = jnp.concatenate(sel_idx, axis=0)

    tri = jnp.where(lax.broadcasted_iota(I32, (tm, tm), 0) < lax.broadcasted_iota(I32, (tm, tm), 1),
                    1.0, 0.0).astype(BF16)
    before = carry_ref[:, 0:1] + jnp.dot(hot.astype(BF16), tri, preferred_element_type=F32)
    rank = [jnp.sum(jnp.where(h, before, 0.0), axis=0, keepdims=True) for h in sel_hit]
    rank_ref[...] = jnp.concatenate(rank, axis=0).astype(I32)
    carry_ref[...] = carry_ref[...] + jnp.sum(hot, axis=1, keepdims=True)
    cnt_ref[...] = carry_ref[...]


def _mixer(xa, xs, h, attn, seq_lo, seq_hi, p, *, alpha, u_block, gate_block0):
    d = xa.shape[1]
    t = xa.shape[0] + xs.shape[0]
    tm = MIX_TM
    assert xa.shape[0] % tm == 0 and xs.shape[0] % tm == 0
    n_a = xa.shape[0] // tm
    map_a, map_s = _two_array_maps(n_a)
    pw = p["wpb"].shape[0]
    aw = attn.shape[1]
    n_experts = p["rwh"].shape[0]
    halo_per_tile = tm // POOL_HALO
    n_halo_blocks = t // POOL_HALO
    gp_col = gate_block0 * INPROJ_TN // d
    in_specs = [
        pl.BlockSpec((tm, d), map_a),
        pl.BlockSpec((tm, d), map_s),
        pl.BlockSpec((POOL_HALO, pw), lambda i, lo, hi: (jnp.maximum(i * halo_per_tile - 1, 0), u_block)),
        pl.BlockSpec((tm, pw), lambda i, lo, hi: (i, u_block)),
        pl.BlockSpec((POOL_HALO, pw),
                     lambda i, lo, hi: (jnp.minimum((i + 1) * halo_per_tile, n_halo_blocks - 1), u_block)),
        pl.BlockSpec((tm, d), lambda i, lo, hi: (i, gp_col)),
        pl.BlockSpec((tm, d), lambda i, lo, hi: (i, gp_col + 1)),
        pl.BlockSpec((tm, aw), lambda i, lo, hi: (i, 0)),
    ]
    weights = [p["pool_w"], p["pool_b"], p["pool_scale"], p["wpb"], p["wab"], p["wo"], p["ln1_g"], p["ln1_b"],
               p["rwh"], p["rwl"], p["rb"]]
    in_specs += [pl.BlockSpec(w.shape, (lambda nd: (lambda i, lo, hi: (0,) * nd))(w.ndim),
                              pipeline_mode=pl.Buffered(1)) for w in weights]
    n_pk = d // (2 * LANES)
    out_shape = (jax.ShapeDtypeStruct((t * n_pk, LANES), I32),
                 jax.ShapeDtypeStruct((t, d), F32),
                 jax.ShapeDtypeStruct((TOP_K, t), I32),
                 jax.ShapeDtypeStruct((TOP_K, t), F32),
                 jax.ShapeDtypeStruct((TOP_K, t), I32),
                 jax.ShapeDtypeStruct((n_experts, LANES), F32))
    out_specs = (pl.BlockSpec((tm * n_pk, LANES), lambda i, lo, hi: (i, 0)),
                 pl.BlockSpec((tm, d), lambda i, lo, hi: (i, 0)),
                 pl.BlockSpec((TOP_K, tm), lambda i, lo, hi: (0, i)),
                 pl.BlockSpec((TOP_K, tm), lambda i, lo, hi: (0, i)),
                 pl.BlockSpec((TOP_K, tm), lambda i, lo, hi: (0, i)),
                 pl.BlockSpec((n_experts, LANES), lambda i, lo, hi: (0, 0)))
    grid_spec = pltpu.PrefetchScalarGridSpec(
        num_scalar_prefetch=2, grid=(t // tm,), in_specs=in_specs, out_specs=out_specs,
        scratch_shapes=[pltpu.VMEM((n_experts, LANES), F32)])
    return pl.pallas_call(
        functools.partial(_mixer_body, n_a=n_a, alpha=alpha, n_experts=n_experts),
        grid_spec=grid_spec, out_shape=out_shape,
        compiler_params=_cparams(("arbitrary",)),
        name="mixer",
    )(seq_lo, seq_hi, xa, xs, h, h, h, h, h, attn, *weights)


def _shared_ffn_body(x_ref, s13_ref, s2_ref, o_ref, *, alpha):
    x = x_ref[...]
    sd = s2_ref.shape[0]
    hc = jnp.dot(x.astype(BF16), s13_ref[...], preferred_element_type=F32)
    hdn = jax.nn.silu(hc[:, :sd]) * hc[:, sd:]
    o_ref[...] = alpha * x + jnp.dot(hdn.astype(BF16), s2_ref[...], preferred_element_type=F32)


def _shared_ffn(x1f, s13, s2, *, alpha):
    t, d = x1f.shape
    tm = SHARED_TM
    assert t % tm == 0
    return pl.pallas_call(
        functools.partial(_shared_ffn_body, alpha=alpha),
        grid=(t // tm,),
        in_specs=[pl.BlockSpec((tm, d), lambda i: (i, 0)),
                  pl.BlockSpec(s13.shape, lambda i: (0, 0), pipeline_mode=pl.Buffered(1)),
                  pl.BlockSpec(s2.shape, lambda i: (0, 0), pipeline_mode=pl.Buffered(1))],
        out_specs=pl.BlockSpec((tm, d), lambda i: (i, 0)),
        out_shape=jax.ShapeDtypeStruct((t, d), F32),
        compiler_params=_cparams(("arbitrary",)),
        name="shared_ffn",
    )(x1f, s13, s2)


SC_CORES = 2
SC_SUBCORES = 16
SC_CHUNK = 32


def _sc_dispatch(x1t3, dest3, n_rows):
    t, n_lt, _ = x1t3.shape
    n_workers = SC_CORES * SC_SUBCORES
    assert t % (n_workers * SC_CHUNK) == 0
    chunks_per_worker = t // (n_workers * SC_CHUNK)
    mesh = plsc.VectorSubcoreMesh(core_axis_name="c", subcore_axis_name="s",
                                  num_cores=SC_CORES, num_subcores=SC_SUBCORES)

    @functools.partial(
        pl.kernel, mesh=mesh,
        out_type=jax.ShapeDtypeStruct((n_rows, n_lt, LANES), x1t3.dtype),
        scratch_types=[pltpu.VMEM((TOP_K, SC_CHUNK), I32),
                       pltpu.VMEM((SC_CHUNK, n_lt, LANES), x1t3.dtype),
                       pltpu.SemaphoreType.DMA],
        name="sc_dispatch")
    def dispatch(x_hbm, dest_hbm, out_hbm, idx_v, rows_v, sem):
        wid = lax.axis_index("s") * SC_CORES + lax.axis_index("c")

        @pl.loop(0, chunks_per_worker)
        def _(j):
            chunk = wid * chunks_per_worker + j
            base = pl.multiple_of(chunk * SC_CHUNK, SC_CHUNK)
            pltpu.sync_copy(x_hbm.at[pl.ds(base, SC_CHUNK)], rows_v)
            pltpu.sync_copy(dest_hbm.at[chunk], idx_v)
            copies = [pltpu.make_async_copy(rows_v, out_hbm.at[idx_v.at[k]], sem) for k in range(TOP_K)]
            for cp in copies:
                cp.start()
            for cp in copies:
                cp.wait()

    return dispatch(x1t3, dest3)


SC_RING = 3


def _sc_gather(yb3, dest3, t):
    _, n_lt, _ = yb3.shape
    n_workers = SC_CORES * SC_SUBCORES
    chunks_per_worker = t // (n_workers * SC_CHUNK)
    mesh = plsc.VectorSubcoreMesh(core_axis_name="c", subcore_axis_name="s",
                                  num_cores=SC_CORES, num_subcores=SC_SUBCORES)

    @functools.partial(
        pl.kernel, mesh=mesh,
        out_type=jax.ShapeDtypeStruct((TOP_K * t, n_lt, LANES), yb3.dtype),
        scratch_types=[pltpu.VMEM((TOP_K, SC_CHUNK), I32),
                       pltpu.VMEM((SC_RING, SC_CHUNK, n_lt, LANES), yb3.dtype)]
                      + [pltpu.SemaphoreType.DMA] * (2 * SC_RING),
        name="sc_gather")
    def gather(y_hbm, dest_hbm, out_hbm, idx_v, rows_v, *sems):
        gsem, wsem = sems[:SC_RING], sems[SC_RING:]
        wid = lax.axis_index("s") * SC_CORES + lax.axis_index("c")

        @pl.loop(0, chunks_per_worker)
        def _(j):
            chunk = wid * chunks_per_worker + j
            base = pl.multiple_of(chunk * SC_CHUNK, SC_CHUNK)
            pltpu.sync_copy(dest_hbm.at[chunk], idx_v)
            gets = [pltpu.make_async_copy(y_hbm.at[idx_v.at[k]], rows_v.at[k % SC_RING], gsem[k % SC_RING])
                    for k in range(TOP_K)]
            puts = [pltpu.make_async_copy(rows_v.at[k % SC_RING], out_hbm.at[pl.ds(k * t + base, SC_CHUNK)],
                                          wsem[k % SC_RING]) for k in range(TOP_K)]
            put_waited = set()
            for k in range(SC_RING - 1):
                gets[k].start()
            for k in range(TOP_K):
                gets[k].wait()
                puts[k].start()
                nxt = k + SC_RING - 1
                if nxt < TOP_K:
                    if k >= 1:
                        puts[k - 1].wait()
                        put_waited.add(k - 1)
                    gets[nxt].start()
            for k in range(TOP_K):
                if k not in put_waited:
                    puts[k].wait()

    return gather(yb3, dest3)


def _experts_body(be_ref, nact_ref, first_ref, nxte_ref, wslot_ref, valid_ref,
                  xs_ref, w1_hbm, w3_hbm, w2_hbm, o_ref,
                  wst1, wst3, wst2, w13b, w2b, sem_w):
    b = pl.program_id(0)
    nact = nact_ref[0]
    ed = wst2.shape[1]
    d = wst2.shape[2]
    n_pk = d // (2 * LANES)
    bm = xs_ref.shape[0] // n_pk
    active = b < nact

    def weight_copies(e, s):
        return (pltpu.make_async_copy(w1_hbm.at[e], wst1.at[s], sem_w.at[s]),
                pltpu.make_async_copy(w3_hbm.at[e], wst3.at[s], sem_w.at[s]),
                pltpu.make_async_copy(w2_hbm.at[e], wst2.at[s], sem_w.at[s]))

    @pl.when(b == 0)
    def _():
        for cp in weight_copies(be_ref[0], wslot_ref[0]):
            cp.start()

    @pl.when(first_ref[b] == 1)
    def _():
        ws = wslot_ref[b]
        for cp in weight_copies(be_ref[b], ws):
            cp.wait()

        @pl.when(nxte_ref[b] >= 0)
        def _():
            for cp in weight_copies(nxte_ref[b], 1 - ws):
                cp.start(priority=1)

        for c in range(ed // MXU_N):
            w13b[:, 2 * c * MXU_N:(2 * c + 1) * MXU_N] = wst1[ws, :, c * MXU_N:(c + 1) * MXU_N].astype(BF16)
            w13b[:, (2 * c + 1) * MXU_N:(2 * c + 2) * MXU_N] = wst3[ws, :, c * MXU_N:(c + 1) * MXU_N].astype(BF16)
        w2b[...] = wst2[ws].astype(BF16)

    @pl.when(active)
    def _():
        keep = lax.broadcasted_iota(I32, (bm, LANES), 0) < valid_ref[b]
        los, his = [], []
        for c in range(n_pk):
            lo, hi = _unpack_bf16_pair(jnp.where(keep, xs_ref[pl.ds(c, bm, stride=n_pk), :], 0))
            los.append(lo.astype(BF16))
            his.append(hi.astype(BF16))
        xb = jnp.concatenate(los + his, axis=1)
        parts = []
        for c in range(ed // MXU_N):
            hc = jnp.dot(xb, w13b[:, 2 * c * MXU_N:(2 * c + 2) * MXU_N], preferred_element_type=F32)
            parts.append((jax.nn.silu(hc[:, :MXU_N]) * hc[:, MXU_N:]).astype(BF16))
        hdn = jnp.concatenate(parts, axis=1)
        n_down = d // MXU_N
        ys = [jnp.dot(hdn, w2b[:, c * MXU_N:(c + 1) * MXU_N], preferred_element_type=F32) for c in range(n_down)]
        tiles_per_chunk = MXU_N // LANES
        for c in range(n_down // 2):
            word = _pack_bf16_pair(ys[c], ys[c + n_down // 2])
            for h in range(tiles_per_chunk):
                o_ref[pl.ds(c * tiles_per_chunk + h, bm, stride=n_pk), :] = word[:, h * LANES:(h + 1) * LANES]

    @pl.when(jnp.logical_not(active))
    def _():
        o_ref[...] = jnp.zeros_like(o_ref)


def _experts(xs2d, block_e, nact, first, nxt_e, wslot, valid, w1, w3, w2, bm):
    d, ed = w1.shape[1], w1.shape[2]
    n_pk = d // (2 * LANES)
    n_blk = block_e.shape[0]
    assert xs2d.shape == (n_blk * bm * n_pk, LANES)
    assert ed % MXU_N == 0 and d % MXU_N == 0 and bm % SUBLANES == 0
    grid_spec = pltpu.PrefetchScalarGridSpec(
        num_scalar_prefetch=6,
        grid=(n_blk,),
        in_specs=[pl.BlockSpec((bm * n_pk, LANES),
                               lambda b, be, na, *_: (jnp.maximum(jnp.minimum(b, na[0] - 1), 0), 0)),
                  pl.BlockSpec(memory_space=pl.ANY),
                  pl.BlockSpec(memory_space=pl.ANY),
                  pl.BlockSpec(memory_space=pl.ANY)],
        out_specs=pl.BlockSpec((bm * n_pk, LANES), lambda b, *_: (b, 0)),
        scratch_shapes=[pltpu.VMEM((2, d, ed), F32),
                        pltpu.VMEM((2, d, ed), F32),
                        pltpu.VMEM((2, ed, d), F32),
                        pltpu.VMEM((d, 2 * ed), BF16),
                        pltpu.VMEM((ed, d), BF16),
                        pltpu.SemaphoreType.DMA((2,))],
    )
    return pl.pallas_call(
        _experts_body,
        grid_spec=grid_spec,
        out_shape=jax.ShapeDtypeStruct((n_blk * bm * n_pk, LANES), I32),
        compiler_params=_cparams(("arbitrary",)),
        name="experts",
    )(block_e, nact, first, nxt_e, wslot, valid, xs2d, w1, w3, w2)


def _finalize_body(base_ref, gate_ref, g2_ref, b2_ref, *refs, n_a):
    slabs, (oa_ref, os_ref) = refs[:TOP_K], refs[TOP_K:]
    i = pl.program_id(0)
    tm, d = base_ref.shape
    half = d // 2
    n_pk = half // LANES
    acc_lo = [base_ref[:, c * LANES:(c + 1) * LANES] for c in range(n_pk)]
    acc_hi = [base_ref[:, half + c * LANES:half + (c + 1) * LANES] for c in range(n_pk)]
    for k in range(TOP_K):
        g = gate_ref[:, k:k + 1]
        for c in range(n_pk):
            lo, hi = _unpack_bf16_pair(slabs[k][pl.ds(c, tm, stride=n_pk), :])
            acc_lo[c] = acc_lo[c] + g * lo
            acc_hi[c] = acc_hi[c] + g * hi
    y = _layer_norm(jnp.concatenate(acc_lo + acc_hi, axis=1), g2_ref[...], b2_ref[...])

    @pl.when(i < n_a)
    def _():
        oa_ref[...] = y

    @pl.when(i >= n_a)
    def _():
        os_ref[...] = y


def _finalize(base, gate_t, slabs2d, ln_g, ln_b, t_a):
    t, d = base.shape
    tm = FINAL_TM
    n_pk = d // (2 * LANES)
    assert t_a % tm == 0 and 0 < t_a < t and slabs2d.shape == (TOP_K * t * n_pk, LANES)
    n_a = t_a // tm
    n_steps = t // tm
    map_a, map_s = _two_array_maps(n_a)
    slab_specs = [pl.BlockSpec((tm * n_pk, LANES), (lambda k: (lambda i: (k * n_steps + i, 0)))(k))
                  for k in range(TOP_K)]
    return pl.pallas_call(
        functools.partial(_finalize_body, n_a=n_a),
        grid=(n_steps,),
        in_specs=[pl.BlockSpec((tm, d), lambda i: (i, 0)),
                  pl.BlockSpec((tm, TOP_K), lambda i: (i, 0)),
                  pl.BlockSpec((1, d), lambda i: (0, 0)),
                  pl.BlockSpec((1, d), lambda i: (0, 0))] + slab_specs,
        out_specs=(pl.BlockSpec((tm, d), map_a), pl.BlockSpec((tm, d), map_s)),
        out_shape=(jax.ShapeDtypeStruct((t_a, d), F32), jax.ShapeDtypeStruct((t - t_a, d), F32)),
        compiler_params=_cparams(("arbitrary",)),
        name="finalize",
    )(base, gate_t, ln_g, ln_b, *([slabs2d] * TOP_K))


def _split_hi_lo(w):
    hi = w.astype(BF16)
    lo = (w - hi.astype(F32)).astype(BF16)
    return hi, lo


def _encoder_layer(xa, xs, seqs, lp, alpha):
    (w_in, pool_w, pool_b, pool_scale, w_pool_br, w_attn_br, attn_rpb, w_out, ln1_g, ln1_b,
     router_w, router_bias, exp_w1, exp_w3, exp_w2, sh_w1, sh_w3, sh_w2, ln2_g, ln2_b) = lp
    d = xa.shape[1]
    t = xa.shape[0] + xs.shape[0]
    pw = w_pool_br.shape[0]
    aw = w_attn_br.shape[0]
    n_heads = attn_rpb.shape[0]
    head_dim = aw // n_heads
    n_experts = router_w.shape[1]
    assert pw == INPROJ_TN and aw == INPROJ_TN and d % INPROJ_TN == 0
    u_block, q_block = 0, pw // INPROJ_TN
    gate_block0 = (pw + 3 * aw) // INPROJ_TN

    h = _inproj(xa, xs, w_in.astype(BF16), q_block=q_block, gate_block0=gate_block0, q_scale=head_dim ** -0.5)

    blk = ATT_ROWS * GRID_W
    images, seq_lo, seq_hi = [], [], []
    for s0, s1 in seqs:
        images.append((s0 // blk, (s1 - s0) // blk))
        for _ in range((s1 - s0) // MIX_TM):
            seq_lo.append(s0)
            seq_hi.append(s1)
    bias_tab = _attn_bias_table(attn_rpb, 2 * LANES // head_dim)
    attn = _attention(h, bias_tab, images, q_block=q_block, n_heads=n_heads, head_dim=head_dim)

    rwh, rwl = _split_hi_lo(router_w.T)
    p = dict(pool_w=pool_w.astype(BF16), pool_b=pool_b.reshape(1, pw), pool_scale=pool_scale.reshape(1, pw),
             wpb=w_pool_br.astype(BF16), wab=w_attn_br.astype(BF16), wo=w_out.astype(BF16),
             ln1_g=ln1_g.reshape(1, d), ln1_b=ln1_b.reshape(1, d),
             rwh=rwh, rwl=rwl, rb=router_bias.reshape(n_experts, 1).astype(F32))
    x1, x1f, idx_t, gate_t, rank_t, counts = _mixer(
        xa, xs, h, attn, jnp.asarray(seq_lo, I32), jnp.asarray(seq_hi, I32), p,
        alpha=alpha, u_block=u_block, gate_block0=gate_block0)

    bm = EXPERT_BM
    cnt = counts[:, 0].astype(I32)
    pcnt = (cnt + bm - 1) // bm * bm
    pend = jnp.cumsum(pcnt)
    pstart = pend - pcnt
    eids = jnp.arange(n_experts, dtype=I32)[:, None, None]
    dest = jnp.sum(jnp.where(idx_t[None] == eids, pstart[:, None, None], 0), axis=0) + rank_t
    n_blk = (t * TOP_K + n_experts * (bm - 1)) // bm
    blk_row = jnp.arange(n_blk, dtype=I32) * bm
    block_e = jnp.minimum(jnp.sum((pend[None, :] <= blk_row[:, None]).astype(I32), axis=1), n_experts - 1)
    nact = (pend[-1] // bm).astype(I32).reshape(1)
    block_e = jnp.where(jnp.arange(n_blk) < nact[0], block_e, block_e[jnp.maximum(nact[0] - 1, 0)])
    eids1 = jnp.arange(n_experts, dtype=I32)
    present = cnt > 0
    later = present[None, :] & (eids1[None, :] > eids1[:, None])
    nxt_of_e = jnp.min(jnp.where(later, eids1[None, :], n_experts), axis=1)
    nxt_of_e = jnp.where(nxt_of_e >= n_experts, -1, nxt_of_e).astype(I32)
    ord_of_e = (jnp.cumsum(present.astype(I32)) - 1).astype(I32)
    onehot_be = (block_e[:, None] == eids1[None, :]).astype(I32)
    blk_ids = jnp.arange(n_blk, dtype=I32)
    is_act = blk_ids < nact[0]
    prev_e = jnp.concatenate([jnp.full((1,), -1, I32), block_e[:-1]])
    first = (is_act & (block_e != prev_e)).astype(I32)
    nxt_e = jnp.sum(onehot_be * nxt_of_e[None, :], axis=1).astype(I32)
    wslot = (jnp.sum(onehot_be * ord_of_e[None, :], axis=1) % 2).astype(I32)
    start_of_e = jnp.sum(onehot_be * pstart[None, :], axis=1)
    cnt_of_e = jnp.sum(onehot_be * cnt[None, :], axis=1)
    valid = jnp.where(is_act, jnp.clip(cnt_of_e - (blk_row - start_of_e), 0, bm), 0).astype(I32)
    n_pk = d // (2 * LANES)
    dest3 = jnp.transpose(dest.reshape(TOP_K, t // SC_CHUNK, SC_CHUNK), (1, 0, 2))
    xs = _sc_dispatch(x1.reshape(t, n_pk, LANES), dest3, n_blk * bm)
    base = _shared_ffn(x1f, jnp.concatenate([sh_w1, sh_w3], axis=1).astype(BF16), sh_w2.astype(BF16), alpha=alpha)
    yb = _experts(xs.reshape(n_blk * bm * n_pk, LANES), block_e, nact, first, nxt_e, wslot, valid,
                  exp_w1, exp_w3, exp_w2, bm)

    slabs = _sc_gather(yb.reshape(n_blk * bm, n_pk, LANES), dest3, t)
    return _finalize(base, gate_t.T, slabs.reshape(TOP_K * t * n_pk, LANES),
                     ln2_g.reshape(1, d), ln2_b.reshape(1, d), xa.shape[0])


def kernel(x_prompt, x_sample, w_in, pool_w, pool_b, pool_scale, w_pool_br, w_attn_br, attn_rpb, w_out,
           ln1_g, ln1_b, router_w, router_bias, exp_w1, exp_w3, exp_w2, sh_w1, sh_w3, sh_w2, ln2_g, ln2_b):
    depth = w_in.shape[0]
    alpha = (2.0 * depth) ** 0.25
    bp, sp, d = x_prompt.shape
    bs, ss, _ = x_sample.shape
    seqs = [(i * sp, (i + 1) * sp) for i in range(bp)]
    seqs += [(bp * sp + i * ss, bp * sp + (i + 1) * ss) for i in range(bs)]
    xa, xs = x_prompt.reshape(bp * sp, d), x_sample.reshape(bs * ss, d)
    for l in range(depth):
        lp = (w_in[l], pool_w[l], pool_b[l], pool_scale[l], w_pool_br[l], w_attn_br[l], attn_rpb[l], w_out[l],
              ln1_g[l], ln1_b[l], router_w[l], router_bias[l], exp_w1[l], exp_w3[l], exp_w2[l],
              sh_w1[l], sh_w3[l], sh_w2[l], ln2_g[l], ln2_b[l])
        xa, xs = _encoder_layer(xa, xs, seqs, lp, alpha)
    return (xa.reshape(bp, sp, d), xs.reshape(bs, ss, d))
```

```python
import functools

import numpy as np
import jax
import jax.numpy as jnp
from jax import lax
from jax.experimental import pallas as pl
from jax.experimental.pallas import tpu as pltpu
from jax.experimental.pallas import tpu_sc as plsc

F32 = jnp.float32
BF16 = jnp.bfloat16
I32 = jnp.int32

GRID_W = 64
POOL_WINDOWS = (2, 4, 8, 16)
WIN_R = 8
WIN_C = 16
TOP_K = 8
N_GROUPS = 8
TOPK_GROUPS = 4
ROUTED_SCALE = 2.5
LN_EPS = 1e-5
NEG_BIG = -1e30

LANES = 128
SUBLANES = 8
MXU_N = 256
VMEM_LIMIT = 56 * 1024 * 1024

INPROJ_TM = 1024
INPROJ_TN = 1024
ATT_ROWS = 8
MIX_TM = 256
POOL_HALO = 16
EXPERT_BM = 512
FINAL_TM = 256


def _cparams(sem):
    return pltpu.CompilerParams(dimension_semantics=sem, vmem_limit_bytes=VMEM_LIMIT)


def _inproj_body(xa_ref, xs_ref, w_ref, o_ref, xb_ref, *, n_a, q_block, gate_block0, q_scale):
    i = pl.program_id(0)
    j = pl.program_id(1)

    @pl.when((j == 0) & (i < n_a))
    def _():
        xb_ref[...] = xa_ref[...].astype(BF16)

    @pl.when((j == 0) & (i >= n_a))
    def _():
        xb_ref[...] = xs_ref[...].astype(BF16)

    is_gate = j >= gate_block0
    scale = jnp.where(j == q_block, q_scale, 1.0).astype(F32)
    for c in range(w_ref.shape[1] // MXU_N):
        acc = jnp.dot(xb_ref[...], w_ref[:, c * MXU_N:(c + 1) * MXU_N], preferred_element_type=F32)
        out = jnp.where(is_gate, jax.nn.sigmoid(acc), acc * scale)
        o_ref[:, c * MXU_N:(c + 1) * MXU_N] = out.astype(BF16)


def _two_array_maps(n_a):
    return (lambda i, *_: (jnp.minimum(i, n_a - 1), 0)), (lambda i, *_: (jnp.maximum(i - n_a, 0), 0))


def _inproj(xa, xs, w_bf16, *, q_block, gate_block0, q_scale):
    d = xa.shape[1]
    t = xa.shape[0] + xs.shape[0]
    n = w_bf16.shape[1]
    tm, tn = INPROJ_TM, INPROJ_TN
    assert xa.shape[0] % tm == 0 and xs.shape[0] % tm == 0 and xa.shape[0] > 0 and xs.shape[0] > 0
    n_a = xa.shape[0] // tm
    map_a, map_s = _two_array_maps(n_a)
    return pl.pallas_call(
        functools.partial(_inproj_body, n_a=n_a, q_block=q_block, gate_block0=gate_block0, q_scale=q_scale),
        grid=(t // tm, n // tn),
        in_specs=[pl.BlockSpec((tm, d), map_a),
                  pl.BlockSpec((tm, d), map_s),
                  pl.BlockSpec((d, tn), lambda i, j: (0, j))],
        out_specs=pl.BlockSpec((tm, tn), lambda i, j: (i, j)),
        out_shape=jax.ShapeDtypeStruct((t, n), BF16),
        scratch_shapes=[pltpu.VMEM((tm, d), BF16)],
        compiler_params=_cparams(("arbitrary", "arbitrary")),
        name="inproj",
    )(xa, xs, w_bf16)


def _attn_body(prev_ref, next_ref, kst_ref, off_ref,
               q_ref, kp_ref, kc_ref, kn_ref, vp_ref, vc_ref, vn_ref, bias_ref, o_ref,
               kbuf, vbuf, *, heads_per_group, head_dim):
    del prev_ref, next_ref
    b = pl.program_id(0)
    blk = ATT_ROWS * GRID_W
    kbuf[0:blk] = kp_ref[...]
    kbuf[blk:2 * blk] = kc_ref[...]
    kbuf[2 * blk:3 * blk] = kn_ref[...]
    vbuf[0:blk] = vp_ref[...]
    vbuf[blk:2 * blk] = vc_ref[...]
    vbuf[2 * blk:3 * blk] = vn_ref[...]

    gw = heads_per_group * head_dim
    n_groups = q_ref.shape[1] // gw
    win = WIN_R * GRID_W
    nq = heads_per_group * GRID_W
    row_head = lax.broadcasted_iota(I32, (nq, gw), 0) // GRID_W
    lane_head = lax.broadcasted_iota(I32, (nq, gw), 1) // head_dim
    head_mask = row_head == lane_head
    out_lane_head = lax.broadcasted_iota(I32, (GRID_W, gw), 1) // head_dim

    def row_step(i, carry):
        st = pl.multiple_of(kst_ref[b * ATT_ROWS + i] * GRID_W, GRID_W)
        dr0 = (WIN_R - 1) - off_ref[b * ATT_ROWS + i]
        q0 = pl.multiple_of(i * GRID_W, GRID_W)
        for g in range(n_groups):
            qg = q_ref[pl.ds(q0, GRID_W), g * gw:(g + 1) * gw]
            kg = kbuf[pl.ds(st, win), g * gw:(g + 1) * gw]
            vg = vbuf[pl.ds(st, win), g * gw:(g + 1) * gw]
            wt = jnp.concatenate([qg] * heads_per_group, axis=0)
            wt = jnp.where(head_mask, wt, jnp.zeros_like(wt))
            s = lax.dot_general(wt, kg, (((1,), (1,)), ((), ())),
                                preferred_element_type=F32)
            n_chunk = win // LANES
            sj = [s[:, j * LANES:(j + 1) * LANES] + bias_ref[dr0 + 2 * j, g] for j in range(n_chunk)]
            m = sj[0]
            for j in range(1, n_chunk):
                m = jnp.maximum(m, sj[j])
            m = jnp.max(m, axis=1, keepdims=True)
            ej = [jnp.exp(x - m) for x in sj]
            l = ej[0]
            for j in range(1, n_chunk):
                l = l + ej[j]
            inv = 1.0 / jnp.sum(l, axis=1, keepdims=True)
            p = jnp.concatenate([x.astype(BF16) for x in ej], axis=1)
            of = jnp.dot(p, vg, preferred_element_type=F32) * inv
            out = jnp.zeros((GRID_W, gw), F32)
            for h in range(heads_per_group):
                out = out + jnp.where(out_lane_head == h, of[h * GRID_W:(h + 1) * GRID_W, :], 0.0)
            o_ref[pl.ds(q0, GRID_W), g * gw:(g + 1) * gw] = out.astype(o_ref.dtype)
        return carry

    lax.fori_loop(0, ATT_ROWS, row_step, 0, unroll=4)


def _attn_tables(images):
    prev, nxt, kst, off = [], [], [], []
    for s, n in images:
        rows = n * ATT_ROWS
        kr = min(WIN_R, rows)
        assert kr == WIN_R
        for bl in range(n):
            b = s + bl
            prev.append(max(b - 1, s))
            nxt.append(min(b + 1, s + n - 1))
            for i in range(ATT_ROWS):
                r = bl * ATT_ROWS + i
                rs = min(max(r - kr // 2, 0), rows - kr)
                kst.append(rs - bl * ATT_ROWS + ATT_ROWS)
                off.append(r - rs)
    return (np.asarray(prev, np.int32), np.asarray(nxt, np.int32),
            np.asarray(kst, np.int32), np.asarray(off, np.int32))


def _attn_bias_table(rpb, heads_per_group):
    n_heads, n_dr, n_dc = rpb.shape
    n_pair = n_dr - 1
    c = np.arange(GRID_W)
    cs = np.clip(c - WIN_C // 2, 0, GRID_W - WIN_C)
    kc = np.arange(GRID_W)
    valid = (kc[None, :] >= cs[:, None]) & (kc[None, :] < cs[:, None] + WIN_C)
    dc = np.clip(kc[None, :] - c[:, None] + (WIN_C - 1), 0, n_dc - 1)
    sel = np.zeros((2, n_dc, GRID_W, 2, GRID_W), np.float32)
    ci, ki = np.nonzero(valid)
    for p in range(2):
        sel[p, dc[ci, ki], ci, p, ki] = 1.0
    sel = jnp.asarray(sel.reshape(2 * n_dc, GRID_W * 2 * GRID_W))
    r = rpb.astype(F32)
    pair = jnp.stack([r[:, :n_pair], r[:, 1:]], axis=2)
    pair = jnp.transpose(pair, (1, 0, 2, 3)).reshape(n_pair * n_heads, 2 * n_dc)
    tab = jnp.dot(pair, sel, precision=lax.Precision.HIGHEST)
    tab = tab.reshape(n_pair, n_heads, GRID_W, 2 * GRID_W)
    valid2 = jnp.asarray(np.tile(valid[:, None, :], (1, 2, 1)).reshape(GRID_W, 2 * GRID_W))
    tab = jnp.where(valid2[None, None], tab, NEG_BIG)
    return tab.reshape(n_pair, n_heads // heads_per_group, heads_per_group * GRID_W, 2 * GRID_W)


def _attention(h, bias_tab, images, *, q_block, n_heads, head_dim):
    t = h.shape[0]
    blk = ATT_ROWS * GRID_W
    aw = n_heads * head_dim
    heads_per_group = 2 * LANES // head_dim
    prev, nxt, kst, off = _attn_tables(images)
    n_blocks = t // blk
    assert prev.shape[0] == n_blocks
    kcol, vcol = q_block + 1, q_block + 2
    spec = lambda col, which: pl.BlockSpec(
        (blk, aw), {"cur": lambda b, p, n, k, o: (b, col),
                    "prev": lambda b, p, n, k, o: (p[b], col),
                    "next": lambda b, p, n, k, o: (n[b], col)}[which])
    grid_spec = pltpu.PrefetchScalarGridSpec(
        num_scalar_prefetch=4,
        grid=(n_blocks,),
        in_specs=[spec(q_block, "cur"),
                  spec(kcol, "prev"), spec(kcol, "cur"), spec(kcol, "next"),
                  spec(vcol, "prev"), spec(vcol, "cur"), spec(vcol, "next"),
                  pl.BlockSpec(bias_tab.shape, lambda b, p, n, k, o: (0, 0, 0, 0),
                               pipeline_mode=pl.Buffered(1))],
        out_specs=pl.BlockSpec((blk, aw), lambda b, p, n, k, o: (b, 0)),
        scratch_shapes=[pltpu.VMEM((3 * blk, aw), BF16), pltpu.VMEM((3 * blk, aw), BF16)],
    )
    return pl.pallas_call(
        functools.partial(_attn_body, heads_per_group=heads_per_group, head_dim=head_dim),
        grid_spec=grid_spec,
        out_shape=jax.ShapeDtypeStruct((t, aw), BF16),
        compiler_params=_cparams(("arbitrary",)),
        name="natten",
    )(jnp.asarray(prev), jnp.asarray(nxt), jnp.asarray(kst), jnp.asarray(off),
      h, h, h, h, h, h, h, bias_tab)


def _layer_norm(r, g, b):
    mu = jnp.mean(r, axis=-1, keepdims=True)
    c = r - mu
    var = jnp.mean(c * c, axis=-1, keepdims=True)
    return c * lax.rsqrt(var + LN_EPS) * g + b


HI_MASK = -65536


def _pack_bf16_pair(lo, hi):
    lo_bits = lax.bitcast_convert_type(lo.astype(BF16).astype(F32), I32)
    hi_bits = lax.bitcast_convert_type(hi.astype(BF16).astype(F32), I32)
    return (hi_bits & HI_MASK) | lax.shift_right_logical(lo_bits, 16)


def _unpack_bf16_pair(word):
    lo = lax.bitcast_convert_type(lax.shift_left(word, 16), F32)
    hi = lax.bitcast_convert_type(word & HI_MASK, F32)
    return lo, hi


def _first_index_of(mask, idx, sentinel, axis):
    return jnp.min(jnp.where(mask, idx, sentinel), axis=axis, keepdims=True)


def _mixer_body(seq_lo_ref, seq_hi_ref,
                xa_ref, xs_ref, up_ref, uc_ref, un_ref, gp_ref, ga_ref, at_ref,
                pw_ref, pb_ref, ps_ref, wpb_ref, wab_ref, wo_ref, g1_ref, b1_ref,
                rwh_ref, rwl_ref, rb_ref,
                x1_ref, x1f_ref, idx_ref, gate_ref, rank_ref, cnt_ref,
                carry_ref, *, n_a, alpha, n_experts):
    i = pl.program_id(0)
    tm = xa_ref.shape[0]
    t0 = i * tm
    lo_seq = seq_lo_ref[i]
    hi_seq = seq_hi_ref[i]

    @pl.when(i == 0)
    def _():
        carry_ref[...] = jnp.zeros_like(carry_ref)

    u_ext = jnp.concatenate([up_ref[...], uc_ref[...], un_ref[...]], axis=0)
    ext = tm + 2 * POOL_HALO
    tok_r = t0 + lax.broadcasted_iota(I32, (tm, ext), 0)
    tok_c = t0 - POOL_HALO + lax.broadcasted_iota(I32, (tm, ext), 1)
    tok_q = t0 + lax.broadcasted_iota(I32, (tm, 1), 0)
    n_pg = pw_ref.shape[0]
    pg = pw_ref.shape[1]
    parts = []
    for g in range(n_pg):
        w = POOL_WINDOWS[g]
        lo = jnp.maximum(tok_r - w // 2, lo_seq)
        hi = jnp.minimum(tok_r - w // 2 + w, hi_seq)
        band = jnp.where((tok_c >= lo) & (tok_c < hi), 1.0, 0.0).astype(BF16)
        cnt = (jnp.minimum(tok_q - w // 2 + w, hi_seq) - jnp.maximum(tok_q - w // 2, lo_seq)).astype(F32)
        ug = u_ext[:, g * pg:(g + 1) * pg]
        sums = jnp.dot(band, ug, preferred_element_type=F32)
        z = sums / cnt - uc_ref[:, g * pg:(g + 1) * pg].astype(F32)
        y = jnp.dot(z.astype(BF16), pw_ref[g], preferred_element_type=F32)
        y = (y + pb_ref[:, g * pg:(g + 1) * pg]) * ps_ref[:, g * pg:(g + 1) * pg]
        parts.append(y.astype(BF16))
    a = jnp.concatenate(parts, axis=1)
    a_pool = jnp.dot(a, wpb_ref[...], preferred_element_type=F32)
    a_attn = jnp.dot(at_ref[...], wab_ref[...], preferred_element_type=F32)
    merged = gp_ref[...].astype(F32) * a_pool + ga_ref[...].astype(F32) * a_attn
    mix = jnp.dot(merged.astype(BF16), wo_ref[...], preferred_element_type=F32)
    x_in = jnp.where(i < n_a, xa_ref[...], xs_ref[...])
    x1 = _layer_norm(alpha * x_in + mix, g1_ref[...], b1_ref[...])
    half = x1.shape[1] // 2
    n_pk = half // LANES
    word = _pack_bf16_pair(x1[:, :half], x1[:, half:])
    for c in range(n_pk):
        x1_ref[pl.ds(c, tm, stride=n_pk), :] = word[:, c * LANES:(c + 1) * LANES]
    x1f_ref[...] = x1
    x1b = x1.astype(BF16)

    x1l = (x1 - x1b.astype(F32)).astype(BF16)
    nt = (((1,), (1,)), ((), ()))
    logits = (lax.dot_general(rwh_ref[...], x1b, nt, preferred_element_type=F32)
              + lax.dot_general(rwh_ref[...], x1l, nt, preferred_element_type=F32)
              + lax.dot_general(rwl_ref[...], x1b, nt, preferred_element_type=F32))
    scores = jax.nn.sigmoid(logits)
    biased = scores + rb_ref[...]
    gsz = n_experts // N_GROUPS
    neg_inf = -jnp.inf

    mem = lax.broadcasted_iota(I32, (gsz, tm), 0)
    gs_rows = []
    for g in range(N_GROUPS):
        bg = biased[g * gsz:(g + 1) * gsz, :]
        m1 = jnp.max(bg, axis=0, keepdims=True)
        first = _first_index_of(bg == m1, mem, gsz, 0)
        m2 = jnp.max(jnp.where(mem == first, neg_inf, bg), axis=0, keepdims=True)
        gs_rows.append(m1 + m2)
    gwork = jnp.concatenate(gs_rows, axis=0)

    gidx = lax.broadcasted_iota(I32, (N_GROUPS, tm), 0)
    egrp = lax.broadcasted_iota(I32, (n_experts, tm), 0) // gsz
    eidx = lax.broadcasted_iota(I32, (n_experts, tm), 0)
    work = jnp.full((n_experts, tm), neg_inf, F32)
    for _ in range(TOPK_GROUPS):
        gm = jnp.max(gwork, axis=0, keepdims=True)
        gf = _first_index_of(gwork == gm, gidx, N_GROUPS, 0)
        gwork = jnp.where(gidx == gf, neg_inf, gwork)
        work = jnp.where(egrp == gf, biased, work)

    hot = jnp.zeros((n_experts, tm), F32)
    sel_idx, sel_w, sel_hit = [], [], []
    for _ in range(TOP_K):
        em = jnp.max(work, axis=0, keepdims=True)
        ef = _first_index_of(work == em, eidx, n_experts, 0)
        hit = eidx == ef
        sel_idx.append(ef)
        sel_w.append(jnp.sum(jnp.where(hit, scores, 0.0), axis=0, keepdims=True))
        sel_hit.append(hit)
        hot = jnp.where(hit, 1.0, hot)
        work = jnp.where(hit, neg_inf, work)
    wsel = jnp.concatenate(sel_w, axis=0)
    gate_ref[...] = wsel / jnp.sum(wsel, axis=0, keepdims=True) * ROUTED_SCALE
    idx_ref[...] = jnp.concatenate(sel_idx, axis=0)

    tri = jnp.where(lax.broadcasted_iota(I32, (tm, tm), 0) < lax.broadcasted_iota(I32, (tm, tm), 1),
                    1.0, 0.0).astype(BF16)
    before = carry_ref[:, 0:1] + jnp.dot(hot.astype(BF16), tri, preferred_element_type=F32)
    rank = [jnp.sum(jnp.where(h, before, 0.0), axis=0, keepdims=True) for h in sel_hit]
    rank_ref[...] = jnp.concatenate(rank, axis=0).astype(I32)
    carry_ref[...] = carry_ref[...] + jnp.sum(hot, axis=1, keepdims=True)
    cnt_ref[...] = carry_ref[...]


def _mixer(xa, xs, h, attn, seq_lo, seq_hi, p, *, alpha, u_block, gate_block0):
    d = xa.shape[1]
    t = xa.shape[0] + xs.shape[0]
    tm = MIX_TM
    assert xa.shape[0] % tm == 0 and xs.shape[0] % tm == 0
    n_a = xa.shape[0] // tm
    map_a, map_s = _two_array_maps(n_a)
    pw = p["wpb"].shape[0]
    aw = attn.shape[1]
    n_experts = p["rwh"].shape[0]
    halo_per_tile = tm // POOL_HALO
    n_halo_blocks = t // POOL_HALO
    gp_col = gate_block0 * INPROJ_TN // d
    in_specs = [
        pl.BlockSpec((tm, d), map_a),
        pl.BlockSpec((tm, d), map_s),
        pl.BlockSpec((POOL_HALO, pw), lambda i, lo, hi: (jnp.maximum(i * halo_per_tile - 1, 0), u_block)),
        pl.BlockSpec((tm, pw), lambda i, lo, hi: (i, u_block)),
        pl.BlockSpec((POOL_HALO, pw),
                     lambda i, lo, hi: (jnp.minimum((i + 1) * halo_per_tile, n_halo_blocks - 1), u_block)),
        pl.BlockSpec((tm, d), lambda i, lo, hi: (i, gp_col)),
        pl.BlockSpec((tm, d), lambda i, lo, hi: (i, gp_col + 1)),
        pl.BlockSpec((tm, aw), lambda i, lo, hi: (i, 0)),
    ]
    weights = [p["pool_w"], p["pool_b"], p["pool_scale"], p["wpb"], p["wab"], p["wo"], p["ln1_g"], p["ln1_b"],
               p["rwh"], p["rwl"], p["rb"]]
    in_specs += [pl.BlockSpec(w.shape, (lambda nd: (lambda i, lo, hi: (0,) * nd))(w.ndim),
                              pipeline_mode=pl.Buffered(1)) for w in weights]
    n_pk = d // (2 * LANES)
    out_shape = (jax.ShapeDtypeStruct((t * n_pk, LANES), I32),
                 jax.ShapeDtypeStruct((t, d), F32),
                 jax.ShapeDtypeStruct((TOP_K, t), I32),
                 jax.ShapeDtypeStruct((TOP_K, t), F32),
                 jax.ShapeDtypeStruct((TOP_K, t), I32),
                 jax.ShapeDtypeStruct((n_experts, LANES), F32))
    out_specs = (pl.BlockSpec((tm * n_pk, LANES), lambda i, lo, hi: (i, 0)),
                 pl.BlockSpec((tm, d), lambda i, lo, hi: (i, 0)),
                 pl.BlockSpec((TOP_K, tm), lambda i, lo, hi: (0, i)),
                 pl.BlockSpec((TOP_K, tm), lambda i, lo, hi: (0, i)),
                 pl.BlockSpec((TOP_K, tm), lambda i, lo, hi: (0, i)),
                 pl.BlockSpec((n_experts, LANES), lambda i, lo, hi: (0, 0)))
    grid_spec = pltpu.PrefetchScalarGridSpec(
        num_scalar_prefetch=2, grid=(t // tm,), in_specs=in_specs, out_specs=out_specs,
        scratch_shapes=[pltpu.VMEM((n_experts, LANES), F32)])
    return pl.pallas_call(
        functools.partial(_mixer_body, n_a=n_a, alpha=alpha, n_experts=n_experts),
        grid_spec=grid_spec, out_shape=out_shape,
        compiler_params=_cparams(("arbitrary",)),
        name="mixer",
    )(seq_lo, seq_hi, xa, xs, h, h, h, h, h, attn, *weights)


SC_CORES = 2
SC_SUBCORES = 16
SC_CHUNK = 32


def _sc_dispatch(x1t3, dest3, n_rows):
    t, n_lt, _ = x1t3.shape
    n_workers = SC_CORES * SC_SUBCORES
    assert t % (n_workers * SC_CHUNK) == 0
    chunks_per_worker = t // (n_workers * SC_CHUNK)
    mesh = plsc.VectorSubcoreMesh(core_axis_name="c", subcore_axis_name="s",
                                  num_cores=SC_CORES, num_subcores=SC_SUBCORES)

    @functools.partial(
        pl.kernel, mesh=mesh,
        out_type=jax.ShapeDtypeStruct((n_rows, n_lt, LANES), x1t3.dtype),
        scratch_types=[pltpu.VMEM((TOP_K, SC_CHUNK), I32),
                       pltpu.VMEM((SC_CHUNK, n_lt, LANES), x1t3.dtype),
                       pltpu.SemaphoreType.DMA],
        name="sc_dispatch")
    def dispatch(x_hbm, dest_hbm, out_hbm, idx_v, rows_v, sem):
        wid = lax.axis_index("s") * SC_CORES + lax.axis_index("c")

        @pl.loop(0, chunks_per_worker)
        def _(j):
            chunk = wid * chunks_per_worker + j
            base = pl.multiple_of(chunk * SC_CHUNK, SC_CHUNK)
            pltpu.sync_copy(x_hbm.at[pl.ds(base, SC_CHUNK)], rows_v)
            pltpu.sync_copy(dest_hbm.at[chunk], idx_v)
            copies = [pltpu.make_async_copy(rows_v, out_hbm.at[idx_v.at[k]], sem) for k in range(TOP_K)]
            for cp in copies:
                cp.start()
            for cp in copies:
                cp.wait()

    return dispatch(x1t3, dest3)


SC_RING = 3


def _sc_gather(yb3, dest3, t):
    _, n_lt, _ = yb3.shape
    n_workers = SC_CORES * SC_SUBCORES
    chunks_per_worker = t // (n_workers * SC_CHUNK)
    mesh = plsc.VectorSubcoreMesh(core_axis_name="c", subcore_axis_name="s",
                                  num_cores=SC_CORES, num_subcores=SC_SUBCORES)

    @functools.partial(
        pl.kernel, mesh=mesh,
        out_type=jax.ShapeDtypeStruct((TOP_K * t, n_lt, LANES), yb3.dtype),
        scratch_types=[pltpu.VMEM((TOP_K, SC_CHUNK), I32),
                       pltpu.VMEM((SC_RING, SC_CHUNK, n_lt, LANES), yb3.dtype)]
                      + [pltpu.SemaphoreType.DMA] * (2 * SC_RING),
        name="sc_gather")
    def gather(y_hbm, dest_hbm, out_hbm, idx_v, rows_v, *sems):
        gsem, wsem = sems[:SC_RING], sems[SC_RING:]
        wid = lax.axis_index("s") * SC_CORES + lax.axis_index("c")

        @pl.loop(0, chunks_per_worker)
        def _(j):
            chunk = wid * chunks_per_worker + j
            base = pl.multiple_of(chunk * SC_CHUNK, SC_CHUNK)
            pltpu.sync_copy(dest_hbm.at[chunk], idx_v)
            gets = [pltpu.make_async_copy(y_hbm.at[idx_v.at[k]], rows_v.at[k % SC_RING], gsem[k % SC_RING])
                    for k in range(TOP_K)]
            puts = [pltpu.make_async_copy(rows_v.at[k % SC_RING], out_hbm.at[pl.ds(k * t + base, SC_CHUNK)],
                                          wsem[k % SC_RING]) for k in range(TOP_K)]
            put_waited = set()
            for k in range(SC_RING - 1):
                gets[k].start()
            for k in range(TOP_K):
                gets[k].wait()
                puts[k].start()
                nxt = k + SC_RING - 1
                if nxt < TOP_K:
                    if k >= 1:
                        puts[k - 1].wait()
                        put_waited.add(k - 1)
                    gets[nxt].start()
            for k in range(TOP_K):
                if k not in put_waited:
                    puts[k].wait()

    return gather(yb3, dest3)


def _experts_body(be_ref, nact_ref, first_ref, nxte_ref, wslot_ref, valid_ref,
                  xs_ref, w1_hbm, w3_hbm, w2_hbm, o_ref,
                  wst1, wst3, wst2, w13b, w2b, sem_w):
    b = pl.program_id(0)
    nact = nact_ref[0]
    ed = wst2.shape[1]
    d = wst2.shape[2]
    n_pk = d // (2 * LANES)
    bm = xs_ref.shape[0] // n_pk
    active = b < nact

    def weight_copies(e, s):
        return (pltpu.make_async_copy(w1_hbm.at[e], wst1.at[s], sem_w.at[s]),
                pltpu.make_async_copy(w3_hbm.at[e], wst3.at[s], sem_w.at[s]),
                pltpu.make_async_copy(w2_hbm.at[e], wst2.at[s], sem_w.at[s]))

    @pl.when(b == 0)
    def _():
        for cp in weight_copies(be_ref[0], wslot_ref[0]):
            cp.start()

    @pl.when(first_ref[b] == 1)
    def _():
        ws = wslot_ref[b]
        for cp in weight_copies(be_ref[b], ws):
            cp.wait()

        @pl.when(nxte_ref[b] >= 0)
        def _():
            for cp in weight_copies(nxte_ref[b], 1 - ws):
                cp.start(priority=1)

        for c in range(ed // MXU_N):
            w13b[:, 2 * c * MXU_N:(2 * c + 1) * MXU_N] = wst1[ws, :, c * MXU_N:(c + 1) * MXU_N].astype(BF16)
            w13b[:, (2 * c + 1) * MXU_N:(2 * c + 2) * MXU_N] = wst3[ws, :, c * MXU_N:(c + 1) * MXU_N].astype(BF16)
        w2b[...] = wst2[ws].astype(BF16)

    @pl.when(active)
    def _():
        keep = lax.broadcasted_iota(I32, (bm, LANES), 0) < valid_ref[b]
        los, his = [], []
        for c in range(n_pk):
            lo, hi = _unpack_bf16_pair(jnp.where(keep, xs_ref[pl.ds(c, bm, stride=n_pk), :], 0))
            los.append(lo.astype(BF16))
            his.append(hi.astype(BF16))
        xb = jnp.concatenate(los + his, axis=1)
        parts = []
        for c in range(ed // MXU_N):
            hc = jnp.dot(xb, w13b[:, 2 * c * MXU_N:(2 * c + 2) * MXU_N], preferred_element_type=F32)
            parts.append((jax.nn.silu(hc[:, :MXU_N]) * hc[:, MXU_N:]).astype(BF16))
        hdn = jnp.concatenate(parts, axis=1)
        n_down = d // MXU_N
        ys = [jnp.dot(hdn, w2b[:, c * MXU_N:(c + 1) * MXU_N], preferred_element_type=F32) for c in range(n_down)]
        tiles_per_chunk = MXU_N // LANES
        for c in range(n_down // 2):
            word = _pack_bf16_pair(ys[c], ys[c + n_down // 2])
            for h in range(tiles_per_chunk):
                o_ref[pl.ds(c * tiles_per_chunk + h, bm, stride=n_pk), :] = word[:, h * LANES:(h + 1) * LANES]

    @pl.when(jnp.logical_not(active))
    def _():
        o_ref[...] = jnp.zeros_like(o_ref)


def _experts(xs2d, block_e, nact, first, nxt_e, wslot, valid, w1, w3, w2, bm):
    d, ed = w1.shape[1], w1.shape[2]
    n_pk = d // (2 * LANES)
    n_blk = block_e.shape[0]
    assert xs2d.shape == (n_blk * bm * n_pk, LANES)
    assert ed % MXU_N == 0 and d % MXU_N == 0 and bm % SUBLANES == 0
    grid_spec = pltpu.PrefetchScalarGridSpec(
        num_scalar_prefetch=6,
        grid=(n_blk,),
        in_specs=[pl.BlockSpec((bm * n_pk, LANES),
                               lambda b, be, na, *_: (jnp.maximum(jnp.minimum(b, na[0] - 1), 0), 0)),
                  pl.BlockSpec(memory_space=pl.ANY),
                  pl.BlockSpec(memory_space=pl.ANY),
                  pl.BlockSpec(memory_space=pl.ANY)],
        out_specs=pl.BlockSpec((bm * n_pk, LANES), lambda b, *_: (b, 0)),
        scratch_shapes=[pltpu.VMEM((2, d, ed), F32),
                        pltpu.VMEM((2, d, ed), F32),
                        pltpu.VMEM((2, ed, d), F32),
                        pltpu.VMEM((d, 2 * ed), BF16),
                        pltpu.VMEM((ed, d), BF16),
                        pltpu.SemaphoreType.DMA((2,))],
    )
    return pl.pallas_call(
        _experts_body,
        grid_spec=grid_spec,
        out_shape=jax.ShapeDtypeStruct((n_blk * bm * n_pk, LANES), I32),
        compiler_params=_cparams(("arbitrary",)),
        name="experts",
    )(block_e, nact, first, nxt_e, wslot, valid, xs2d, w1, w3, w2)


def _finalize_body(x_ref, s13_ref, s2_ref, gate_ref, g2_ref, b2_ref, *refs, n_a, alpha):
    slabs, (oa_ref, os_ref) = refs[:TOP_K], refs[TOP_K:]
    i = pl.program_id(0)
    tm, d = x_ref.shape
    half = d // 2
    n_pk = half // LANES
    x = x_ref[...]
    sd = s2_ref.shape[0]
    hc = jnp.dot(x.astype(BF16), s13_ref[...], preferred_element_type=F32)
    hdn = jax.nn.silu(hc[:, :sd]) * hc[:, sd:]
    base = alpha * x + jnp.dot(hdn.astype(BF16), s2_ref[...], preferred_element_type=F32)
    acc_lo = [base[:, c * LANES:(c + 1) * LANES] for c in range(n_pk)]
    acc_hi = [base[:, half + c * LANES:half + (c + 1) * LANES] for c in range(n_pk)]
    for k in range(TOP_K):
        g = gate_ref[:, k:k + 1]
        for c in range(n_pk):
            lo, hi = _unpack_bf16_pair(slabs[k][pl.ds(c, tm, stride=n_pk), :])
            acc_lo[c] = acc_lo[c] + g * lo
            acc_hi[c] = acc_hi[c] + g * hi
    y = _layer_norm(jnp.concatenate(acc_lo + acc_hi, axis=1), g2_ref[...], b2_ref[...])

    @pl.when(i < n_a)
    def _():
        oa_ref[...] = y

    @pl.when(i >= n_a)
    def _():
        os_ref[...] = y


def _finalize(x1f, s13, s2, gate_t, slabs2d, ln_g, ln_b, t_a, *, alpha):
    t, d = x1f.shape
    tm = FINAL_TM
    n_pk = d // (2 * LANES)
    assert t_a % tm == 0 and 0 < t_a < t and slabs2d.shape == (TOP_K * t * n_pk, LANES)
    n_a = t_a // tm
    n_steps = t // tm
    map_a, map_s = _two_array_maps(n_a)
    slab_specs = [pl.BlockSpec((tm * n_pk, LANES), (lambda k: (lambda i: (k * n_steps + i, 0)))(k))
                  for k in range(TOP_K)]
    return pl.pallas_call(
        functools.partial(_finalize_body, n_a=n_a, alpha=alpha),
        grid=(n_steps,),
        in_specs=[pl.BlockSpec((tm, d), lambda i: (i, 0)),
                  pl.BlockSpec(s13.shape, lambda i: (0, 0), pipeline_mode=pl.Buffered(1)),
                  pl.BlockSpec(s2.shape, lambda i: (0, 0), pipeline_mode=pl.Buffered(1)),
                  pl.BlockSpec((tm, TOP_K), lambda i: (i, 0)),
                  pl.BlockSpec((1, d), lambda i: (0, 0)),
                  pl.BlockSpec((1, d), lambda i: (0, 0))] + slab_specs,
        out_specs=(pl.BlockSpec((tm, d), map_a), pl.BlockSpec((tm, d), map_s)),
        out_shape=(jax.ShapeDtypeStruct((t_a, d), F32), jax.ShapeDtypeStruct((t - t_a, d), F32)),
        compiler_params=_cparams(("arbitrary",)),
        name="finalize",
    )(x1f, s13, s2, gate_t, ln_g, ln_b, *([slabs2d] * TOP_K))


def _split_hi_lo(w):
    hi = w.astype(BF16)
    lo = (w - hi.astype(F32)).astype(BF16)
    return hi, lo


def _encoder_layer(xa, xs, seqs, lp, alpha):
    (w_in, pool_w, pool_b, pool_scale, w_pool_br, w_attn_br, attn_rpb, w_out, ln1_g, ln1_b,
     router_w, router_bias, exp_w1, exp_w3, exp_w2, sh_w1, sh_w3, sh_w2, ln2_g, ln2_b) = lp
    d = xa.shape[1]
    t = xa.shape[0] + xs.shape[0]
    pw = w_pool_br.shape[0]
    aw = w_attn_br.shape[0]
    n_heads = attn_rpb.shape[0]
    head_dim = aw // n_heads
    n_experts = router_w.shape[1]
    assert pw == INPROJ_TN and aw == INPROJ_TN and d % INPROJ_TN == 0
    u_block, q_block = 0, pw // INPROJ_TN
    gate_block0 = (pw + 3 * aw) // INPROJ_TN

    h = _inproj(xa, xs, w_in.astype(BF16), q_block=q_block, gate_block0=gate_block0, q_scale=head_dim ** -0.5)

    blk = ATT_ROWS * GRID_W
    images, seq_lo, seq_hi = [], [], []
    for s0, s1 in seqs:
        images.append((s0 // blk, (s1 - s0) // blk))
        for _ in range((s1 - s0) // MIX_TM):
            seq_lo.append(s0)
            seq_hi.append(s1)
    bias_tab = _attn_bias_table(attn_rpb, 2 * LANES // head_dim)
    attn = _attention(h, bias_tab, images, q_block=q_block, n_heads=n_heads, head_dim=head_dim)

    rwh, rwl = _split_hi_lo(router_w.T)
    p = dict(pool_w=pool_w.astype(BF16), pool_b=pool_b.reshape(1, pw), pool_scale=pool_scale.reshape(1, pw),
             wpb=w_pool_br.astype(BF16), wab=w_attn_br.astype(BF16), wo=w_out.astype(BF16),
             ln1_g=ln1_g.reshape(1, d), ln1_b=ln1_b.reshape(1, d),
             rwh=rwh, rwl=rwl, rb=router_bias.reshape(n_experts, 1).astype(F32))
    x1, x1f, idx_t, gate_t, rank_t, counts = _mixer(
        xa, xs, h, attn, jnp.asarray(seq_lo, I32), jnp.asarray(seq_hi, I32), p,
        alpha=alpha, u_block=u_block, gate_block0=gate_block0)

    bm = EXPERT_BM
    cnt = counts[:, 0].astype(I32)
    pcnt = (cnt + bm - 1) // bm * bm
    pend = jnp.cumsum(pcnt)
    pstart = pend - pcnt
    eids = jnp.arange(n_experts, dtype=I32)[:, None, None]
    dest = jnp.sum(jnp.where(idx_t[None] == eids, pstart[:, None, None], 0), axis=0) + rank_t
    n_blk = (t * TOP_K + n_experts * (bm - 1)) // bm
    blk_row = jnp.arange(n_blk, dtype=I32) * bm
    block_e = jnp.minimum(jnp.sum((pend[None, :] <= blk_row[:, None]).astype(I32), axis=1), n_experts - 1)
    nact = (pend[-1] // bm).astype(I32).reshape(1)
    block_e = jnp.where(jnp.arange(n_blk) < nact[0], block_e, block_e[jnp.maximum(nact[0] - 1, 0)])
    eids1 = jnp.arange(n_experts, dtype=I32)
    present = cnt > 0
    later = present[None, :] & (eids1[None, :] > eids1[:, None])
    nxt_of_e = jnp.min(jnp.where(later, eids1[None, :], n_experts), axis=1)
    nxt_of_e = jnp.where(nxt_of_e >= n_experts, -1, nxt_of_e).astype(I32)
    ord_of_e = (jnp.cumsum(present.astype(I32)) - 1).astype(I32)
    onehot_be = (block_e[:, None] == eids1[None, :]).astype(I32)
    blk_ids = jnp.arange(n_blk, dtype=I32)
    is_act = blk_ids < nact[0]
    prev_e = jnp.concatenate([jnp.full((1,), -1, I32), block_e[:-1]])
    first = (is_act & (block_e != prev_e)).astype(I32)
    nxt_e = jnp.sum(onehot_be * nxt_of_e[None, :], axis=1).astype(I32)
    wslot = (jnp.sum(onehot_be * ord_of_e[None, :], axis=1) % 2).astype(I32)
    start_of_e = jnp.sum(onehot_be * pstart[None, :], axis=1)
    cnt_of_e = jnp.sum(onehot_be * cnt[None, :], axis=1)
    valid = jnp.where(is_act, jnp.clip(cnt_of_e - (blk_row - start_of_e), 0, bm), 0).astype(I32)
    n_pk = d // (2 * LANES)
    dest3 = jnp.transpose(dest.reshape(TOP_K, t // SC_CHUNK, SC_CHUNK), (1, 0, 2))
    xs = _sc_dispatch(x1.reshape(t, n_pk, LANES), dest3, n_blk * bm)
    yb = _experts(xs.reshape(n_blk * bm * n_pk, LANES), block_e, nact, first, nxt_e, wslot, valid,
                  exp_w1, exp_w3, exp_w2, bm)

    slabs = _sc_gather(yb.reshape(n_blk * bm, n_pk, LANES), dest3, t)
    return _finalize(x1f, jnp.concatenate([sh_w1, sh_w3], axis=1).astype(BF16), sh_w2.astype(BF16), gate_t.T,
                     slabs.reshape(TOP_K * t * n_pk, LANES), ln2_g.reshape(1, d), ln2_b.reshape(1, d),
                     xa.shape[0], alpha=alpha)


def kernel(x_prompt, x_sample, w_in, pool_w, pool_b, pool_scale, w_pool_br, w_attn_br, attn_rpb, w_out,
           ln1_g, ln1_b, router_w, router_bias, exp_w1, exp_w3, exp_w2, sh_w1, sh_w3, sh_w2, ln2_g, ln2_b):
    depth = w_in.shape[0]
    alpha = (2.0 * depth) ** 0.25
    bp, sp, d = x_prompt.shape
    bs, ss, _ = x_sample.shape
    seqs = [(i * sp, (i + 1) * sp) for i in range(bp)]
    seqs += [(bp * sp + i * ss, bp * sp + (i + 1) * ss) for i in range(bs)]
    xa, xs = x_prompt.reshape(bp * sp, d), x_sample.reshape(bs * ss, d)
    for l in range(depth):
        lp = (w_in[l], pool_w[l], pool_b[l], pool_scale[l], w_pool_br[l], w_attn_br[l], attn_rpb[l], w_out[l],
              ln1_g[l], ln1_b[l], router_w[l], router_bias[l], exp_w1[l], exp_w3[l], exp_w2[l],
              sh_w1[l], sh_w3[l], sh_w2[l], ln2_g[l], ln2_b[l])
        xa, xs = _encoder_layer(xa, xs, seqs, lp, alpha)
    return (xa.reshape(bp, sp, d), xs.reshape(bs, ss, d))
```
